```python
import jax, jax.numpy as jnp
from jax import lax
import numpy as np

D_MODEL = 2048
BATCH = 8
SEQ = 8192
DEPTH = 4

CHUNK = 64
EPS = 1e-6
POOL_WIDTH = 1024
POOL_WINDOWS = (2, 4, 8, 16)
N_POOL_GROUPS = 4
POOL_GROUP = POOL_WIDTH // N_POOL_GROUPS
N_Q_HEADS = 16
N_KV_HEADS = 4
HEAD_DIM = 64
ATTN_WIDTH = N_Q_HEADS * HEAD_DIM
KV_WIDTH = N_KV_HEADS * HEAD_DIM
WINDOW = 128
WINDOW_CHUNKS = WINDOW // CHUNK
CONV_WIDTH = 1024
CONV_KERNEL = 31
N_BRANCHES = 3
IN_SPLITS = (POOL_WIDTH, POOL_WIDTH, ATTN_WIDTH, KV_WIDTH, KV_WIDTH, ATTN_WIDTH,
             CONV_WIDTH, CONV_WIDTH, CONV_WIDTH, N_BRANCHES * D_MODEL)
IN_WIDTH = 2 * POOL_WIDTH + 2 * ATTN_WIDTH + 2 * KV_WIDTH + 3 * CONV_WIDTH + N_BRANCHES * D_MODEL

kernel_name = "hybrid_pool_swa_conformer_parallel"


def rms_norm(x, g):
    xf = x.astype(jnp.float32)
    y = xf * lax.rsqrt(jnp.mean(xf * xf, axis=-1, keepdims=True) + EPS)
    return y.astype(x.dtype) * g


def layer_norm(x, g, b):
    xf = x.astype(jnp.float32)
    mu = jnp.mean(xf, axis=-1, keepdims=True)
    var = jnp.mean(jnp.square(xf - mu), axis=-1, keepdims=True)
    return ((xf - mu) * lax.rsqrt(var + EPS)).astype(x.dtype) * g + b


def multiscale_pool(u, pool_w, pool_scale):
    B, S, _ = u.shape
    ug = u.astype(jnp.float32).reshape(B, S, N_POOL_GROUPS, POOL_GROUP)
    cs = jnp.cumsum(ug, axis=1)
    t = jnp.arange(S)
    means = []
    for gi, w in enumerate(POOL_WINDOWS):
        c_g = cs[:, :, gi]
        prev = jnp.pad(c_g, ((0, 0), (w, 0), (0, 0)))[:, :S]
        cnt = jnp.minimum(t + 1, w).astype(jnp.float32)[None, :, None]
        means.append((c_g - prev) / cnt)
    mixed = (jnp.stack(means, axis=2) - ug).astype(u.dtype)
    y = jnp.einsum('bsgc,gcd->bsgd', mixed, pool_w)
    return y.reshape(B, S, POOL_WIDTH) * pool_scale


def window_attention(q, k, v, sink):
    B, S = q.shape[:2]
    nc = S // CHUNK
    grp = N_Q_HEADS // N_KV_HEADS
    qc = q.reshape(B, nc, CHUNK, N_KV_HEADS, grp, HEAD_DIM)
    pad = WINDOW_CHUNKS * CHUNK

    def band(t):
        tp = jnp.pad(t, ((0, 0), (pad, 0), (0, 0), (0, 0)))
        tp = tp.reshape(B, nc + WINDOW_CHUNKS, CHUNK, N_KV_HEADS, HEAD_DIM)
        return jnp.concatenate([tp[:, i:i + nc] for i in range(WINDOW_CHUNKS + 1)], axis=2)

    kb, vb = band(k), band(v)
    s = jnp.einsum('bnqhgd,bnkhd->bnhgqk', qc, kb).astype(jnp.float32) * (HEAD_DIM ** -0.5)
    n_keys = (WINDOW_CHUNKS + 1) * CHUNK
    key_chunk = jnp.arange(nc)[:, None] - WINDOW_CHUNKS + jnp.arange(n_keys)[None, :] // CHUNK
    valid = (key_chunk >= 0)[None, :, None, None, None, :]
    s = jnp.where(valid, s, -jnp.inf)
    sk = sink.astype(jnp.float32).reshape(1, 1, N_KV_HEADS, grp, 1, 1)
    m = jnp.maximum(jnp.max(s, axis=-1, keepdims=True), sk)
    p = jnp.exp(s - m)
    denom = jnp.sum(p, axis=-1, keepdims=True) + jnp.exp(sk - m)
    p = (p / denom).astype(v.dtype)
    o = jnp.einsum('bnhgqk,bnkhd->bnqhgd', p, vb)
    return o.reshape(B, S, ATTN_WIDTH)


def conformer_conv(a, b, dw, dw_b, ln_g, ln_b, pw):
    g = a * jax.nn.sigmoid(b)
    gp = jnp.pad(g, ((0, 0), (CONV_KERNEL - 1, 0), (0, 0)))
    y = lax.conv_general_dilated(gp, dw[:, None, :], window_strides=(1,), padding='VALID',
                                 dimension_numbers=('NWC', 'WIO', 'NWC'),
                                 feature_group_count=CONV_WIDTH) + dw_b
    y = jax.nn.silu(layer_norm(y, ln_g, ln_b))
    return y @ pw


def _fwd_setup_inputs(seed: int = 0) -> dict:
    key = jax.random.key(seed)
    ks = jax.random.split(key, 20)
    f32 = jnp.float32

    def nrm(k, shape, scale):
        return jax.random.normal(k, shape, f32) * scale

    L, D = DEPTH, D_MODEL
    return {
        "x": nrm(ks[0], (BATCH, SEQ, D), 1.0),
        "c": nrm(ks[1], (BATCH, D), 1.0),
        "norm_g": 1.0 + nrm(ks[2], (L, D), 0.05),
        "w_ada": nrm(ks[3], (L, D, 3 * D), 0.5 * D ** -0.5),
        "b_ada": nrm(ks[4], (L, 3 * D), 0.01),
        "w_in": nrm(ks[5], (L, D, IN_WIDTH), D ** -0.5),
        "pool_w": nrm(ks[6], (L, N_POOL_GROUPS, POOL_GROUP, POOL_GROUP), POOL_GROUP ** -0.5),
        "pool_scale": 1.0 + nrm(ks[7], (L, POOL_WIDTH), 0.1),
        "attn_sink": nrm(ks[8], (L, N_Q_HEADS), 1.0),
        "conv_dw": nrm(ks[9], (L, CONV_KERNEL, CONV_WIDTH), CONV_KERNEL ** -0.5),
        "conv_dw_b": nrm(ks[10], (L, CONV_WIDTH), 0.01),
        "conv_ln_g": 1.0 + nrm(ks[11], (L, CONV_WIDTH), 0.05),
        "conv_ln_b": nrm(ks[12], (L, CONV_WIDTH), 0.01),
        "conv_pw": nrm(ks[13], (L, CONV_WIDTH, CONV_WIDTH), CONV_WIDTH ** -0.5),
        "w_branch_pool": nrm(ks[14], (L, POOL_WIDTH, D), POOL_WIDTH ** -0.5),
        "w_branch_attn": nrm(ks[15], (L, ATTN_WIDTH, D), ATTN_WIDTH ** -0.5),
        "w_branch_conv": nrm(ks[16], (L, CONV_WIDTH, D), CONV_WIDTH ** -0.5),
        "w_out": nrm(ks[17], (L, D, D), D ** -0.5),
        "final_g": 1.0 + nrm(ks[18], (D,), 0.05),
    }


def _fwd_reference(x, c, norm_g, w_ada, b_ada, w_in, pool_w, pool_scale, attn_sink, conv_dw, conv_dw_b,
              conv_ln_g, conv_ln_b, conv_pw, w_branch_pool, w_branch_attn, w_branch_conv, w_out,
              final_g):
    B, S, _ = x.shape
    split_idx = np.cumsum(IN_SPLITS)[:-1].tolist()
    c_act = jax.nn.silu(c)
    for l in range(DEPTH):
        mod = c_act @ w_ada[l] + b_ada[l]
        shift, scale, gate = jnp.split(mod, 3, axis=-1)
        h = rms_norm(x, norm_g[l]) * (1 + scale[:, None]) + shift[:, None]
        proj = h @ w_in[l]
        pool_u, pool_z, q, k, v, attn_z, conv_a, conv_b, conv_z, gates = jnp.split(proj, split_idx, axis=-1)
        y_pool = multiscale_pool(pool_u, pool_w[l], pool_scale[l]) * jax.nn.silu(pool_z)
        y_attn = window_attention(q.reshape(B, S, N_Q_HEADS, HEAD_DIM),
                                  k.reshape(B, S, N_KV_HEADS, HEAD_DIM),
                                  v.reshape(B, S, N_KV_HEADS, HEAD_DIM),
                                  attn_sink[l]) * jax.nn.silu(attn_z)
        y_conv = conformer_conv(conv_a, conv_b, conv_dw[l], conv_dw_b[l], conv_ln_g[l], conv_ln_b[l],
                                conv_pw[l]) * jax.nn.silu(conv_z)
        g_pool, g_attn, g_conv = jnp.split(jax.nn.sigmoid(gates), 3, axis=-1)
        merged = (g_pool * (y_pool @ w_branch_pool[l])
                  + g_attn * (y_attn @ w_branch_attn[l])
                  + g_conv * (y_conv @ w_branch_conv[l]))
        x = x + gate[:, None] * (merged @ w_out[l])
    return rms_norm(x, final_g)


import jax as _jax
import jax.numpy as _jnp

TWIN_FORMAT = 'train_step'
FWD_PARAMS = ['x', 'c', 'norm_g', 'w_ada', 'b_ada', 'w_in', 'pool_w', 'pool_scale', 'attn_sink', 'conv_dw', 'conv_dw_b', 'conv_ln_g', 'conv_ln_b', 'conv_pw', 'w_branch_pool', 'w_branch_attn', 'w_branch_conv', 'w_out', 'final_g']
TWIN_WEIGHTS = ['norm_g', 'w_ada', 'b_ada', 'w_in', 'pool_w', 'pool_scale', 'attn_sink', 'conv_dw', 'conv_dw_b', 'conv_ln_g', 'conv_ln_b', 'conv_pw', 'w_branch_pool', 'w_branch_attn', 'w_branch_conv', 'w_out', 'final_g']
TWIN_DIFF_INPUT = 'x'
TWIN_INPUTS = ['x', 'c', 'norm_g', 'w_ada', 'b_ada', 'w_in', 'pool_w', 'pool_scale', 'attn_sink', 'conv_dw', 'conv_dw_b', 'conv_ln_g', 'conv_ln_b', 'conv_pw', 'w_branch_pool', 'w_branch_attn', 'w_branch_conv', 'w_out', 'final_g', 'loss_target', 'm_norm_g', 'm_w_ada', 'm_b_ada', 'm_w_in', 'm_pool_w', 'm_pool_scale', 'm_attn_sink', 'm_conv_dw', 'm_conv_dw_b', 'm_conv_ln_g', 'm_conv_ln_b', 'm_conv_pw', 'm_w_branch_pool', 'm_w_branch_attn', 'm_w_branch_conv', 'm_w_out', 'm_final_g', 'v_norm_g', 'v_w_ada', 'v_b_ada', 'v_w_in', 'v_pool_w', 'v_pool_scale', 'v_attn_sink', 'v_conv_dw', 'v_conv_dw_b', 'v_conv_ln_g', 'v_conv_ln_b', 'v_conv_pw', 'v_w_branch_pool', 'v_w_branch_attn', 'v_w_branch_conv', 'v_w_out', 'v_final_g']
TWIN_OUTPUTS = ['loss', 'grad_x', 'grad_norm_g', 'grad_w_ada', 'grad_b_ada', 'grad_w_in', 'grad_pool_w', 'grad_pool_scale', 'grad_attn_sink', 'grad_conv_dw', 'grad_conv_dw_b', 'grad_conv_ln_g', 'grad_conv_ln_b', 'grad_conv_pw', 'grad_w_branch_pool', 'grad_w_branch_attn', 'grad_w_branch_conv', 'grad_w_out', 'grad_final_g', 'delta_norm_g', 'delta_w_ada', 'delta_b_ada', 'delta_w_in', 'delta_pool_w', 'delta_pool_scale', 'delta_attn_sink', 'delta_conv_dw', 'delta_conv_dw_b', 'delta_conv_ln_g', 'delta_conv_ln_b', 'delta_conv_pw', 'delta_w_branch_pool', 'delta_w_branch_attn', 'delta_w_branch_conv', 'delta_w_out', 'delta_final_g', 'new_m_norm_g', 'new_m_w_ada', 'new_m_b_ada', 'new_m_w_in', 'new_m_pool_w', 'new_m_pool_scale', 'new_m_attn_sink', 'new_m_conv_dw', 'new_m_conv_dw_b', 'new_m_conv_ln_g', 'new_m_conv_ln_b', 'new_m_conv_pw', 'new_m_w_branch_pool', 'new_m_w_branch_attn', 'new_m_w_branch_conv', 'new_m_w_out', 'new_m_final_g', 'new_v_norm_g', 'new_v_w_ada', 'new_v_b_ada', 'new_v_w_in', 'new_v_pool_w', 'new_v_pool_scale', 'new_v_attn_sink', 'new_v_conv_dw', 'new_v_conv_dw_b', 'new_v_conv_ln_g', 'new_v_conv_ln_b', 'new_v_conv_pw', 'new_v_w_branch_pool', 'new_v_w_branch_attn', 'new_v_w_branch_conv', 'new_v_w_out', 'new_v_final_g']
TWIN_LEAF_KINDS = {'loss': 'loss', 'grad_x': 'grad_x', 'grad_norm_g': 'grad_w', 'grad_w_ada': 'grad_w', 'grad_b_ada': 'grad_w', 'grad_w_in': 'grad_w', 'grad_pool_w': 'grad_w', 'grad_pool_scale': 'grad_w', 'grad_attn_sink': 'grad_w', 'grad_conv_dw': 'grad_w', 'grad_conv_dw_b': 'grad_w', 'grad_conv_ln_g': 'grad_w', 'grad_conv_ln_b': 'grad_w', 'grad_conv_pw': 'grad_w', 'grad_w_branch_pool': 'grad_w', 'grad_w_branch_attn': 'grad_w', 'grad_w_branch_conv': 'grad_w', 'grad_w_out': 'grad_w', 'grad_final_g': 'grad_w', 'delta_norm_g': 'delta_w', 'delta_w_ada': 'delta_w', 'delta_b_ada': 'delta_w', 'delta_w_in': 'delta_w', 'delta_pool_w': 'delta_w', 'delta_pool_scale': 'delta_w', 'delta_attn_sink': 'delta_w', 'delta_conv_dw': 'delta_w', 'delta_conv_dw_b': 'delta_w', 'delta_conv_ln_g': 'delta_w', 'delta_conv_ln_b': 'delta_w', 'delta_conv_pw': 'delta_w', 'delta_w_branch_pool': 'delta_w', 'delta_w_branch_attn': 'delta_w', 'delta_w_branch_conv': 'delta_w', 'delta_w_out': 'delta_w', 'delta_final_g': 'delta_w', 'new_m_norm_g': 'new_m', 'new_m_w_ada': 'new_m', 'new_m_b_ada': 'new_m', 'new_m_w_in': 'new_m', 'new_m_pool_w': 'new_m', 'new_m_pool_scale': 'new_m', 'new_m_attn_sink': 'new_m', 'new_m_conv_dw': 'new_m', 'new_m_conv_dw_b': 'new_m', 'new_m_conv_ln_g': 'new_m', 'new_m_conv_ln_b': 'new_m', 'new_m_conv_pw': 'new_m', 'new_m_w_branch_pool': 'new_m', 'new_m_w_branch_attn': 'new_m', 'new_m_w_branch_conv': 'new_m', 'new_m_w_out': 'new_m', 'new_m_final_g': 'new_m', 'new_v_norm_g': 'new_v', 'new_v_w_ada': 'new_v', 'new_v_b_ada': 'new_v', 'new_v_w_in': 'new_v', 'new_v_pool_w': 'new_v', 'new_v_pool_scale': 'new_v', 'new_v_attn_sink': 'new_v', 'new_v_conv_dw': 'new_v', 'new_v_conv_dw_b': 'new_v', 'new_v_conv_ln_g': 'new_v', 'new_v_conv_ln_b': 'new_v', 'new_v_conv_pw': 'new_v', 'new_v_w_branch_pool': 'new_v', 'new_v_w_branch_attn': 'new_v', 'new_v_w_branch_conv': 'new_v', 'new_v_w_out': 'new_v', 'new_v_final_g': 'new_v'}


def _forward(args):
    return _fwd_reference(*[args[k] for k in FWD_PARAMS])


def _output_shape():
    def fwd():
        inp = _fwd_setup_inputs(0)
        return _fwd_reference(*[inp[k] for k in FWD_PARAMS])
    out = _jax.eval_shape(fwd)
    return out.shape, out.dtype

N_MICROBATCH = 1
ADAM_LR = 0.001
ADAM_B1 = 0.9
ADAM_B2 = 0.999
ADAM_EPS = 1e-08
ADAM_WD = 0.01
ADAM_STEP = 10
PER_EXAMPLE_BATCH_AXIS = {'x': 0, 'c': 0, 'loss_target': 0}
SHARED_INPUTS = []
_WEIGHT_DTYPES = {'norm_g': _jnp.float32, 'w_ada': _jnp.float32, 'b_ada': _jnp.float32, 'w_in': _jnp.float32, 'pool_w': _jnp.float32, 'pool_scale': _jnp.float32, 'attn_sink': _jnp.float32, 'conv_dw': _jnp.float32, 'conv_dw_b': _jnp.float32, 'conv_ln_g': _jnp.float32, 'conv_ln_b': _jnp.float32, 'conv_pw': _jnp.float32, 'w_branch_pool': _jnp.float32, 'w_branch_attn': _jnp.float32, 'w_branch_conv': _jnp.float32, 'w_out': _jnp.float32, 'final_g': _jnp.float32}
MOMENT_SCALE = {'norm_g': 2.281438e-02, 'w_ada': 2.317671e-02, 'b_ada': 4.164238e-02, 'w_in': 9.165486e-03, 'pool_w': 1.756508e-02, 'pool_scale': 1.747156e-02, 'attn_sink': 4.202254e-04, 'conv_dw': 1.193279e-02, 'conv_dw_b': 2.319936e-02, 'conv_ln_g': 1.421245e-02, 'conv_ln_b': 1.221345e-02, 'conv_pw': 1.157743e-02, 'w_branch_pool': 1.242360e-02, 'w_branch_attn': 4.613973e-03, 'w_branch_conv': 8.139488e-03, 'w_out': 1.548388e-02, 'final_g': 3.202572e+01}


def _to_microbatches(a, axis):
    t = _jnp.moveaxis(a, axis, 0)
    t = t.reshape((N_MICROBATCH, t.shape[0] // N_MICROBATCH) + t.shape[1:])
    return _jnp.moveaxis(t, 1, axis + 1)


def setup_inputs(seed: int = 0) -> dict:
    inp = _fwd_setup_inputs(seed)
    key = _jax.random.fold_in(_jax.random.key(seed), 7919)
    shape, _ = _output_shape()
    out = dict(inp)
    out["loss_target"] = _jax.random.normal(_jax.random.fold_in(key, 0), shape, _jnp.float32)
    for i, name in enumerate(TWIN_WEIGHTS):
        w = inp[name].astype(_jnp.float32)
        if MOMENT_SCALE is None:
            s = _jnp.sqrt(_jnp.mean(_jnp.square(w)) + 1e-30)
        else:
            s = MOMENT_SCALE[name]
        km, kv = _jax.random.split(_jax.random.fold_in(key, i + 1))
        out[name] = w
        out["m_" + name] = s * _jax.random.normal(km, w.shape, _jnp.float32)
        out["v_" + name] = (s * s) * _jax.random.uniform(kv, w.shape, _jnp.float32, 0.5, 1.5)
    if N_MICROBATCH > 1:
        for name, axis in PER_EXAMPLE_BATCH_AXIS.items():
            out[name] = _to_microbatches(out[name], axis)
    return {'x': out['x'], 'c': out['c'], 'norm_g': out['norm_g'], 'w_ada': out['w_ada'], 'b_ada': out['b_ada'], 'w_in': out['w_in'], 'pool_w': out['pool_w'], 'pool_scale': out['pool_scale'], 'attn_sink': out['attn_sink'], 'conv_dw': out['conv_dw'], 'conv_dw_b': out['conv_dw_b'], 'conv_ln_g': out['conv_ln_g'], 'conv_ln_b': out['conv_ln_b'], 'conv_pw': out['conv_pw'], 'w_branch_pool': out['w_branch_pool'], 'w_branch_attn': out['w_branch_attn'], 'w_branch_conv': out['w_branch_conv'], 'w_out': out['w_out'], 'final_g': out['final_g'], 'loss_target': out['loss_target'], 'm_norm_g': out['m_norm_g'], 'm_w_ada': out['m_w_ada'], 'm_b_ada': out['m_b_ada'], 'm_w_in': out['m_w_in'], 'm_pool_w': out['m_pool_w'], 'm_pool_scale': out['m_pool_scale'], 'm_attn_sink': out['m_attn_sink'], 'm_conv_dw': out['m_conv_dw'], 'm_conv_dw_b': out['m_conv_dw_b'], 'm_conv_ln_g': out['m_conv_ln_g'], 'm_conv_ln_b': out['m_conv_ln_b'], 'm_conv_pw': out['m_conv_pw'], 'm_w_branch_pool': out['m_w_branch_pool'], 'm_w_branch_attn': out['m_w_branch_attn'], 'm_w_branch_conv': out['m_w_branch_conv'], 'm_w_out': out['m_w_out'], 'm_final_g': out['m_final_g'], 'v_norm_g': out['v_norm_g'], 'v_w_ada': out['v_w_ada'], 'v_b_ada': out['v_b_ada'], 'v_w_in': out['v_w_in'], 'v_pool_w': out['v_pool_w'], 'v_pool_scale': out['v_pool_scale'], 'v_attn_sink': out['v_attn_sink'], 'v_conv_dw': out['v_conv_dw'], 'v_conv_dw_b': out['v_conv_dw_b'], 'v_conv_ln_g': out['v_conv_ln_g'], 'v_conv_ln_b': out['v_conv_ln_b'], 'v_conv_pw': out['v_conv_pw'], 'v_w_branch_pool': out['v_w_branch_pool'], 'v_w_branch_attn': out['v_w_branch_attn'], 'v_w_branch_conv': out['v_w_branch_conv'], 'v_w_out': out['v_w_out'], 'v_final_g': out['v_final_g']}


def _loss(weights, diff, rest, loss_target):
    with _jax.named_scope("forward"):
        args = {**rest, TWIN_DIFF_INPUT: diff, **{k: w.astype(_WEIGHT_DTYPES[k]) for k, w in weights.items()}}
        y = _forward(args)
    with _jax.named_scope("loss_head"):
        err = _jnp.square(y.astype(_jnp.float32) - loss_target)
        return 0.5 * _jnp.sum(_jnp.mean(err, axis=-1)) if err.ndim else 0.5 * err


def _adamw(w, g, m, v):
    m = ADAM_B1 * m + (1.0 - ADAM_B1) * g
    v = ADAM_B2 * v + (1.0 - ADAM_B2) * _jnp.square(g)
    m_hat = m / (1.0 - ADAM_B1 ** ADAM_STEP)
    v_hat = v / (1.0 - ADAM_B2 ** ADAM_STEP)
    delta = -ADAM_LR * (m_hat / (_jnp.sqrt(v_hat) + ADAM_EPS) + ADAM_WD * w)
    return delta, m, v


def reference(x, c, norm_g, w_ada, b_ada, w_in, pool_w, pool_scale, attn_sink, conv_dw, conv_dw_b, conv_ln_g, conv_ln_b, conv_pw, w_branch_pool, w_branch_attn, w_branch_conv, w_out, final_g, loss_target, m_norm_g, m_w_ada, m_b_ada, m_w_in, m_pool_w, m_pool_scale, m_attn_sink, m_conv_dw, m_conv_dw_b, m_conv_ln_g, m_conv_ln_b, m_conv_pw, m_w_branch_pool, m_w_branch_attn, m_w_branch_conv, m_w_out, m_final_g, v_norm_g, v_w_ada, v_b_ada, v_w_in, v_pool_w, v_pool_scale, v_attn_sink, v_conv_dw, v_conv_dw_b, v_conv_ln_g, v_conv_ln_b, v_conv_pw, v_w_branch_pool, v_w_branch_attn, v_w_branch_conv, v_w_out, v_final_g):
    given = dict(x=x, c=c, norm_g=norm_g, w_ada=w_ada, b_ada=b_ada, w_in=w_in, pool_w=pool_w, pool_scale=pool_scale, attn_sink=attn_sink, conv_dw=conv_dw, conv_dw_b=conv_dw_b, conv_ln_g=conv_ln_g, conv_ln_b=conv_ln_b, conv_pw=conv_pw, w_branch_pool=w_branch_pool, w_branch_attn=w_branch_attn, w_branch_conv=w_branch_conv, w_out=w_out, final_g=final_g, loss_target=loss_target, m_norm_g=m_norm_g, m_w_ada=m_w_ada, m_b_ada=m_b_ada, m_w_in=m_w_in, m_pool_w=m_pool_w, m_pool_scale=m_pool_scale, m_attn_sink=m_attn_sink, m_conv_dw=m_conv_dw, m_conv_dw_b=m_conv_dw_b, m_conv_ln_g=m_conv_ln_g, m_conv_ln_b=m_conv_ln_b, m_conv_pw=m_conv_pw, m_w_branch_pool=m_w_branch_pool, m_w_branch_attn=m_w_branch_attn, m_w_branch_conv=m_w_branch_conv, m_w_out=m_w_out, m_final_g=m_final_g, v_norm_g=v_norm_g, v_w_ada=v_w_ada, v_b_ada=v_b_ada, v_w_in=v_w_in, v_pool_w=v_pool_w, v_pool_scale=v_pool_scale, v_attn_sink=v_attn_sink, v_conv_dw=v_conv_dw, v_conv_dw_b=v_conv_dw_b, v_conv_ln_g=v_conv_ln_g, v_conv_ln_b=v_conv_ln_b, v_conv_pw=v_conv_pw, v_w_branch_pool=v_w_branch_pool, v_w_branch_attn=v_w_branch_attn, v_w_branch_conv=v_w_branch_conv, v_w_out=v_w_out, v_final_g=v_final_g)
    weights = {n: given[n] for n in TWIN_WEIGHTS}
    shared = {n: given[n] for n in SHARED_INPUTS}
    per_example = {n: given[n] for n in ['x', 'c']}
    grad_fn = _jax.value_and_grad(_loss, argnums=(0, 1))

    def one_microbatch(ex, loss_target):
        ex = dict(ex)
        diff = ex.pop(TWIN_DIFF_INPUT)
        return grad_fn(weights, diff, {**shared, **ex}, loss_target)

    if N_MICROBATCH == 1:
        loss, (grad_w, grad_x) = one_microbatch(per_example, given["loss_target"])
    else:
        def body(carry, xs):
            loss_sum, grad_sum = carry
            l_k, (gw_k, gx_k) = one_microbatch(xs[0], xs[1])
            with _jax.named_scope("update"):
                return (loss_sum + l_k, _jax.tree.map(_jnp.add, grad_sum, gw_k)), gx_k

        init = (_jnp.zeros((), _jnp.float32), _jax.tree.map(_jnp.zeros_like, weights))
        (loss, grad_w), grad_x = _jax.lax.scan(body, init, (per_example, given["loss_target"]))
    with _jax.named_scope("update"):
        delta_w, new_m, new_v = {}, {}, {}
        for n in TWIN_WEIGHTS:
            delta_w[n], new_m[n], new_v[n] = _adamw(weights[n], grad_w[n], given["m_" + n], given["v_" + n])
    return (loss, grad_x, *[grad_w[n] for n in TWIN_WEIGHTS], *[delta_w[n] for n in TWIN_WEIGHTS],
            *[new_m[n] for n in TWIN_WEIGHTS], *[new_v[n] for n in TWIN_WEIGHTS])
```

```python
import functools

import jax
import jax.numpy as jnp
from jax import lax
from jax.experimental import pallas as pl
from jax.experimental.pallas import tpu as pltpu

F32 = jnp.float32
BF16 = jnp.bfloat16
MESH = pl.DeviceIdType.MESH
ANY = pl.BlockSpec(memory_space=pl.ANY)

CHUNK = 64
HEAD_DIM = 64
N_Q_HEADS = 16
N_KV_HEADS = 4
Q_PER_KV = N_Q_HEADS // N_KV_HEADS
WINDOW_CHUNKS = 2
POOL_WIDTH = 1024
POOL_WINDOWS = (2, 4, 8, 16)
POOL_GROUP = 256
ATTN_WIDTH = 1024
KV_WIDTH = 256
CONV_WIDTH = 1024
CONV_KERNEL = 31
EPS = 1e-6
OFF_U, OFF_Z, OFF_Q, OFF_K, OFF_V, OFF_AZ, OFF_CA, OFF_CB, OFF_CZ, OFF_G = (
    0, 1024, 2048, 3072, 3328, 3584, 4608, 5632, 6656, 7680)
HALF = 512
POOL_HALO = 16
CONV_HALO = 32
ATTN_Q_BLOCK = 256
ATTN_HALO = WINDOW_CHUNKS * CHUNK
NEG_INF = -1e30

ADAM_LR, ADAM_B1, ADAM_B2, ADAM_EPS, ADAM_WD, ADAM_STEP = 0.001, 0.9, 0.999, 1e-08, 0.01, 10

N_DEV = 8
N_CHIP = 4
VMEM_CAP_BYTES = 56 * 2**20
MIB = 2**20


def _div(n, cap, mult=128):
    if n <= cap:
        return n
    best = None
    for t in range(mult, cap + 1, mult):
        if n % t == 0:
            best = t
    assert best is not None, (n, cap, mult)
    return best


def _params(n_grid, vmem_bytes=None):
    kw = dict(dimension_semantics=("arbitrary",) * n_grid)
    if vmem_bytes is not None:
        kw["vmem_limit_bytes"] = int(min(max(vmem_bytes * 5 // 4 + 4 * MIB, 32 * MIB), VMEM_CAP_BYTES))
    return pltpu.CompilerParams(**kw)


def _silu(z):
    return z * jax.nn.sigmoid(z)


def _dsilu(z):
    s = jax.nn.sigmoid(z)
    return s * (1.0 + z * (1.0 - s))


def _nbytes(shape, dtype):
    n = 1
    for d in shape:
        n *= d
    return n * jnp.dtype(dtype).itemsize


def _mm(a, b, mode, out_dtypes, *, name, b_layer=None, into=None, into_layer=None, extras=(), epilogue=None,
        tn_cap=1024):
    if mode == "tn":
        K, M = a.shape
        N = b.shape[-1]
    elif mode == "nt":
        M, K = a.shape
        N = b.shape[-2]
    else:
        M, K = a.shape
        N = b.shape[-1]
    tm = _div(M, 1024)
    tn = _div(N, tn_cap)
    tk = _div(K, 1024 if mode == "tn" else 2048)
    nk = K // tk
    n_out = len(out_dtypes)
    n_ex = len(extras)
    stacked = b.ndim == 3

    def body(*refs):
        a_ref, b_ref = refs[0], refs[1]
        ex_refs = refs[2:2 + n_ex]
        pos = 2 + n_ex + (1 if into is not None else 0)
        out_refs = refs[pos:pos + n_out]
        acc_ref = refs[pos + n_out] if nk > 1 else None
        k = pl.program_id(2)
        av = a_ref[...].astype(BF16)
        bv = b_ref[...].astype(BF16)
        if mode == "nn":
            p = jnp.dot(av, bv, preferred_element_type=F32)
        elif mode == "nt":
            p = lax.dot_general(av, bv, (((1,), (1,)), ((), ())), preferred_element_type=F32)
        else:
            p = lax.dot_general(av, bv, (((0,), (0,)), ((), ())), preferred_element_type=F32)

        def finish(acc):
            vals = epilogue(acc, *[r[...] for r in ex_refs]) if epilogue is not None else (acc,)
            for r, v in zip(out_refs, vals):
                r[...] = v.astype(r.dtype)

        if nk == 1:
            finish(p)
        else:
            @pl.when(k == 0)
            def _():
                acc_ref[...] = p

            @pl.when(k > 0)
            def _():
                acc_ref[...] += p

            @pl.when(k == nk - 1)
            def _():
                finish(acc_ref[...])

    if mode == "tn":
        a_spec = pl.BlockSpec((tk, tm), lambda i, j, k: (k, i))
    else:
        a_spec = pl.BlockSpec((tm, tk), lambda i, j, k: (i, k))
    if mode == "nt":
        b_blk, b_idx = (tn, tk), (lambda i, j, k: (j, k))
    else:
        b_blk, b_idx = (tk, tn), (lambda i, j, k: (k, j))
    if stacked:
        b_spec = pl.BlockSpec((None,) + b_blk, lambda i, j, k, f=b_idx: (b_layer,) + f(i, j, k))
    else:
        b_spec = pl.BlockSpec(b_blk, b_idx)
    in_specs = [a_spec, b_spec]
    operands = [a, b]
    vmem = 2 * (tm * tk * a.dtype.itemsize + tk * tn * b.dtype.itemsize) + tm * tn * 4 * 3
    for arr, kind, off in extras:
        if kind == "tile":
            assert off % tn == 0, (name, off, tn)
            in_specs.append(pl.BlockSpec((tm, tn), lambda i, j, k, o=off // tn: (i, o + j)))
        else:
            in_specs.append(pl.BlockSpec((1, tn), lambda i, j, k: (0, j)))
        operands.append(arr)
        vmem += 2 * tm * tn * arr.dtype.itemsize
    aliases = {}
    if into is not None:
        assert n_out == 1
        in_specs.append(ANY)
        operands.append(into)
        aliases = {len(operands) - 1: 0}
        out_shape = [jax.ShapeDtypeStruct(into.shape, into.dtype)]
        out_specs = [pl.BlockSpec((None, tm, tn), lambda i, j, k: (into_layer, i, j))]
        vmem += 2 * tm * tn * into.dtype.itemsize
    else:
        out_shape = [jax.ShapeDtypeStruct((M, N), dt) for dt in out_dtypes]
        out_specs = [pl.BlockSpec((tm, tn), lambda i, j, k: (i, j)) for _ in out_dtypes]
        vmem += sum(2 * tm * tn * jnp.dtype(dt).itemsize for dt in out_dtypes)
    outs = pl.pallas_call(
        body, name=name, grid=(M // tm, N // tn, nk), in_specs=in_specs, out_specs=out_specs, out_shape=out_shape,
        scratch_shapes=[pltpu.VMEM((tm, tn), F32)] if nk > 1 else [], input_output_aliases=aliases,
        compiler_params=_params(3, vmem))(*operands)
    return outs[0] if n_out == 1 else outs


def _merge(ys, wbs, layer, proj, D, *, name):
    T = ys[0].shape[0]
    tm = _div(T, 1024)
    tn = HALF
    kw = ys[0].shape[1]
    g_off = [(OFF_G + b * D) // tn for b in range(3)]

    def body(y0, y1, y2, w0, w1, w2, g0, g1, g2, merged_ref, b0, b1, b2):
        acc = None
        for y, w, g, bo in ((y0, w0, g0, b0), (y1, w1, g1, b1), (y2, w2, g2, b2)):
            p = jnp.dot(y[...], w[...], preferred_element_type=F32)
            bo[...] = p.astype(bo.dtype)
            t = jax.nn.sigmoid(g[...].astype(F32)) * p
            acc = t if acc is None else acc + t
        merged_ref[...] = acc.astype(merged_ref.dtype)

    y_spec = pl.BlockSpec((tm, kw), lambda i, j: (i, 0))
    w_spec = pl.BlockSpec((None, kw, tn), lambda i, j: (layer, 0, j))
    g_specs = [pl.BlockSpec((tm, tn), lambda i, j, o=o: (i, o + j)) for o in g_off]
    o_spec = pl.BlockSpec((tm, tn), lambda i, j: (i, j))
    vmem = 2 * (3 * tm * kw * 2 + 3 * kw * tn * 2 + 3 * tm * tn * proj.dtype.itemsize + 4 * tm * tn * 2) + 4 * tm * tn * 4
    return pl.pallas_call(
        body, name=name, grid=(T // tm, D // tn), in_specs=[y_spec] * 3 + [w_spec] * 3 + g_specs,
        out_specs=[o_spec] * 4, out_shape=[jax.ShapeDtypeStruct((T, D), BF16)] * 4,
        compiler_params=_params(2, vmem))(*ys, *wbs, proj, proj, proj)


def _row_tile(T, width, n_arrays):
    cap = max(8, (24 * MIB) // (2 * n_arrays * width * 4))
    return _div(T, min(cap, 1024), 8)


def _norm_mod(x, ng, scale, shift, *, name):
    T, D = x.shape
    tm = _row_tile(T, D, 3)

    def body(x_ref, ng_ref, sc_ref, sh_ref, h_ref):
        xv = x_ref[...]
        r = lax.rsqrt(jnp.mean(xv * xv, axis=-1, keepdims=True) + EPS)
        h = (xv * r) * ng_ref[...] * (1.0 + sc_ref[...]) + sh_ref[...]
        h_ref[...] = h.astype(h_ref.dtype)

    row = pl.BlockSpec((1, D), lambda i: (0, 0))
    tile = pl.BlockSpec((tm, D), lambda i: (i, 0))
    return pl.pallas_call(body, name=name, grid=(T // tm,), in_specs=[tile, row, row, row], out_specs=tile,
                          out_shape=jax.ShapeDtypeStruct((T, D), BF16), compiler_params=_params(1))(x, ng, scale, shift)


def _norm_mod_bwd(x, dh, dx_out, ng, scale, *, name):
    T, D = x.shape
    tm = _row_tile(T, D, 6)

    def body(x_ref, dh_ref, dxo_ref, ng_ref, sc_ref, dx_ref, dng_ref, dsc_ref, dsh_ref):
        i = pl.program_id(0)
        xv = x_ref[...]
        dh_v = dh_ref[...].astype(F32)
        r = lax.rsqrt(jnp.mean(xv * xv, axis=-1, keepdims=True) + EPS)
        xn = xv * r
        one_sc = 1.0 + sc_ref[...]
        dxn = dh_v * (ng_ref[...] * one_sc)
        dx_ref[...] = dxo_ref[...] + r * (dxn - xn * jnp.mean(dxn * xn, axis=-1, keepdims=True))
        t = dh_v * xn
        parts = (jnp.sum(t * one_sc, axis=0, keepdims=True), jnp.sum(t * ng_ref[...], axis=0, keepdims=True),
                 jnp.sum(dh_v, axis=0, keepdims=True))
        for ref, p in zip((dng_ref, dsc_ref, dsh_ref), parts):
            @pl.when(i == 0)
            def _(ref=ref, p=p):
                ref[...] = p

            @pl.when(i > 0)
            def _(ref=ref, p=p):
                ref[...] += p

    row = pl.BlockSpec((1, D), lambda i: (0, 0))
    tile = pl.BlockSpec((tm, D), lambda i: (i, 0))
    vec = jax.ShapeDtypeStruct((1, D), F32)
    return pl.pallas_call(body, name=name, grid=(T // tm,), in_specs=[tile, tile, tile, row, row],
                          out_specs=[tile, row, row, row], out_shape=[jax.ShapeDtypeStruct((T, D), F32), vec, vec, vec],
                          compiler_params=_params(1))(x, dh, dx_out, ng, scale)


def _final_loss(x, target, fg, *, name):
    T, D = x.shape
    tm = _row_tile(T, D, 4)

    def body(x_ref, t_ref, g_ref, loss_ref, dx_ref, dg_ref):
        i = pl.program_id(0)
        xv = x_ref[...]
        r = lax.rsqrt(jnp.mean(xv * xv, axis=-1, keepdims=True) + EPS)
        xn = xv * r
        err = xn * g_ref[...] - t_ref[...]
        part = 0.5 * jnp.sum(jnp.sum(err * err, axis=1, keepdims=True), axis=0, keepdims=True) / D
        dy = err / D
        dxn = dy * g_ref[...]
        dx_ref[...] = r * (dxn - xn * jnp.mean(dxn * xn, axis=-1, keepdims=True))
        dg = jnp.sum(dy * xn, axis=0, keepdims=True)

        @pl.when(i == 0)
        def _():
            loss_ref[...] = part
            dg_ref[...] = dg

        @pl.when(i > 0)
        def _():
            loss_ref[...] += part
            dg_ref[...] += dg

    row = pl.BlockSpec((1, D), lambda i: (0, 0))
    tile = pl.BlockSpec((tm, D), lambda i: (i, 0))
    one = pl.BlockSpec((1, 1), lambda i: (0, 0))
    return pl.pallas_call(
        body, name=name, grid=(T // tm,), in_specs=[tile, tile, row], out_specs=[one, tile, row],
        out_shape=[jax.ShapeDtypeStruct((1, 1), F32), jax.ShapeDtypeStruct((T, D), F32), jax.ShapeDtypeStruct((1, D), F32)],
        compiler_params=_params(1))(x, target, fg)


def _gate_out_bwd(dx_out, o, gate, *, name):
    T, D = dx_out.shape
    tm = _row_tile(T, D, 3)

    def body(dx_ref, o_ref, g_ref, dmo_ref, dg_ref):
        i = pl.program_id(0)
        dxv = dx_ref[...]
        dmo_ref[...] = (dxv * g_ref[...]).astype(dmo_ref.dtype)
        p = jnp.sum(dxv * o_ref[...].astype(F32), axis=0, keepdims=True)

        @pl.when(i == 0)
        def _():
            dg_ref[...] = p

        @pl.when(i > 0)
        def _():
            dg_ref[...] += p

    row = pl.BlockSpec((1, D), lambda i: (0, 0))
    tile = pl.BlockSpec((tm, D), lambda i: (i, 0))
    return pl.pallas_call(body, name=name, grid=(T // tm,), in_specs=[tile, tile, row], out_specs=[tile, row],
                          out_shape=[jax.ShapeDtypeStruct((T, D), BF16), jax.ShapeDtypeStruct((1, D), F32)],
                          compiler_params=_params(1))(dx_out, o, gate)


def _merge_bwd(dmerged, branches, proj, D, *, name):
    T = dmerged.shape[0]
    tm = _div(T, 1024)
    tn = HALF
    g_off = [(OFF_G + b * D) // tn for b in range(3)]

    def body(dm_ref, b0, b1, b2, g0, g1, g2, db0, db1, db2, dg0, dg1, dg2):
        dm = dm_ref[...].astype(F32)
        for b, g, db, dg in ((b0, g0, db0, dg0), (b1, g1, db1, dg1), (b2, g2, db2, dg2)):
            s = jax.nn.sigmoid(g[...].astype(F32))
            db[...] = (dm * s).astype(db.dtype)
            dg[...] = (dm * b[...].astype(F32) * s * (1.0 - s)).astype(dg.dtype)

    tile = pl.BlockSpec((tm, tn), lambda i, j: (i, j))
    g_specs = [pl.BlockSpec((tm, tn), lambda i, j, o=o: (i, o + j)) for o in g_off]
    return pl.pallas_call(body, name=name, grid=(T // tm, D // tn), in_specs=[tile] * 4 + g_specs, out_specs=[tile] * 6,
                          out_shape=[jax.ShapeDtypeStruct((T, D), BF16)] * 6,
                          compiler_params=_params(2))(dmerged, *branches, proj, proj, proj)


def _conv_out_bwd(dy, cpre, proj, *, name):
    T = dy.shape[0]
    tm = _div(T, 1024)
    tn = HALF

    def body(dy_ref, c_ref, z_ref, dc_ref, dz_ref):
        dyv = dy_ref[...].astype(F32)
        z = z_ref[...].astype(F32)
        dc_ref[...] = (dyv * _silu(z)).astype(dc_ref.dtype)
        dz_ref[...] = (dyv * c_ref[...].astype(F32) * _dsilu(z)).astype(dz_ref.dtype)

    tile = pl.BlockSpec((tm, tn), lambda i, j: (i, j))
    z_spec = pl.BlockSpec((tm, tn), lambda i, j: (i, OFF_CZ // tn + j))
    return pl.pallas_call(body, name=name, grid=(T // tm, CONV_WIDTH // tn), in_specs=[tile, tile, z_spec],
                          out_specs=[tile, tile], out_shape=[jax.ShapeDtypeStruct((T, CONV_WIDTH), BF16)] * 2,
                          compiler_params=_params(2))(dy, cpre, proj)


def _pool_mixed(ext, u, g, row0):
    w = POOL_WINDOWS[g]
    s = ext
    shift = 1
    while shift < w:
        s = s + pltpu.roll(s, shift, 0)
        shift *= 2
    tm = u.shape[0]
    t = row0 + lax.broadcasted_iota(jnp.int32, (tm, 1), 0)
    inv = 1.0 / jnp.minimum(t + 1, w).astype(F32)
    return s[POOL_HALO:, :] * inv - u, inv


def _pool_specs(T, tm):
    per = tm // POOL_HALO
    cur = lambda col: pl.BlockSpec((tm, POOL_WIDTH), lambda i, c=col: (i, c))
    prev = pl.BlockSpec((POOL_HALO, POOL_WIDTH), lambda i: (jnp.maximum(i * per - 1, 0), 0))
    return cur, prev


def _pool_fwd(proj, pool_w, layer, scale, *, name):
    T = proj.shape[0]
    tm = _div(T, 512, 16)
    cur, prev = _pool_specs(T, tm)

    def body(u_ref, up_ref, z_ref, w_ref, sc_ref, y_ref):
        i = pl.program_id(0)
        u = u_ref[...].astype(F32)
        halo = jnp.where(i == 0, 0.0, up_ref[...].astype(F32))
        ext = jnp.concatenate([halo, u], axis=0)
        for g in range(len(POOL_WINDOWS)):
            cols = slice(g * POOL_GROUP, (g + 1) * POOL_GROUP)
            mixed, _ = _pool_mixed(ext[:, cols], u[:, cols], g, i * tm)
            p = jnp.dot(mixed.astype(BF16), w_ref[g], preferred_element_type=F32)
            y = p * sc_ref[:, cols] * _silu(z_ref[:, cols].astype(F32))
            y_ref[:, cols] = y.astype(y_ref.dtype)

    w_spec = pl.BlockSpec((None, len(POOL_WINDOWS), POOL_GROUP, POOL_GROUP), lambda i: (layer, 0, 0, 0))
    row = pl.BlockSpec((1, POOL_WIDTH), lambda i: (0, 0))
    return pl.pallas_call(body, name=name, grid=(T // tm,), in_specs=[cur(0), prev, cur(1), w_spec, row],
                          out_specs=pl.BlockSpec((tm, POOL_WIDTH), lambda i: (i, 0)),
                          out_shape=jax.ShapeDtypeStruct((T, POOL_WIDTH), BF16),
                          compiler_params=_params(1))(proj, proj, proj, pool_w, scale)


def _pool_bwd(proj, dy, pool_w, layer, scale, *, name):
    T = proj.shape[0]
    tm = _div(T, 512, 16)
    cur, prev = _pool_specs(T, tm)
    n_g = len(POOL_WINDOWS)

    def body(u_ref, up_ref, z_ref, dy_ref, w_ref, sc_ref, dz_ref, dmn_ref, dsc_ref, dw_ref):
        i = pl.program_id(0)
        u = u_ref[...].astype(F32)
        halo = jnp.where(i == 0, 0.0, up_ref[...].astype(F32))
        ext = jnp.concatenate([halo, u], axis=0)
        for g in range(n_g):
            cols = slice(g * POOL_GROUP, (g + 1) * POOL_GROUP)
            mixed, inv = _pool_mixed(ext[:, cols], u[:, cols], g, i * tm)
            mixed = mixed.astype(BF16)
            w = w_ref[g]
            p = jnp.dot(mixed, w, preferred_element_type=F32)
            z = z_ref[:, cols].astype(F32)
            dyv = dy_ref[:, cols].astype(F32)
            sc = sc_ref[:, cols]
            dypre = dyv * _silu(z)
            dz_ref[:, cols] = (dyv * (p * sc) * _dsilu(z)).astype(dz_ref.dtype)
            dsc = jnp.sum(dypre * p, axis=0, keepdims=True)
            dp = (dypre * sc).astype(BF16)
            dwg = lax.dot_general(mixed, dp, (((0,), (0,)), ((), ())), preferred_element_type=F32)
            dmixed = lax.dot_general(dp, w, (((1,), (1,)), ((), ())), preferred_element_type=F32)
            dmn_ref[:, cols] = dmixed * inv

            @pl.when(i == 0)
            def _(g=g, cols=cols, dsc=dsc, dwg=dwg):
                dsc_ref[:, cols] = dsc
                dw_ref[g] = dwg

            @pl.when(i > 0)
            def _(g=g, cols=cols, dsc=dsc, dwg=dwg):
                dsc_ref[:, cols] += dsc
                dw_ref[g] += dwg

    w_spec = pl.BlockSpec((None, n_g, POOL_GROUP, POOL_GROUP), lambda i: (layer, 0, 0, 0))
    row = pl.BlockSpec((1, POOL_WIDTH), lambda i: (0, 0))
    tile = pl.BlockSpec((tm, POOL_WIDTH), lambda i: (i, 0))
    dw_spec = pl.BlockSpec((n_g, POOL_GROUP, POOL_GROUP), lambda i: (0, 0, 0))
    return pl.pallas_call(
        body, name=name, grid=(T // tm,), in_specs=[cur(0), prev, cur(1), tile, w_spec, row],
        out_specs=[tile, tile, row, dw_spec],
        out_shape=[jax.ShapeDtypeStruct((T, POOL_WIDTH), BF16), jax.ShapeDtypeStruct((T, POOL_WIDTH), F32),
                   jax.ShapeDtypeStruct((1, POOL_WIDTH), F32), jax.ShapeDtypeStruct((n_g, POOL_GROUP, POOL_GROUP), F32)],
        compiler_params=_params(1))(proj, proj, proj, dy, pool_w, scale)


def _pool_bwd_window(dmn, *, name):
    T = dmn.shape[0]
    tm = _div(T, 512, 16)
    per = tm // POOL_HALO
    last = T // POOL_HALO - 1
    nb = T // tm

    def body(c_ref, n_ref, du_ref):
        i = pl.program_id(0)
        cur = c_ref[...]
        nxt = jnp.where(i == nb - 1, 0.0, n_ref[...])
        ext = jnp.concatenate([cur, nxt], axis=0)
        rows = tm + POOL_HALO
        t = i * tm + lax.broadcasted_iota(jnp.int32, (tm, 1), 0)
        for g, w in enumerate(POOL_WINDOWS):
            cols = slice(g * POOL_GROUP, (g + 1) * POOL_GROUP)
            s = ext[:, cols]
            shift = 1
            while shift < w:
                s = s + pltpu.roll(s, rows - shift, 0)
                shift *= 2
            cnt = jnp.minimum(t + 1, w).astype(F32)
            du_ref[:, cols] = (s[:tm, :] - cur[:, cols] * cnt).astype(du_ref.dtype)

    tile = pl.BlockSpec((tm, POOL_WIDTH), lambda i: (i, 0))
    nxt = pl.BlockSpec((POOL_HALO, POOL_WIDTH), lambda i: (jnp.minimum((i + 1) * per, last), 0))
    return pl.pallas_call(body, name=name, grid=(nb,), in_specs=[tile, nxt], out_specs=tile,
                          out_shape=jax.ShapeDtypeStruct((T, POOL_WIDTH), BF16), compiler_params=_params(1))(dmn, dmn)


def _attn_mask(i):
    qb, keys = ATTN_Q_BLOCK, ATTN_Q_BLOCK + ATTN_HALO
    qi = lax.broadcasted_iota(jnp.int32, (qb, keys), 0) // CHUNK
    kj = lax.broadcasted_iota(jnp.int32, (qb, keys), 1) // CHUNK - WINDOW_CHUNKS
    return (kj <= qi) & (kj >= qi - WINDOW_CHUNKS) & (kj + i * (qb // CHUNK) >= 0)


def _attn_specs(T, order):
    qb = ATTN_Q_BLOCK
    per = qb // ATTN_HALO
    cur = lambda width, col: pl.BlockSpec((qb, width), lambda i, c=col: (order(i), c))
    prev = lambda col: pl.BlockSpec((ATTN_HALO, KV_WIDTH), lambda i, c=col: (jnp.maximum(order(i) * per - 1, 0), c))
    return cur, prev


def _attn_fwd(proj, sink, *, name):
    T = proj.shape[0]
    qb = ATTN_Q_BLOCK
    cur, prev = _attn_specs(T, lambda i: i)

    def body(sink_ref, q_ref, kc_ref, kp_ref, vc_ref, vp_ref, z0_ref, z1_ref, o_ref, y_ref, lse_ref):
        i = pl.program_id(0)
        q = q_ref[...].astype(BF16)
        kk = jnp.concatenate([kp_ref[...], kc_ref[...]], axis=0).astype(BF16)
        vv = jnp.concatenate([vp_ref[...], vc_ref[...]], axis=0).astype(BF16)
        mask = _attn_mask(i)
        lane = lax.broadcasted_iota(jnp.int32, (qb, 128), 1)
        lse = jnp.zeros((qb, 128), F32)
        for h in range(N_Q_HEADS):
            hs = slice(h * HEAD_DIM, (h + 1) * HEAD_DIM)
            ks = slice((h // Q_PER_KV) * HEAD_DIM, (h // Q_PER_KV + 1) * HEAD_DIM)
            s = lax.dot_general(q[:, hs], kk[:, ks], (((1,), (1,)), ((), ())), preferred_element_type=F32)
            s = jnp.where(mask, s * (HEAD_DIM ** -0.5), NEG_INF)
            sk = sink_ref[h]
            m = jnp.maximum(jnp.max(s, axis=1, keepdims=True), sk)
            p = jnp.exp(s - m)
            den = jnp.sum(p, axis=1, keepdims=True) + jnp.exp(sk - m)
            oh = jnp.dot(p.astype(BF16), vv[:, ks], preferred_element_type=F32) / den
            zr = z0_ref if h < N_Q_HEADS // 2 else z1_ref
            zs = slice((h % (N_Q_HEADS // 2)) * HEAD_DIM, (h % (N_Q_HEADS // 2) + 1) * HEAD_DIM)
            o_ref[:, hs] = oh.astype(o_ref.dtype)
            y_ref[:, hs] = (oh * _silu(zr[:, zs].astype(F32))).astype(y_ref.dtype)
            lse = jnp.where(lane == h, m + jnp.log(den), lse)
        lse_ref[...] = lse

    kcol, vcol = OFF_K // KV_WIDTH, OFF_V // KV_WIDTH
    tile = pl.BlockSpec((qb, ATTN_WIDTH), lambda i: (i, 0))
    in_specs = [pl.BlockSpec(memory_space=pltpu.SMEM), cur(ATTN_WIDTH, OFF_Q // ATTN_WIDTH), cur(KV_WIDTH, kcol), prev(kcol),
                cur(KV_WIDTH, vcol), prev(vcol), cur(HALF, OFF_AZ // HALF), cur(HALF, OFF_AZ // HALF + 1)]
    return pl.pallas_call(
        body, name=name, grid=(T // qb,), in_specs=in_specs,
        out_specs=[tile, tile, pl.BlockSpec((qb, 128), lambda i: (i, 0))],
        out_shape=[jax.ShapeDtypeStruct((T, ATTN_WIDTH), BF16), jax.ShapeDtypeStruct((T, ATTN_WIDTH), BF16),
                   jax.ShapeDtypeStruct((T, 128), F32)],
        compiler_params=_params(1))(sink, *([proj] * 7))


def _attn_bwd(proj, sink, o, lse, dy, *, name):
    T = proj.shape[0]
    qb = ATTN_Q_BLOCK
    nb = T // qb
    order = lambda i: nb - 1 - i
    cur, prev = _attn_specs(T, order)

    def body(sink_ref, q_ref, kc_ref, kp_ref, vc_ref, vp_ref, z0_ref, z1_ref, o_ref, lse_ref, dy_ref,
             dq_ref, dk_ref, dv_ref, dz_ref, dsink_ref, dk_carry, dv_carry):
        i = pl.program_id(0)
        blk = order(i)
        q = q_ref[...].astype(BF16)
        kk = jnp.concatenate([kp_ref[...], kc_ref[...]], axis=0).astype(BF16)
        vv = jnp.concatenate([vp_ref[...], vc_ref[...]], axis=0).astype(BF16)
        mask = _attn_mask(blk)
        lane = lax.broadcasted_iota(jnp.int32, (1, 128), 1)
        dsink = jnp.zeros((1, 128), F32)
        scale = HEAD_DIM ** -0.5
        for kv in range(N_KV_HEADS):
            ks = slice(kv * HEAD_DIM, (kv + 1) * HEAD_DIM)
            dk_acc = jnp.zeros((qb + ATTN_HALO, HEAD_DIM), F32)
            dv_acc = jnp.zeros((qb + ATTN_HALO, HEAD_DIM), F32)
            for h in range(kv * Q_PER_KV, (kv + 1) * Q_PER_KV):
                hs = slice(h * HEAD_DIM, (h + 1) * HEAD_DIM)
                zr = z0_ref if h < N_Q_HEADS // 2 else z1_ref
                zs = slice((h % (N_Q_HEADS // 2)) * HEAD_DIM, (h % (N_Q_HEADS // 2) + 1) * HEAD_DIM)
                z = zr[:, zs].astype(F32)
                dyh = dy_ref[:, hs].astype(F32)
                oh = o_ref[:, hs].astype(F32)
                do = dyh * _silu(z)
                dz_ref[:, hs] = (dyh * oh * _dsilu(z)).astype(dz_ref.dtype)
                drow = jnp.sum(do * oh, axis=1, keepdims=True)
                lse_h = lse_ref[:, h:h + 1]
                s = lax.dot_general(q[:, hs], kk[:, ks], (((1,), (1,)), ((), ())), preferred_element_type=F32)
                p = jnp.exp(jnp.where(mask, s * scale, NEG_INF) - lse_h)
                do_b = do.astype(BF16)
                dv_acc = dv_acc + lax.dot_general(p.astype(BF16), do_b, (((0,), (0,)), ((), ())),
                                                  preferred_element_type=F32)
                dp = lax.dot_general(do_b, vv[:, ks], (((1,), (1,)), ((), ())), preferred_element_type=F32)
                ds = (p * (dp - drow)).astype(BF16)
                dq_ref[:, hs] = (jnp.dot(ds, kk[:, ks], preferred_element_type=F32) * scale).astype(dq_ref.dtype)
                dk_acc = dk_acc + lax.dot_general(ds, q[:, hs], (((0,), (0,)), ((), ())),
                                                  preferred_element_type=F32) * scale
                p_sink = jnp.exp(sink_ref[h] - lse_h)
                dsink = jnp.where(lane == h, -jnp.sum(p_sink * drow, axis=0, keepdims=True), dsink)
            for acc, carry, out in ((dk_acc, dk_carry, dk_ref), (dv_acc, dv_carry, dv_ref)):
                tail = acc[qb:, :] + jnp.where(i == 0, 0.0, carry[:, ks])
                out[:, ks] = jnp.concatenate([acc[ATTN_HALO:qb, :], tail], axis=0).astype(out.dtype)
                carry[:, ks] = acc[:ATTN_HALO, :]

        @pl.when(i == 0)
        def _():
            dsink_ref[...] = dsink

        @pl.when(i > 0)
        def _():
            dsink_ref[...] += dsink

    kcol, vcol = OFF_K // KV_WIDTH, OFF_V // KV_WIDTH
    tile = pl.BlockSpec((qb, ATTN_WIDTH), lambda i: (order(i), 0))
    kv_tile = pl.BlockSpec((qb, KV_WIDTH), lambda i: (order(i), 0))
    lse_spec = pl.BlockSpec((qb, 128), lambda i: (order(i), 0))
    in_specs = [pl.BlockSpec(memory_space=pltpu.SMEM), cur(ATTN_WIDTH, OFF_Q // ATTN_WIDTH), cur(KV_WIDTH, kcol), prev(kcol),
                cur(KV_WIDTH, vcol), prev(vcol), cur(HALF, OFF_AZ // HALF), cur(HALF, OFF_AZ // HALF + 1),
                tile, lse_spec, tile]
    return pl.pallas_call(
        body, name=name, grid=(nb,), in_specs=in_specs,
        out_specs=[tile, kv_tile, kv_tile, tile, pl.BlockSpec((1, 128), lambda i: (0, 0))],
        out_shape=[jax.ShapeDtypeStruct((T, ATTN_WIDTH), BF16), jax.ShapeDtypeStruct((T, KV_WIDTH), BF16),
                   jax.ShapeDtypeStruct((T, KV_WIDTH), BF16), jax.ShapeDtypeStruct((T, ATTN_WIDTH), BF16),
                   jax.ShapeDtypeStruct((1, 128), F32)],
        scratch_shapes=[pltpu.VMEM((ATTN_HALO, KV_WIDTH), F32), pltpu.VMEM((ATTN_HALO, KV_WIDTH), F32)],
        compiler_params=_params(1))(sink, *([proj] * 7), o, lse, dy)


def _conv_specs(T, tm):
    per = tm // CONV_HALO
    ca, cb = OFF_CA // HALF, OFF_CB // HALF
    cur = lambda col: pl.BlockSpec((tm, HALF), lambda i, c=col: (i, c))
    prev = lambda col: pl.BlockSpec((CONV_HALO, HALF), lambda i, c=col: (jnp.maximum(i * per - 1, 0), c))
    return [cur(ca), cur(ca + 1), cur(cb), cur(cb + 1), prev(ca), prev(ca + 1), prev(cb), prev(cb + 1)]


def _conv_glu_ext(refs, i, ext_ref):
    a0, a1, b0, b1, pa0, pa1, pb0, pb1 = refs
    a = jnp.concatenate([a0[...], a1[...]], axis=1).astype(F32)
    sb = jax.nn.sigmoid(jnp.concatenate([b0[...], b1[...]], axis=1).astype(F32))
    pa = jnp.concatenate([pa0[...], pa1[...]], axis=1).astype(F32)
    pb = jnp.concatenate([pb0[...], pb1[...]], axis=1).astype(F32)
    ext_ref[:CONV_HALO, :] = jnp.where(i == 0, 0.0, pa * jax.nn.sigmoid(pb))
    ext_ref[CONV_HALO:, :] = a * sb
    return a, sb


def _conv_dw(ext_ref, dw_ref, tm):
    y = None
    for j in range(CONV_KERNEL):
        start = CONV_HALO - (CONV_KERNEL - 1) + j
        t = dw_ref[j:j + 1, :] * ext_ref[start:start + tm, :]
        y = t if y is None else y + t
    return y


def _conv_fwd(proj, dw, dwb, lng, lnb, *, name):
    T = proj.shape[0]
    tm = _div(T, 256, 32)

    def body(*refs):
        dw_ref, dwb_ref, g_ref, b_ref, s_ref, ext_ref = refs[8:]
        i = pl.program_id(0)
        _conv_glu_ext(refs[:8], i, ext_ref)
        yc = _conv_dw(ext_ref, dw_ref, tm) + dwb_ref[...]
        mu = jnp.mean(yc, axis=-1, keepdims=True)
        d = yc - mu
        rstd = lax.rsqrt(jnp.mean(d * d, axis=-1, keepdims=True) + EPS)
        s_ref[...] = _silu(d * rstd * g_ref[...] + b_ref[...]).astype(s_ref.dtype)

    row = pl.BlockSpec((1, CONV_WIDTH), lambda i: (0, 0))
    taps = pl.BlockSpec((CONV_KERNEL, CONV_WIDTH), lambda i: (0, 0))
    return pl.pallas_call(
        body, name=name, grid=(T // tm,), in_specs=_conv_specs(T, tm) + [taps, row, row, row],
        out_specs=pl.BlockSpec((tm, CONV_WIDTH), lambda i: (i, 0)), out_shape=jax.ShapeDtypeStruct((T, CONV_WIDTH), BF16),
        scratch_shapes=[pltpu.VMEM((tm + CONV_HALO, CONV_WIDTH), F32)],
        compiler_params=_params(1))(*([proj] * 8), dw, dwb, lng, lnb)


def _conv_bwd(proj, ds, dw, dwb, lng, lnb, *, name):
    T = proj.shape[0]
    tm = _div(T, 256, 32)

    def body(*refs):
        ds_ref, dw_ref, dwb_ref, g_ref, b_ref, dyc_ref, ddw_ref, ddwb_ref, dg_ref, db_ref, ext_ref = refs[8:]
        i = pl.program_id(0)
        _conv_glu_ext(refs[:8], i, ext_ref)
        yc = _conv_dw(ext_ref, dw_ref, tm) + dwb_ref[...]
        mu = jnp.mean(yc, axis=-1, keepdims=True)
        d = yc - mu
        rstd = lax.rsqrt(jnp.mean(d * d, axis=-1, keepdims=True) + EPS)
        xhat = d * rstd
        dln = ds_ref[...].astype(F32) * _dsilu(xhat * g_ref[...] + b_ref[...])
        dxhat = dln * g_ref[...]
        dyc = rstd * (dxhat - jnp.mean(dxhat, axis=-1, keepdims=True)
                      - xhat * jnp.mean(dxhat * xhat, axis=-1, keepdims=True))
        dyc_ref[...] = dyc
        first = i == 0

        def accumulate(ref, idx, val):
            @pl.when(first)
            def _():
                ref[idx] = val

            @pl.when(jnp.logical_not(first))
            def _():
                ref[idx] += val

        accumulate(dg_ref, slice(None), jnp.sum(dln * xhat, axis=0, keepdims=True))
        accumulate(db_ref, slice(None), jnp.sum(dln, axis=0, keepdims=True))
        accumulate(ddwb_ref, slice(None), jnp.sum(dyc, axis=0, keepdims=True))
        for j in range(CONV_KERNEL):
            start = CONV_HALO - (CONV_KERNEL - 1) + j
            accumulate(ddw_ref, slice(j, j + 1), jnp.sum(dyc * ext_ref[start:start + tm, :], axis=0, keepdims=True))

    row = pl.BlockSpec((1, CONV_WIDTH), lambda i: (0, 0))
    taps = pl.BlockSpec((CONV_KERNEL, CONV_WIDTH), lambda i: (0, 0))
    tile = pl.BlockSpec((tm, CONV_WIDTH), lambda i: (i, 0))
    vec = jax.ShapeDtypeStruct((1, CONV_WIDTH), F32)
    return pl.pallas_call(
        body, name=name, grid=(T // tm,), in_specs=_conv_specs(T, tm) + [tile, taps, row, row, row],
        out_specs=[tile, taps, row, row, row],
        out_shape=[jax.ShapeDtypeStruct((T, CONV_WIDTH), F32), jax.ShapeDtypeStruct((CONV_KERNEL, CONV_WIDTH), F32), vec, vec, vec],
        scratch_shapes=[pltpu.VMEM((tm + CONV_HALO, CONV_WIDTH), F32)],
        compiler_params=_params(1))(*([proj] * 8), ds, dw, dwb, lng, lnb)


def _conv_bwd_input(proj, dyc, dw, *, name):
    T = proj.shape[0]
    tm = _div(T, 256, 32)
    per = tm // CONV_HALO
    last = T // CONV_HALO - 1
    nb = T // tm
    ca, cb = OFF_CA // HALF, OFF_CB // HALF

    def body(a0, a1, b0, b1, c_ref, n_ref, dw_ref, da_ref, db_ref, ext_ref):
        i = pl.program_id(0)
        ext_ref[:tm, :] = c_ref[...]
        ext_ref[tm:, :] = jnp.where(i == nb - 1, 0.0, n_ref[...])
        dg = None
        for j in range(CONV_KERNEL):
            start = CONV_KERNEL - 1 - j
            t = dw_ref[j:j + 1, :] * ext_ref[start:start + tm, :]
            dg = t if dg is None else dg + t
        a = jnp.concatenate([a0[...], a1[...]], axis=1).astype(F32)
        sb = jax.nn.sigmoid(jnp.concatenate([b0[...], b1[...]], axis=1).astype(F32))
        da_ref[...] = (dg * sb).astype(da_ref.dtype)
        db_ref[...] = (dg * a * sb * (1.0 - sb)).astype(db_ref.dtype)

    cur = lambda col: pl.BlockSpec((tm, HALF), lambda i, c=col: (i, c))
    tile = pl.BlockSpec((tm, CONV_WIDTH), lambda i: (i, 0))
    nxt = pl.BlockSpec((CONV_HALO, CONV_WIDTH), lambda i: (jnp.minimum((i + 1) * per, last), 0))
    taps = pl.BlockSpec((CONV_KERNEL, CONV_WIDTH), lambda i: (0, 0))
    return pl.pallas_call(
        body, name=name, grid=(nb,), in_specs=[cur(ca), cur(ca + 1), cur(cb), cur(cb + 1), tile, nxt, taps],
        out_specs=[tile, tile], out_shape=[jax.ShapeDtypeStruct((T, CONV_WIDTH), BF16)] * 2,
        scratch_shapes=[pltpu.VMEM((tm + CONV_HALO, CONV_WIDTH), F32)],
        compiler_params=_params(1))(proj, proj, proj, proj, dyc, dyc, dw)


def _ada_mod(c_all, w_ada, b_slab, *, name):
    L, D, N = w_ada.shape
    tn = _div(N, 512)

    def body(c_ref, w_ref, b_ref, o_ref):
        ca = _silu(c_ref[...]).astype(BF16)
        o_ref[...] = jnp.dot(ca, w_ref[...].astype(BF16), preferred_element_type=F32) + b_ref[...]

    return pl.pallas_call(
        body, name=name, grid=(L, N // tn),
        in_specs=[pl.BlockSpec((N_DEV, D), lambda l, j: (0, 0)), pl.BlockSpec((None, D, tn), lambda l, j: (l, 0, j)),
                  pl.BlockSpec((None, 1, tn), lambda l, j: (l, 0, j))],
        out_specs=pl.BlockSpec((None, N_DEV, tn), lambda l, j: (l, 0, j)),
        out_shape=jax.ShapeDtypeStruct((L, N_DEV, N), F32), compiler_params=_params(2))(c_all, w_ada, b_slab)


def _ada_grad(c_all_t, dmod_slab, *, name):
    D = c_all_t.shape[0]
    L, _, N = dmod_slab.shape
    tm = _div(D, 512)
    tn = _div(N, 512)

    def body(c_ref, d_ref, o_ref):
        ca = _silu(c_ref[...]).astype(BF16).astype(F32)
        dm = d_ref[...].astype(BF16).astype(F32)
        acc = None
        for b in range(N_DEV):
            t = ca[:, b:b + 1] * dm[b:b + 1, :]
            acc = t if acc is None else acc + t
        o_ref[...] = acc

    return pl.pallas_call(
        body, name=name, grid=(L, D // tm, N // tn),
        in_specs=[pl.BlockSpec((tm, N_DEV), lambda l, i, j: (i, 0)), pl.BlockSpec((None, N_DEV, tn), lambda l, i, j: (l, 0, j))],
        out_specs=pl.BlockSpec((None, tm, tn), lambda l, i, j: (l, i, j)),
        out_shape=jax.ShapeDtypeStruct((L, D, N), F32), compiler_params=_params(3))(c_all_t, dmod_slab)


def _flat_tile(R, C, n_arrays):
    cap = max(8, (20 * MIB) // (2 * n_arrays * C * 4))
    return _div(R, cap, 8) if R % 8 == 0 else R


def _adamw_math(w, g, m, v):
    m = ADAM_B1 * m + (1.0 - ADAM_B1) * g
    v = ADAM_B2 * v + (1.0 - ADAM_B2) * (g * g)
    m_hat = m / (1.0 - ADAM_B1 ** ADAM_STEP)
    v_hat = v / (1.0 - ADAM_B2 ** ADAM_STEP)
    delta = -ADAM_LR * (m_hat / (jnp.sqrt(v_hat) + ADAM_EPS) + ADAM_WD * w)
    return delta, m, v


def _adamw(w, m, v, gs, *, name):
    R, C = w.shape
    n_g = len(gs)
    tr = _flat_tile(R, C, 7 + n_g)

    def body(*refs):
        w_ref, m_ref, v_ref = refs[:3]
        g_refs = refs[3:3 + n_g]
        go_ref, d_ref, mo_ref, vo_ref = refs[3 + n_g:]
        g = g_refs[0][...]
        for r in g_refs[1:]:
            g = g + r[...]
        d, mn, vn = _adamw_math(w_ref[...], g, m_ref[...], v_ref[...])
        go_ref[...] = g
        d_ref[...] = d
        mo_ref[...] = mn
        vo_ref[...] = vn

    tile = pl.BlockSpec((tr, C), lambda i: (i, 0))
    shp = jax.ShapeDtypeStruct((R, C), F32)
    return pl.pallas_call(body, name=name, grid=(R // tr,), in_specs=[tile] * (3 + n_g), out_specs=[tile] * 4,
                          out_shape=[shp] * 4, compiler_params=_params(1, 2 * (7 + n_g) * tr * C * 4))(w, m, v, *gs)


def _sum_landed(land, *, name):
    _, R, C = land.shape
    tr = _flat_tile(R, C, 4)

    def body(l_ref, o_ref):
        o_ref[...] = ((l_ref[3].astype(F32) + l_ref[0].astype(F32)) + l_ref[1].astype(F32)) + l_ref[2].astype(F32)

    return pl.pallas_call(body, name=name, grid=(R // tr,), in_specs=[pl.BlockSpec((4, tr, C), lambda i: (0, i, 0))],
                          out_specs=pl.BlockSpec((tr, C), lambda i: (i, 0)), out_shape=jax.ShapeDtypeStruct((R, C), F32),
                          compiler_params=_params(1, 2 * 4 * tr * C * 4))(land)


def _cast_bf16(w, *, name):
    R, C = w.shape
    tr = _flat_tile(R, C, 2)

    def body(w_ref, o_ref):
        o_ref[...] = w_ref[...].astype(BF16)

    tile = pl.BlockSpec((tr, C), lambda i: (i, 0))
    return pl.pallas_call(body, name=name, grid=(R // tr,), in_specs=[tile], out_specs=tile,
                          out_shape=jax.ShapeDtypeStruct((R, C), BF16), compiler_params=_params(1, 4 * tr * C * 4))(w)


def _place():
    x, y, c = lax.axis_index("x"), lax.axis_index("y"), lax.axis_index("c")
    chips = [(1 - x, y), (x, 1 - y), (1 - x, 1 - y)]
    return x, y, c, chips


def _small_exchange(v, reduce, *, name):
    m_per, n = v.shape
    assert m_per % 8 == 0 and n % 128 == 0

    def body(x_ref, out_ref, *scratch):
        if reduce:
            all_ref, send_sems, recv_sems, local_sem = scratch
        else:
            all_ref = out_ref
            send_sems, recv_sems, local_sem = scratch
        x, y, c, chips = _place()
        me, sibling = (x, y, c), (x, y, 1 - c)

        def rows(px, py, pc):
            return all_ref.at[pl.ds((4 * px + 2 * py + pc) * m_per, m_per), :]

        def copy(k, block, to, src=None):
            return pltpu.make_async_remote_copy(
                src_ref=rows(*block) if src is None else src, dst_ref=rows(*block), send_sem=send_sems.at[k],
                recv_sem=recv_sems.at[k], device_id=to, device_id_type=MESH)

        mine = pltpu.make_async_copy(x_ref, rows(*me), local_sem)
        mine.start()
        first = [copy(0, me, sibling, src=x_ref)]
        first += [copy(1 + j, me, (*chip, c), src=x_ref) for j, chip in enumerate(chips)]
        for cp in first:
            cp.start()
        passed = [copy(4 + j, (*chip, c), sibling) for j, chip in enumerate(chips)]
        for j, chip in enumerate(chips):
            copy(1 + j, (*chip, c), me).wait_recv()
            passed[j].start()
        copy(0, sibling, me).wait_recv()
        for j, chip in enumerate(chips):
            copy(4 + j, (*chip, 1 - c), me).wait_recv()
        for cp in first + passed:
            cp.wait_send()
        mine.wait()
        if reduce:
            acc = all_ref[0:m_per, :]
            for d in range(1, N_DEV):
                acc = acc + all_ref[d * m_per:(d + 1) * m_per, :]
            out_ref[...] = acc

    scratch = [pltpu.SemaphoreType.DMA((7,)), pltpu.SemaphoreType.DMA((7,)), pltpu.SemaphoreType.DMA]
    if reduce:
        scratch = [pltpu.VMEM((N_DEV * m_per, n), F32)] + scratch
    out_rows = m_per if reduce else N_DEV * m_per
    return pl.pallas_call(
        body, name=name, out_shape=jax.ShapeDtypeStruct((out_rows, n), v.dtype),
        in_specs=[pl.BlockSpec(memory_space=pltpu.VMEM)], out_specs=pl.BlockSpec(memory_space=pltpu.VMEM),
        scratch_shapes=scratch,
        compiler_params=pltpu.CompilerParams(vmem_limit_bytes=int(min(VMEM_CAP_BYTES, 4 * N_DEV * m_per * n * 4 + 16 * MIB))))(v)


def _slab(kind, ref, s):
    if kind == "cols":
        w = ref.shape[2] // N_CHIP
        return ref.at[:, :, pl.ds(s * w, w)]
    if kind == "rows":
        w = ref.shape[1] // N_CHIP
        return ref.at[:, pl.ds(s * w, w), :]
    w = ref.shape[2] // N_CHIP
    return ref.at[:, :, pl.ds(s * w, w), :]


def _slab_shape(kind, shape):
    if kind == "cols":
        return (shape[0], shape[1], shape[2] // N_CHIP)
    if kind == "rows":
        return (shape[0], shape[1] // N_CHIP, shape[2])
    return (shape[0], shape[1], shape[2] // N_CHIP, shape[3])


def _full_shape(kind, shape):
    if kind == "cols":
        return (shape[0], shape[1], shape[2] * N_CHIP)
    if kind == "rows":
        return (shape[0], shape[1] * N_CHIP, shape[2])
    return (shape[0], shape[1], shape[2] * N_CHIP, shape[3])


def _gather_weights(shards, kinds, *, name):
    n = len(shards)

    def body(*refs):
        sh, full = refs[:n], refs[n:2 * n]
        send_sems, recv_sems, local_sems = refs[2 * n:]
        x, y, c, chips = _place()
        s_me = 2 * x + y
        local, sent = [], []
        for a in range(n):
            mine = pltpu.make_async_copy(sh[a], _slab(kinds[a], full[a], s_me), local_sems.at[a])
            mine.start()
            local.append(mine)
            for j, (px, py) in enumerate(chips):
                cp = pltpu.make_async_remote_copy(
                    src_ref=sh[a], dst_ref=_slab(kinds[a], full[a], s_me), send_sem=send_sems.at[a * 3 + j],
                    recv_sem=recv_sems.at[a * 3 + j], device_id=(px, py, c), device_id_type=MESH)
                cp.start()
                sent.append(cp)
        for a in range(n):
            for j, (px, py) in enumerate(chips):
                pltpu.make_async_remote_copy(
                    src_ref=sh[a], dst_ref=_slab(kinds[a], full[a], 2 * px + py), send_sem=send_sems.at[a * 3 + j],
                    recv_sem=recv_sems.at[a * 3 + j], device_id=(px, py, c), device_id_type=MESH).wait_recv()
        for cp in sent:
            cp.wait_send()
        for cp in local:
            cp.wait()

    out_shape = [jax.ShapeDtypeStruct(_full_shape(k, s.shape), s.dtype) for k, s in zip(kinds, shards)]
    return pl.pallas_call(
        body, name=name, out_shape=out_shape, in_specs=[ANY] * n, out_specs=[ANY] * n,
        scratch_shapes=[pltpu.SemaphoreType.DMA((3 * n,)), pltpu.SemaphoreType.DMA((3 * n,)), pltpu.SemaphoreType.DMA((n,))],
    )(*shards)


def _scatter_grads(fulls, kinds, *, name):
    n = len(fulls)

    def body(*refs):
        full, land = refs[:n], refs[n:2 * n]
        send_sems, recv_sems, local_sems = refs[2 * n:]
        x, y, c, chips = _place()
        s_me = 2 * x + y
        local, sent = [], []
        for a in range(n):
            mine = pltpu.make_async_copy(_slab(kinds[a], full[a], s_me), land[a].at[3], local_sems.at[a])
            mine.start()
            local.append(mine)
            for j, (px, py) in enumerate(chips):
                cp = pltpu.make_async_remote_copy(
                    src_ref=_slab(kinds[a], full[a], 2 * px + py), dst_ref=land[a].at[j], send_sem=send_sems.at[a * 3 + j],
                    recv_sem=recv_sems.at[a * 3 + j], device_id=(px, py, c), device_id_type=MESH)
                cp.start()
                sent.append(cp)
        for a in range(n):
            for j, (px, py) in enumerate(chips):
                pltpu.make_async_remote_copy(
                    src_ref=_slab(kinds[a], full[a], s_me), dst_ref=land[a].at[j], send_sem=send_sems.at[a * 3 + j],
                    recv_sem=recv_sems.at[a * 3 + j], device_id=(px, py, c), device_id_type=MESH).wait_recv()
        for cp in sent:
            cp.wait_send()
        for cp in local:
            cp.wait()

    out_shape = [jax.ShapeDtypeStruct((4,) + _slab_shape(k, f.shape), f.dtype) for k, f in zip(kinds, fulls)]
    return pl.pallas_call(
        body, name=name, out_shape=out_shape, in_specs=[ANY] * n, out_specs=[ANY] * n,
        scratch_shapes=[pltpu.SemaphoreType.DMA((3 * n,)), pltpu.SemaphoreType.DMA((3 * n,)), pltpu.SemaphoreType.DMA((n,))],
    )(*fulls)


def _swap_with_sibling(parts, *, name):
    n = len(parts)

    def body(*refs):
        src, dst = refs[:n], refs[n:2 * n]
        send_sems, recv_sems = refs[2 * n:]
        x, y, c, _ = _place()
        cps = [pltpu.make_async_remote_copy(src_ref=src[a], dst_ref=dst[a], send_sem=send_sems.at[a], recv_sem=recv_sems.at[a],
                                            device_id=(x, y, 1 - c), device_id_type=MESH) for a in range(n)]
        for cp in cps:
            cp.start()
        for cp in cps:
            cp.wait_recv()
        for cp in cps:
            cp.wait_send()

    return pl.pallas_call(
        body, name=name, out_shape=[jax.ShapeDtypeStruct(p.shape, p.dtype) for p in parts], in_specs=[ANY] * n,
        out_specs=[ANY] * n, scratch_shapes=[pltpu.SemaphoreType.DMA((n,)), pltpu.SemaphoreType.DMA((n,))])(*parts)


def _pad_rows(v, rows):
    return jnp.pad(v, ((0, rows - v.shape[0]), (0, 0)))


def _pack(vectors):
    flat = jnp.concatenate([v.reshape(-1) for v in vectors])
    n = -(-flat.shape[0] // 1024) * 1024
    return jnp.pad(flat, (0, n - flat.shape[0])).reshape(8, n // 8)


def _unpack(block, shapes):
    flat = block.reshape(-1)
    out, pos = [], 0
    for shp in shapes:
        size = 1
        for d in shp:
            size *= d
        out.append(flat[pos:pos + size].reshape(shp))
        pos += size
    return out


def kernel(x, c, norm_g, w_ada, b_ada, w_in, pool_w, pool_scale, attn_sink, conv_dw, conv_dw_b, conv_ln_g, conv_ln_b, conv_pw, w_branch_pool, w_branch_attn, w_branch_conv, w_out, final_g, loss_target, m_norm_g, m_w_ada, m_b_ada, m_w_in, m_pool_w, m_pool_scale, m_attn_sink, m_conv_dw, m_conv_dw_b, m_conv_ln_g, m_conv_ln_b, m_conv_pw, m_w_branch_pool, m_w_branch_attn, m_w_branch_conv, m_w_out, m_final_g, v_norm_g, v_w_ada, v_b_ada, v_w_in, v_pool_w, v_pool_scale, v_attn_sink, v_conv_dw, v_conv_dw_b, v_conv_ln_g, v_conv_ln_b, v_conv_pw, v_w_branch_pool, v_w_branch_attn, v_w_branch_conv, v_w_out, v_final_g):
    _, T, D = x.shape
    L = norm_g.shape[0]
    IN = w_in.shape[2] * N_CHIP
    assert IN == OFF_G + 3 * D and D % HALF == 0 and T % 512 == 0
    xi, yi, ci = lax.axis_index("x"), lax.axis_index("y"), lax.axis_index("c")
    chip = 2 * xi + yi
    dev = 2 * chip + ci
    x0 = x.reshape(T, D)
    target = loss_target.reshape(T, D)

    big = [("cols", w_in, m_w_in, v_w_in), ("cols", w_branch_pool, m_w_branch_pool, v_w_branch_pool),
           ("cols", w_branch_attn, m_w_branch_attn, v_w_branch_attn), ("cols", w_branch_conv, m_w_branch_conv, v_w_branch_conv),
           ("rows", w_out, m_w_out, v_w_out), ("rows", conv_pw, m_conv_pw, v_conv_pw), ("pool", pool_w, m_pool_w, v_pool_w)]
    kinds = [b[0] for b in big]
    shards16 = [_cast_bf16(b[1].reshape(-1, b[1].shape[-1]), name=f"cast{a}").reshape(b[1].shape) for a, b in enumerate(big)]
    win_f, wbp_f, wba_f, wbc_f, wout_f, cpw_f, poolw_f = _gather_weights(shards16, kinds, name="gather_weights")

    c_all = _small_exchange(_pad_rows(c, 8), False, name="gather_c")[0::8]
    taps_rows = -(-(L * CONV_KERNEL) // 8) * 8
    dw_blocks = _small_exchange(_pad_rows(conv_dw.reshape(L * CONV_KERNEL, -1), taps_rows), False, name="gather_taps")
    dw_blocks = dw_blocks.reshape(N_CHIP, 2, taps_rows, -1)[:, 0, :L * CONV_KERNEL]
    conv_dw_full = dw_blocks.reshape(N_CHIP, L, CONV_KERNEL, -1).transpose(1, 2, 0, 3).reshape(L, CONV_KERNEL, CONV_WIDTH)
    n_ada = w_ada.shape[2]
    b_slab = lax.dynamic_slice_in_dim(b_ada, chip * n_ada, n_ada, axis=1).reshape(L, 1, n_ada)
    mod_part = _ada_mod(c_all, w_ada, b_slab, name="ada_mod")
    mod_blocks = _small_exchange(mod_part.reshape(L * N_DEV, n_ada), False, name="gather_mod")
    mod_blocks = mod_blocks.reshape(N_CHIP, 2, L, N_DEV, n_ada)[:, 0]
    mod_all = mod_blocks.transpose(1, 2, 0, 3).reshape(L, N_DEV, 3 * D)
    mod = lax.dynamic_index_in_dim(mod_all, dev, axis=1, keepdims=False)
    shift, scale, gate = mod[:, :D], mod[:, D:2 * D], mod[:, 2 * D:]

    row = lambda v: v.reshape(1, -1)

    xs, saved = [x0], []
    xl = x0
    for l in range(L):
        h = _norm_mod(xl, row(norm_g[l]), row(scale[l]), row(shift[l]), name=f"norm{l}")
        proj = _mm(h, win_f, "nn", [F32], name=f"proj{l}", b_layer=l)
        y_pool = _pool_fwd(proj, poolw_f, l, row(pool_scale[l]), name=f"pool{l}")
        o_attn, y_attn, lse = _attn_fwd(proj, attn_sink[l], name=f"attn{l}")
        s_conv = _conv_fwd(proj, conv_dw_full[l], row(conv_dw_b[l]), row(conv_ln_g[l]), row(conv_ln_b[l]), name=f"conv{l}")
        cpre, y_conv = _mm(s_conv, cpw_f, "nn", [BF16, BF16], name=f"conv_pw{l}", b_layer=l, tn_cap=HALF,
                           extras=[(proj, "tile", OFF_CZ)], epilogue=lambda acc, z: (acc, acc * _silu(z)))
        merged, bp, ba, bc = _merge((y_pool, y_attn, y_conv), (wbp_f, wba_f, wbc_f), l, proj, D, name=f"merge{l}")
        x_new, o = _mm(merged, wout_f, "nn", [F32, BF16], name=f"out{l}", b_layer=l,
                       extras=[(xl, "tile", 0), (row(gate[l]), "row", 0)],
                       epilogue=lambda acc, xv, g: (xv + g * acc, acc))
        saved.append(dict(h=h, proj=proj, y_pool=y_pool, o_attn=o_attn, y_attn=y_attn, lse=lse, s_conv=s_conv, cpre=cpre,
                          y_conv=y_conv, merged=merged, bp=bp, ba=ba, bc=bc, o=o))
        xl = x_new
        xs.append(xl)

    loss_part, dx, d_final_g = _final_loss(xl, target, row(final_g), name="final_loss")
    loss = lax.psum(loss_part[0, 0], ("x", "y", "c"))

    g_shapes = [(L, D, IN), (L, POOL_WIDTH, D), (L, ATTN_WIDTH, D), (L, CONV_WIDTH, D), (L, D, D),
                (L, CONV_WIDTH, CONV_WIDTH), (L, len(POOL_WINDOWS), POOL_GROUP, POOL_GROUP)]
    g_win, g_wbp, g_wba, g_wbc, g_wout, g_cpw = [lax.empty(s, BF16) for s in g_shapes[:6]]
    g_poolw, small, dmods = [], [], []
    for l in reversed(range(L)):
        sv = saved[l]
        proj = sv["proj"]
        dmo, d_gate = _gate_out_bwd(dx, sv["o"], row(gate[l]), name=f"gate_out_bwd{l}")
        dmerged = _mm(dmo, wout_f, "nt", [BF16], name=f"d_merged{l}", b_layer=l)
        g_wout = _mm(sv["merged"], dmo, "tn", [BF16], name=f"g_wout{l}", into=g_wout, into_layer=l)
        dbp, dba, dbc, dgp, dga, dgc = _merge_bwd(dmerged, (sv["bp"], sv["ba"], sv["bc"]), proj, D, name=f"merge_bwd{l}")
        dy_pool = _mm(dbp, wbp_f, "nt", [BF16], name=f"dy_pool{l}", b_layer=l)
        dy_attn = _mm(dba, wba_f, "nt", [BF16], name=f"dy_attn{l}", b_layer=l)
        dy_conv = _mm(dbc, wbc_f, "nt", [BF16], name=f"dy_conv{l}", b_layer=l)
        g_wbp = _mm(sv["y_pool"], dbp, "tn", [BF16], name=f"g_wbp{l}", into=g_wbp, into_layer=l)
        g_wba = _mm(sv["y_attn"], dba, "tn", [BF16], name=f"g_wba{l}", into=g_wba, into_layer=l)
        g_wbc = _mm(sv["y_conv"], dbc, "tn", [BF16], name=f"g_wbc{l}", into=g_wbc, into_layer=l)
        dz_pool, dmn, d_pool_scale, d_poolw = _pool_bwd(proj, dy_pool, poolw_f, l, row(pool_scale[l]), name=f"pool_bwd{l}")
        du_pool = _pool_bwd_window(dmn, name=f"pool_bwd_window{l}")
        g_poolw.append(d_poolw)
        dq, dk, dv, dz_attn, d_sink = _attn_bwd(proj, attn_sink[l], sv["o_attn"], sv["lse"], dy_attn, name=f"attn_bwd{l}")
        dcpre, dz_conv = _conv_out_bwd(dy_conv, sv["cpre"], proj, name=f"conv_out_bwd{l}")
        ds_conv = _mm(dcpre, cpw_f, "nt", [BF16], name=f"ds_conv{l}", b_layer=l)
        g_cpw = _mm(sv["s_conv"], dcpre, "tn", [BF16], name=f"g_cpw{l}", into=g_cpw, into_layer=l)
        taps = conv_dw_full[l]
        dyc, d_taps, d_dwb, d_lng, d_lnb = _conv_bwd(proj, ds_conv, taps, row(conv_dw_b[l]), row(conv_ln_g[l]),
                                                    row(conv_ln_b[l]), name=f"conv_bwd{l}")
        da_conv, db_conv = _conv_bwd_input(proj, dyc, taps, name=f"conv_bwd_input{l}")
        dproj = jnp.concatenate([du_pool, dz_pool, dq, dk, dv, dz_attn, da_conv, db_conv, dz_conv, dgp, dga, dgc], axis=1)
        dh = _mm(dproj, win_f, "nt", [F32], name=f"dh{l}", b_layer=l)
        g_win = _mm(sv["h"], dproj, "tn", [BF16], name=f"g_win{l}", into=g_win, into_layer=l, tn_cap=1536)
        dx, d_ng, d_scale, d_shift = _norm_mod_bwd(xs[l], dh, dx, row(norm_g[l]), row(scale[l]), name=f"norm_bwd{l}")
        dmods.append(jnp.concatenate([d_shift, d_scale, d_gate], axis=1))
        small.append([d_ng, d_pool_scale, d_sink[:, :N_Q_HEADS], d_taps, d_dwb, d_lng, d_lnb])
    small.reverse()
    dmods.reverse()
    g_poolw.reverse()
    grad_x = dx.reshape(1, T, D)

    fulls = [g_win, g_wbp, g_wba, g_wbc, g_wout, g_cpw, jnp.stack(g_poolw).astype(BF16)]
    landed = _scatter_grads(fulls, kinds, name="scatter_grads")
    parts = [_sum_landed(ld.reshape(4, -1, ld.shape[-1]), name=f"sum_landed{a}") for a, ld in enumerate(landed)]
    others = _swap_with_sibling(parts, name="swap_sibling")

    names = ["ng", "ps", "sink", "taps", "dwb", "lng", "lnb"]
    stacked = [jnp.stack([small[l][k] for l in range(L)]) for k in range(len(names))]
    dmod_mine = jnp.concatenate(dmods, axis=0)
    small_shapes = [s.shape for s in stacked] + [d_final_g.shape, dmod_mine.shape]
    reduced = _small_exchange(_pack(stacked + [d_final_g, dmod_mine]), True, name="reduce_small")
    r_ng, r_ps, r_sink, r_taps, r_dwb, r_lng, r_lnb, r_fg, r_bada = _unpack(reduced, small_shapes)
    g_norm_g, g_pool_scale, g_attn_sink = r_ng.reshape(L, D), r_ps.reshape(L, POOL_WIDTH), r_sink.reshape(L, N_Q_HEADS)
    g_conv_dw = lax.dynamic_slice_in_dim(r_taps, chip * (CONV_WIDTH // N_CHIP), CONV_WIDTH // N_CHIP, axis=2)
    g_dwb, g_lng, g_lnb = r_dwb.reshape(L, CONV_WIDTH), r_lng.reshape(L, CONV_WIDTH), r_lnb.reshape(L, CONV_WIDTH)
    g_final_g, g_b_ada = r_fg.reshape(D), r_bada

    dmod_all = _small_exchange(_pad_rows(dmod_mine, 8), False, name="gather_dmod").reshape(N_DEV, 8, 3 * D)[:, :L]
    dmod_slab = lax.dynamic_slice_in_dim(dmod_all, chip * n_ada, n_ada, axis=2).transpose(1, 0, 2)
    g_w_ada = _ada_grad(c_all.T, dmod_slab, name="ada_grad")

    results = {}
    for a, (kind, w, m, v) in enumerate(big):
        flat = lambda t: t.reshape(-1, t.shape[-1])
        g, d, mn, vn = _adamw(flat(w), flat(m), flat(v), [parts[a], others[a]], name=f"adamw{a}")
        results[a] = [t.reshape(w.shape) for t in (g, d, mn, vn)]
    flat = lambda t: t.reshape(-1, t.shape[-1])
    ada = [t.reshape(w_ada.shape) for t in _adamw(flat(w_ada), flat(m_w_ada), flat(v_w_ada), [flat(g_w_ada)], name="adamw_ada")]
    small_w = [norm_g, b_ada, pool_scale, attn_sink, conv_dw, conv_dw_b, conv_ln_g, conv_ln_b, final_g]
    small_m = [m_norm_g, m_b_ada, m_pool_scale, m_attn_sink, m_conv_dw, m_conv_dw_b, m_conv_ln_g, m_conv_ln_b, m_final_g]
    small_v = [v_norm_g, v_b_ada, v_pool_scale, v_attn_sink, v_conv_dw, v_conv_dw_b, v_conv_ln_g, v_conv_ln_b, v_final_g]
    small_g = [g_norm_g, g_b_ada, g_pool_scale, g_attn_sink, g_conv_dw, g_dwb, g_lng, g_lnb, g_final_g]
    sm = _adamw(_pack(small_w), _pack(small_m), _pack(small_v), [_pack(small_g)], name="adamw_small")
    shp = [t.shape for t in small_w]
    sm_g, sm_d, sm_m, sm_v = [_unpack(t, shp) for t in sm]

    def leaves(k, pick):
        s = pick
        return [s[0], ada[k], s[1], results[0][k], results[6][k], s[2], s[3], s[4], s[5], s[6], s[7], results[5][k],
                results[1][k], results[2][k], results[3][k], results[4][k], s[8]]

    return (loss, grad_x, *leaves(0, sm_g), *leaves(1, sm_d), *leaves(2, sm_m), *leaves(3, sm_v))
```

```python
import functools

import jax
import jax.numpy as jnp
from jax import lax
from jax.experimental import pallas as pl
from jax.experimental.pallas import tpu as pltpu

F32 = jnp.float32
BF16 = jnp.bfloat16
MESH = pl.DeviceIdType.MESH
ANY = pl.BlockSpec(memory_space=pl.ANY)

CHUNK = 64
HEAD_DIM = 64
N_Q_HEADS = 16
N_KV_HEADS = 4
Q_PER_KV = N_Q_HEADS // N_KV_HEADS
WINDOW_CHUNKS = 2
POOL_WIDTH = 1024
POOL_WINDOWS = (2, 4, 8, 16)
POOL_GROUP = 256
ATTN_WIDTH = 1024
KV_WIDTH = 256
CONV_WIDTH = 1024
CONV_KERNEL = 31
EPS = 1e-6
OFF_U, OFF_Z, OFF_Q, OFF_K, OFF_V, OFF_AZ, OFF_CA, OFF_CB, OFF_CZ, OFF_G = (
    0, 1024, 2048, 3072, 3328, 3584, 4608, 5632, 6656, 7680)
HALF = 512
POOL_HALO = 16
CONV_HALO = 32
ATTN_Q_BLOCK = 256
ATTN_HALO = WINDOW_CHUNKS * CHUNK
NEG_INF = -1e30

ADAM_LR, ADAM_B1, ADAM_B2, ADAM_EPS, ADAM_WD, ADAM_STEP = 0.001, 0.9, 0.999, 1e-08, 0.01, 10

N_DEV = 8
N_CHIP = 4
VMEM_CAP_BYTES = 56 * 2**20
MIB = 2**20


def _div(n, cap, mult=128):
    if n <= cap:
        return n
    best = None
    for t in range(mult, cap + 1, mult):
        if n % t == 0:
            best = t
    assert best is not None, (n, cap, mult)
    return best


def _params(n_grid, vmem_bytes=None):
    kw = dict(dimension_semantics=("arbitrary",) * n_grid)
    if vmem_bytes is not None:
        kw["vmem_limit_bytes"] = int(min(max(vmem_bytes * 5 // 4 + 4 * MIB, 32 * MIB), VMEM_CAP_BYTES))
    return pltpu.CompilerParams(**kw)


def _silu(z):
    return z * jax.nn.sigmoid(z)


def _dsilu(z):
    s = jax.nn.sigmoid(z)
    return s * (1.0 + z * (1.0 - s))


def _nbytes(shape, dtype):
    n = 1
    for d in shape:
        n *= d
    return n * jnp.dtype(dtype).itemsize


def _mm(a, b, mode, out_dtypes, *, name, b_layer=None, extras=(), epilogue=None, tn_cap=1024):
    if mode == "tn":
        K, M = a.shape
        N = b.shape[-1]
    elif mode == "nt":
        M, K = a.shape
        N = b.shape[-2]
    else:
        M, K = a.shape
        N = b.shape[-1]
    tm = _div(M, 1024)
    tn = _div(N, tn_cap)
    tk = _div(K, 1024 if mode == "tn" else 2048)
    nk = K // tk
    n_out = len(out_dtypes)
    n_ex = len(extras)
    stacked = b.ndim == 3

    def body(*refs):
        a_ref, b_ref = refs[0], refs[1]
        ex_refs = refs[2:2 + n_ex]
        pos = 2 + n_ex
        out_refs = refs[pos:pos + n_out]
        acc_ref = refs[pos + n_out] if nk > 1 else None
        k = pl.program_id(2)
        av = a_ref[...].astype(BF16)
        bv = b_ref[...].astype(BF16)
        if mode == "nn":
            p = jnp.dot(av, bv, preferred_element_type=F32)
        elif mode == "nt":
            p = lax.dot_general(av, bv, (((1,), (1,)), ((), ())), preferred_element_type=F32)
        else:
            p = lax.dot_general(av, bv, (((0,), (0,)), ((), ())), preferred_element_type=F32)

        def finish(acc):
            vals = epilogue(acc, *[r[...] for r in ex_refs]) if epilogue is not None else (acc,)
            for r, v in zip(out_refs, vals):
                r[...] = v.astype(r.dtype)

        if nk == 1:
            finish(p)
        else:
            @pl.when(k == 0)
            def _():
                acc_ref[...] = p

            @pl.when(k > 0)
            def _():
                acc_ref[...] += p

            @pl.when(k == nk - 1)
            def _():
                finish(acc_ref[...])

    if mode == "tn":
        a_spec = pl.BlockSpec((tk, tm), lambda i, j, k: (k, i))
    else:
        a_spec = pl.BlockSpec((tm, tk), lambda i, j, k: (i, k))
    if mode == "nt":
        b_blk, b_idx = (tn, tk), (lambda i, j, k: (j, k))
    else:
        b_blk, b_idx = (tk, tn), (lambda i, j, k: (k, j))
    if stacked:
        b_spec = pl.BlockSpec((None,) + b_blk, lambda i, j, k, f=b_idx: (b_layer,) + f(i, j, k))
    else:
        b_spec = pl.BlockSpec(b_blk, b_idx)
    in_specs = [a_spec, b_spec]
    operands = [a, b]
    vmem = 2 * (tm * tk * a.dtype.itemsize + tk * tn * b.dtype.itemsize) + tm * tn * 4 * 3
    for arr, kind, off in extras:
        if kind == "tile":
            assert off % tn == 0, (name, off, tn)
            in_specs.append(pl.BlockSpec((tm, tn), lambda i, j, k, o=off // tn: (i, o + j)))
        else:
            in_specs.append(pl.BlockSpec((1, tn), lambda i, j, k: (0, j)))
        operands.append(arr)
        vmem += 2 * tm * tn * arr.dtype.itemsize
    out_shape = [jax.ShapeDtypeStruct((M, N), dt) for dt in out_dtypes]
    out_specs = [pl.BlockSpec((tm, tn), lambda i, j, k: (i, j)) for _ in out_dtypes]
    vmem += sum(2 * tm * tn * jnp.dtype(dt).itemsize for dt in out_dtypes)
    outs = pl.pallas_call(
        body, name=name, grid=(M // tm, N // tn, nk), in_specs=in_specs, out_specs=out_specs, out_shape=out_shape,
        scratch_shapes=[pltpu.VMEM((tm, tn), F32)] if nk > 1 else [], compiler_params=_params(3, vmem))(*operands)
    return outs[0] if n_out == 1 else outs


def _merge(ys, wbs, proj, D, *, name):
    T = ys[0].shape[0]
    tm = _div(T, 1024)
    tn = HALF
    kw = ys[0].shape[1]
    g_off = [(OFF_G + b * D) // tn for b in range(3)]

    def body(y0, y1, y2, w0, w1, w2, g0, g1, g2, merged_ref, b0, b1, b2):
        acc = None
        for y, w, g, bo in ((y0, w0, g0, b0), (y1, w1, g1, b1), (y2, w2, g2, b2)):
            p = jnp.dot(y[...], w[...], preferred_element_type=F32)
            bo[...] = p.astype(bo.dtype)
            t = jax.nn.sigmoid(g[...].astype(F32)) * p
            acc = t if acc is None else acc + t
        merged_ref[...] = acc.astype(merged_ref.dtype)

    y_spec = pl.BlockSpec((tm, kw), lambda i, j: (i, 0))
    w_spec = pl.BlockSpec((None, kw, tn), lambda i, j: (0, 0, j))
    g_specs = [pl.BlockSpec((tm, tn), lambda i, j, o=o: (i, o + j)) for o in g_off]
    o_spec = pl.BlockSpec((tm, tn), lambda i, j: (i, j))
    vmem = 2 * (3 * tm * kw * 2 + 3 * kw * tn * 2 + 3 * tm * tn * proj.dtype.itemsize + 4 * tm * tn * 2) + 4 * tm * tn * 4
    return pl.pallas_call(
        body, name=name, grid=(T // tm, D // tn), in_specs=[y_spec] * 3 + [w_spec] * 3 + g_specs,
        out_specs=[o_spec] * 4, out_shape=[jax.ShapeDtypeStruct((T, D), BF16)] * 4,
        compiler_params=_params(2, vmem))(*ys, *wbs, proj, proj, proj)


def _row_tile(T, width, n_arrays):
    cap = max(8, (24 * MIB) // (2 * n_arrays * width * 4))
    return _div(T, min(cap, 1024), 8)


def _norm_mod(x, ng, scale, shift, *, name):
    T, D = x.shape
    tm = _row_tile(T, D, 3)

    def body(x_ref, ng_ref, sc_ref, sh_ref, h_ref):
        xv = x_ref[...]
        r = lax.rsqrt(jnp.mean(xv * xv, axis=-1, keepdims=True) + EPS)
        h = (xv * r) * ng_ref[...] * (1.0 + sc_ref[...]) + sh_ref[...]
        h_ref[...] = h.astype(h_ref.dtype)

    row = pl.BlockSpec((1, D), lambda i: (0, 0))
    tile = pl.BlockSpec((tm, D), lambda i: (i, 0))
    return pl.pallas_call(body, name=name, grid=(T // tm,), in_specs=[tile, row, row, row], out_specs=tile,
                          out_shape=jax.ShapeDtypeStruct((T, D), BF16), compiler_params=_params(1))(x, ng, scale, shift)


def _norm_mod_bwd(x, dh, dx_out, ng, scale, *, name):
    T, D = x.shape
    tm = _row_tile(T, D, 6)

    def body(x_ref, dh_ref, dxo_ref, ng_ref, sc_ref, dx_ref, dng_ref, dsc_ref, dsh_ref):
        i = pl.program_id(0)
        xv = x_ref[...]
        dh_v = dh_ref[...].astype(F32)
        r = lax.rsqrt(jnp.mean(xv * xv, axis=-1, keepdims=True) + EPS)
        xn = xv * r
        one_sc = 1.0 + sc_ref[...]
        dxn = dh_v * (ng_ref[...] * one_sc)
        dx_ref[...] = dxo_ref[...] + r * (dxn - xn * jnp.mean(dxn * xn, axis=-1, keepdims=True))
        t = dh_v * xn
        parts = (jnp.sum(t * one_sc, axis=0, keepdims=True), jnp.sum(t * ng_ref[...], axis=0, keepdims=True),
                 jnp.sum(dh_v, axis=0, keepdims=True))
        for ref, p in zip((dng_ref, dsc_ref, dsh_ref), parts):
            @pl.when(i == 0)
            def _(ref=ref, p=p):
                ref[...] = p

            @pl.when(i > 0)
            def _(ref=ref, p=p):
                ref[...] += p

    row = pl.BlockSpec((1, D), lambda i: (0, 0))
    tile = pl.BlockSpec((tm, D), lambda i: (i, 0))
    vec = jax.ShapeDtypeStruct((1, D), F32)
    return pl.pallas_call(body, name=name, grid=(T // tm,), in_specs=[tile, tile, tile, row, row],
                          out_specs=[tile, row, row, row], out_shape=[jax.ShapeDtypeStruct((T, D), F32), vec, vec, vec],
                          compiler_params=_params(1))(x, dh, dx_out, ng, scale)


def _final_loss(x, target, fg, *, name):
    T, D = x.shape
    tm = _row_tile(T, D, 4)

    def body(x_ref, t_ref, g_ref, loss_ref, dx_ref, dg_ref):
        i = pl.program_id(0)
        xv = x_ref[...]
        r = lax.rsqrt(jnp.mean(xv * xv, axis=-1, keepdims=True) + EPS)
        xn = xv * r
        err = xn * g_ref[...] - t_ref[...]
        part = 0.5 * jnp.sum(jnp.sum(err * err, axis=1, keepdims=True), axis=0, keepdims=True) / D
        dy = err / D
        dxn = dy * g_ref[...]
        dx_ref[...] = r * (dxn - xn * jnp.mean(dxn * xn, axis=-1, keepdims=True))
        dg = jnp.sum(dy * xn, axis=0, keepdims=True)

        @pl.when(i == 0)
        def _():
            loss_ref[...] = part
            dg_ref[...] = dg

        @pl.when(i > 0)
        def _():
            loss_ref[...] += part
            dg_ref[...] += dg

    row = pl.BlockSpec((1, D), lambda i: (0, 0))
    tile = pl.BlockSpec((tm, D), lambda i: (i, 0))
    one = pl.BlockSpec((1, 1), lambda i: (0, 0))
    return pl.pallas_call(
        body, name=name, grid=(T // tm,), in_specs=[tile, tile, row], out_specs=[one, tile, row],
        out_shape=[jax.ShapeDtypeStruct((1, 1), F32), jax.ShapeDtypeStruct((T, D), F32), jax.ShapeDtypeStruct((1, D), F32)],
        compiler_params=_params(1))(x, target, fg)


def _gate_out_bwd(dx_out, o, gate, *, name):
    T, D = dx_out.shape
    tm = _row_tile(T, D, 3)

    def body(dx_ref, o_ref, g_ref, dmo_ref, dg_ref):
        i = pl.program_id(0)
        dxv = dx_ref[...]
        dmo_ref[...] = (dxv * g_ref[...]).astype(dmo_ref.dtype)
        p = jnp.sum(dxv * o_ref[...].astype(F32), axis=0, keepdims=True)

        @pl.when(i == 0)
        def _():
            dg_ref[...] = p

        @pl.when(i > 0)
        def _():
            dg_ref[...] += p

    row = pl.BlockSpec((1, D), lambda i: (0, 0))
    tile = pl.BlockSpec((tm, D), lambda i: (i, 0))
    return pl.pallas_call(body, name=name, grid=(T // tm,), in_specs=[tile, tile, row], out_specs=[tile, row],
                          out_shape=[jax.ShapeDtypeStruct((T, D), BF16), jax.ShapeDtypeStruct((1, D), F32)],
                          compiler_params=_params(1))(dx_out, o, gate)


def _merge_bwd(dmerged, branches, proj, D, *, name):
    T = dmerged.shape[0]
    tm = _div(T, 1024)
    tn = HALF
    g_off = [(OFF_G + b * D) // tn for b in range(3)]

    def body(dm_ref, b0, b1, b2, g0, g1, g2, db0, db1, db2, dg0, dg1, dg2):
        dm = dm_ref[...].astype(F32)
        for b, g, db, dg in ((b0, g0, db0, dg0), (b1, g1, db1, dg1), (b2, g2, db2, dg2)):
            s = jax.nn.sigmoid(g[...].astype(F32))
            db[...] = (dm * s).astype(db.dtype)
            dg[...] = (dm * b[...].astype(F32) * s * (1.0 - s)).astype(dg.dtype)

    tile = pl.BlockSpec((tm, tn), lambda i, j: (i, j))
    g_specs = [pl.BlockSpec((tm, tn), lambda i, j, o=o: (i, o + j)) for o in g_off]
    return pl.pallas_call(body, name=name, grid=(T // tm, D // tn), in_specs=[tile] * 4 + g_specs, out_specs=[tile] * 6,
                          out_shape=[jax.ShapeDtypeStruct((T, D), BF16)] * 6,
                          compiler_params=_params(2))(dmerged, *branches, proj, proj, proj)


def _conv_out_bwd(dy, cpre, proj, *, name):
    T = dy.shape[0]
    tm = _div(T, 1024)
    tn = HALF

    def body(dy_ref, c_ref, z_ref, dc_ref, dz_ref):
        dyv = dy_ref[...].astype(F32)
        z = z_ref[...].astype(F32)
        dc_ref[...] = (dyv * _silu(z)).astype(dc_ref.dtype)
        dz_ref[...] = (dyv * c_ref[...].astype(F32) * _dsilu(z)).astype(dz_ref.dtype)

    tile = pl.BlockSpec((tm, tn), lambda i, j: (i, j))
    z_spec = pl.BlockSpec((tm, tn), lambda i, j: (i, OFF_CZ // tn + j))
    return pl.pallas_call(body, name=name, grid=(T // tm, CONV_WIDTH // tn), in_specs=[tile, tile, z_spec],
                          out_specs=[tile, tile], out_shape=[jax.ShapeDtypeStruct((T, CONV_WIDTH), BF16)] * 2,
                          compiler_params=_params(2))(dy, cpre, proj)


def _pool_mixed(ext, u, g, row0):
    w = POOL_WINDOWS[g]
    s = ext
    shift = 1
    while shift < w:
        s = s + pltpu.roll(s, shift, 0)
        shift *= 2
    tm = u.shape[0]
    t = row0 + lax.broadcasted_iota(jnp.int32, (tm, 1), 0)
    inv = 1.0 / jnp.minimum(t + 1, w).astype(F32)
    return s[POOL_HALO:, :] * inv - u, inv


def _pool_specs(T, tm):
    per = tm // POOL_HALO
    cur = lambda col: pl.BlockSpec((tm, POOL_WIDTH), lambda i, c=col: (i, c))
    prev = pl.BlockSpec((POOL_HALO, POOL_WIDTH), lambda i: (jnp.maximum(i * per - 1, 0), 0))
    return cur, prev


def _pool_fwd(proj, pool_w, scale, *, name):
    T = proj.shape[0]
    tm = _div(T, 512, 16)
    cur, prev = _pool_specs(T, tm)

    def body(u_ref, up_ref, z_ref, w_ref, sc_ref, y_ref):
        i = pl.program_id(0)
        u = u_ref[...].astype(F32)
        halo = jnp.where(i == 0, 0.0, up_ref[...].astype(F32))
        ext = jnp.concatenate([halo, u], axis=0)
        for g in range(len(POOL_WINDOWS)):
            cols = slice(g * POOL_GROUP, (g + 1) * POOL_GROUP)
            mixed, _ = _pool_mixed(ext[:, cols], u[:, cols], g, i * tm)
            p = jnp.dot(mixed.astype(BF16), w_ref[g], preferred_element_type=F32)
            y = p * sc_ref[:, cols] * _silu(z_ref[:, cols].astype(F32))
            y_ref[:, cols] = y.astype(y_ref.dtype)

    w_spec = pl.BlockSpec((len(POOL_WINDOWS), POOL_GROUP, POOL_GROUP), lambda i: (0, 0, 0))
    row = pl.BlockSpec((1, POOL_WIDTH), lambda i: (0, 0))
    return pl.pallas_call(body, name=name, grid=(T // tm,), in_specs=[cur(0), prev, cur(1), w_spec, row],
                          out_specs=pl.BlockSpec((tm, POOL_WIDTH), lambda i: (i, 0)),
                          out_shape=jax.ShapeDtypeStruct((T, POOL_WIDTH), BF16),
                          compiler_params=_params(1))(proj, proj, proj, pool_w, scale)


def _pool_bwd(proj, dy, pool_w, scale, *, name):
    T = proj.shape[0]
    tm = _div(T, 512, 16)
    cur, prev = _pool_specs(T, tm)
    n_g = len(POOL_WINDOWS)

    def body(u_ref, up_ref, z_ref, dy_ref, w_ref, sc_ref, dz_ref, dmn_ref, dsc_ref, dw_ref):
        i = pl.program_id(0)
        u = u_ref[...].astype(F32)
        halo = jnp.where(i == 0, 0.0, up_ref[...].astype(F32))
        ext = jnp.concatenate([halo, u], axis=0)
        for g in range(n_g):
            cols = slice(g * POOL_GROUP, (g + 1) * POOL_GROUP)
            mixed, inv = _pool_mixed(ext[:, cols], u[:, cols], g, i * tm)
            mixed = mixed.astype(BF16)
            w = w_ref[g]
            p = jnp.dot(mixed, w, preferred_element_type=F32)
            z = z_ref[:, cols].astype(F32)
            dyv = dy_ref[:, cols].astype(F32)
            sc = sc_ref[:, cols]
            dypre = dyv * _silu(z)
            dz_ref[:, cols] = (dyv * (p * sc) * _dsilu(z)).astype(dz_ref.dtype)
            dsc = jnp.sum(dypre * p, axis=0, keepdims=True)
            dp = (dypre * sc).astype(BF16)
            dwg = lax.dot_general(mixed, dp, (((0,), (0,)), ((), ())), preferred_element_type=F32)
            dmixed = lax.dot_general(dp, w, (((1,), (1,)), ((), ())), preferred_element_type=F32)
            dmn_ref[:, cols] = dmixed * inv

            @pl.when(i == 0)
            def _(g=g, cols=cols, dsc=dsc, dwg=dwg):
                dsc_ref[:, cols] = dsc
                dw_ref[g] = dwg

            @pl.when(i > 0)
            def _(g=g, cols=cols, dsc=dsc, dwg=dwg):
                dsc_ref[:, cols] += dsc
                dw_ref[g] += dwg

    w_spec = pl.BlockSpec((n_g, POOL_GROUP, POOL_GROUP), lambda i: (0, 0, 0))
    row = pl.BlockSpec((1, POOL_WIDTH), lambda i: (0, 0))
    tile = pl.BlockSpec((tm, POOL_WIDTH), lambda i: (i, 0))
    dw_spec = pl.BlockSpec((n_g, POOL_GROUP, POOL_GROUP), lambda i: (0, 0, 0))
    return pl.pallas_call(
        body, name=name, grid=(T // tm,), in_specs=[cur(0), prev, cur(1), tile, w_spec, row],
        out_specs=[tile, tile, row, dw_spec],
        out_shape=[jax.ShapeDtypeStruct((T, POOL_WIDTH), BF16), jax.ShapeDtypeStruct((T, POOL_WIDTH), F32),
                   jax.ShapeDtypeStruct((1, POOL_WIDTH), F32), jax.ShapeDtypeStruct((n_g, POOL_GROUP, POOL_GROUP), F32)],
        compiler_params=_params(1))(proj, proj, proj, dy, pool_w, scale)


def _pool_bwd_window(dmn, *, name):
    T = dmn.shape[0]
    tm = _div(T, 512, 16)
    per = tm // POOL_HALO
    last = T // POOL_HALO - 1
    nb = T // tm

    def body(c_ref, n_ref, du_ref):
        i = pl.program_id(0)
        cur = c_ref[...]
        nxt = jnp.where(i == nb - 1, 0.0, n_ref[...])
        ext = jnp.concatenate([cur, nxt], axis=0)
        rows = tm + POOL_HALO
        t = i * tm + lax.broadcasted_iota(jnp.int32, (tm, 1), 0)
        for g, w in enumerate(POOL_WINDOWS):
            cols = slice(g * POOL_GROUP, (g + 1) * POOL_GROUP)
            s = ext[:, cols]
            shift = 1
            while shift < w:
                s = s + pltpu.roll(s, rows - shift, 0)
                shift *= 2
            cnt = jnp.minimum(t + 1, w).astype(F32)
            du_ref[:, cols] = (s[:tm, :] - cur[:, cols] * cnt).astype(du_ref.dtype)

    tile = pl.BlockSpec((tm, POOL_WIDTH), lambda i: (i, 0))
    nxt = pl.BlockSpec((POOL_HALO, POOL_WIDTH), lambda i: (jnp.minimum((i + 1) * per, last), 0))
    return pl.pallas_call(body, name=name, grid=(nb,), in_specs=[tile, nxt], out_specs=tile,
                          out_shape=jax.ShapeDtypeStruct((T, POOL_WIDTH), BF16), compiler_params=_params(1))(dmn, dmn)


def _attn_mask(i):
    qb, keys = ATTN_Q_BLOCK, ATTN_Q_BLOCK + ATTN_HALO
    qi = lax.broadcasted_iota(jnp.int32, (qb, keys), 0) // CHUNK
    kj = lax.broadcasted_iota(jnp.int32, (qb, keys), 1) // CHUNK - WINDOW_CHUNKS
    return (kj <= qi) & (kj >= qi - WINDOW_CHUNKS) & (kj + i * (qb // CHUNK) >= 0)


def _attn_specs(T, order):
    qb = ATTN_Q_BLOCK
    per = qb // ATTN_HALO
    cur = lambda width, col: pl.BlockSpec((qb, width), lambda i, c=col: (order(i), c))
    prev = lambda col: pl.BlockSpec((ATTN_HALO, KV_WIDTH), lambda i, c=col: (jnp.maximum(order(i) * per - 1, 0), c))
    return cur, prev


def _attn_fwd(proj, sink, *, name):
    T = proj.shape[0]
    qb = ATTN_Q_BLOCK
    cur, prev = _attn_specs(T, lambda i: i)

    def body(sink_ref, q_ref, kc_ref, kp_ref, vc_ref, vp_ref, z0_ref, z1_ref, o_ref, y_ref, lse_ref):
        i = pl.program_id(0)
        q = q_ref[...].astype(BF16)
        kk = jnp.concatenate([kp_ref[...], kc_ref[...]], axis=0).astype(BF16)
        vv = jnp.concatenate([vp_ref[...], vc_ref[...]], axis=0).astype(BF16)
        mask = _attn_mask(i)
        lane = lax.broadcasted_iota(jnp.int32, (qb, 128), 1)
        lse = jnp.zeros((qb, 128), F32)
        for h in range(N_Q_HEADS):
            hs = slice(h * HEAD_DIM, (h + 1) * HEAD_DIM)
            ks = slice((h // Q_PER_KV) * HEAD_DIM, (h // Q_PER_KV + 1) * HEAD_DIM)
            s = lax.dot_general(q[:, hs], kk[:, ks], (((1,), (1,)), ((), ())), preferred_element_type=F32)
            s = jnp.where(mask, s * (HEAD_DIM ** -0.5), NEG_INF)
            sk = sink_ref[h]
            m = jnp.maximum(jnp.max(s, axis=1, keepdims=True), sk)
            p = jnp.exp(s - m)
            den = jnp.sum(p, axis=1, keepdims=True) + jnp.exp(sk - m)
            oh = jnp.dot(p.astype(BF16), vv[:, ks], preferred_element_type=F32) / den
            zr = z0_ref if h < N_Q_HEADS // 2 else z1_ref
            zs = slice((h % (N_Q_HEADS // 2)) * HEAD_DIM, (h % (N_Q_HEADS // 2) + 1) * HEAD_DIM)
            o_ref[:, hs] = oh.astype(o_ref.dtype)
            y_ref[:, hs] = (oh * _silu(zr[:, zs].astype(F32))).astype(y_ref.dtype)
            lse = jnp.where(lane == h, m + jnp.log(den), lse)
        lse_ref[...] = lse

    kcol, vcol = OFF_K // KV_WIDTH, OFF_V // KV_WIDTH
    tile = pl.BlockSpec((qb, ATTN_WIDTH), lambda i: (i, 0))
    in_specs = [pl.BlockSpec(memory_space=pltpu.SMEM), cur(ATTN_WIDTH, OFF_Q // ATTN_WIDTH), cur(KV_WIDTH, kcol), prev(kcol),
                cur(KV_WIDTH, vcol), prev(vcol), cur(HALF, OFF_AZ // HALF), cur(HALF, OFF_AZ // HALF + 1)]
    return pl.pallas_call(
        body, name=name, grid=(T // qb,), in_specs=in_specs,
        out_specs=[tile, tile, pl.BlockSpec((qb, 128), lambda i: (i, 0))],
        out_shape=[jax.ShapeDtypeStruct((T, ATTN_WIDTH), BF16), jax.ShapeDtypeStruct((T, ATTN_WIDTH), BF16),
                   jax.ShapeDtypeStruct((T, 128), F32)],
        compiler_params=_params(1))(sink, *([proj] * 7))


def _attn_bwd(proj, sink, o, lse, dy, *, name):
    T = proj.shape[0]
    qb = ATTN_Q_BLOCK
    nb = T // qb
    order = lambda i: nb - 1 - i
    cur, prev = _attn_specs(T, order)

    def body(sink_ref, q_ref, kc_ref, kp_ref, vc_ref, vp_ref, z0_ref, z1_ref, o_ref, lse_ref, dy_ref,
             dq_ref, dk_ref, dv_ref, dz_ref, dsink_ref, dk_carry, dv_carry):
        i = pl.program_id(0)
        blk = order(i)
        q = q_ref[...].astype(BF16)
        kk = jnp.concatenate([kp_ref[...], kc_ref[...]], axis=0).astype(BF16)
        vv = jnp.concatenate([vp_ref[...], vc_ref[...]], axis=0).astype(BF16)
        mask = _attn_mask(blk)
        lane = lax.broadcasted_iota(jnp.int32, (1, 128), 1)
        dsink = jnp.zeros((1, 128), F32)
        scale = HEAD_DIM ** -0.5
        for kv in range(N_KV_HEADS):
            ks = slice(kv * HEAD_DIM, (kv + 1) * HEAD_DIM)
            dk_acc = jnp.zeros((qb + ATTN_HALO, HEAD_DIM), F32)
            dv_acc = jnp.zeros((qb + ATTN_HALO, HEAD_DIM), F32)
            for h in range(kv * Q_PER_KV, (kv + 1) * Q_PER_KV):
                hs = slice(h * HEAD_DIM, (h + 1) * HEAD_DIM)
                zr = z0_ref if h < N_Q_HEADS // 2 else z1_ref
                zs = slice((h % (N_Q_HEADS // 2)) * HEAD_DIM, (h % (N_Q_HEADS // 2) + 1) * HEAD_DIM)
                z = zr[:, zs].astype(F32)
                dyh = dy_ref[:, hs].astype(F32)
                oh = o_ref[:, hs].astype(F32)
                do = dyh * _silu(z)
                dz_ref[:, hs] = (dyh * oh * _dsilu(z)).astype(dz_ref.dtype)
                drow = jnp.sum(do * oh, axis=1, keepdims=True)
                lse_h = lse_ref[:, h:h + 1]
                s = lax.dot_general(q[:, hs], kk[:, ks], (((1,), (1,)), ((), ())), preferred_element_type=F32)
                p = jnp.exp(jnp.where(mask, s * scale, NEG_INF) - lse_h)
                do_b = do.astype(BF16)
                dv_acc = dv_acc + lax.dot_general(p.astype(BF16), do_b, (((0,), (0,)), ((), ())),
                                                  preferred_element_type=F32)
                dp = lax.dot_general(do_b, vv[:, ks], (((1,), (1,)), ((), ())), preferred_element_type=F32)
                ds = (p * (dp - drow)).astype(BF16)
                dq_ref[:, hs] = (jnp.dot(ds, kk[:, ks], preferred_element_type=F32) * scale).astype(dq_ref.dtype)
                dk_acc = dk_acc + lax.dot_general(ds, q[:, hs], (((0,), (0,)), ((), ())),
                                                  preferred_element_type=F32) * scale
                p_sink = jnp.exp(sink_ref[h] - lse_h)
                dsink = jnp.where(lane == h, -jnp.sum(p_sink * drow, axis=0, keepdims=True), dsink)
            for acc, carry, out in ((dk_acc, dk_carry, dk_ref), (dv_acc, dv_carry, dv_ref)):
                tail = acc[qb:, :] + jnp.where(i == 0, 0.0, carry[:, ks])
                out[:, ks] = jnp.concatenate([acc[ATTN_HALO:qb, :], tail], axis=0).astype(out.dtype)
                carry[:, ks] = acc[:ATTN_HALO, :]

        @pl.when(i == 0)
        def _():
            dsink_ref[...] = dsink

        @pl.when(i > 0)
        def _():
            dsink_ref[...] += dsink

    kcol, vcol = OFF_K // KV_WIDTH, OFF_V // KV_WIDTH
    tile = pl.BlockSpec((qb, ATTN_WIDTH), lambda i: (order(i), 0))
    kv_tile = pl.BlockSpec((qb, KV_WIDTH), lambda i: (order(i), 0))
    lse_spec = pl.BlockSpec((qb, 128), lambda i: (order(i), 0))
    in_specs = [pl.BlockSpec(memory_space=pltpu.SMEM), cur(ATTN_WIDTH, OFF_Q // ATTN_WIDTH), cur(KV_WIDTH, kcol), prev(kcol),
                cur(KV_WIDTH, vcol), prev(vcol), cur(HALF, OFF_AZ // HALF), cur(HALF, OFF_AZ // HALF + 1),
                tile, lse_spec, tile]
    return pl.pallas_call(
        body, name=name, grid=(nb,), in_specs=in_specs,
        out_specs=[tile, kv_tile, kv_tile, tile, pl.BlockSpec((1, 128), lambda i: (0, 0))],
        out_shape=[jax.ShapeDtypeStruct((T, ATTN_WIDTH), BF16), jax.ShapeDtypeStruct((T, KV_WIDTH), BF16),
                   jax.ShapeDtypeStruct((T, KV_WIDTH), BF16), jax.ShapeDtypeStruct((T, ATTN_WIDTH), BF16),
                   jax.ShapeDtypeStruct((1, 128), F32)],
        scratch_shapes=[pltpu.VMEM((ATTN_HALO, KV_WIDTH), F32), pltpu.VMEM((ATTN_HALO, KV_WIDTH), F32)],
        compiler_params=_params(1))(sink, *([proj] * 7), o, lse, dy)


def _conv_specs(T, tm):
    per = tm // CONV_HALO
    ca, cb = OFF_CA // HALF, OFF_CB // HALF
    cur = lambda col: pl.BlockSpec((tm, HALF), lambda i, c=col: (i, c))
    prev = lambda col: pl.BlockSpec((CONV_HALO, HALF), lambda i, c=col: (jnp.maximum(i * per - 1, 0), c))
    return [cur(ca), cur(ca + 1), cur(cb), cur(cb + 1), prev(ca), prev(ca + 1), prev(cb), prev(cb + 1)]


def _conv_glu_ext(refs, i, ext_ref):
    a0, a1, b0, b1, pa0, pa1, pb0, pb1 = refs
    a = jnp.concatenate([a0[...], a1[...]], axis=1).astype(F32)
    sb = jax.nn.sigmoid(jnp.concatenate([b0[...], b1[...]], axis=1).astype(F32))
    pa = jnp.concatenate([pa0[...], pa1[...]], axis=1).astype(F32)
    pb = jnp.concatenate([pb0[...], pb1[...]], axis=1).astype(F32)
    ext_ref[:CONV_HALO, :] = jnp.where(i == 0, 0.0, pa * jax.nn.sigmoid(pb))
    ext_ref[CONV_HALO:, :] = a * sb
    return a, sb


def _conv_scratch(tm):
    return [pltpu.VMEM((tm + CONV_HALO, CONV_WIDTH), F32), pltpu.VMEM((7, tm + CONV_HALO - 8, CONV_WIDTH), F32)]


def _conv_fill_shifted(ext_ref, sh_ref):
    rows = sh_ref.shape[1]
    for b in range(1, 8):
        sh_ref[b - 1] = ext_ref[b:b + rows, :]


def _conv_window(ext_ref, sh_ref, start, tm):
    a, b = divmod(start, 8)
    if b == 0:
        return ext_ref[8 * a:8 * a + tm, :]
    return sh_ref[b - 1, 8 * a:8 * a + tm, :]


def _conv_dw(ext_ref, sh_ref, dw_ref, tm):
    y = None
    for j in range(CONV_KERNEL):
        t = dw_ref[j:j + 1, :] * _conv_window(ext_ref, sh_ref, CONV_HALO - (CONV_KERNEL - 1) + j, tm)
        y = t if y is None else y + t
    return y


def _conv_fwd(proj, dw, dwb, lng, lnb, *, name):
    T = proj.shape[0]
    tm = _div(T, 256, 32)

    def body(*refs):
        dw_ref, dwb_ref, g_ref, b_ref, s_ref, ext_ref, sh_ref = refs[8:]
        i = pl.program_id(0)
        _conv_glu_ext(refs[:8], i, ext_ref)
        _conv_fill_shifted(ext_ref, sh_ref)
        yc = _conv_dw(ext_ref, sh_ref, dw_ref, tm) + dwb_ref[...]
        mu = jnp.mean(yc, axis=-1, keepdims=True)
        d = yc - mu
        rstd = lax.rsqrt(jnp.mean(d * d, axis=-1, keepdims=True) + EPS)
        s_ref[...] = _silu(d * rstd * g_ref[...] + b_ref[...]).astype(s_ref.dtype)

    row = pl.BlockSpec((1, CONV_WIDTH), lambda i: (0, 0))
    taps = pl.BlockSpec((CONV_KERNEL, CONV_WIDTH), lambda i: (0, 0))
    return pl.pallas_call(
        body, name=name, grid=(T // tm,), in_specs=_conv_specs(T, tm) + [taps, row, row, row],
        out_specs=pl.BlockSpec((tm, CONV_WIDTH), lambda i: (i, 0)), out_shape=jax.ShapeDtypeStruct((T, CONV_WIDTH), BF16),
        scratch_shapes=_conv_scratch(tm), compiler_params=_params(1))(*([proj] * 8), dw, dwb, lng, lnb)


def _conv_bwd(proj, ds, dw, dwb, lng, lnb, *, name):
    T = proj.shape[0]
    tm = _div(T, 256, 32)

    def body(*refs):
        ds_ref, dw_ref, dwb_ref, g_ref, b_ref, dyc_ref, ddw_ref, ddwb_ref, dg_ref, db_ref, ext_ref, sh_ref = refs[8:]
        i = pl.program_id(0)
        _conv_glu_ext(refs[:8], i, ext_ref)
        _conv_fill_shifted(ext_ref, sh_ref)
        yc = _conv_dw(ext_ref, sh_ref, dw_ref, tm) + dwb_ref[...]
        mu = jnp.mean(yc, axis=-1, keepdims=True)
        d = yc - mu
        rstd = lax.rsqrt(jnp.mean(d * d, axis=-1, keepdims=True) + EPS)
        xhat = d * rstd
        dln = ds_ref[...].astype(F32) * _dsilu(xhat * g_ref[...] + b_ref[...])
        dxhat = dln * g_ref[...]
        dyc = rstd * (dxhat - jnp.mean(dxhat, axis=-1, keepdims=True)
                      - xhat * jnp.mean(dxhat * xhat, axis=-1, keepdims=True))
        dyc_ref[...] = dyc
        first = i == 0

        def accumulate(ref, idx, val):
            @pl.when(first)
            def _():
                ref[idx] = val

            @pl.when(jnp.logical_not(first))
            def _():
                ref[idx] += val

        accumulate(dg_ref, slice(None), jnp.sum(dln * xhat, axis=0, keepdims=True))
        accumulate(db_ref, slice(None), jnp.sum(dln, axis=0, keepdims=True))
        accumulate(ddwb_ref, slice(None), jnp.sum(dyc, axis=0, keepdims=True))
        for j in range(CONV_KERNEL):
            window = _conv_window(ext_ref, sh_ref, CONV_HALO - (CONV_KERNEL - 1) + j, tm)
            accumulate(ddw_ref, slice(j, j + 1), jnp.sum(dyc * window, axis=0, keepdims=True))

    row = pl.BlockSpec((1, CONV_WIDTH), lambda i: (0, 0))
    taps = pl.BlockSpec((CONV_KERNEL, CONV_WIDTH), lambda i: (0, 0))
    tile = pl.BlockSpec((tm, CONV_WIDTH), lambda i: (i, 0))
    vec = jax.ShapeDtypeStruct((1, CONV_WIDTH), F32)
    return pl.pallas_call(
        body, name=name, grid=(T // tm,), in_specs=_conv_specs(T, tm) + [tile, taps, row, row, row],
        out_specs=[tile, taps, row, row, row],
        out_shape=[jax.ShapeDtypeStruct((T, CONV_WIDTH), F32), jax.ShapeDtypeStruct((CONV_KERNEL, CONV_WIDTH), F32), vec, vec, vec],
        scratch_shapes=_conv_scratch(tm), compiler_params=_params(1))(*([proj] * 8), ds, dw, dwb, lng, lnb)


def _conv_bwd_input(proj, dyc, dw, *, name):
    T = proj.shape[0]
    tm = _div(T, 256, 32)
    per = tm // CONV_HALO
    last = T // CONV_HALO - 1
    nb = T // tm
    ca, cb = OFF_CA // HALF, OFF_CB // HALF

    def body(a0, a1, b0, b1, c_ref, n_ref, dw_ref, da_ref, db_ref, ext_ref, sh_ref):
        i = pl.program_id(0)
        ext_ref[:tm, :] = c_ref[...]
        ext_ref[tm:, :] = jnp.where(i == nb - 1, 0.0, n_ref[...])
        _conv_fill_shifted(ext_ref, sh_ref)
        dg = None
        for j in range(CONV_KERNEL):
            t = dw_ref[j:j + 1, :] * _conv_window(ext_ref, sh_ref, CONV_KERNEL - 1 - j, tm)
            dg = t if dg is None else dg + t
        a = jnp.concatenate([a0[...], a1[...]], axis=1).astype(F32)
        sb = jax.nn.sigmoid(jnp.concatenate([b0[...], b1[...]], axis=1).astype(F32))
        da_ref[...] = (dg * sb).astype(da_ref.dtype)
        db_ref[...] = (dg * a * sb * (1.0 - sb)).astype(db_ref.dtype)

    cur = lambda col: pl.BlockSpec((tm, HALF), lambda i, c=col: (i, c))
    tile = pl.BlockSpec((tm, CONV_WIDTH), lambda i: (i, 0))
    nxt = pl.BlockSpec((CONV_HALO, CONV_WIDTH), lambda i: (jnp.minimum((i + 1) * per, last), 0))
    taps = pl.BlockSpec((CONV_KERNEL, CONV_WIDTH), lambda i: (0, 0))
    return pl.pallas_call(
        body, name=name, grid=(nb,), in_specs=[cur(ca), cur(ca + 1), cur(cb), cur(cb + 1), tile, nxt, taps],
        out_specs=[tile, tile], out_shape=[jax.ShapeDtypeStruct((T, CONV_WIDTH), BF16)] * 2,
        scratch_shapes=_conv_scratch(tm), compiler_params=_params(1))(proj, proj, proj, proj, dyc, dyc, dw)


def _ada_mod(c_all, w_ada, b_slab, *, name):
    L, D, N = w_ada.shape
    tn = _div(N, 512)

    def body(c_ref, w_ref, b_ref, o_ref):
        ca = _silu(c_ref[...]).astype(BF16)
        o_ref[...] = jnp.dot(ca, w_ref[...].astype(BF16), preferred_element_type=F32) + b_ref[...]

    return pl.pallas_call(
        body, name=name, grid=(L, N // tn),
        in_specs=[pl.BlockSpec((N_DEV, D), lambda l, j: (0, 0)), pl.BlockSpec((None, D, tn), lambda l, j: (l, 0, j)),
                  pl.BlockSpec((None, 1, tn), lambda l, j: (l, 0, j))],
        out_specs=pl.BlockSpec((None, N_DEV, tn), lambda l, j: (l, 0, j)),
        out_shape=jax.ShapeDtypeStruct((L, N_DEV, N), F32), compiler_params=_params(2))(c_all, w_ada, b_slab)


def _ada_grad(c_all_t, dmod_slab, *, name):
    D = c_all_t.shape[0]
    L, _, N = dmod_slab.shape
    tm = _div(D, 512)
    tn = _div(N, 512)

    def body(c_ref, d_ref, o_ref):
        ca = _silu(c_ref[...]).astype(BF16).astype(F32)
        dm = d_ref[...].astype(BF16).astype(F32)
        acc = None
        for b in range(N_DEV):
            t = ca[:, b:b + 1] * dm[b:b + 1, :]
            acc = t if acc is None else acc + t
        o_ref[...] = acc

    return pl.pallas_call(
        body, name=name, grid=(L, D // tm, N // tn),
        in_specs=[pl.BlockSpec((tm, N_DEV), lambda l, i, j: (i, 0)), pl.BlockSpec((None, N_DEV, tn), lambda l, i, j: (l, 0, j))],
        out_specs=pl.BlockSpec((None, tm, tn), lambda l, i, j: (l, i, j)),
        out_shape=jax.ShapeDtypeStruct((L, D, N), F32), compiler_params=_params(3))(c_all_t, dmod_slab)


def _flat_tile(R, C, n_arrays):
    cap = max(8, (20 * MIB) // (2 * n_arrays * C * 4))
    return _div(R, cap, 8) if R % 8 == 0 else R


def _adamw_math(w, g, m, v):
    m = ADAM_B1 * m + (1.0 - ADAM_B1) * g
    v = ADAM_B2 * v + (1.0 - ADAM_B2) * (g * g)
    m_hat = m / (1.0 - ADAM_B1 ** ADAM_STEP)
    v_hat = v / (1.0 - ADAM_B2 ** ADAM_STEP)
    delta = -ADAM_LR * (m_hat / (jnp.sqrt(v_hat) + ADAM_EPS) + ADAM_WD * w)
    return delta, m, v


def _adamw(w, m, v, gs, *, name):
    R, C = w.shape
    n_g = len(gs)
    tr = _flat_tile(R, C, 7 + n_g)

    def body(*refs):
        w_ref, m_ref, v_ref = refs[:3]
        g_refs = refs[3:3 + n_g]
        go_ref, d_ref, mo_ref, vo_ref = refs[3 + n_g:]
        g = g_refs[0][...]
        for r in g_refs[1:]:
            g = g + r[...]
        d, mn, vn = _adamw_math(w_ref[...], g, m_ref[...], v_ref[...])
        go_ref[...] = g
        d_ref[...] = d
        mo_ref[...] = mn
        vo_ref[...] = vn

    tile = pl.BlockSpec((tr, C), lambda i: (i, 0))
    shp = jax.ShapeDtypeStruct((R, C), F32)
    return pl.pallas_call(body, name=name, grid=(R // tr,), in_specs=[tile] * (3 + n_g), out_specs=[tile] * 4,
                          out_shape=[shp] * 4, compiler_params=_params(1, 2 * (7 + n_g) * tr * C * 4))(w, m, v, *gs)


def _adamw_layer(w, m, v, layer, gs, outs, *, name):
    _, R, C = w.shape
    n_g = len(gs)
    tr = _flat_tile(R, C, 7 + n_g)

    def body(*refs):
        w_ref, m_ref, v_ref = refs[:3]
        g_refs = refs[3:3 + n_g]
        go_ref, d_ref, mo_ref, vo_ref = refs[3 + n_g + 4:]
        g = g_refs[0][...]
        for r in g_refs[1:]:
            g = g + r[...]
        d, mn, vn = _adamw_math(w_ref[...], g, m_ref[...], v_ref[...])
        go_ref[...] = g
        d_ref[...] = d
        mo_ref[...] = mn
        vo_ref[...] = vn

    lay = pl.BlockSpec((None, tr, C), lambda i: (layer, i, 0))
    tile = pl.BlockSpec((tr, C), lambda i: (i, 0))
    return pl.pallas_call(
        body, name=name, grid=(R // tr,), in_specs=[lay] * 3 + [tile] * n_g + [ANY] * 4, out_specs=[lay] * 4,
        out_shape=[jax.ShapeDtypeStruct(o.shape, o.dtype) for o in outs],
        input_output_aliases={3 + n_g + k: k for k in range(4)},
        compiler_params=_params(1, 2 * (7 + n_g) * tr * C * 4))(w, m, v, *gs, *outs)


def _full_shape(kind, slab_shape):
    G, r, c = slab_shape
    return (G, r, c * N_CHIP) if kind == "cols" else (G, r * N_CHIP, c)


def _slab_tile(G, r, c, n_arrays):
    cap = max(16, (20 * MIB) // (2 * n_arrays * G * c * 4))
    return _div(r, cap, 16)


def _slab_block(kind, G, r, c, tr):
    if kind == "cols":
        return pl.BlockSpec((G, tr, c), lambda i, chip: (0, i, chip[0]))
    per = r // tr
    return pl.BlockSpec((G, tr, c), lambda i, chip: (0, chip[0] * per + i, 0))


def _cast_into_full(chip, w, layer, kind, *, name):
    _, G, r, c = w.shape
    tr = _slab_tile(G, r, c, 2)

    def body(chip_ref, w_ref, o_ref):
        o_ref[...] = w_ref[...].astype(BF16)

    grid_spec = pltpu.PrefetchScalarGridSpec(
        num_scalar_prefetch=1, grid=(r // tr,),
        in_specs=[pl.BlockSpec((None, G, tr, c), lambda i, chip: (layer, 0, i, 0))], out_specs=_slab_block(kind, G, r, c, tr))
    return pl.pallas_call(body, name=name, grid_spec=grid_spec,
                          out_shape=jax.ShapeDtypeStruct(_full_shape(kind, (G, r, c)), BF16),
                          compiler_params=_params(1, 4 * G * tr * c * 4))(chip, w)


def _sum_contribs(chip, full, land, kind, *, name):
    _, G, r, c = land.shape
    tr = _slab_tile(G, r, c, 5)

    def body(chip_ref, f_ref, l_ref, o_ref):
        o_ref[...] = ((f_ref[...].astype(F32) + l_ref[0].astype(F32)) + l_ref[1].astype(F32)) + l_ref[2].astype(F32)

    grid_spec = pltpu.PrefetchScalarGridSpec(
        num_scalar_prefetch=1, grid=(r // tr,),
        in_specs=[_slab_block(kind, G, r, c, tr), pl.BlockSpec((3, G, tr, c), lambda i, chip: (0, 0, i, 0))],
        out_specs=pl.BlockSpec((G, tr, c), lambda i, chip: (0, i, 0)))
    return pl.pallas_call(body, name=name, grid_spec=grid_spec, out_shape=jax.ShapeDtypeStruct((G, r, c), F32),
                          compiler_params=_params(1, 2 * 5 * G * tr * c * 4))(chip, full, land)


def _place():
    x, y, c = lax.axis_index("x"), lax.axis_index("y"), lax.axis_index("c")
    chips = [(1 - x, y), (x, 1 - y), (1 - x, 1 - y)]
    return x, y, c, chips


def _small_exchange(v, reduce, *, name):
    m_per, n = v.shape
    assert m_per % 8 == 0 and n % 128 == 0

    def body(x_ref, out_ref, *scratch):
        if reduce:
            all_ref, send_sems, recv_sems, local_sem = scratch
        else:
            all_ref = out_ref
            send_sems, recv_sems, local_sem = scratch
        x, y, c, chips = _place()
        me, sibling = (x, y, c), (x, y, 1 - c)

        def rows(px, py, pc):
            return all_ref.at[pl.ds((4 * px + 2 * py + pc) * m_per, m_per), :]

        def copy(k, block, to, src=None):
            return pltpu.make_async_remote_copy(
                src_ref=rows(*block) if src is None else src, dst_ref=rows(*block), send_sem=send_sems.at[k],
                recv_sem=recv_sems.at[k], device_id=to, device_id_type=MESH)

        mine = pltpu.make_async_copy(x_ref, rows(*me), local_sem)
        mine.start()
        first = [copy(0, me, sibling, src=x_ref)]
        first += [copy(1 + j, me, (*chip, c), src=x_ref) for j, chip in enumerate(chips)]
        for cp in first:
            cp.start()
        passed = [copy(4 + j, (*chip, c), sibling) for j, chip in enumerate(chips)]
        for j, chip in enumerate(chips):
            copy(1 + j, (*chip, c), me).wait_recv()
            passed[j].start()
        copy(0, sibling, me).wait_recv()
        for j, chip in enumerate(chips):
            copy(4 + j, (*chip, 1 - c), me).wait_recv()
        for cp in first + passed:
            cp.wait_send()
        mine.wait()
        if reduce:
            acc = all_ref[0:m_per, :]
            for d in range(1, N_DEV):
                acc = acc + all_ref[d * m_per:(d + 1) * m_per, :]
            out_ref[...] = acc

    scratch = [pltpu.SemaphoreType.DMA((7,)), pltpu.SemaphoreType.DMA((7,)), pltpu.SemaphoreType.DMA]
    if reduce:
        scratch = [pltpu.VMEM((N_DEV * m_per, n), F32)] + scratch
    out_rows = m_per if reduce else N_DEV * m_per
    return pl.pallas_call(
        body, name=name, out_shape=jax.ShapeDtypeStruct((out_rows, n), v.dtype),
        in_specs=[pl.BlockSpec(memory_space=pltpu.VMEM)], out_specs=pl.BlockSpec(memory_space=pltpu.VMEM),
        scratch_shapes=scratch,
        compiler_params=pltpu.CompilerParams(vmem_limit_bytes=int(min(VMEM_CAP_BYTES, 4 * N_DEV * m_per * n * 4 + 16 * MIB))))(v)


def _slab(kind, ref, s):
    if kind == "cols":
        w = ref.shape[2] // N_CHIP
        return ref.at[:, :, pl.ds(s * w, w)]
    w = ref.shape[1] // N_CHIP
    return ref.at[:, pl.ds(s * w, w), :]


HBM = pl.BlockSpec(memory_space=pltpu.HBM)
SEM = pl.BlockSpec(memory_space=pltpu.SEMAPHORE)
EFFECT = pltpu.SideEffectType.DATAFLOW_SIDE_EFFECTING


def _in_hbm(v):
    return pltpu.with_memory_space_constraint(v, pltpu.HBM)


def _hbm_like(arrays):
    return [pltpu.HBM(v.shape, v.dtype) for v in arrays]


def _gather_copy(kinds, full, send_sems, recv_sems, a, j, peer, c, s_src, s_dst):
    return pltpu.make_async_remote_copy(
        src_ref=_slab(kinds[a], full[a], s_src), dst_ref=_slab(kinds[a], full[a], s_dst), send_sem=send_sems.at[a * 3 + j],
        recv_sem=recv_sems.at[a * 3 + j], device_id=(*peer, c), device_id_type=MESH)


def _gather_start(fulls, kinds, *, name):
    n = len(fulls)

    def body(*refs):
        full, send_sems, recv_sems, token = refs[:n], refs[n], refs[n + 1], refs[-1]
        x, y, c, chips = _place()
        s_me = 2 * x + y
        for a in range(n):
            for j, peer in enumerate(chips):
                _gather_copy(kinds, full, send_sems, recv_sems, a, j, peer, c, s_me, s_me).start()
        token[...] = jnp.zeros_like(token)

    sems = pltpu.SemaphoreType.DMA((3 * n,))
    out = pl.pallas_call(
        body, name=name, out_shape=(sems, sems, *_hbm_like(fulls), jax.ShapeDtypeStruct((8, 128), F32)),
        in_specs=[HBM] * n, out_specs=(SEM, SEM, *[HBM] * n, pl.BlockSpec(memory_space=pltpu.VMEM)),
        input_output_aliases={a: 2 + a for a in range(n)},
        compiler_params=pltpu.CompilerParams(has_side_effects=EFFECT))(*[_in_hbm(f) for f in fulls])
    return out[0], out[1], list(out[2:2 + n]), out[-1]


def _gather_wait(fulls, kinds, send, recv, after, *, name):
    n = len(fulls)

    def body(*refs):
        full, send_sems, recv_sems = refs[:n], refs[n], refs[n + 1]
        x, y, c, chips = _place()
        s_me = 2 * x + y
        for a in range(n):
            for j, peer in enumerate(chips):
                cp = _gather_copy(kinds, full, send_sems, recv_sems, a, j, peer, c, s_me, 2 * peer[0] + peer[1])
                cp.wait_send()
                cp.wait_recv()

    return pl.pallas_call(
        body, name=name, out_shape=_hbm_like(fulls), in_specs=[HBM] * n + [SEM, SEM] + [ANY] * len(after),
        out_specs=[HBM] * n, input_output_aliases={a: a for a in range(n)},
        compiler_params=pltpu.CompilerParams(has_side_effects=EFFECT))(*fulls, send, recv, *after)


def _scatter_copy(kinds, full, land, send_sems, recv_sems, a, j, peer, c):
    return pltpu.make_async_remote_copy(
        src_ref=_slab(kinds[a], full[a], 2 * peer[0] + peer[1]), dst_ref=land[a].at[j], send_sem=send_sems.at[a * 3 + j],
        recv_sem=recv_sems.at[a * 3 + j], device_id=(*peer, c), device_id_type=MESH)


def _scatter_start(fulls, lands, kinds, *, name):
    n = len(fulls)

    def body(*refs):
        full, land, send_sems, recv_sems, token = refs[:n], refs[n:2 * n], refs[2 * n], refs[2 * n + 1], refs[-1]
        _, _, c, chips = _place()
        for a in range(n):
            for j, peer in enumerate(chips):
                _scatter_copy(kinds, full, land, send_sems, recv_sems, a, j, peer, c).start()
        token[...] = jnp.zeros_like(token)

    sems = pltpu.SemaphoreType.DMA((3 * n,))
    out = pl.pallas_call(
        body, name=name,
        out_shape=(sems, sems, *_hbm_like(fulls), *_hbm_like(lands), jax.ShapeDtypeStruct((8, 128), F32)),
        in_specs=[HBM] * (2 * n), out_specs=(SEM, SEM, *[HBM] * (2 * n), pl.BlockSpec(memory_space=pltpu.VMEM)),
        input_output_aliases={a: 2 + a for a in range(2 * n)},
        compiler_params=pltpu.CompilerParams(has_side_effects=EFFECT))(*[_in_hbm(f) for f in list(fulls) + list(lands)])
    return out[0], out[1], list(out[2:2 + n]), list(out[2 + n:2 + 2 * n]), out[-1]


def _scatter_wait(fulls, lands, kinds, send, recv, after, *, name):
    n = len(fulls)

    def body(*refs):
        full, land, send_sems, recv_sems = refs[:n], refs[n:2 * n], refs[2 * n], refs[2 * n + 1]
        _, _, c, chips = _place()
        for a in range(n):
            for j, peer in enumerate(chips):
                cp = _scatter_copy(kinds, full, land, send_sems, recv_sems, a, j, peer, c)
                cp.wait_send()
                cp.wait_recv()

    out = pl.pallas_call(
        body, name=name, out_shape=_hbm_like(list(fulls) + list(lands)),
        in_specs=[HBM] * (2 * n) + [SEM, SEM] + [ANY] * len(after), out_specs=[HBM] * (2 * n),
        input_output_aliases={a: a for a in range(2 * n)},
        compiler_params=pltpu.CompilerParams(has_side_effects=EFFECT))(*fulls, *lands, send, recv, *after)
    return list(out[:n]), list(out[n:])


def _swap_with_sibling(parts, *, name):
    n = len(parts)

    def body(*refs):
        src, dst = refs[:n], refs[n:2 * n]
        send_sems, recv_sems = refs[2 * n:]
        x, y, c, _ = _place()
        cps = [pltpu.make_async_remote_copy(src_ref=src[a], dst_ref=dst[a], send_sem=send_sems.at[a], recv_sem=recv_sems.at[a],
                                            device_id=(x, y, 1 - c), device_id_type=MESH) for a in range(n)]
        for cp in cps:
            cp.start()
        for cp in cps:
            cp.wait_recv()
        for cp in cps:
            cp.wait_send()

    return pl.pallas_call(
        body, name=name, out_shape=[jax.ShapeDtypeStruct(p.shape, p.dtype) for p in parts], in_specs=[ANY] * n,
        out_specs=[ANY] * n, scratch_shapes=[pltpu.SemaphoreType.DMA((n,)), pltpu.SemaphoreType.DMA((n,))])(*parts)


def _pad_rows(v, rows):
    return jnp.pad(v, ((0, rows - v.shape[0]), (0, 0)))


def _pack(vectors):
    flat = jnp.concatenate([v.reshape(-1) for v in vectors])
    n = -(-flat.shape[0] // 1024) * 1024
    return jnp.pad(flat, (0, n - flat.shape[0])).reshape(8, n // 8)


def _unpack(block, shapes):
    flat = block.reshape(-1)
    out, pos = [], 0
    for shp in shapes:
        size = 1
        for d in shp:
            size *= d
        out.append(flat[pos:pos + size].reshape(shp))
        pos += size
    return out


def kernel(x, c, norm_g, w_ada, b_ada, w_in, pool_w, pool_scale, attn_sink, conv_dw, conv_dw_b, conv_ln_g, conv_ln_b, conv_pw, w_branch_pool, w_branch_attn, w_branch_conv, w_out, final_g, loss_target, m_norm_g, m_w_ada, m_b_ada, m_w_in, m_pool_w, m_pool_scale, m_attn_sink, m_conv_dw, m_conv_dw_b, m_conv_ln_g, m_conv_ln_b, m_conv_pw, m_w_branch_pool, m_w_branch_attn, m_w_branch_conv, m_w_out, m_final_g, v_norm_g, v_w_ada, v_b_ada, v_w_in, v_pool_w, v_pool_scale, v_attn_sink, v_conv_dw, v_conv_dw_b, v_conv_ln_g, v_conv_ln_b, v_conv_pw, v_w_branch_pool, v_w_branch_attn, v_w_branch_conv, v_w_out, v_final_g):
    _, T, D = x.shape
    L = norm_g.shape[0]
    IN = w_in.shape[2] * N_CHIP
    assert IN == OFF_G + 3 * D and D % HALF == 0 and T % 512 == 0
    xi, yi, ci = lax.axis_index("x"), lax.axis_index("y"), lax.axis_index("c")
    chip = 2 * xi + yi
    dev = 2 * chip + ci
    x0 = x.reshape(T, D)
    target = loss_target.reshape(T, D)

    big = [("cols", w_in, m_w_in, v_w_in), ("cols", w_branch_pool, m_w_branch_pool, v_w_branch_pool),
           ("cols", w_branch_attn, m_w_branch_attn, v_w_branch_attn), ("cols", w_branch_conv, m_w_branch_conv, v_w_branch_conv),
           ("rows", w_out, m_w_out, v_w_out), ("rows", conv_pw, m_conv_pw, v_conv_pw), ("rows", pool_w, m_pool_w, v_pool_w)]
    kinds = [b[0] for b in big]
    n_big = len(big)
    as_groups = lambda t: t if t.ndim == 4 else t.reshape(L, 1, t.shape[1], t.shape[2])
    chip_arr = jnp.reshape(chip, (1,)).astype(jnp.int32)
    weights, gather_tokens = [], []
    for l in range(L):
        fulls = [_cast_into_full(chip_arr, as_groups(b[1]), l, b[0], name=f"cast{a}_{l}") for a, b in enumerate(big)]
        send, recv, fulls, token = _gather_start(fulls, kinds, name=f"gather_start{l}")
        weights.append((fulls, send, recv))
        gather_tokens.append(token[0:1, 0:1])
    started = functools.reduce(lambda p, q: p + q, gather_tokens)

    c_all = _small_exchange(_pad_rows(c, 8), False, name="gather_c")[0::8]
    taps_rows = -(-(L * CONV_KERNEL) // 8) * 8
    dw_blocks = _small_exchange(_pad_rows(conv_dw.reshape(L * CONV_KERNEL, -1), taps_rows), False, name="gather_taps")
    dw_blocks = dw_blocks.reshape(N_CHIP, 2, taps_rows, -1)[:, 0, :L * CONV_KERNEL]
    conv_dw_full = dw_blocks.reshape(N_CHIP, L, CONV_KERNEL, -1).transpose(1, 2, 0, 3).reshape(L, CONV_KERNEL, CONV_WIDTH)
    n_ada = w_ada.shape[2]
    b_slab = lax.dynamic_slice_in_dim(b_ada, chip * n_ada, n_ada, axis=1).reshape(L, 1, n_ada)
    mod_part = _ada_mod(c_all, w_ada, b_slab, name="ada_mod")
    mod_blocks = _small_exchange(mod_part.reshape(L * N_DEV, n_ada), False, name="gather_mod")
    mod_blocks = mod_blocks.reshape(N_CHIP, 2, L, N_DEV, n_ada)[:, 0]
    mod_all = mod_blocks.transpose(1, 2, 0, 3).reshape(L, N_DEV, 3 * D)
    mod = lax.dynamic_index_in_dim(mod_all, dev, axis=1, keepdims=False)
    shift, scale, gate = mod[:, :D], mod[:, D:2 * D], mod[:, 2 * D:]

    row = lambda v: v.reshape(1, -1)

    xs, saved = [x0], []
    xl = x0
    full_w = []
    for l in range(L):
        h = _norm_mod(xl, row(norm_g[l]), row(scale[l]) + started if l == 0 else row(scale[l]), row(shift[l]), name=f"norm{l}")
        fulls, send, recv = weights[l]
        win_f, wbp_f, wba_f, wbc_f, wout_f, cpw_f, poolw_f = _gather_wait(fulls, kinds, send, recv, [h], name=f"gather_wait{l}")
        full_w.append((win_f, wbp_f, wba_f, wbc_f, wout_f, cpw_f, poolw_f))
        proj = _mm(h, win_f, "nn", [F32], name=f"proj{l}", b_layer=0)
        y_pool = _pool_fwd(proj, poolw_f, row(pool_scale[l]), name=f"pool{l}")
        o_attn, y_attn, lse = _attn_fwd(proj, attn_sink[l], name=f"attn{l}")
        s_conv = _conv_fwd(proj, conv_dw_full[l], row(conv_dw_b[l]), row(conv_ln_g[l]), row(conv_ln_b[l]), name=f"conv{l}")
        cpre, y_conv = _mm(s_conv, cpw_f, "nn", [BF16, BF16], name=f"conv_pw{l}", b_layer=0, tn_cap=HALF,
                           extras=[(proj, "tile", OFF_CZ)], epilogue=lambda acc, z: (acc, acc * _silu(z)))
        merged, bp, ba, bc = _merge((y_pool, y_attn, y_conv), (wbp_f, wba_f, wbc_f), proj, D, name=f"merge{l}")
        x_new, o = _mm(merged, wout_f, "nn", [F32, BF16], name=f"out{l}", b_layer=0,
                       extras=[(xl, "tile", 0), (row(gate[l]), "row", 0)],
                       epilogue=lambda acc, xv, g: (xv + g * acc, acc))
        saved.append(dict(h=h, proj=proj, y_pool=y_pool, o_attn=o_attn, y_attn=y_attn, lse=lse, s_conv=s_conv, cpre=cpre,
                          y_conv=y_conv, merged=merged, bp=bp, ba=ba, bc=bc, o=o))
        xl = x_new
        xs.append(xl)

    loss_part, dx, d_final_g = _final_loss(xl, target, row(final_g), name="final_loss")
    loss = lax.psum(loss_part[0, 0], ("x", "y", "c"))

    small, dmods, scattering = [], [], {}
    scattered = jnp.zeros((1, 1), F32)
    for l in reversed(range(L)):
        sv = saved[l]
        proj = sv["proj"]
        win_f, wbp_f, wba_f, wbc_f, wout_f, cpw_f, poolw_f = full_w[l]
        dmo, d_gate = _gate_out_bwd(dx, sv["o"], row(gate[l]) + scattered, name=f"gate_out_bwd{l}")
        dmerged = _mm(dmo, wout_f, "nt", [BF16], name=f"d_merged{l}", b_layer=0)
        g_wout = _mm(sv["merged"], dmo, "tn", [BF16], name=f"g_wout{l}")
        dbp, dba, dbc, dgp, dga, dgc = _merge_bwd(dmerged, (sv["bp"], sv["ba"], sv["bc"]), proj, D, name=f"merge_bwd{l}")
        dy_pool = _mm(dbp, wbp_f, "nt", [BF16], name=f"dy_pool{l}", b_layer=0)
        dy_attn = _mm(dba, wba_f, "nt", [BF16], name=f"dy_attn{l}", b_layer=0)
        dy_conv = _mm(dbc, wbc_f, "nt", [BF16], name=f"dy_conv{l}", b_layer=0)
        g_wbp = _mm(sv["y_pool"], dbp, "tn", [BF16], name=f"g_wbp{l}")
        g_wba = _mm(sv["y_attn"], dba, "tn", [BF16], name=f"g_wba{l}")
        g_wbc = _mm(sv["y_conv"], dbc, "tn", [BF16], name=f"g_wbc{l}")
        dz_pool, dmn, d_pool_scale, g_poolw = _pool_bwd(proj, dy_pool, poolw_f, row(pool_scale[l]), name=f"pool_bwd{l}")
        du_pool = _pool_bwd_window(dmn, name=f"pool_bwd_window{l}")
        dq, dk, dv, dz_attn, d_sink = _attn_bwd(proj, attn_sink[l], sv["o_attn"], sv["lse"], dy_attn, name=f"attn_bwd{l}")
        dcpre, dz_conv = _conv_out_bwd(dy_conv, sv["cpre"], proj, name=f"conv_out_bwd{l}")
        ds_conv = _mm(dcpre, cpw_f, "nt", [BF16], name=f"ds_conv{l}", b_layer=0)
        g_cpw = _mm(sv["s_conv"], dcpre, "tn", [BF16], name=f"g_cpw{l}")
        taps = conv_dw_full[l]
        dyc, d_taps, d_dwb, d_lng, d_lnb = _conv_bwd(proj, ds_conv, taps, row(conv_dw_b[l]), row(conv_ln_g[l]),
                                                    row(conv_ln_b[l]), name=f"conv_bwd{l}")
        da_conv, db_conv = _conv_bwd_input(proj, dyc, taps, name=f"conv_bwd_input{l}")
        dproj = jnp.concatenate([du_pool, dz_pool, dq, dk, dv, dz_attn, da_conv, db_conv, dz_conv, dgp, dga, dgc], axis=1)
        dh = _mm(dproj, win_f, "nt", [F32], name=f"dh{l}", b_layer=0)
        g_win = _mm(sv["h"], dproj, "tn", [BF16], name=f"g_win{l}", tn_cap=1536)
        dx, d_ng, d_scale, d_shift = _norm_mod_bwd(xs[l], dh, dx, row(norm_g[l]), row(scale[l]), name=f"norm_bwd{l}")
        dmods.append(jnp.concatenate([d_shift, d_scale, d_gate], axis=1))
        small.append([d_ng, d_pool_scale, d_sink[:, :N_Q_HEADS], d_taps, d_dwb, d_lng, d_lnb])
        grads = [g[None] for g in (g_win, g_wbp, g_wba, g_wbc, g_wout, g_cpw)] + [g_poolw.astype(BF16)]
        lands = [lax.empty((3,) + as_groups(b[1]).shape[1:], BF16) for b in big]
        send, recv, grads, lands, token = _scatter_start(grads, lands, kinds, name=f"scatter_start{l}")
        scattering[l] = (grads, lands, send, recv)
        scattered = token[0:1, 0:1]
    small.reverse()
    dmods.reverse()
    grad_x = dx.reshape(1, T, D)

    names = ["ng", "ps", "sink", "taps", "dwb", "lng", "lnb"]
    stacked = [jnp.stack([small[l][k] for l in range(L)]) for k in range(len(names))]
    dmod_mine = jnp.concatenate(dmods, axis=0)
    small_shapes = [s.shape for s in stacked] + [d_final_g.shape, dmod_mine.shape]
    reduced = _small_exchange(_pack(stacked + [d_final_g, dmod_mine]) + scattered, True, name="reduce_small")
    r_ng, r_ps, r_sink, r_taps, r_dwb, r_lng, r_lnb, r_fg, r_bada = _unpack(reduced, small_shapes)
    g_norm_g, g_pool_scale, g_attn_sink = r_ng.reshape(L, D), r_ps.reshape(L, POOL_WIDTH), r_sink.reshape(L, N_Q_HEADS)
    g_conv_dw = lax.dynamic_slice_in_dim(r_taps, chip * (CONV_WIDTH // N_CHIP), CONV_WIDTH // N_CHIP, axis=2)
    g_dwb, g_lng, g_lnb = r_dwb.reshape(L, CONV_WIDTH), r_lng.reshape(L, CONV_WIDTH), r_lnb.reshape(L, CONV_WIDTH)
    g_final_g, g_b_ada = r_fg.reshape(D), r_bada

    dmod_all = _small_exchange(_pad_rows(dmod_mine, 8), False, name="gather_dmod").reshape(N_DEV, 8, 3 * D)[:, :L]
    dmod_slab = lax.dynamic_slice_in_dim(dmod_all, chip * n_ada, n_ada, axis=2).transpose(1, 0, 2)
    g_w_ada = _ada_grad(c_all.T, dmod_slab, name="ada_grad")

    flat = lambda t: t.reshape(-1, t.shape[-1])
    ada = [t.reshape(w_ada.shape) for t in _adamw(flat(w_ada), flat(m_w_ada), flat(v_w_ada), [flat(g_w_ada)], name="adamw_ada")]
    small_w = [norm_g, b_ada, pool_scale, attn_sink, conv_dw, conv_dw_b, conv_ln_g, conv_ln_b, final_g]
    small_m = [m_norm_g, m_b_ada, m_pool_scale, m_attn_sink, m_conv_dw, m_conv_dw_b, m_conv_ln_g, m_conv_ln_b, m_final_g]
    small_v = [v_norm_g, v_b_ada, v_pool_scale, v_attn_sink, v_conv_dw, v_conv_dw_b, v_conv_ln_g, v_conv_ln_b, v_final_g]
    small_g = [g_norm_g, g_b_ada, g_pool_scale, g_attn_sink, g_conv_dw, g_dwb, g_lng, g_lnb, g_final_g]
    sm = _adamw(_pack(small_w), _pack(small_m), _pack(small_v), [_pack(small_g)], name="adamw_small")
    shp = [t.shape for t in small_w]
    sm_g, sm_d, sm_m, sm_v = [_unpack(t, shp) for t in sm]

    stacked3 = lambda t: t.reshape(L, -1, t.shape[-1])
    outs = [[lax.empty(stacked3(b[1]).shape, F32) for _ in range(4)] for b in big]
    after = [ada[0], sm[0]]
    for l in reversed(range(L)):
        grads, lands, send, recv = scattering[l]
        grads, lands = _scatter_wait(grads, lands, kinds, send, recv, after, name=f"scatter_wait{l}")
        parts = [_sum_contribs(chip_arr, grads[a], lands[a], kinds[a], name=f"sum_grads{a}_{l}") for a in range(n_big)]
        others = _swap_with_sibling(parts, name=f"swap_sibling{l}")
        for a, (_, w, m, v) in enumerate(big):
            two = lambda t: t.reshape(-1, t.shape[-1])
            outs[a] = _adamw_layer(stacked3(w), stacked3(m), stacked3(v), l, [two(parts[a]), two(others[a])], outs[a],
                                   name=f"adamw{a}_{l}")
        after = [outs[a][0] for a in range(n_big)]
    results = {a: [t.reshape(big[a][1].shape) for t in outs[a]] for a in range(n_big)}

    def leaves(k, pick):
        s = pick
        return [s[0], ada[k], s[1], results[0][k], results[6][k], s[2], s[3], s[4], s[5], s[6], s[7], results[5][k],
                results[1][k], results[2][k], results[3][k], results[4][k], s[8]]

    return (loss, grad_x, *leaves(0, sm_g), *leaves(1, sm_d), *leaves(2, sm_m), *leaves(3, sm_v))
```

```python
import functools

import jax
import jax.numpy as jnp
from jax import lax
from jax.experimental import pallas as pl
from jax.experimental.pallas import tpu as pltpu

F32 = jnp.float32
BF16 = jnp.bfloat16
MESH = pl.DeviceIdType.MESH
ANY = pl.BlockSpec(memory_space=pl.ANY)

CHUNK = 64
HEAD_DIM = 64
N_Q_HEADS = 16
N_KV_HEADS = 4
Q_PER_KV = N_Q_HEADS // N_KV_HEADS
WINDOW_CHUNKS = 2
POOL_WIDTH = 1024
POOL_WINDOWS = (2, 4, 8, 16)
POOL_GROUP = 256
ATTN_WIDTH = 1024
KV_WIDTH = 256
CONV_WIDTH = 1024
CONV_KERNEL = 31
EPS = 1e-6
OFF_U, OFF_Z, OFF_Q, OFF_K, OFF_V, OFF_AZ, OFF_CA, OFF_CB, OFF_CZ, OFF_G = (
    0, 1024, 2048, 3072, 3328, 3584, 4608, 5632, 6656, 7680)
HALF = 512
POOL_HALO = 16
CONV_HALO = 32
ATTN_Q_BLOCK = 256
ATTN_HALO = WINDOW_CHUNKS * CHUNK
NEG_INF = -1e30

ADAM_LR, ADAM_B1, ADAM_B2, ADAM_EPS, ADAM_WD, ADAM_STEP = 0.001, 0.9, 0.999, 1e-08, 0.01, 10

N_DEV = 8
N_CHIP = 4
VMEM_CAP_BYTES = 56 * 2**20
MIB = 2**20


def _div(n, cap, mult=128):
    if n <= cap:
        return n
    best = None
    for t in range(mult, cap + 1, mult):
        if n % t == 0:
            best = t
    assert best is not None, (n, cap, mult)
    return best


def _params(n_grid, vmem_bytes=None):
    kw = dict(dimension_semantics=("arbitrary",) * n_grid)
    if vmem_bytes is not None:
        kw["vmem_limit_bytes"] = int(min(max(vmem_bytes * 5 // 4 + 4 * MIB, 32 * MIB), VMEM_CAP_BYTES))
    return pltpu.CompilerParams(**kw)


def _silu(z):
    return z * jax.nn.sigmoid(z)


def _dsilu(z):
    s = jax.nn.sigmoid(z)
    return s * (1.0 + z * (1.0 - s))


def _nbytes(shape, dtype):
    n = 1
    for d in shape:
        n *= d
    return n * jnp.dtype(dtype).itemsize


def _mm(a, b, mode, out_dtypes, *, name, b_layer=None, extras=(), epilogue=None, tn_cap=1024):
    if mode == "tn":
        K, M = a.shape
        N = b.shape[-1]
    elif mode == "nt":
        M, K = a.shape
        N = b.shape[-2]
    else:
        M, K = a.shape
        N = b.shape[-1]
    tm = _div(M, 1024)
    tn = _div(N, tn_cap)
    tk = _div(K, 1024 if mode == "tn" else 2048)
    nk = K // tk
    n_out = len(out_dtypes)
    n_ex = len(extras)
    stacked = b.ndim == 3

    def body(*refs):
        a_ref, b_ref = refs[0], refs[1]
        ex_refs = refs[2:2 + n_ex]
        pos = 2 + n_ex
        out_refs = refs[pos:pos + n_out]
        acc_ref = refs[pos + n_out] if nk > 1 else None
        k = pl.program_id(2)
        av = a_ref[...].astype(BF16)
        bv = b_ref[...].astype(BF16)
        if mode == "nn":
            p = jnp.dot(av, bv, preferred_element_type=F32)
        elif mode == "nt":
            p = lax.dot_general(av, bv, (((1,), (1,)), ((), ())), preferred_element_type=F32)
        else:
            p = lax.dot_general(av, bv, (((0,), (0,)), ((), ())), preferred_element_type=F32)

        def finish(acc):
            vals = epilogue(acc, *[r[...] for r in ex_refs]) if epilogue is not None else (acc,)
            for r, v in zip(out_refs, vals):
                r[...] = v.astype(r.dtype)

        if nk == 1:
            finish(p)
        else:
            @pl.when(k == 0)
            def _():
                acc_ref[...] = p

            @pl.when(k > 0)
            def _():
                acc_ref[...] += p

            @pl.when(k == nk - 1)
            def _():
                finish(acc_ref[...])

    if mode == "tn":
        a_spec = pl.BlockSpec((tk, tm), lambda i, j, k: (k, i))
    else:
        a_spec = pl.BlockSpec((tm, tk), lambda i, j, k: (i, k))
    if mode == "nt":
        b_blk, b_idx = (tn, tk), (lambda i, j, k: (j, k))
    else:
        b_blk, b_idx = (tk, tn), (lambda i, j, k: (k, j))
    if stacked:
        b_spec = pl.BlockSpec((None,) + b_blk, lambda i, j, k, f=b_idx: (b_layer,) + f(i, j, k))
    else:
        b_spec = pl.BlockSpec(b_blk, b_idx)
    in_specs = [a_spec, b_spec]
    operands = [a, b]
    vmem = 2 * (tm * tk * a.dtype.itemsize + tk * tn * b.dtype.itemsize) + tm * tn * 4 * 3
    for arr, kind, off in extras:
        if kind == "tile":
            assert off % tn == 0, (name, off, tn)
            in_specs.append(pl.BlockSpec((tm, tn), lambda i, j, k, o=off // tn: (i, o + j)))
        else:
            in_specs.append(pl.BlockSpec((1, tn), lambda i, j, k: (0, j)))
        operands.append(arr)
        vmem += 2 * tm * tn * arr.dtype.itemsize
    out_shape = [jax.ShapeDtypeStruct((M, N), dt) for dt in out_dtypes]
    out_specs = [pl.BlockSpec((tm, tn), lambda i, j, k: (i, j)) for _ in out_dtypes]
    vmem += sum(2 * tm * tn * jnp.dtype(dt).itemsize for dt in out_dtypes)
    outs = pl.pallas_call(
        body, name=name, grid=(M // tm, N // tn, nk), in_specs=in_specs, out_specs=out_specs, out_shape=out_shape,
        scratch_shapes=[pltpu.VMEM((tm, tn), F32)] if nk > 1 else [], compiler_params=_params(3, vmem))(*operands)
    return outs[0] if n_out == 1 else outs


def _merge(ys, wbs, proj, D, *, name):
    T = ys[0].shape[0]
    tm = _div(T, 1024)
    tn = HALF
    kw = ys[0].shape[1]
    g_off = [(OFF_G + b * D) // tn for b in range(3)]

    def body(y0, y1, y2, w0, w1, w2, g0, g1, g2, merged_ref, b0, b1, b2):
        acc = None
        for y, w, g, bo in ((y0, w0, g0, b0), (y1, w1, g1, b1), (y2, w2, g2, b2)):
            p = jnp.dot(y[...], w[...], preferred_element_type=F32)
            bo[...] = p.astype(bo.dtype)
            t = jax.nn.sigmoid(g[...].astype(F32)) * p
            acc = t if acc is None else acc + t
        merged_ref[...] = acc.astype(merged_ref.dtype)

    y_spec = pl.BlockSpec((tm, kw), lambda i, j: (i, 0))
    w_spec = pl.BlockSpec((None, kw, tn), lambda i, j: (0, 0, j))
    g_specs = [pl.BlockSpec((tm, tn), lambda i, j, o=o: (i, o + j)) for o in g_off]
    o_spec = pl.BlockSpec((tm, tn), lambda i, j: (i, j))
    vmem = 2 * (3 * tm * kw * 2 + 3 * kw * tn * 2 + 3 * tm * tn * proj.dtype.itemsize + 4 * tm * tn * 2) + 4 * tm * tn * 4
    return pl.pallas_call(
        body, name=name, grid=(T // tm, D // tn), in_specs=[y_spec] * 3 + [w_spec] * 3 + g_specs,
        out_specs=[o_spec] * 4, out_shape=[jax.ShapeDtypeStruct((T, D), BF16)] * 4,
        compiler_params=_params(2, vmem))(*ys, *wbs, proj, proj, proj)


def _row_tile(T, width, n_arrays):
    cap = max(8, (24 * MIB) // (2 * n_arrays * width * 4))
    return _div(T, min(cap, 1024), 8)


def _norm_mod(x, ng, scale, shift, *, name):
    T, D = x.shape
    tm = _row_tile(T, D, 3)

    def body(x_ref, ng_ref, sc_ref, sh_ref, h_ref):
        xv = x_ref[...]
        r = lax.rsqrt(jnp.mean(xv * xv, axis=-1, keepdims=True) + EPS)
        h = (xv * r) * ng_ref[...] * (1.0 + sc_ref[...]) + sh_ref[...]
        h_ref[...] = h.astype(h_ref.dtype)

    row = pl.BlockSpec((1, D), lambda i: (0, 0))
    tile = pl.BlockSpec((tm, D), lambda i: (i, 0))
    return pl.pallas_call(body, name=name, grid=(T // tm,), in_specs=[tile, row, row, row], out_specs=tile,
                          out_shape=jax.ShapeDtypeStruct((T, D), BF16), compiler_params=_params(1))(x, ng, scale, shift)


def _norm_mod_bwd(x, dh, dx_out, ng, scale, *, name):
    T, D = x.shape
    tm = _row_tile(T, D, 6)

    def body(x_ref, dh_ref, dxo_ref, ng_ref, sc_ref, dx_ref, dng_ref, dsc_ref, dsh_ref):
        i = pl.program_id(0)
        xv = x_ref[...]
        dh_v = dh_ref[...].astype(F32)
        r = lax.rsqrt(jnp.mean(xv * xv, axis=-1, keepdims=True) + EPS)
        xn = xv * r
        one_sc = 1.0 + sc_ref[...]
        dxn = dh_v * (ng_ref[...] * one_sc)
        dx_ref[...] = dxo_ref[...] + r * (dxn - xn * jnp.mean(dxn * xn, axis=-1, keepdims=True))
        t = dh_v * xn
        parts = (jnp.sum(t * one_sc, axis=0, keepdims=True), jnp.sum(t * ng_ref[...], axis=0, keepdims=True),
                 jnp.sum(dh_v, axis=0, keepdims=True))
        for ref, p in zip((dng_ref, dsc_ref, dsh_ref), parts):
            @pl.when(i == 0)
            def _(ref=ref, p=p):
                ref[...] = p

            @pl.when(i > 0)
            def _(ref=ref, p=p):
                ref[...] += p

    row = pl.BlockSpec((1, D), lambda i: (0, 0))
    tile = pl.BlockSpec((tm, D), lambda i: (i, 0))
    vec = jax.ShapeDtypeStruct((1, D), F32)
    return pl.pallas_call(body, name=name, grid=(T // tm,), in_specs=[tile, tile, tile, row, row],
                          out_specs=[tile, row, row, row], out_shape=[jax.ShapeDtypeStruct((T, D), F32), vec, vec, vec],
                          compiler_params=_params(1))(x, dh, dx_out, ng, scale)


def _final_loss(x, target, fg, *, name):
    T, D = x.shape
    tm = _row_tile(T, D, 4)

    def body(x_ref, t_ref, g_ref, loss_ref, dx_ref, dg_ref):
        i = pl.program_id(0)
        xv = x_ref[...]
        r = lax.rsqrt(jnp.mean(xv * xv, axis=-1, keepdims=True) + EPS)
        xn = xv * r
        err = xn * g_ref[...] - t_ref[...]
        part = 0.5 * jnp.sum(jnp.sum(err * err, axis=1, keepdims=True), axis=0, keepdims=True) / D
        dy = err / D
        dxn = dy * g_ref[...]
        dx_ref[...] = r * (dxn - xn * jnp.mean(dxn * xn, axis=-1, keepdims=True))
        dg = jnp.sum(dy * xn, axis=0, keepdims=True)

        @pl.when(i == 0)
        def _():
            loss_ref[...] = part
            dg_ref[...] = dg

        @pl.when(i > 0)
        def _():
            loss_ref[...] += part
            dg_ref[...] += dg

    row = pl.BlockSpec((1, D), lambda i: (0, 0))
    tile = pl.BlockSpec((tm, D), lambda i: (i, 0))
    one = pl.BlockSpec((1, 1), lambda i: (0, 0))
    return pl.pallas_call(
        body, name=name, grid=(T // tm,), in_specs=[tile, tile, row], out_specs=[one, tile, row],
        out_shape=[jax.ShapeDtypeStruct((1, 1), F32), jax.ShapeDtypeStruct((T, D), F32), jax.ShapeDtypeStruct((1, D), F32)],
        compiler_params=_params(1))(x, target, fg)


def _gate_out_bwd(dx_out, o, gate, *, name):
    T, D = dx_out.shape
    tm = _row_tile(T, D, 3)

    def body(dx_ref, o_ref, g_ref, dmo_ref, dg_ref):
        i = pl.program_id(0)
        dxv = dx_ref[...]
        dmo_ref[...] = (dxv * g_ref[...]).astype(dmo_ref.dtype)
        p = jnp.sum(dxv * o_ref[...].astype(F32), axis=0, keepdims=True)

        @pl.when(i == 0)
        def _():
            dg_ref[...] = p

        @pl.when(i > 0)
        def _():
            dg_ref[...] += p

    row = pl.BlockSpec((1, D), lambda i: (0, 0))
    tile = pl.BlockSpec((tm, D), lambda i: (i, 0))
    return pl.pallas_call(body, name=name, grid=(T // tm,), in_specs=[tile, tile, row], out_specs=[tile, row],
                          out_shape=[jax.ShapeDtypeStruct((T, D), BF16), jax.ShapeDtypeStruct((1, D), F32)],
                          compiler_params=_params(1))(dx_out, o, gate)


def _merge_bwd(dmerged, branches, proj, D, *, name):
    T = dmerged.shape[0]
    tm = _div(T, 1024)
    tn = HALF
    g_off = [(OFF_G + b * D) // tn for b in range(3)]

    def body(dm_ref, b0, b1, b2, g0, g1, g2, db0, db1, db2, dg0, dg1, dg2):
        dm = dm_ref[...].astype(F32)
        for b, g, db, dg in ((b0, g0, db0, dg0), (b1, g1, db1, dg1), (b2, g2, db2, dg2)):
            s = jax.nn.sigmoid(g[...].astype(F32))
            db[...] = (dm * s).astype(db.dtype)
            dg[...] = (dm * b[...].astype(F32) * s * (1.0 - s)).astype(dg.dtype)

    tile = pl.BlockSpec((tm, tn), lambda i, j: (i, j))
    g_specs = [pl.BlockSpec((tm, tn), lambda i, j, o=o: (i, o + j)) for o in g_off]
    return pl.pallas_call(body, name=name, grid=(T // tm, D // tn), in_specs=[tile] * 4 + g_specs, out_specs=[tile] * 6,
                          out_shape=[jax.ShapeDtypeStruct((T, D), BF16)] * 6,
                          compiler_params=_params(2))(dmerged, *branches, proj, proj, proj)


def _conv_out_bwd(dy, cpre, proj, *, name):
    T = dy.shape[0]
    tm = _div(T, 1024)
    tn = HALF

    def body(dy_ref, c_ref, z_ref, dc_ref, dz_ref):
        dyv = dy_ref[...].astype(F32)
        z = z_ref[...].astype(F32)
        dc_ref[...] = (dyv * _silu(z)).astype(dc_ref.dtype)
        dz_ref[...] = (dyv * c_ref[...].astype(F32) * _dsilu(z)).astype(dz_ref.dtype)

    tile = pl.BlockSpec((tm, tn), lambda i, j: (i, j))
    z_spec = pl.BlockSpec((tm, tn), lambda i, j: (i, OFF_CZ // tn + j))
    return pl.pallas_call(body, name=name, grid=(T // tm, CONV_WIDTH // tn), in_specs=[tile, tile, z_spec],
                          out_specs=[tile, tile], out_shape=[jax.ShapeDtypeStruct((T, CONV_WIDTH), BF16)] * 2,
                          compiler_params=_params(2))(dy, cpre, proj)


def _pool_mixed(ext, u, g, row0):
    w = POOL_WINDOWS[g]
    s = ext
    shift = 1
    while shift < w:
        s = s + pltpu.roll(s, shift, 0)
        shift *= 2
    tm = u.shape[0]
    t = row0 + lax.broadcasted_iota(jnp.int32, (tm, 1), 0)
    inv = 1.0 / jnp.minimum(t + 1, w).astype(F32)
    return s[POOL_HALO:, :] * inv - u, inv


def _pool_specs(T, tm):
    per = tm // POOL_HALO
    cur = lambda col: pl.BlockSpec((tm, POOL_WIDTH), lambda i, c=col: (i, c))
    prev = pl.BlockSpec((POOL_HALO, POOL_WIDTH), lambda i: (jnp.maximum(i * per - 1, 0), 0))
    return cur, prev


def _pool_fwd(proj, pool_w, scale, *, name):
    T = proj.shape[0]
    tm = _div(T, 512, 16)
    cur, prev = _pool_specs(T, tm)

    def body(u_ref, up_ref, z_ref, w_ref, sc_ref, y_ref):
        i = pl.program_id(0)
        u = u_ref[...].astype(F32)
        halo = jnp.where(i == 0, 0.0, up_ref[...].astype(F32))
        ext = jnp.concatenate([halo, u], axis=0)
        for g in range(len(POOL_WINDOWS)):
            cols = slice(g * POOL_GROUP, (g + 1) * POOL_GROUP)
            mixed, _ = _pool_mixed(ext[:, cols], u[:, cols], g, i * tm)
            p = jnp.dot(mixed.astype(BF16), w_ref[g], preferred_element_type=F32)
            y = p * sc_ref[:, cols] * _silu(z_ref[:, cols].astype(F32))
            y_ref[:, cols] = y.astype(y_ref.dtype)

    w_spec = pl.BlockSpec((len(POOL_WINDOWS), POOL_GROUP, POOL_GROUP), lambda i: (0, 0, 0))
    row = pl.BlockSpec((1, POOL_WIDTH), lambda i: (0, 0))
    return pl.pallas_call(body, name=name, grid=(T // tm,), in_specs=[cur(0), prev, cur(1), w_spec, row],
                          out_specs=pl.BlockSpec((tm, POOL_WIDTH), lambda i: (i, 0)),
                          out_shape=jax.ShapeDtypeStruct((T, POOL_WIDTH), BF16),
                          compiler_params=_params(1))(proj, proj, proj, pool_w, scale)


def _pool_bwd(proj, dy, pool_w, scale, *, name):
    T = proj.shape[0]
    tm = _div(T, 512, 16)
    cur, prev = _pool_specs(T, tm)
    n_g = len(POOL_WINDOWS)

    def body(u_ref, up_ref, z_ref, dy_ref, w_ref, sc_ref, dz_ref, dmn_ref, dsc_ref, dw_ref):
        i = pl.program_id(0)
        u = u_ref[...].astype(F32)
        halo = jnp.where(i == 0, 0.0, up_ref[...].astype(F32))
        ext = jnp.concatenate([halo, u], axis=0)
        for g in range(n_g):
            cols = slice(g * POOL_GROUP, (g + 1) * POOL_GROUP)
            mixed, inv = _pool_mixed(ext[:, cols], u[:, cols], g, i * tm)
            mixed = mixed.astype(BF16)
            w = w_ref[g]
            p = jnp.dot(mixed, w, preferred_element_type=F32)
            z = z_ref[:, cols].astype(F32)
            dyv = dy_ref[:, cols].astype(F32)
            sc = sc_ref[:, cols]
            dypre = dyv * _silu(z)
            dz_ref[:, cols] = (dyv * (p * sc) * _dsilu(z)).astype(dz_ref.dtype)
            dsc = jnp.sum(dypre * p, axis=0, keepdims=True)
            dp = (dypre * sc).astype(BF16)
            dwg = lax.dot_general(mixed, dp, (((0,), (0,)), ((), ())), preferred_element_type=F32)
            dmixed = lax.dot_general(dp, w, (((1,), (1,)), ((), ())), preferred_element_type=F32)
            dmn_ref[:, cols] = dmixed * inv

            @pl.when(i == 0)
            def _(g=g, cols=cols, dsc=dsc, dwg=dwg):
                dsc_ref[:, cols] = dsc
                dw_ref[g] = dwg

            @pl.when(i > 0)
            def _(g=g, cols=cols, dsc=dsc, dwg=dwg):
                dsc_ref[:, cols] += dsc
                dw_ref[g] += dwg

    w_spec = pl.BlockSpec((n_g, POOL_GROUP, POOL_GROUP), lambda i: (0, 0, 0))
    row = pl.BlockSpec((1, POOL_WIDTH), lambda i: (0, 0))
    tile = pl.BlockSpec((tm, POOL_WIDTH), lambda i: (i, 0))
    dw_spec = pl.BlockSpec((n_g, POOL_GROUP, POOL_GROUP), lambda i: (0, 0, 0))
    return pl.pallas_call(
        body, name=name, grid=(T // tm,), in_specs=[cur(0), prev, cur(1), tile, w_spec, row],
        out_specs=[tile, tile, row, dw_spec],
        out_shape=[jax.ShapeDtypeStruct((T, POOL_WIDTH), BF16), jax.ShapeDtypeStruct((T, POOL_WIDTH), F32),
                   jax.ShapeDtypeStruct((1, POOL_WIDTH), F32), jax.ShapeDtypeStruct((n_g, POOL_GROUP, POOL_GROUP), F32)],
        compiler_params=_params(1))(proj, proj, proj, dy, pool_w, scale)


def _pool_bwd_window(dmn, *, name):
    T = dmn.shape[0]
    tm = _div(T, 512, 16)
    per = tm // POOL_HALO
    last = T // POOL_HALO - 1
    nb = T // tm

    def body(c_ref, n_ref, du_ref):
        i = pl.program_id(0)
        cur = c_ref[...]
        nxt = jnp.where(i == nb - 1, 0.0, n_ref[...])
        ext = jnp.concatenate([cur, nxt], axis=0)
        rows = tm + POOL_HALO
        t = i * tm + lax.broadcasted_iota(jnp.int32, (tm, 1), 0)
        for g, w in enumerate(POOL_WINDOWS):
            cols = slice(g * POOL_GROUP, (g + 1) * POOL_GROUP)
            s = ext[:, cols]
            shift = 1
            while shift < w:
                s = s + pltpu.roll(s, rows - shift, 0)
                shift *= 2
            cnt = jnp.minimum(t + 1, w).astype(F32)
            du_ref[:, cols] = (s[:tm, :] - cur[:, cols] * cnt).astype(du_ref.dtype)

    tile = pl.BlockSpec((tm, POOL_WIDTH), lambda i: (i, 0))
    nxt = pl.BlockSpec((POOL_HALO, POOL_WIDTH), lambda i: (jnp.minimum((i + 1) * per, last), 0))
    return pl.pallas_call(body, name=name, grid=(nb,), in_specs=[tile, nxt], out_specs=tile,
                          out_shape=jax.ShapeDtypeStruct((T, POOL_WIDTH), BF16), compiler_params=_params(1))(dmn, dmn)


def _attn_mask(i):
    qb, keys = ATTN_Q_BLOCK, ATTN_Q_BLOCK + ATTN_HALO
    qi = lax.broadcasted_iota(jnp.int32, (qb, keys), 0) // CHUNK
    kj = lax.broadcasted_iota(jnp.int32, (qb, keys), 1) // CHUNK - WINDOW_CHUNKS
    return (kj <= qi) & (kj >= qi - WINDOW_CHUNKS) & (kj + i * (qb // CHUNK) >= 0)


def _attn_specs(T, order):
    qb = ATTN_Q_BLOCK
    per = qb // ATTN_HALO
    cur = lambda width, col: pl.BlockSpec((qb, width), lambda i, c=col: (order(i), c))
    prev = lambda col: pl.BlockSpec((ATTN_HALO, KV_WIDTH), lambda i, c=col: (jnp.maximum(order(i) * per - 1, 0), c))
    return cur, prev


def _attn_fwd(proj, sink, *, name):
    T = proj.shape[0]
    qb = ATTN_Q_BLOCK
    cur, prev = _attn_specs(T, lambda i: i)

    def body(sink_ref, q_ref, kc_ref, kp_ref, vc_ref, vp_ref, z0_ref, z1_ref, o_ref, y_ref, lse_ref):
        i = pl.program_id(0)
        q = q_ref[...].astype(BF16)
        kk = jnp.concatenate([kp_ref[...], kc_ref[...]], axis=0).astype(BF16)
        vv = jnp.concatenate([vp_ref[...], vc_ref[...]], axis=0).astype(BF16)
        mask = _attn_mask(i)
        lane = lax.broadcasted_iota(jnp.int32, (qb, 128), 1)
        lse = jnp.zeros((qb, 128), F32)
        for h in range(N_Q_HEADS):
            hs = slice(h * HEAD_DIM, (h + 1) * HEAD_DIM)
            ks = slice((h // Q_PER_KV) * HEAD_DIM, (h // Q_PER_KV + 1) * HEAD_DIM)
            s = lax.dot_general(q[:, hs], kk[:, ks], (((1,), (1,)), ((), ())), preferred_element_type=F32)
            s = jnp.where(mask, s * (HEAD_DIM ** -0.5), NEG_INF)
            sk = sink_ref[h]
            m = jnp.maximum(jnp.max(s, axis=1, keepdims=True), sk)
            p = jnp.exp(s - m)
            den = jnp.sum(p, axis=1, keepdims=True) + jnp.exp(sk - m)
            oh = jnp.dot(p.astype(BF16), vv[:, ks], preferred_element_type=F32) / den
            zr = z0_ref if h < N_Q_HEADS // 2 else z1_ref
            zs = slice((h % (N_Q_HEADS // 2)) * HEAD_DIM, (h % (N_Q_HEADS // 2) + 1) * HEAD_DIM)
            o_ref[:, hs] = oh.astype(o_ref.dtype)
            y_ref[:, hs] = (oh * _silu(zr[:, zs].astype(F32))).astype(y_ref.dtype)
            lse = jnp.where(lane == h, m + jnp.log(den), lse)
        lse_ref[...] = lse

    kcol, vcol = OFF_K // KV_WIDTH, OFF_V // KV_WIDTH
    tile = pl.BlockSpec((qb, ATTN_WIDTH), lambda i: (i, 0))
    in_specs = [pl.BlockSpec(memory_space=pltpu.SMEM), cur(ATTN_WIDTH, OFF_Q // ATTN_WIDTH), cur(KV_WIDTH, kcol), prev(kcol),
                cur(KV_WIDTH, vcol), prev(vcol), cur(HALF, OFF_AZ // HALF), cur(HALF, OFF_AZ // HALF + 1)]
    return pl.pallas_call(
        body, name=name, grid=(T // qb,), in_specs=in_specs,
        out_specs=[tile, tile, pl.BlockSpec((qb, 128), lambda i: (i, 0))],
        out_shape=[jax.ShapeDtypeStruct((T, ATTN_WIDTH), BF16), jax.ShapeDtypeStruct((T, ATTN_WIDTH), BF16),
                   jax.ShapeDtypeStruct((T, 128), F32)],
        compiler_params=_params(1))(sink, *([proj] * 7))


def _attn_bwd(proj, sink, o, lse, dy, *, name):
    T = proj.shape[0]
    qb = ATTN_Q_BLOCK
    nb = T // qb
    order = lambda i: nb - 1 - i
    cur, prev = _attn_specs(T, order)

    def body(sink_ref, q_ref, kc_ref, kp_ref, vc_ref, vp_ref, z0_ref, z1_ref, o_ref, lse_ref, dy_ref,
             dq_ref, dk_ref, dv_ref, dz_ref, dsink_ref, dk_carry, dv_carry):
        i = pl.program_id(0)
        blk = order(i)
        q = q_ref[...].astype(BF16)
        kk = jnp.concatenate([kp_ref[...], kc_ref[...]], axis=0).astype(BF16)
        vv = jnp.concatenate([vp_ref[...], vc_ref[...]], axis=0).astype(BF16)
        mask = _attn_mask(blk)
        lane = lax.broadcasted_iota(jnp.int32, (1, 128), 1)
        dsink = jnp.zeros((1, 128), F32)
        scale = HEAD_DIM ** -0.5
        for kv in range(N_KV_HEADS):
            ks = slice(kv * HEAD_DIM, (kv + 1) * HEAD_DIM)
            dk_acc = jnp.zeros((qb + ATTN_HALO, HEAD_DIM), F32)
            dv_acc = jnp.zeros((qb + ATTN_HALO, HEAD_DIM), F32)
            for h in range(kv * Q_PER_KV, (kv + 1) * Q_PER_KV):
                hs = slice(h * HEAD_DIM, (h + 1) * HEAD_DIM)
                zr = z0_ref if h < N_Q_HEADS // 2 else z1_ref
                zs = slice((h % (N_Q_HEADS // 2)) * HEAD_DIM, (h % (N_Q_HEADS // 2) + 1) * HEAD_DIM)
                z = zr[:, zs].astype(F32)
                dyh = dy_ref[:, hs].astype(F32)
                oh = o_ref[:, hs].astype(F32)
                do = dyh * _silu(z)
                dz_ref[:, hs] = (dyh * oh * _dsilu(z)).astype(dz_ref.dtype)
                drow = jnp.sum(do * oh, axis=1, keepdims=True)
                lse_h = lse_ref[:, h:h + 1]
                s = lax.dot_general(q[:, hs], kk[:, ks], (((1,), (1,)), ((), ())), preferred_element_type=F32)
                p = jnp.exp(jnp.where(mask, s * scale, NEG_INF) - lse_h)
                do_b = do.astype(BF16)
                dv_acc = dv_acc + lax.dot_general(p.astype(BF16), do_b, (((0,), (0,)), ((), ())),
                                                  preferred_element_type=F32)
                dp = lax.dot_general(do_b, vv[:, ks], (((1,), (1,)), ((), ())), preferred_element_type=F32)
                ds = (p * (dp - drow)).astype(BF16)
                dq_ref[:, hs] = (jnp.dot(ds, kk[:, ks], preferred_element_type=F32) * scale).astype(dq_ref.dtype)
                dk_acc = dk_acc + lax.dot_general(ds, q[:, hs], (((0,), (0,)), ((), ())),
                                                  preferred_element_type=F32) * scale
                p_sink = jnp.exp(sink_ref[h] - lse_h)
                dsink = jnp.where(lane == h, -jnp.sum(p_sink * drow, axis=0, keepdims=True), dsink)
            for acc, carry, out in ((dk_acc, dk_carry, dk_ref), (dv_acc, dv_carry, dv_ref)):
                tail = acc[qb:, :] + jnp.where(i == 0, 0.0, carry[:, ks])
                out[:, ks] = jnp.concatenate([acc[ATTN_HALO:qb, :], tail], axis=0).astype(out.dtype)
                carry[:, ks] = acc[:ATTN_HALO, :]

        @pl.when(i == 0)
        def _():
            dsink_ref[...] = dsink

        @pl.when(i > 0)
        def _():
            dsink_ref[...] += dsink

    kcol, vcol = OFF_K // KV_WIDTH, OFF_V // KV_WIDTH
    tile = pl.BlockSpec((qb, ATTN_WIDTH), lambda i: (order(i), 0))
    kv_tile = pl.BlockSpec((qb, KV_WIDTH), lambda i: (order(i), 0))
    lse_spec = pl.BlockSpec((qb, 128), lambda i: (order(i), 0))
    in_specs = [pl.BlockSpec(memory_space=pltpu.SMEM), cur(ATTN_WIDTH, OFF_Q // ATTN_WIDTH), cur(KV_WIDTH, kcol), prev(kcol),
                cur(KV_WIDTH, vcol), prev(vcol), cur(HALF, OFF_AZ // HALF), cur(HALF, OFF_AZ // HALF + 1),
                tile, lse_spec, tile]
    return pl.pallas_call(
        body, name=name, grid=(nb,), in_specs=in_specs,
        out_specs=[tile, kv_tile, kv_tile, tile, pl.BlockSpec((1, 128), lambda i: (0, 0))],
        out_shape=[jax.ShapeDtypeStruct((T, ATTN_WIDTH), BF16), jax.ShapeDtypeStruct((T, KV_WIDTH), BF16),
                   jax.ShapeDtypeStruct((T, KV_WIDTH), BF16), jax.ShapeDtypeStruct((T, ATTN_WIDTH), BF16),
                   jax.ShapeDtypeStruct((1, 128), F32)],
        scratch_shapes=[pltpu.VMEM((ATTN_HALO, KV_WIDTH), F32), pltpu.VMEM((ATTN_HALO, KV_WIDTH), F32)],
        compiler_params=_params(1))(sink, *([proj] * 7), o, lse, dy)


def _conv_specs(T, tm):
    per = tm // CONV_HALO
    ca, cb = OFF_CA // HALF, OFF_CB // HALF
    cur = lambda col: pl.BlockSpec((tm, HALF), lambda i, c=col: (i, c))
    prev = lambda col: pl.BlockSpec((CONV_HALO, HALF), lambda i, c=col: (jnp.maximum(i * per - 1, 0), c))
    return [cur(ca), cur(ca + 1), cur(cb), cur(cb + 1), prev(ca), prev(ca + 1), prev(cb), prev(cb + 1)]


def _conv_glu_ext(refs, i, ext_ref):
    a0, a1, b0, b1, pa0, pa1, pb0, pb1 = refs
    a = jnp.concatenate([a0[...], a1[...]], axis=1).astype(F32)
    sb = jax.nn.sigmoid(jnp.concatenate([b0[...], b1[...]], axis=1).astype(F32))
    pa = jnp.concatenate([pa0[...], pa1[...]], axis=1).astype(F32)
    pb = jnp.concatenate([pb0[...], pb1[...]], axis=1).astype(F32)
    ext_ref[:CONV_HALO, :] = jnp.where(i == 0, 0.0, pa * jax.nn.sigmoid(pb))
    ext_ref[CONV_HALO:, :] = a * sb
    return a, sb


def _conv_scratch(tm):
    return [pltpu.VMEM((tm + CONV_HALO, CONV_WIDTH), F32), pltpu.VMEM((7, tm + CONV_HALO - 8, CONV_WIDTH), F32)]


def _conv_fill_shifted(ext_ref, sh_ref):
    rows = sh_ref.shape[1]
    for b in range(1, 8):
        sh_ref[b - 1] = ext_ref[b:b + rows, :]


def _conv_window(ext_ref, sh_ref, start, tm, cols):
    a, b = divmod(start, 8)
    if b == 0:
        return ext_ref[8 * a:8 * a + tm, cols]
    return sh_ref[b - 1, 8 * a:8 * a + tm, cols]


LANES = 128


def _lane_blocks(width):
    return [slice(k, k + LANES) for k in range(0, width, LANES)]


def _conv_taps(ext_ref, sh_ref, dw_ref, out_ref, first_start, step, tm, bias_ref=None):
    for cols in _lane_blocks(CONV_WIDTH):
        y = None
        for j in range(CONV_KERNEL):
            t = dw_ref[j:j + 1, cols] * _conv_window(ext_ref, sh_ref, first_start + step * j, tm, cols)
            y = t if y is None else y + t
        out_ref[:, cols] = y if bias_ref is None else y + bias_ref[:, cols]


def _conv_fwd(proj, dw, dwb, lng, lnb, *, name):
    T = proj.shape[0]
    tm = _div(T, 256, 32)

    def body(*refs):
        dw_ref, dwb_ref, g_ref, b_ref, s_ref, yc_ref, ext_ref, sh_ref = refs[8:]
        i = pl.program_id(0)
        _conv_glu_ext(refs[:8], i, ext_ref)
        _conv_fill_shifted(ext_ref, sh_ref)
        _conv_taps(ext_ref, sh_ref, dw_ref, yc_ref, CONV_HALO - (CONV_KERNEL - 1), 1, tm, dwb_ref)
        yc = yc_ref[...]
        mu = jnp.mean(yc, axis=-1, keepdims=True)
        d = yc - mu
        rstd = lax.rsqrt(jnp.mean(d * d, axis=-1, keepdims=True) + EPS)
        s_ref[...] = _silu(d * rstd * g_ref[...] + b_ref[...]).astype(s_ref.dtype)

    row = pl.BlockSpec((1, CONV_WIDTH), lambda i: (0, 0))
    taps = pl.BlockSpec((CONV_KERNEL, CONV_WIDTH), lambda i: (0, 0))
    tile = pl.BlockSpec((tm, CONV_WIDTH), lambda i: (i, 0))
    return pl.pallas_call(
        body, name=name, grid=(T // tm,), in_specs=_conv_specs(T, tm) + [taps, row, row, row], out_specs=[tile, tile],
        out_shape=[jax.ShapeDtypeStruct((T, CONV_WIDTH), BF16), jax.ShapeDtypeStruct((T, CONV_WIDTH), F32)],
        scratch_shapes=_conv_scratch(tm), compiler_params=_params(1))(*([proj] * 8), dw, dwb, lng, lnb)


def _conv_bwd(proj, yc_saved, ds, lng, lnb, *, name):
    T = proj.shape[0]
    tm = _div(T, 256, 32)
    nb = T // tm
    sub = tm // 8

    def body(*refs):
        yc_ref, ds_ref, g_ref, b_ref, dyc_ref, ddw_ref, ddwb_ref, dg_ref, db_ref, ext_ref, sh_ref, acc_ref = refs[8:]
        i = pl.program_id(0)
        _conv_glu_ext(refs[:8], i, ext_ref)
        _conv_fill_shifted(ext_ref, sh_ref)
        yc = yc_ref[...]
        mu = jnp.mean(yc, axis=-1, keepdims=True)
        d = yc - mu
        rstd = lax.rsqrt(jnp.mean(d * d, axis=-1, keepdims=True) + EPS)
        xhat = d * rstd
        dln = ds_ref[...].astype(F32) * _dsilu(xhat * g_ref[...] + b_ref[...])
        dxhat = dln * g_ref[...]
        dyc = rstd * (dxhat - jnp.mean(dxhat, axis=-1, keepdims=True)
                      - xhat * jnp.mean(dxhat * xhat, axis=-1, keepdims=True))
        dyc_ref[...] = dyc
        first = i == 0

        def accumulate(ref, idx, val):
            @pl.when(first)
            def _():
                ref[idx] = val

            @pl.when(jnp.logical_not(first))
            def _():
                ref[idx] += val

        accumulate(dg_ref, slice(None), jnp.sum(dln * xhat, axis=0, keepdims=True))
        accumulate(db_ref, slice(None), jnp.sum(dln, axis=0, keepdims=True))
        accumulate(ddwb_ref, slice(None), jnp.sum(dyc, axis=0, keepdims=True))
        for cols in _lane_blocks(CONV_WIDTH):
            dyc_b = dyc_ref[:, cols]
            for j in range(CONV_KERNEL):
                window = _conv_window(ext_ref, sh_ref, CONV_HALO - (CONV_KERNEL - 1) + j, tm, cols)
                part = jnp.sum((dyc_b * window).reshape(sub, 8, LANES), axis=0)
                accumulate(acc_ref, (j, slice(None), cols), part)

        @pl.when(i == nb - 1)
        def _():
            for j in range(CONV_KERNEL):
                ddw_ref[j:j + 1, :] = jnp.sum(acc_ref[j], axis=0, keepdims=True)

    row = pl.BlockSpec((1, CONV_WIDTH), lambda i: (0, 0))
    taps = pl.BlockSpec((CONV_KERNEL, CONV_WIDTH), lambda i: (0, 0))
    tile = pl.BlockSpec((tm, CONV_WIDTH), lambda i: (i, 0))
    vec = jax.ShapeDtypeStruct((1, CONV_WIDTH), F32)
    return pl.pallas_call(
        body, name=name, grid=(nb,), in_specs=_conv_specs(T, tm) + [tile, tile, row, row],
        out_specs=[tile, taps, row, row, row],
        out_shape=[jax.ShapeDtypeStruct((T, CONV_WIDTH), F32), jax.ShapeDtypeStruct((CONV_KERNEL, CONV_WIDTH), F32), vec, vec, vec],
        scratch_shapes=_conv_scratch(tm) + [pltpu.VMEM((CONV_KERNEL, 8, CONV_WIDTH), F32)],
        compiler_params=_params(1))(*([proj] * 8), yc_saved, ds, lng, lnb)


def _conv_bwd_input(proj, dyc, dw, *, name):
    T = proj.shape[0]
    tm = _div(T, 256, 32)
    per = tm // CONV_HALO
    last = T // CONV_HALO - 1
    nb = T // tm
    ca, cb = OFF_CA // HALF, OFF_CB // HALF

    def body(a0, a1, b0, b1, c_ref, n_ref, dw_ref, da_ref, db_ref, ext_ref, sh_ref, dg_ref):
        i = pl.program_id(0)
        ext_ref[:tm, :] = c_ref[...]
        ext_ref[tm:, :] = jnp.where(i == nb - 1, 0.0, n_ref[...])
        _conv_fill_shifted(ext_ref, sh_ref)
        _conv_taps(ext_ref, sh_ref, dw_ref, dg_ref, CONV_KERNEL - 1, -1, tm)
        dg = dg_ref[...]
        a = jnp.concatenate([a0[...], a1[...]], axis=1).astype(F32)
        sb = jax.nn.sigmoid(jnp.concatenate([b0[...], b1[...]], axis=1).astype(F32))
        da_ref[...] = (dg * sb).astype(da_ref.dtype)
        db_ref[...] = (dg * a * sb * (1.0 - sb)).astype(db_ref.dtype)

    cur = lambda col: pl.BlockSpec((tm, HALF), lambda i, c=col: (i, c))
    tile = pl.BlockSpec((tm, CONV_WIDTH), lambda i: (i, 0))
    nxt = pl.BlockSpec((CONV_HALO, CONV_WIDTH), lambda i: (jnp.minimum((i + 1) * per, last), 0))
    taps = pl.BlockSpec((CONV_KERNEL, CONV_WIDTH), lambda i: (0, 0))
    return pl.pallas_call(
        body, name=name, grid=(nb,), in_specs=[cur(ca), cur(ca + 1), cur(cb), cur(cb + 1), tile, nxt, taps],
        out_specs=[tile, tile], out_shape=[jax.ShapeDtypeStruct((T, CONV_WIDTH), BF16)] * 2,
        scratch_shapes=_conv_scratch(tm) + [pltpu.VMEM((tm, CONV_WIDTH), F32)],
        compiler_params=_params(1))(proj, proj, proj, proj, dyc, dyc, dw)


def _ada_mod(c_all, w_ada, b_slab, *, name):
    L, D, N = w_ada.shape
    tn = _div(N, 512)

    def body(c_ref, w_ref, b_ref, o_ref):
        ca = _silu(c_ref[...]).astype(BF16)
        o_ref[...] = jnp.dot(ca, w_ref[...].astype(BF16), preferred_element_type=F32) + b_ref[...]

    return pl.pallas_call(
        body, name=name, grid=(L, N // tn),
        in_specs=[pl.BlockSpec((N_DEV, D), lambda l, j: (0, 0)), pl.BlockSpec((None, D, tn), lambda l, j: (l, 0, j)),
                  pl.BlockSpec((None, 1, tn), lambda l, j: (l, 0, j))],
        out_specs=pl.BlockSpec((None, N_DEV, tn), lambda l, j: (l, 0, j)),
        out_shape=jax.ShapeDtypeStruct((L, N_DEV, N), F32), compiler_params=_params(2))(c_all, w_ada, b_slab)


def _ada_grad(c_all_t, dmod_slab, *, name):
    D = c_all_t.shape[0]
    L, _, N = dmod_slab.shape
    tm = _div(D, 512)
    tn = _div(N, 512)

    def body(c_ref, d_ref, o_ref):
        ca = _silu(c_ref[...]).astype(BF16).astype(F32)
        dm = d_ref[...].astype(BF16).astype(F32)
        acc = None
        for b in range(N_DEV):
            t = ca[:, b:b + 1] * dm[b:b + 1, :]
            acc = t if acc is None else acc + t
        o_ref[...] = acc

    return pl.pallas_call(
        body, name=name, grid=(L, D // tm, N // tn),
        in_specs=[pl.BlockSpec((tm, N_DEV), lambda l, i, j: (i, 0)), pl.BlockSpec((None, N_DEV, tn), lambda l, i, j: (l, 0, j))],
        out_specs=pl.BlockSpec((None, tm, tn), lambda l, i, j: (l, i, j)),
        out_shape=jax.ShapeDtypeStruct((L, D, N), F32), compiler_params=_params(3))(c_all_t, dmod_slab)


def _flat_tile(R, C, n_arrays):
    cap = max(8, (20 * MIB) // (2 * n_arrays * C * 4))
    return _div(R, cap, 8) if R % 8 == 0 else R


def _adamw_math(w, g, m, v):
    m = ADAM_B1 * m + (1.0 - ADAM_B1) * g
    v = ADAM_B2 * v + (1.0 - ADAM_B2) * (g * g)
    m_hat = m / (1.0 - ADAM_B1 ** ADAM_STEP)
    v_hat = v / (1.0 - ADAM_B2 ** ADAM_STEP)
    delta = -ADAM_LR * (m_hat / (jnp.sqrt(v_hat) + ADAM_EPS) + ADAM_WD * w)
    return delta, m, v


def _adamw(w, m, v, gs, *, name):
    R, C = w.shape
    n_g = len(gs)
    tr = _flat_tile(R, C, 7 + n_g)

    def body(*refs):
        w_ref, m_ref, v_ref = refs[:3]
        g_refs = refs[3:3 + n_g]
        go_ref, d_ref, mo_ref, vo_ref = refs[3 + n_g:]
        g = g_refs[0][...]
        for r in g_refs[1:]:
            g = g + r[...]
        d, mn, vn = _adamw_math(w_ref[...], g, m_ref[...], v_ref[...])
        go_ref[...] = g
        d_ref[...] = d
        mo_ref[...] = mn
        vo_ref[...] = vn

    tile = pl.BlockSpec((tr, C), lambda i: (i, 0))
    shp = jax.ShapeDtypeStruct((R, C), F32)
    return pl.pallas_call(body, name=name, grid=(R // tr,), in_specs=[tile] * (3 + n_g), out_specs=[tile] * 4,
                          out_shape=[shp] * 4, compiler_params=_params(1, 2 * (7 + n_g) * tr * C * 4))(w, m, v, *gs)


def _adamw_layer(w, m, v, layer, gs, outs, *, name):
    _, R, C = w.shape
    n_g = len(gs)
    tr = _flat_tile(R, C, 7 + n_g)

    def body(*refs):
        w_ref, m_ref, v_ref = refs[:3]
        g_refs = refs[3:3 + n_g]
        go_ref, d_ref, mo_ref, vo_ref = refs[3 + n_g + 4:]
        g = g_refs[0][...]
        for r in g_refs[1:]:
            g = g + r[...]
        d, mn, vn = _adamw_math(w_ref[...], g, m_ref[...], v_ref[...])
        go_ref[...] = g
        d_ref[...] = d
        mo_ref[...] = mn
        vo_ref[...] = vn

    lay = pl.BlockSpec((None, tr, C), lambda i: (layer, i, 0))
    tile = pl.BlockSpec((tr, C), lambda i: (i, 0))
    return pl.pallas_call(
        body, name=name, grid=(R // tr,), in_specs=[lay] * 3 + [tile] * n_g + [ANY] * 4, out_specs=[lay] * 4,
        out_shape=[jax.ShapeDtypeStruct(o.shape, o.dtype) for o in outs],
        input_output_aliases={3 + n_g + k: k for k in range(4)},
        compiler_params=_params(1, 2 * (7 + n_g) * tr * C * 4))(w, m, v, *gs, *outs)


def _full_shape(kind, slab_shape):
    G, r, c = slab_shape
    return (G, r, c * N_CHIP) if kind == "cols" else (G, r * N_CHIP, c)


def _slab_tile(G, r, c, n_arrays):
    cap = max(16, (20 * MIB) // (2 * n_arrays * G * c * 4))
    return _div(r, cap, 16)


def _slab_block(kind, G, r, c, tr):
    if kind == "cols":
        return pl.BlockSpec((G, tr, c), lambda i, chip: (0, i, chip[0]))
    per = r // tr
    return pl.BlockSpec((G, tr, c), lambda i, chip: (0, chip[0] * per + i, 0))


def _cast_into_full(chip, w, layer, kind, *, name):
    _, G, r, c = w.shape
    tr = _slab_tile(G, r, c, 2)

    def body(chip_ref, w_ref, o_ref):
        o_ref[...] = w_ref[...].astype(BF16)

    grid_spec = pltpu.PrefetchScalarGridSpec(
        num_scalar_prefetch=1, grid=(r // tr,),
        in_specs=[pl.BlockSpec((None, G, tr, c), lambda i, chip: (layer, 0, i, 0))], out_specs=_slab_block(kind, G, r, c, tr))
    return pl.pallas_call(body, name=name, grid_spec=grid_spec,
                          out_shape=jax.ShapeDtypeStruct(_full_shape(kind, (G, r, c)), BF16),
                          compiler_params=_params(1, 4 * G * tr * c * 4))(chip, w)


def _sum_contribs(chip, full, land, kind, *, name):
    _, G, r, c = land.shape
    tr = _slab_tile(G, r, c, 5)

    def body(chip_ref, f_ref, l_ref, o_ref):
        o_ref[...] = ((f_ref[...].astype(F32) + l_ref[0].astype(F32)) + l_ref[1].astype(F32)) + l_ref[2].astype(F32)

    grid_spec = pltpu.PrefetchScalarGridSpec(
        num_scalar_prefetch=1, grid=(r // tr,),
        in_specs=[_slab_block(kind, G, r, c, tr), pl.BlockSpec((3, G, tr, c), lambda i, chip: (0, 0, i, 0))],
        out_specs=pl.BlockSpec((G, tr, c), lambda i, chip: (0, i, 0)))
    return pl.pallas_call(body, name=name, grid_spec=grid_spec, out_shape=jax.ShapeDtypeStruct((G, r, c), F32),
                          compiler_params=_params(1, 2 * 5 * G * tr * c * 4))(chip, full, land)


def _place():
    x, y, c = lax.axis_index("x"), lax.axis_index("y"), lax.axis_index("c")
    chips = [(1 - x, y), (x, 1 - y), (1 - x, 1 - y)]
    return x, y, c, chips


def _small_exchange(v, reduce, *, name):
    m_per, n = v.shape
    assert m_per % 8 == 0 and n % 128 == 0

    def body(x_ref, out_ref, *scratch):
        if reduce:
            all_ref, send_sems, recv_sems, local_sem = scratch
        else:
            all_ref = out_ref
            send_sems, recv_sems, local_sem = scratch
        x, y, c, chips = _place()
        me, sibling = (x, y, c), (x, y, 1 - c)

        def rows(px, py, pc):
            return all_ref.at[pl.ds((4 * px + 2 * py + pc) * m_per, m_per), :]

        def copy(k, block, to, src=None):
            return pltpu.make_async_remote_copy(
                src_ref=rows(*block) if src is None else src, dst_ref=rows(*block), send_sem=send_sems.at[k],
                recv_sem=recv_sems.at[k], device_id=to, device_id_type=MESH)

        mine = pltpu.make_async_copy(x_ref, rows(*me), local_sem)
        mine.start()
        first = [copy(0, me, sibling, src=x_ref)]
        first += [copy(1 + j, me, (*chip, c), src=x_ref) for j, chip in enumerate(chips)]
        for cp in first:
            cp.start()
        passed = [copy(4 + j, (*chip, c), sibling) for j, chip in enumerate(chips)]
        for j, chip in enumerate(chips):
            copy(1 + j, (*chip, c), me).wait_recv()
            passed[j].start()
        copy(0, sibling, me).wait_recv()
        for j, chip in enumerate(chips):
            copy(4 + j, (*chip, 1 - c), me).wait_recv()
        for cp in first + passed:
            cp.wait_send()
        mine.wait()
        if reduce:
            acc = all_ref[0:m_per, :]
            for d in range(1, N_DEV):
                acc = acc + all_ref[d * m_per:(d + 1) * m_per, :]
            out_ref[...] = acc

    scratch = [pltpu.SemaphoreType.DMA((7,)), pltpu.SemaphoreType.DMA((7,)), pltpu.SemaphoreType.DMA]
    if reduce:
        scratch = [pltpu.VMEM((N_DEV * m_per, n), F32)] + scratch
    out_rows = m_per if reduce else N_DEV * m_per
    return pl.pallas_call(
        body, name=name, out_shape=jax.ShapeDtypeStruct((out_rows, n), v.dtype),
        in_specs=[pl.BlockSpec(memory_space=pltpu.VMEM)], out_specs=pl.BlockSpec(memory_space=pltpu.VMEM),
        scratch_shapes=scratch,
        compiler_params=pltpu.CompilerParams(vmem_limit_bytes=int(min(VMEM_CAP_BYTES, 4 * N_DEV * m_per * n * 4 + 16 * MIB))))(v)


def _slab(kind, ref, s):
    if kind == "cols":
        w = ref.shape[2] // N_CHIP
        return ref.at[:, :, pl.ds(s * w, w)]
    w = ref.shape[1] // N_CHIP
    return ref.at[:, pl.ds(s * w, w), :]


HBM = pl.BlockSpec(memory_space=pltpu.HBM)
SEM = pl.BlockSpec(memory_space=pltpu.SEMAPHORE)
EFFECT = pltpu.SideEffectType.DATAFLOW_SIDE_EFFECTING


def _in_hbm(v):
    return pltpu.with_memory_space_constraint(v, pltpu.HBM)


def _hbm_like(arrays):
    return [pltpu.HBM(v.shape, v.dtype) for v in arrays]


def _gather_copy(kinds, full, send_sems, recv_sems, a, j, peer, c, s_src, s_dst):
    return pltpu.make_async_remote_copy(
        src_ref=_slab(kinds[a], full[a], s_src), dst_ref=_slab(kinds[a], full[a], s_dst), send_sem=send_sems.at[a * 3 + j],
        recv_sem=recv_sems.at[a * 3 + j], device_id=(*peer, c), device_id_type=MESH)


def _gather_start(fulls, kinds, after, *, name):
    n = len(fulls)

    def body(*refs):
        k = n + len(after)
        full, send_sems, recv_sems, token = refs[:n], refs[k], refs[k + 1], refs[-1]
        x, y, c, chips = _place()
        s_me = 2 * x + y
        for a in range(n):
            for j, peer in enumerate(chips):
                _gather_copy(kinds, full, send_sems, recv_sems, a, j, peer, c, s_me, s_me).start()
        token[...] = jnp.zeros_like(token)

    sems = pltpu.SemaphoreType.DMA((3 * n,))
    out = pl.pallas_call(
        body, name=name, out_shape=(sems, sems, *_hbm_like(fulls), jax.ShapeDtypeStruct((8, 128), F32)),
        in_specs=[HBM] * n + [ANY] * len(after), out_specs=(SEM, SEM, *[HBM] * n, pl.BlockSpec(memory_space=pltpu.VMEM)),
        input_output_aliases={a: 2 + a for a in range(n)},
        compiler_params=pltpu.CompilerParams(has_side_effects=EFFECT))(*[_in_hbm(f) for f in fulls], *after)
    return out[0], out[1], list(out[2:2 + n]), out[-1]


def _gather_wait(fulls, kinds, send, recv, after, *, name):
    n = len(fulls)

    def body(*refs):
        full, send_sems, recv_sems = refs[:n], refs[n], refs[n + 1]
        x, y, c, chips = _place()
        s_me = 2 * x + y
        for a in range(n):
            for j, peer in enumerate(chips):
                cp = _gather_copy(kinds, full, send_sems, recv_sems, a, j, peer, c, s_me, 2 * peer[0] + peer[1])
                cp.wait_send()
                cp.wait_recv()

    return pl.pallas_call(
        body, name=name, out_shape=_hbm_like(fulls), in_specs=[HBM] * n + [SEM, SEM] + [ANY] * len(after),
        out_specs=[HBM] * n, input_output_aliases={a: a for a in range(n)},
        compiler_params=pltpu.CompilerParams(has_side_effects=EFFECT))(*fulls, send, recv, *after)


def _scatter_copy(kinds, full, land, send_sems, recv_sems, a, j, peer, c):
    return pltpu.make_async_remote_copy(
        src_ref=_slab(kinds[a], full[a], 2 * peer[0] + peer[1]), dst_ref=land[a].at[j], send_sem=send_sems.at[a * 3 + j],
        recv_sem=recv_sems.at[a * 3 + j], device_id=(*peer, c), device_id_type=MESH)


def _scatter_start(fulls, lands, kinds, after, *, name):
    n = len(fulls)

    def body(*refs):
        k = 2 * n + len(after)
        full, land, send_sems, recv_sems, token = refs[:n], refs[n:2 * n], refs[k], refs[k + 1], refs[-1]
        _, _, c, chips = _place()
        for a in range(n):
            for j, peer in enumerate(chips):
                _scatter_copy(kinds, full, land, send_sems, recv_sems, a, j, peer, c).start()
        token[...] = jnp.zeros_like(token)

    sems = pltpu.SemaphoreType.DMA((3 * n,))
    out = pl.pallas_call(
        body, name=name,
        out_shape=(sems, sems, *_hbm_like(fulls), *_hbm_like(lands), jax.ShapeDtypeStruct((8, 128), F32)),
        in_specs=[HBM] * (2 * n) + [ANY] * len(after),
        out_specs=(SEM, SEM, *[HBM] * (2 * n), pl.BlockSpec(memory_space=pltpu.VMEM)),
        input_output_aliases={a: 2 + a for a in range(2 * n)},
        compiler_params=pltpu.CompilerParams(has_side_effects=EFFECT))(*[_in_hbm(f) for f in list(fulls) + list(lands)], *after)
    return out[0], out[1], list(out[2:2 + n]), list(out[2 + n:2 + 2 * n]), out[-1]


def _scatter_wait(fulls, lands, kinds, send, recv, after, *, name):
    n = len(fulls)

    def body(*refs):
        full, land, send_sems, recv_sems = refs[:n], refs[n:2 * n], refs[2 * n], refs[2 * n + 1]
        _, _, c, chips = _place()
        for a in range(n):
            for j, peer in enumerate(chips):
                cp = _scatter_copy(kinds, full, land, send_sems, recv_sems, a, j, peer, c)
                cp.wait_send()
                cp.wait_recv()

    out = pl.pallas_call(
        body, name=name, out_shape=_hbm_like(list(fulls) + list(lands)),
        in_specs=[HBM] * (2 * n) + [SEM, SEM] + [ANY] * len(after), out_specs=[HBM] * (2 * n),
        input_output_aliases={a: a for a in range(2 * n)},
        compiler_params=pltpu.CompilerParams(has_side_effects=EFFECT))(*fulls, *lands, send, recv, *after)
    return list(out[:n]), list(out[n:])


def _swap_copy(src, dst, send_sems, recv_sems, a):
    x, y, c, _ = _place()
    return pltpu.make_async_remote_copy(src_ref=src[a], dst_ref=dst[a], send_sem=send_sems.at[a], recv_sem=recv_sems.at[a],
                                        device_id=(x, y, 1 - c), device_id_type=MESH)


def _swap_start(parts, lands, *, name):
    n = len(parts)

    def body(*refs):
        src, dst, send_sems, recv_sems, token = refs[:n], refs[n:2 * n], refs[2 * n], refs[2 * n + 1], refs[-1]
        for a in range(n):
            _swap_copy(src, dst, send_sems, recv_sems, a).start()
        token[...] = jnp.zeros_like(token)

    sems = pltpu.SemaphoreType.DMA((n,))
    out = pl.pallas_call(
        body, name=name,
        out_shape=(sems, sems, *_hbm_like(parts), *_hbm_like(lands), jax.ShapeDtypeStruct((8, 128), F32)),
        in_specs=[HBM] * (2 * n), out_specs=(SEM, SEM, *[HBM] * (2 * n), pl.BlockSpec(memory_space=pltpu.VMEM)),
        input_output_aliases={a: 2 + a for a in range(2 * n)},
        compiler_params=pltpu.CompilerParams(has_side_effects=EFFECT))(*[_in_hbm(f) for f in list(parts) + list(lands)])
    return out[0], out[1], list(out[2:2 + n]), list(out[2 + n:2 + 2 * n]), out[-1]


def _swap_wait(parts, lands, send, recv, after, *, name):
    n = len(parts)

    def body(*refs):
        src, dst, send_sems, recv_sems = refs[:n], refs[n:2 * n], refs[2 * n], refs[2 * n + 1]
        for a in range(n):
            cp = _swap_copy(src, dst, send_sems, recv_sems, a)
            cp.wait_send()
            cp.wait_recv()

    out = pl.pallas_call(
        body, name=name, out_shape=_hbm_like(list(parts) + list(lands)),
        in_specs=[HBM] * (2 * n) + [SEM, SEM] + [ANY] * len(after), out_specs=[HBM] * (2 * n),
        input_output_aliases={a: a for a in range(2 * n)},
        compiler_params=pltpu.CompilerParams(has_side_effects=EFFECT))(*parts, *lands, send, recv, *after)
    return list(out[:n]), list(out[n:])


def _pad_rows(v, rows):
    return jnp.pad(v, ((0, rows - v.shape[0]), (0, 0)))


def _pack(vectors):
    flat = jnp.concatenate([v.reshape(-1) for v in vectors])
    n = -(-flat.shape[0] // 1024) * 1024
    return jnp.pad(flat, (0, n - flat.shape[0])).reshape(8, n // 8)


def _unpack(block, shapes):
    flat = block.reshape(-1)
    out, pos = [], 0
    for shp in shapes:
        size = 1
        for d in shp:
            size *= d
        out.append(flat[pos:pos + size].reshape(shp))
        pos += size
    return out


def kernel(x, c, norm_g, w_ada, b_ada, w_in, pool_w, pool_scale, attn_sink, conv_dw, conv_dw_b, conv_ln_g, conv_ln_b, conv_pw, w_branch_pool, w_branch_attn, w_branch_conv, w_out, final_g, loss_target, m_norm_g, m_w_ada, m_b_ada, m_w_in, m_pool_w, m_pool_scale, m_attn_sink, m_conv_dw, m_conv_dw_b, m_conv_ln_g, m_conv_ln_b, m_conv_pw, m_w_branch_pool, m_w_branch_attn, m_w_branch_conv, m_w_out, m_final_g, v_norm_g, v_w_ada, v_b_ada, v_w_in, v_pool_w, v_pool_scale, v_attn_sink, v_conv_dw, v_conv_dw_b, v_conv_ln_g, v_conv_ln_b, v_conv_pw, v_w_branch_pool, v_w_branch_attn, v_w_branch_conv, v_w_out, v_final_g):
    _, T, D = x.shape
    L = norm_g.shape[0]
    IN = w_in.shape[2] * N_CHIP
    assert IN == OFF_G + 3 * D and D % HALF == 0 and T % 512 == 0
    xi, yi, ci = lax.axis_index("x"), lax.axis_index("y"), lax.axis_index("c")
    chip = 2 * xi + yi
    dev = 2 * chip + ci
    x0 = x.reshape(T, D)
    target = loss_target.reshape(T, D)

    big = [("cols", w_in, m_w_in, v_w_in), ("cols", w_branch_pool, m_w_branch_pool, v_w_branch_pool),
           ("cols", w_branch_attn, m_w_branch_attn, v_w_branch_attn), ("cols", w_branch_conv, m_w_branch_conv, v_w_branch_conv),
           ("rows", w_out, m_w_out, v_w_out), ("rows", conv_pw, m_conv_pw, v_conv_pw), ("rows", pool_w, m_pool_w, v_pool_w)]
    kinds = [b[0] for b in big]
    n_big = len(big)
    as_groups = lambda t: t if t.ndim == 4 else t.reshape(L, 1, t.shape[1], t.shape[2])
    chip_arr = jnp.reshape(chip, (1,)).astype(jnp.int32)

    c_all = _small_exchange(_pad_rows(c, 8), False, name="gather_c")[0::8]
    taps_rows = -(-(L * CONV_KERNEL) // 8) * 8
    dw_blocks = _small_exchange(_pad_rows(conv_dw.reshape(L * CONV_KERNEL, -1), taps_rows), False, name="gather_taps")
    dw_blocks = dw_blocks.reshape(N_CHIP, 2, taps_rows, -1)[:, 0, :L * CONV_KERNEL]
    conv_dw_full = dw_blocks.reshape(N_CHIP, L, CONV_KERNEL, -1).transpose(1, 2, 0, 3).reshape(L, CONV_KERNEL, CONV_WIDTH)
    n_ada = w_ada.shape[2]
    b_slab = lax.dynamic_slice_in_dim(b_ada, chip * n_ada, n_ada, axis=1).reshape(L, 1, n_ada)
    mod_part = _ada_mod(c_all, w_ada, b_slab, name="ada_mod")
    mod_blocks = _small_exchange(mod_part.reshape(L * N_DEV, n_ada), False, name="gather_mod")
    mod_blocks = mod_blocks.reshape(N_CHIP, 2, L, N_DEV, n_ada)[:, 0]
    mod_all = mod_blocks.transpose(1, 2, 0, 3).reshape(L, N_DEV, 3 * D)
    mod = lax.dynamic_index_in_dim(mod_all, dev, axis=1, keepdims=False)
    shift, scale, gate = mod[:, :D], mod[:, D:2 * D], mod[:, 2 * D:]

    weights, gather_tokens = [], []
    for l in range(L):
        fulls = [_cast_into_full(chip_arr, as_groups(b[1]), l, b[0], name=f"cast{a}_{l}") for a, b in enumerate(big)]
        send, recv, fulls, token = _gather_start(fulls, kinds, [mod_all, conv_dw_full], name=f"gather_start{l}")
        weights.append((fulls, send, recv))
        gather_tokens.append(token[0:1, 0:1])
    started = functools.reduce(lambda p, q: p + q, gather_tokens)

    row = lambda v: v.reshape(1, -1)

    xs, saved = [x0], []
    xl = x0
    full_w = []
    for l in range(L):
        h = _norm_mod(xl, row(norm_g[l]), row(scale[l]) + started if l == 0 else row(scale[l]), row(shift[l]), name=f"norm{l}")
        fulls, send, recv = weights[l]
        win_f, wbp_f, wba_f, wbc_f, wout_f, cpw_f, poolw_f = _gather_wait(fulls, kinds, send, recv, [h], name=f"gather_wait{l}")
        full_w.append((win_f, wbp_f, wba_f, wbc_f, wout_f, cpw_f, poolw_f))
        proj = _mm(h, win_f, "nn", [F32], name=f"proj{l}", b_layer=0)
        y_pool = _pool_fwd(proj, poolw_f, row(pool_scale[l]), name=f"pool{l}")
        o_attn, y_attn, lse = _attn_fwd(proj, attn_sink[l], name=f"attn{l}")
        s_conv, yc = _conv_fwd(proj, conv_dw_full[l], row(conv_dw_b[l]), row(conv_ln_g[l]), row(conv_ln_b[l]), name=f"conv{l}")
        cpre, y_conv = _mm(s_conv, cpw_f, "nn", [BF16, BF16], name=f"conv_pw{l}", b_layer=0, tn_cap=HALF,
                           extras=[(proj, "tile", OFF_CZ)], epilogue=lambda acc, z: (acc, acc * _silu(z)))
        merged, bp, ba, bc = _merge((y_pool, y_attn, y_conv), (wbp_f, wba_f, wbc_f), proj, D, name=f"merge{l}")
        x_new, o = _mm(merged, wout_f, "nn", [F32, BF16], name=f"out{l}", b_layer=0,
                       extras=[(xl, "tile", 0), (row(gate[l]), "row", 0)],
                       epilogue=lambda acc, xv, g: (xv + g * acc, acc))
        saved.append(dict(h=h, proj=proj, y_pool=y_pool, o_attn=o_attn, y_attn=y_attn, lse=lse, s_conv=s_conv, yc=yc, cpre=cpre,
                          y_conv=y_conv, merged=merged, bp=bp, ba=ba, bc=bc, o=o))
        xl = x_new
        xs.append(xl)

    loss_part, dx, d_final_g = _final_loss(xl, target, row(final_g), name="final_loss")
    loss = lax.psum(loss_part[0, 0], ("x", "y", "c"))

    small, dmods, scattering = [], [], {}
    scattered = jnp.zeros((1, 1), F32)
    for l in reversed(range(L)):
        sv = saved[l]
        proj = sv["proj"]
        win_f, wbp_f, wba_f, wbc_f, wout_f, cpw_f, poolw_f = full_w[l]
        dmo, d_gate = _gate_out_bwd(dx, sv["o"], row(gate[l]) + scattered, name=f"gate_out_bwd{l}")
        dmerged = _mm(dmo, wout_f, "nt", [BF16], name=f"d_merged{l}", b_layer=0)
        g_wout = _mm(sv["merged"], dmo, "tn", [BF16], name=f"g_wout{l}")
        dbp, dba, dbc, dgp, dga, dgc = _merge_bwd(dmerged, (sv["bp"], sv["ba"], sv["bc"]), proj, D, name=f"merge_bwd{l}")
        dy_pool = _mm(dbp, wbp_f, "nt", [BF16], name=f"dy_pool{l}", b_layer=0)
        dy_attn = _mm(dba, wba_f, "nt", [BF16], name=f"dy_attn{l}", b_layer=0)
        dy_conv = _mm(dbc, wbc_f, "nt", [BF16], name=f"dy_conv{l}", b_layer=0)
        g_wbp = _mm(sv["y_pool"], dbp, "tn", [BF16], name=f"g_wbp{l}")
        g_wba = _mm(sv["y_attn"], dba, "tn", [BF16], name=f"g_wba{l}")
        g_wbc = _mm(sv["y_conv"], dbc, "tn", [BF16], name=f"g_wbc{l}")
        dz_pool, dmn, d_pool_scale, g_poolw = _pool_bwd(proj, dy_pool, poolw_f, row(pool_scale[l]), name=f"pool_bwd{l}")
        du_pool = _pool_bwd_window(dmn, name=f"pool_bwd_window{l}")
        dq, dk, dv, dz_attn, d_sink = _attn_bwd(proj, attn_sink[l], sv["o_attn"], sv["lse"], dy_attn, name=f"attn_bwd{l}")
        dcpre, dz_conv = _conv_out_bwd(dy_conv, sv["cpre"], proj, name=f"conv_out_bwd{l}")
        ds_conv = _mm(dcpre, cpw_f, "nt", [BF16], name=f"ds_conv{l}", b_layer=0)
        g_cpw = _mm(sv["s_conv"], dcpre, "tn", [BF16], name=f"g_cpw{l}")
        taps = conv_dw_full[l]
        dyc, d_taps, d_dwb, d_lng, d_lnb = _conv_bwd(proj, sv["yc"], ds_conv, row(conv_ln_g[l]), row(conv_ln_b[l]),
                                                    name=f"conv_bwd{l}")
        da_conv, db_conv = _conv_bwd_input(proj, dyc, taps, name=f"conv_bwd_input{l}")
        dproj = jnp.concatenate([du_pool, dz_pool, dq, dk, dv, dz_attn, da_conv, db_conv, dz_conv, dgp, dga, dgc], axis=1)
        dh = _mm(dproj, win_f, "nt", [F32], name=f"dh{l}", b_layer=0)
        g_win = _mm(sv["h"], dproj, "tn", [BF16], name=f"g_win{l}", tn_cap=1536)
        dx, d_ng, d_scale, d_shift = _norm_mod_bwd(xs[l], dh, dx, row(norm_g[l]), row(scale[l]), name=f"norm_bwd{l}")
        dmods.append(jnp.concatenate([d_shift, d_scale, d_gate], axis=1))
        small.append([d_ng, d_pool_scale, d_sink[:, :N_Q_HEADS], d_taps, d_dwb, d_lng, d_lnb])
        before_start = []
        if l == 0:
            stacked = [jnp.stack([small[L - 1 - k][q] for k in range(L)]) for q in range(len(small[0]))]
            dmod_mine = jnp.concatenate(dmods[::-1], axis=0)
            small_shapes = [s.shape for s in stacked] + [d_final_g.shape, dmod_mine.shape]
            reduced = _small_exchange(_pack(stacked + [d_final_g, dmod_mine]), True, name="reduce_small")
            dmod_all = _small_exchange(_pad_rows(dmod_mine, 8), False, name="gather_dmod").reshape(N_DEV, 8, 3 * D)[:, :L]
            before_start = [reduced, dmod_all]
        grads = [g[None] for g in (g_win, g_wbp, g_wba, g_wbc, g_wout, g_cpw)] + [g_poolw.astype(BF16)]
        lands = [lax.empty((3,) + as_groups(b[1]).shape[1:], BF16) for b in big]
        send, recv, grads, lands, token = _scatter_start(grads, lands, kinds, before_start, name=f"scatter_start{l}")
        scattering[l] = (grads, lands, send, recv)
        scattered = token[0:1, 0:1]
    grad_x = dx.reshape(1, T, D)

    r_ng, r_ps, r_sink, r_taps, r_dwb, r_lng, r_lnb, r_fg, r_bada = _unpack(reduced, small_shapes)
    g_norm_g, g_pool_scale, g_attn_sink = r_ng.reshape(L, D), r_ps.reshape(L, POOL_WIDTH), r_sink.reshape(L, N_Q_HEADS)
    g_conv_dw = lax.dynamic_slice_in_dim(r_taps, chip * (CONV_WIDTH // N_CHIP), CONV_WIDTH // N_CHIP, axis=2)
    g_dwb, g_lng, g_lnb = r_dwb.reshape(L, CONV_WIDTH), r_lng.reshape(L, CONV_WIDTH), r_lnb.reshape(L, CONV_WIDTH)
    g_final_g, g_b_ada = r_fg.reshape(D), r_bada

    dmod_slab = lax.dynamic_slice_in_dim(dmod_all, chip * n_ada, n_ada, axis=2).transpose(1, 0, 2)
    g_w_ada = _ada_grad(c_all.T, dmod_slab + scattered, name="ada_grad")

    flat = lambda t: t.reshape(-1, t.shape[-1])
    ada = [t.reshape(w_ada.shape) for t in _adamw(flat(w_ada), flat(m_w_ada), flat(v_w_ada), [flat(g_w_ada)], name="adamw_ada")]
    small_w = [norm_g, b_ada, pool_scale, attn_sink, conv_dw, conv_dw_b, conv_ln_g, conv_ln_b, final_g]
    small_m = [m_norm_g, m_b_ada, m_pool_scale, m_attn_sink, m_conv_dw, m_conv_dw_b, m_conv_ln_g, m_conv_ln_b, m_final_g]
    small_v = [v_norm_g, v_b_ada, v_pool_scale, v_attn_sink, v_conv_dw, v_conv_dw_b, v_conv_ln_g, v_conv_ln_b, v_final_g]
    small_g = [g_norm_g, g_b_ada, g_pool_scale, g_attn_sink, g_conv_dw, g_dwb, g_lng, g_lnb, g_final_g]
    sm = _adamw(_pack(small_w), _pack(small_m), _pack(small_v), [_pack(small_g)], name="adamw_small")
    shp = [t.shape for t in small_w]
    sm_g, sm_d, sm_m, sm_v = [_unpack(t, shp) for t in sm]

    stacked3 = lambda t: t.reshape(L, -1, t.shape[-1])
    two = lambda t: t.reshape(-1, t.shape[-1])
    outs = [[lax.empty(stacked3(b[1]).shape, F32) for _ in range(4)] for b in big]

    def update(l, swapping, after):
        parts, lands, send, recv = swapping
        parts, others = _swap_wait(parts, lands, send, recv, after, name=f"swap_wait{l}")
        for a, (_, w, m, v) in enumerate(big):
            outs[a] = _adamw_layer(stacked3(w), stacked3(m), stacked3(v), l, [two(parts[a]), two(others[a])], outs[a],
                                   name=f"adamw{a}_{l}")

    after = [ada[0], sm[0]]
    swapping = None
    for l in reversed(range(L)):
        if l == 0 and swapping is not None:
            update(1, swapping, after)
            after, swapping = [outs[a][0] for a in range(n_big)], None
        grads, lands, send, recv = scattering[l]
        grads, lands = _scatter_wait(grads, lands, kinds, send, recv, after, name=f"scatter_wait{l}")
        parts = [_sum_contribs(chip_arr, grads[a], lands[a], kinds[a], name=f"sum_grads{a}_{l}") for a in range(n_big)]
        send, recv, parts, lands, token = _swap_start(parts, [lax.empty(p.shape, F32) for p in parts], name=f"swap_start{l}")
        if swapping is not None:
            update(l + 1, swapping, [token])
            after = [outs[a][0] for a in range(n_big)]
        swapping = (parts, lands, send, recv)
    update(0, swapping, [outs[a][0] for a in range(n_big)] if L > 1 else [ada[0]])
    results = {a: [t.reshape(big[a][1].shape) for t in outs[a]] for a in range(n_big)}

    def leaves(k, pick):
        s = pick
        return [s[0], ada[k], s[1], results[0][k], results[6][k], s[2], s[3], s[4], s[5], s[6], s[7], results[5][k],
                results[1][k], results[2][k], results[3][k], results[4][k], s[8]]

    return (loss, grad_x, *leaves(0, sm_g), *leaves(1, sm_d), *leaves(2, sm_m), *leaves(3, sm_v))
```

```python
import functools

import jax
import jax.numpy as jnp
from jax import lax
from jax.experimental import pallas as pl
from jax.experimental.pallas import tpu as pltpu

F32 = jnp.float32
BF16 = jnp.bfloat16
MESH = pl.DeviceIdType.MESH
ANY = pl.BlockSpec(memory_space=pl.ANY)

CHUNK = 64
HEAD_DIM = 64
N_Q_HEADS = 16
N_KV_HEADS = 4
Q_PER_KV = N_Q_HEADS // N_KV_HEADS
WINDOW_CHUNKS = 2
POOL_WIDTH = 1024
POOL_WINDOWS = (2, 4, 8, 16)
POOL_GROUP = 256
ATTN_WIDTH = 1024
KV_WIDTH = 256
CONV_WIDTH = 1024
CONV_KERNEL = 31
EPS = 1e-6
OFF_U, OFF_Z, OFF_Q, OFF_K, OFF_V, OFF_AZ, OFF_CA, OFF_CB, OFF_CZ, OFF_G = (
    0, 1024, 2048, 3072, 3328, 3584, 4608, 5632, 6656, 7680)
HALF = 512
POOL_HALO = 16
CONV_HALO = 32
ATTN_Q_BLOCK = 256
ATTN_HALO = WINDOW_CHUNKS * CHUNK
NEG_INF = -1e30

ADAM_LR, ADAM_B1, ADAM_B2, ADAM_EPS, ADAM_WD, ADAM_STEP = 0.001, 0.9, 0.999, 1e-08, 0.01, 10

N_DEV = 8
N_CHIP = 4
VMEM_CAP_BYTES = 56 * 2**20
MIB = 2**20


def _div(n, cap, mult=128):
    if n <= cap:
        return n
    best = None
    for t in range(mult, cap + 1, mult):
        if n % t == 0:
            best = t
    assert best is not None, (n, cap, mult)
    return best


def _params(n_grid, vmem_bytes=None):
    kw = dict(dimension_semantics=("arbitrary",) * n_grid)
    if vmem_bytes is not None:
        kw["vmem_limit_bytes"] = int(min(max(vmem_bytes * 5 // 4 + 4 * MIB, 32 * MIB), VMEM_CAP_BYTES))
    return pltpu.CompilerParams(**kw)


def _silu(z):
    return z * jax.nn.sigmoid(z)


def _dsilu(z):
    s = jax.nn.sigmoid(z)
    return s * (1.0 + z * (1.0 - s))


def _nbytes(shape, dtype):
    n = 1
    for d in shape:
        n *= d
    return n * jnp.dtype(dtype).itemsize


def _mm(a, b, mode, out_dtypes, *, name, b_layer=None, extras=(), epilogue=None, tm_cap=1024, tn_cap=1024, tk_cap=None):
    if mode == "tn":
        K, M = a.shape
        N = b.shape[-1]
    elif mode == "nt":
        M, K = a.shape
        N = b.shape[-2]
    else:
        M, K = a.shape
        N = b.shape[-1]
    tm = _div(M, tm_cap)
    tn = _div(N, tn_cap)
    tk = _div(K, tk_cap or (1024 if mode == "tn" else 2048))
    nk = K // tk
    n_out = len(out_dtypes)
    n_ex = len(extras)
    stacked = b.ndim == 3

    def body(*refs):
        a_ref, b_ref = refs[0], refs[1]
        ex_refs = refs[2:2 + n_ex]
        pos = 2 + n_ex
        out_refs = refs[pos:pos + n_out]
        acc_ref = refs[pos + n_out] if nk > 1 else None
        k = pl.program_id(2)
        av = a_ref[...].astype(BF16)
        bv = b_ref[...].astype(BF16)
        if mode == "nn":
            p = jnp.dot(av, bv, preferred_element_type=F32)
        elif mode == "nt":
            p = lax.dot_general(av, bv, (((1,), (1,)), ((), ())), preferred_element_type=F32)
        else:
            p = lax.dot_general(av, bv, (((0,), (0,)), ((), ())), preferred_element_type=F32)

        def finish(acc):
            vals = epilogue(acc, *[r[...] for r in ex_refs]) if epilogue is not None else (acc,)
            for r, v in zip(out_refs, vals):
                r[...] = v.astype(r.dtype)

        if nk == 1:
            finish(p)
        else:
            @pl.when(k == 0)
            def _():
                acc_ref[...] = p

            @pl.when(k > 0)
            def _():
                acc_ref[...] += p

            @pl.when(k == nk - 1)
            def _():
                finish(acc_ref[...])

    if mode == "tn":
        a_spec = pl.BlockSpec((tk, tm), lambda i, j, k: (k, i))
    else:
        a_spec = pl.BlockSpec((tm, tk), lambda i, j, k: (i, k))
    if mode == "nt":
        b_blk, b_idx = (tn, tk), (lambda i, j, k: (j, k))
    else:
        b_blk, b_idx = (tk, tn), (lambda i, j, k: (k, j))
    if stacked:
        b_spec = pl.BlockSpec((None,) + b_blk, lambda i, j, k, f=b_idx: (b_layer,) + f(i, j, k))
    else:
        b_spec = pl.BlockSpec(b_blk, b_idx)
    in_specs = [a_spec, b_spec]
    operands = [a, b]
    vmem = 2 * (tm * tk * a.dtype.itemsize + tk * tn * b.dtype.itemsize) + tm * tn * 4 * 3
    for arr, kind, off in extras:
        if kind == "tile":
            assert off % tn == 0, (name, off, tn)
            in_specs.append(pl.BlockSpec((tm, tn), lambda i, j, k, o=off // tn: (i, o + j)))
        else:
            in_specs.append(pl.BlockSpec((1, tn), lambda i, j, k: (0, j)))
        operands.append(arr)
        vmem += 2 * tm * tn * arr.dtype.itemsize
    out_shape = [jax.ShapeDtypeStruct((M, N), dt) for dt in out_dtypes]
    out_specs = [pl.BlockSpec((tm, tn), lambda i, j, k: (i, j)) for _ in out_dtypes]
    vmem += sum(2 * tm * tn * jnp.dtype(dt).itemsize for dt in out_dtypes)
    outs = pl.pallas_call(
        body, name=name, grid=(M // tm, N // tn, nk), in_specs=in_specs, out_specs=out_specs, out_shape=out_shape,
        scratch_shapes=[pltpu.VMEM((tm, tn), F32)] if nk > 1 else [], compiler_params=_params(3, vmem))(*operands)
    return outs[0] if n_out == 1 else outs


def _merge(ys, wbs, proj, D, *, name):
    T = ys[0].shape[0]
    tm = _div(T, 1024)
    tn = HALF
    kw = ys[0].shape[1]
    g_off = [(OFF_G + b * D) // tn for b in range(3)]

    def body(y0, y1, y2, w0, w1, w2, g0, g1, g2, merged_ref, b0, b1, b2):
        acc = None
        for y, w, g, bo in ((y0, w0, g0, b0), (y1, w1, g1, b1), (y2, w2, g2, b2)):
            p = jnp.dot(y[...], w[...], preferred_element_type=F32)
            bo[...] = p.astype(bo.dtype)
            t = jax.nn.sigmoid(g[...].astype(F32)) * p
            acc = t if acc is None else acc + t
        merged_ref[...] = acc.astype(merged_ref.dtype)

    y_spec = pl.BlockSpec((tm, kw), lambda i, j: (i, 0))
    w_spec = pl.BlockSpec((None, kw, tn), lambda i, j: (0, 0, j))
    g_specs = [pl.BlockSpec((tm, tn), lambda i, j, o=o: (i, o + j)) for o in g_off]
    o_spec = pl.BlockSpec((tm, tn), lambda i, j: (i, j))
    vmem = 2 * (3 * tm * kw * 2 + 3 * kw * tn * 2 + 3 * tm * tn * proj.dtype.itemsize + 4 * tm * tn * 2) + 4 * tm * tn * 4
    return pl.pallas_call(
        body, name=name, grid=(T // tm, D // tn), in_specs=[y_spec] * 3 + [w_spec] * 3 + g_specs,
        out_specs=[o_spec] * 4, out_shape=[jax.ShapeDtypeStruct((T, D), BF16)] * 4,
        compiler_params=_params(2, vmem))(*ys, *wbs, proj, proj, proj)


def _row_tile(T, width, n_arrays):
    cap = max(8, (24 * MIB) // (2 * n_arrays * width * 4))
    return _div(T, min(cap, 1024), 8)


def _norm_mod(x, ng, scale, shift, *, name):
    T, D = x.shape
    tm = _row_tile(T, D, 3)

    def body(x_ref, ng_ref, sc_ref, sh_ref, h_ref):
        xv = x_ref[...]
        r = lax.rsqrt(jnp.mean(xv * xv, axis=-1, keepdims=True) + EPS)
        h = (xv * r) * ng_ref[...] * (1.0 + sc_ref[...]) + sh_ref[...]
        h_ref[...] = h.astype(h_ref.dtype)

    row = pl.BlockSpec((1, D), lambda i: (0, 0))
    tile = pl.BlockSpec((tm, D), lambda i: (i, 0))
    return pl.pallas_call(body, name=name, grid=(T // tm,), in_specs=[tile, row, row, row], out_specs=tile,
                          out_shape=jax.ShapeDtypeStruct((T, D), BF16), compiler_params=_params(1))(x, ng, scale, shift)


def _norm_mod_bwd(x, dh, dx_out, ng, scale, *, name):
    T, D = x.shape
    tm = _row_tile(T, D, 6)

    def body(x_ref, dh_ref, dxo_ref, ng_ref, sc_ref, dx_ref, dng_ref, dsc_ref, dsh_ref):
        i = pl.program_id(0)
        xv = x_ref[...]
        dh_v = dh_ref[...].astype(F32)
        r = lax.rsqrt(jnp.mean(xv * xv, axis=-1, keepdims=True) + EPS)
        xn = xv * r
        one_sc = 1.0 + sc_ref[...]
        dxn = dh_v * (ng_ref[...] * one_sc)
        dx_ref[...] = dxo_ref[...] + r * (dxn - xn * jnp.mean(dxn * xn, axis=-1, keepdims=True))
        t = dh_v * xn
        parts = (jnp.sum(t * one_sc, axis=0, keepdims=True), jnp.sum(t * ng_ref[...], axis=0, keepdims=True),
                 jnp.sum(dh_v, axis=0, keepdims=True))
        for ref, p in zip((dng_ref, dsc_ref, dsh_ref), parts):
            @pl.when(i == 0)
            def _(ref=ref, p=p):
                ref[...] = p

            @pl.when(i > 0)
            def _(ref=ref, p=p):
                ref[...] += p

    row = pl.BlockSpec((1, D), lambda i: (0, 0))
    tile = pl.BlockSpec((tm, D), lambda i: (i, 0))
    vec = jax.ShapeDtypeStruct((1, D), F32)
    return pl.pallas_call(body, name=name, grid=(T // tm,), in_specs=[tile, tile, tile, row, row],
                          out_specs=[tile, row, row, row], out_shape=[jax.ShapeDtypeStruct((T, D), F32), vec, vec, vec],
                          compiler_params=_params(1))(x, dh, dx_out, ng, scale)


def _final_loss(x, target, fg, *, name):
    T, D = x.shape
    tm = _row_tile(T, D, 4)

    def body(x_ref, t_ref, g_ref, loss_ref, dx_ref, dg_ref):
        i = pl.program_id(0)
        xv = x_ref[...]
        r = lax.rsqrt(jnp.mean(xv * xv, axis=-1, keepdims=True) + EPS)
        xn = xv * r
        err = xn * g_ref[...] - t_ref[...]
        part = 0.5 * jnp.sum(jnp.sum(err * err, axis=1, keepdims=True), axis=0, keepdims=True) / D
        dy = err / D
        dxn = dy * g_ref[...]
        dx_ref[...] = r * (dxn - xn * jnp.mean(dxn * xn, axis=-1, keepdims=True))
        dg = jnp.sum(dy * xn, axis=0, keepdims=True)

        @pl.when(i == 0)
        def _():
            loss_ref[...] = part
            dg_ref[...] = dg

        @pl.when(i > 0)
        def _():
            loss_ref[...] += part
            dg_ref[...] += dg

    row = pl.BlockSpec((1, D), lambda i: (0, 0))
    tile = pl.BlockSpec((tm, D), lambda i: (i, 0))
    one = pl.BlockSpec((1, 1), lambda i: (0, 0))
    return pl.pallas_call(
        body, name=name, grid=(T // tm,), in_specs=[tile, tile, row], out_specs=[one, tile, row],
        out_shape=[jax.ShapeDtypeStruct((1, 1), F32), jax.ShapeDtypeStruct((T, D), F32), jax.ShapeDtypeStruct((1, D), F32)],
        compiler_params=_params(1))(x, target, fg)


def _gate_out_bwd(dx_out, o, gate, *, name):
    T, D = dx_out.shape
    tm = _row_tile(T, D, 3)

    def body(dx_ref, o_ref, g_ref, dmo_ref, dg_ref):
        i = pl.program_id(0)
        dxv = dx_ref[...]
        dmo_ref[...] = (dxv * g_ref[...]).astype(dmo_ref.dtype)
        p = jnp.sum(dxv * o_ref[...].astype(F32), axis=0, keepdims=True)

        @pl.when(i == 0)
        def _():
            dg_ref[...] = p

        @pl.when(i > 0)
        def _():
            dg_ref[...] += p

    row = pl.BlockSpec((1, D), lambda i: (0, 0))
    tile = pl.BlockSpec((tm, D), lambda i: (i, 0))
    return pl.pallas_call(body, name=name, grid=(T // tm,), in_specs=[tile, tile, row], out_specs=[tile, row],
                          out_shape=[jax.ShapeDtypeStruct((T, D), BF16), jax.ShapeDtypeStruct((1, D), F32)],
                          compiler_params=_params(1))(dx_out, o, gate)


def _merge_bwd(dmerged, branches, proj, D, *, name):
    T = dmerged.shape[0]
    tm = _div(T, 1024)
    tn = HALF
    g_off = [(OFF_G + b * D) // tn for b in range(3)]

    def body(dm_ref, b0, b1, b2, g0, g1, g2, db0, db1, db2, dg0, dg1, dg2):
        dm = dm_ref[...].astype(F32)
        for b, g, db, dg in ((b0, g0, db0, dg0), (b1, g1, db1, dg1), (b2, g2, db2, dg2)):
            s = jax.nn.sigmoid(g[...].astype(F32))
            db[...] = (dm * s).astype(db.dtype)
            dg[...] = (dm * b[...].astype(F32) * s * (1.0 - s)).astype(dg.dtype)

    tile = pl.BlockSpec((tm, tn), lambda i, j: (i, j))
    g_specs = [pl.BlockSpec((tm, tn), lambda i, j, o=o: (i, o + j)) for o in g_off]
    return pl.pallas_call(body, name=name, grid=(T // tm, D // tn), in_specs=[tile] * 4 + g_specs, out_specs=[tile] * 6,
                          out_shape=[jax.ShapeDtypeStruct((T, D), BF16)] * 6,
                          compiler_params=_params(2))(dmerged, *branches, proj, proj, proj)


def _conv_out_bwd(dy, cpre, proj, *, name):
    T = dy.shape[0]
    tm = _div(T, 1024)
    tn = HALF

    def body(dy_ref, c_ref, z_ref, dc_ref, dz_ref):
        dyv = dy_ref[...].astype(F32)
        z = z_ref[...].astype(F32)
        dc_ref[...] = (dyv * _silu(z)).astype(dc_ref.dtype)
        dz_ref[...] = (dyv * c_ref[...].astype(F32) * _dsilu(z)).astype(dz_ref.dtype)

    tile = pl.BlockSpec((tm, tn), lambda i, j: (i, j))
    z_spec = pl.BlockSpec((tm, tn), lambda i, j: (i, OFF_CZ // tn + j))
    return pl.pallas_call(body, name=name, grid=(T // tm, CONV_WIDTH // tn), in_specs=[tile, tile, z_spec],
                          out_specs=[tile, tile], out_shape=[jax.ShapeDtypeStruct((T, CONV_WIDTH), BF16)] * 2,
                          compiler_params=_params(2))(dy, cpre, proj)


def _pool_mixed(ext, u, g, row0):
    w = POOL_WINDOWS[g]
    s = ext
    shift = 1
    while shift < w:
        s = s + pltpu.roll(s, shift, 0)
        shift *= 2
    tm = u.shape[0]
    t = row0 + lax.broadcasted_iota(jnp.int32, (tm, 1), 0)
    inv = 1.0 / jnp.minimum(t + 1, w).astype(F32)
    return s[POOL_HALO:, :] * inv - u, inv


def _pool_specs(T, tm):
    per = tm // POOL_HALO
    cur = lambda col: pl.BlockSpec((tm, POOL_WIDTH), lambda i, c=col: (i, c))
    prev = pl.BlockSpec((POOL_HALO, POOL_WIDTH), lambda i: (jnp.maximum(i * per - 1, 0), 0))
    return cur, prev


def _pool_fwd(proj, pool_w, scale, *, name):
    T = proj.shape[0]
    tm = _div(T, 512, 16)
    cur, prev = _pool_specs(T, tm)

    def body(u_ref, up_ref, z_ref, w_ref, sc_ref, y_ref):
        i = pl.program_id(0)
        u = u_ref[...].astype(F32)
        halo = jnp.where(i == 0, 0.0, up_ref[...].astype(F32))
        ext = jnp.concatenate([halo, u], axis=0)
        for g in range(len(POOL_WINDOWS)):
            cols = slice(g * POOL_GROUP, (g + 1) * POOL_GROUP)
            mixed, _ = _pool_mixed(ext[:, cols], u[:, cols], g, i * tm)
            p = jnp.dot(mixed.astype(BF16), w_ref[g], preferred_element_type=F32)
            y = p * sc_ref[:, cols] * _silu(z_ref[:, cols].astype(F32))
            y_ref[:, cols] = y.astype(y_ref.dtype)

    w_spec = pl.BlockSpec((len(POOL_WINDOWS), POOL_GROUP, POOL_GROUP), lambda i: (0, 0, 0))
    row = pl.BlockSpec((1, POOL_WIDTH), lambda i: (0, 0))
    return pl.pallas_call(body, name=name, grid=(T // tm,), in_specs=[cur(0), prev, cur(1), w_spec, row],
                          out_specs=pl.BlockSpec((tm, POOL_WIDTH), lambda i: (i, 0)),
                          out_shape=jax.ShapeDtypeStruct((T, POOL_WIDTH), BF16),
                          compiler_params=_params(1))(proj, proj, proj, pool_w, scale)


def _pool_bwd(proj, dy, pool_w, scale, *, name):
    T = proj.shape[0]
    tm = _div(T, 512, 16)
    cur, prev = _pool_specs(T, tm)
    n_g = len(POOL_WINDOWS)

    def body(u_ref, up_ref, z_ref, dy_ref, w_ref, sc_ref, dz_ref, dmn_ref, dsc_ref, dw_ref):
        i = pl.program_id(0)
        u = u_ref[...].astype(F32)
        halo = jnp.where(i == 0, 0.0, up_ref[...].astype(F32))
        ext = jnp.concatenate([halo, u], axis=0)
        for g in range(n_g):
            cols = slice(g * POOL_GROUP, (g + 1) * POOL_GROUP)
            mixed, inv = _pool_mixed(ext[:, cols], u[:, cols], g, i * tm)
            mixed = mixed.astype(BF16)
            w = w_ref[g]
            p = jnp.dot(mixed, w, preferred_element_type=F32)
            z = z_ref[:, cols].astype(F32)
            dyv = dy_ref[:, cols].astype(F32)
            sc = sc_ref[:, cols]
            dypre = dyv * _silu(z)
            dz_ref[:, cols] = (dyv * (p * sc) * _dsilu(z)).astype(dz_ref.dtype)
            dsc = jnp.sum(dypre * p, axis=0, keepdims=True)
            dp = (dypre * sc).astype(BF16)
            dwg = lax.dot_general(mixed, dp, (((0,), (0,)), ((), ())), preferred_element_type=F32)
            dmixed = lax.dot_general(dp, w, (((1,), (1,)), ((), ())), preferred_element_type=F32)
            dmn_ref[:, cols] = dmixed * inv

            @pl.when(i == 0)
            def _(g=g, cols=cols, dsc=dsc, dwg=dwg):
                dsc_ref[:, cols] = dsc
                dw_ref[g] = dwg

            @pl.when(i > 0)
            def _(g=g, cols=cols, dsc=dsc, dwg=dwg):
                dsc_ref[:, cols] += dsc
                dw_ref[g] += dwg

    w_spec = pl.BlockSpec((n_g, POOL_GROUP, POOL_GROUP), lambda i: (0, 0, 0))
    row = pl.BlockSpec((1, POOL_WIDTH), lambda i: (0, 0))
    tile = pl.BlockSpec((tm, POOL_WIDTH), lambda i: (i, 0))
    dw_spec = pl.BlockSpec((n_g, POOL_GROUP, POOL_GROUP), lambda i: (0, 0, 0))
    return pl.pallas_call(
        body, name=name, grid=(T // tm,), in_specs=[cur(0), prev, cur(1), tile, w_spec, row],
        out_specs=[tile, tile, row, dw_spec],
        out_shape=[jax.ShapeDtypeStruct((T, POOL_WIDTH), BF16), jax.ShapeDtypeStruct((T, POOL_WIDTH), F32),
                   jax.ShapeDtypeStruct((1, POOL_WIDTH), F32), jax.ShapeDtypeStruct((n_g, POOL_GROUP, POOL_GROUP), F32)],
        compiler_params=_params(1))(proj, proj, proj, dy, pool_w, scale)


def _pool_bwd_window(dmn, *, name):
    T = dmn.shape[0]
    tm = _div(T, 512, 16)
    per = tm // POOL_HALO
    last = T // POOL_HALO - 1
    nb = T // tm

    def body(c_ref, n_ref, du_ref):
        i = pl.program_id(0)
        cur = c_ref[...]
        nxt = jnp.where(i == nb - 1, 0.0, n_ref[...])
        ext = jnp.concatenate([cur, nxt], axis=0)
        rows = tm + POOL_HALO
        t = i * tm + lax.broadcasted_iota(jnp.int32, (tm, 1), 0)
        for g, w in enumerate(POOL_WINDOWS):
            cols = slice(g * POOL_GROUP, (g + 1) * POOL_GROUP)
            s = ext[:, cols]
            shift = 1
            while shift < w:
                s = s + pltpu.roll(s, rows - shift, 0)
                shift *= 2
            cnt = jnp.minimum(t + 1, w).astype(F32)
            du_ref[:, cols] = (s[:tm, :] - cur[:, cols] * cnt).astype(du_ref.dtype)

    tile = pl.BlockSpec((tm, POOL_WIDTH), lambda i: (i, 0))
    nxt = pl.BlockSpec((POOL_HALO, POOL_WIDTH), lambda i: (jnp.minimum((i + 1) * per, last), 0))
    return pl.pallas_call(body, name=name, grid=(nb,), in_specs=[tile, nxt], out_specs=tile,
                          out_shape=jax.ShapeDtypeStruct((T, POOL_WIDTH), BF16), compiler_params=_params(1))(dmn, dmn)


def _attn_mask(i):
    qb, keys = ATTN_Q_BLOCK, ATTN_Q_BLOCK + ATTN_HALO
    qi = lax.broadcasted_iota(jnp.int32, (qb, keys), 0) // CHUNK
    kj = lax.broadcasted_iota(jnp.int32, (qb, keys), 1) // CHUNK - WINDOW_CHUNKS
    return (kj <= qi) & (kj >= qi - WINDOW_CHUNKS) & (kj + i * (qb // CHUNK) >= 0)


def _attn_specs(T, order):
    qb = ATTN_Q_BLOCK
    per = qb // ATTN_HALO
    cur = lambda width, col: pl.BlockSpec((qb, width), lambda i, c=col: (order(i), c))
    prev = lambda col: pl.BlockSpec((ATTN_HALO, KV_WIDTH), lambda i, c=col: (jnp.maximum(order(i) * per - 1, 0), c))
    return cur, prev


def _attn_fwd(proj, sink, *, name):
    T = proj.shape[0]
    qb = ATTN_Q_BLOCK
    cur, prev = _attn_specs(T, lambda i: i)

    def body(sink_ref, q_ref, kc_ref, kp_ref, vc_ref, vp_ref, z0_ref, z1_ref, o_ref, y_ref, lse_ref):
        i = pl.program_id(0)
        q = q_ref[...].astype(BF16)
        kk = jnp.concatenate([kp_ref[...], kc_ref[...]], axis=0).astype(BF16)
        vv = jnp.concatenate([vp_ref[...], vc_ref[...]], axis=0).astype(BF16)
        mask = _attn_mask(i)
        lane = lax.broadcasted_iota(jnp.int32, (qb, 128), 1)
        lse = jnp.zeros((qb, 128), F32)
        for h in range(N_Q_HEADS):
            hs = slice(h * HEAD_DIM, (h + 1) * HEAD_DIM)
            ks = slice((h // Q_PER_KV) * HEAD_DIM, (h // Q_PER_KV + 1) * HEAD_DIM)
            s = lax.dot_general(q[:, hs], kk[:, ks], (((1,), (1,)), ((), ())), preferred_element_type=F32)
            s = jnp.where(mask, s * (HEAD_DIM ** -0.5), NEG_INF)
            sk = sink_ref[h]
            m = jnp.maximum(jnp.max(s, axis=1, keepdims=True), sk)
            p = jnp.exp(s - m)
            den = jnp.sum(p, axis=1, keepdims=True) + jnp.exp(sk - m)
            oh = jnp.dot(p.astype(BF16), vv[:, ks], preferred_element_type=F32) / den
            zr = z0_ref if h < N_Q_HEADS // 2 else z1_ref
            zs = slice((h % (N_Q_HEADS // 2)) * HEAD_DIM, (h % (N_Q_HEADS // 2) + 1) * HEAD_DIM)
            o_ref[:, hs] = oh.astype(o_ref.dtype)
            y_ref[:, hs] = (oh * _silu(zr[:, zs].astype(F32))).astype(y_ref.dtype)
            lse = jnp.where(lane == h, m + jnp.log(den), lse)
        lse_ref[...] = lse

    kcol, vcol = OFF_K // KV_WIDTH, OFF_V // KV_WIDTH
    tile = pl.BlockSpec((qb, ATTN_WIDTH), lambda i: (i, 0))
    in_specs = [pl.BlockSpec(memory_space=pltpu.SMEM), cur(ATTN_WIDTH, OFF_Q // ATTN_WIDTH), cur(KV_WIDTH, kcol), prev(kcol),
                cur(KV_WIDTH, vcol), prev(vcol), cur(HALF, OFF_AZ // HALF), cur(HALF, OFF_AZ // HALF + 1)]
    return pl.pallas_call(
        body, name=name, grid=(T // qb,), in_specs=in_specs,
        out_specs=[tile, tile, pl.BlockSpec((qb, 128), lambda i: (i, 0))],
        out_shape=[jax.ShapeDtypeStruct((T, ATTN_WIDTH), BF16), jax.ShapeDtypeStruct((T, ATTN_WIDTH), BF16),
                   jax.ShapeDtypeStruct((T, 128), F32)],
        compiler_params=_params(1))(sink, *([proj] * 7))


def _attn_bwd(proj, sink, o, lse, dy, *, name):
    T = proj.shape[0]
    qb = ATTN_Q_BLOCK
    nb = T // qb
    order = lambda i: nb - 1 - i
    cur, prev = _attn_specs(T, order)

    def body(sink_ref, q_ref, kc_ref, kp_ref, vc_ref, vp_ref, z0_ref, z1_ref, o_ref, lse_ref, dy_ref,
             dq_ref, dk_ref, dv_ref, dz_ref, dsink_ref, dk_carry, dv_carry):
        i = pl.program_id(0)
        blk = order(i)
        q = q_ref[...].astype(BF16)
        kk = jnp.concatenate([kp_ref[...], kc_ref[...]], axis=0).astype(BF16)
        vv = jnp.concatenate([vp_ref[...], vc_ref[...]], axis=0).astype(BF16)
        mask = _attn_mask(blk)
        lane = lax.broadcasted_iota(jnp.int32, (1, 128), 1)
        dsink = jnp.zeros((1, 128), F32)
        scale = HEAD_DIM ** -0.5
        for kv in range(N_KV_HEADS):
            ks = slice(kv * HEAD_DIM, (kv + 1) * HEAD_DIM)
            dk_acc = jnp.zeros((qb + ATTN_HALO, HEAD_DIM), F32)
            dv_acc = jnp.zeros((qb + ATTN_HALO, HEAD_DIM), F32)
            for h in range(kv * Q_PER_KV, (kv + 1) * Q_PER_KV):
                hs = slice(h * HEAD_DIM, (h + 1) * HEAD_DIM)
                zr = z0_ref if h < N_Q_HEADS // 2 else z1_ref
                zs = slice((h % (N_Q_HEADS // 2)) * HEAD_DIM, (h % (N_Q_HEADS // 2) + 1) * HEAD_DIM)
                z = zr[:, zs].astype(F32)
                dyh = dy_ref[:, hs].astype(F32)
                oh = o_ref[:, hs].astype(F32)
                do = dyh * _silu(z)
                dz_ref[:, hs] = (dyh * oh * _dsilu(z)).astype(dz_ref.dtype)
                drow = jnp.sum(do * oh, axis=1, keepdims=True)
                lse_h = lse_ref[:, h:h + 1]
                s = lax.dot_general(q[:, hs], kk[:, ks], (((1,), (1,)), ((), ())), preferred_element_type=F32)
                p = jnp.exp(jnp.where(mask, s * scale, NEG_INF) - lse_h)
                do_b = do.astype(BF16)
                dv_acc = dv_acc + lax.dot_general(p.astype(BF16), do_b, (((0,), (0,)), ((), ())),
                                                  preferred_element_type=F32)
                dp = lax.dot_general(do_b, vv[:, ks], (((1,), (1,)), ((), ())), preferred_element_type=F32)
                ds = (p * (dp - drow)).astype(BF16)
                dq_ref[:, hs] = (jnp.dot(ds, kk[:, ks], preferred_element_type=F32) * scale).astype(dq_ref.dtype)
                dk_acc = dk_acc + lax.dot_general(ds, q[:, hs], (((0,), (0,)), ((), ())),
                                                  preferred_element_type=F32) * scale
                p_sink = jnp.exp(sink_ref[h] - lse_h)
                dsink = jnp.where(lane == h, -jnp.sum(p_sink * drow, axis=0, keepdims=True), dsink)
            for acc, carry, out in ((dk_acc, dk_carry, dk_ref), (dv_acc, dv_carry, dv_ref)):
                tail = acc[qb:, :] + jnp.where(i == 0, 0.0, carry[:, ks])
                rows = tail if qb == ATTN_HALO else jnp.concatenate([acc[ATTN_HALO:qb, :], tail], axis=0)
                out[:, ks] = rows.astype(out.dtype)
                carry[:, ks] = acc[:ATTN_HALO, :]

        @pl.when(i == 0)
        def _():
            dsink_ref[...] = dsink

        @pl.when(i > 0)
        def _():
            dsink_ref[...] += dsink

    kcol, vcol = OFF_K // KV_WIDTH, OFF_V // KV_WIDTH
    tile = pl.BlockSpec((qb, ATTN_WIDTH), lambda i: (order(i), 0))
    kv_tile = pl.BlockSpec((qb, KV_WIDTH), lambda i: (order(i), 0))
    lse_spec = pl.BlockSpec((qb, 128), lambda i: (order(i), 0))
    in_specs = [pl.BlockSpec(memory_space=pltpu.SMEM), cur(ATTN_WIDTH, OFF_Q // ATTN_WIDTH), cur(KV_WIDTH, kcol), prev(kcol),
                cur(KV_WIDTH, vcol), prev(vcol), cur(HALF, OFF_AZ // HALF), cur(HALF, OFF_AZ // HALF + 1),
                tile, lse_spec, tile]
    return pl.pallas_call(
        body, name=name, grid=(nb,), in_specs=in_specs,
        out_specs=[tile, kv_tile, kv_tile, tile, pl.BlockSpec((1, 128), lambda i: (0, 0))],
        out_shape=[jax.ShapeDtypeStruct((T, ATTN_WIDTH), BF16), jax.ShapeDtypeStruct((T, KV_WIDTH), BF16),
                   jax.ShapeDtypeStruct((T, KV_WIDTH), BF16), jax.ShapeDtypeStruct((T, ATTN_WIDTH), BF16),
                   jax.ShapeDtypeStruct((1, 128), F32)],
        scratch_shapes=[pltpu.VMEM((ATTN_HALO, KV_WIDTH), F32), pltpu.VMEM((ATTN_HALO, KV_WIDTH), F32)],
        compiler_params=_params(1))(sink, *([proj] * 7), o, lse, dy)


def _conv_specs(T, tm):
    per = tm // CONV_HALO
    ca, cb = OFF_CA // HALF, OFF_CB // HALF
    cur = lambda col: pl.BlockSpec((tm, HALF), lambda i, c=col: (i, c))
    prev = lambda col: pl.BlockSpec((CONV_HALO, HALF), lambda i, c=col: (jnp.maximum(i * per - 1, 0), c))
    return [cur(ca), cur(ca + 1), cur(cb), cur(cb + 1), prev(ca), prev(ca + 1), prev(cb), prev(cb + 1)]


def _conv_glu_ext(refs, i, ext_ref):
    a0, a1, b0, b1, pa0, pa1, pb0, pb1 = refs
    a = jnp.concatenate([a0[...], a1[...]], axis=1).astype(F32)
    sb = jax.nn.sigmoid(jnp.concatenate([b0[...], b1[...]], axis=1).astype(F32))
    pa = jnp.concatenate([pa0[...], pa1[...]], axis=1).astype(F32)
    pb = jnp.concatenate([pb0[...], pb1[...]], axis=1).astype(F32)
    ext_ref[:CONV_HALO, :] = jnp.where(i == 0, 0.0, pa * jax.nn.sigmoid(pb))
    ext_ref[CONV_HALO:, :] = a * sb
    return a, sb


def _conv_scratch(tm):
    return [pltpu.VMEM((tm + CONV_HALO, CONV_WIDTH), F32), pltpu.VMEM((7, tm + CONV_HALO - 8, CONV_WIDTH), F32)]


def _conv_fill_shifted(ext_ref, sh_ref):
    rows = sh_ref.shape[1]
    for b in range(1, 8):
        sh_ref[b - 1] = ext_ref[b:b + rows, :]


def _conv_window(ext_ref, sh_ref, start, tm, cols):
    a, b = divmod(start, 8)
    if b == 0:
        return ext_ref[8 * a:8 * a + tm, cols]
    return sh_ref[b - 1, 8 * a:8 * a + tm, cols]


LANES = 128


def _lane_blocks(width):
    return [slice(k, k + LANES) for k in range(0, width, LANES)]


def _conv_taps(ext_ref, sh_ref, dw_ref, out_ref, first_start, step, tm, bias_ref=None):
    sub = tm // 8
    for cols in _lane_blocks(CONV_WIDTH):
        y = None
        for j in range(CONV_KERNEL):
            tap = jnp.broadcast_to(dw_ref[j:j + 1, cols], (8, LANES))
            window = _conv_window(ext_ref, sh_ref, first_start + step * j, tm, cols).reshape(sub, 8, LANES)
            t = tap * window
            y = t if y is None else y + t
        y = y.reshape(tm, LANES)
        out_ref[:, cols] = y if bias_ref is None else y + bias_ref[:, cols]


def _conv_fwd(proj, dw, dwb, lng, lnb, *, name):
    T = proj.shape[0]
    tm = _div(T, 256, 32)

    def body(*refs):
        dw_ref, dwb_ref, g_ref, b_ref, s_ref, yc_ref, ext_ref, sh_ref = refs[8:]
        i = pl.program_id(0)
        _conv_glu_ext(refs[:8], i, ext_ref)
        _conv_fill_shifted(ext_ref, sh_ref)
        _conv_taps(ext_ref, sh_ref, dw_ref, yc_ref, CONV_HALO - (CONV_KERNEL - 1), 1, tm, dwb_ref)
        yc = yc_ref[...]
        mu = jnp.mean(yc, axis=-1, keepdims=True)
        d = yc - mu
        rstd = lax.rsqrt(jnp.mean(d * d, axis=-1, keepdims=True) + EPS)
        s_ref[...] = _silu(d * rstd * g_ref[...] + b_ref[...]).astype(s_ref.dtype)

    row = pl.BlockSpec((1, CONV_WIDTH), lambda i: (0, 0))
    taps = pl.BlockSpec((CONV_KERNEL, CONV_WIDTH), lambda i: (0, 0))
    tile = pl.BlockSpec((tm, CONV_WIDTH), lambda i: (i, 0))
    return pl.pallas_call(
        body, name=name, grid=(T // tm,), in_specs=_conv_specs(T, tm) + [taps, row, row, row], out_specs=[tile, tile],
        out_shape=[jax.ShapeDtypeStruct((T, CONV_WIDTH), BF16), jax.ShapeDtypeStruct((T, CONV_WIDTH), F32)],
        scratch_shapes=_conv_scratch(tm), compiler_params=_params(1))(*([proj] * 8), dw, dwb, lng, lnb)


def _conv_bwd(proj, yc_saved, ds, lng, lnb, *, name):
    T = proj.shape[0]
    tm = _div(T, 256, 32)
    nb = T // tm
    sub = tm // 8

    def body(*refs):
        yc_ref, ds_ref, g_ref, b_ref, dyc_ref, ddw_ref, ddwb_ref, dg_ref, db_ref, ext_ref, sh_ref, acc_ref = refs[8:]
        i = pl.program_id(0)
        _conv_glu_ext(refs[:8], i, ext_ref)
        _conv_fill_shifted(ext_ref, sh_ref)
        yc = yc_ref[...]
        mu = jnp.mean(yc, axis=-1, keepdims=True)
        d = yc - mu
        rstd = lax.rsqrt(jnp.mean(d * d, axis=-1, keepdims=True) + EPS)
        xhat = d * rstd
        dln = ds_ref[...].astype(F32) * _dsilu(xhat * g_ref[...] + b_ref[...])
        dxhat = dln * g_ref[...]
        dyc = rstd * (dxhat - jnp.mean(dxhat, axis=-1, keepdims=True)
                      - xhat * jnp.mean(dxhat * xhat, axis=-1, keepdims=True))
        dyc_ref[...] = dyc
        first = i == 0

        def accumulate(ref, idx, val):
            @pl.when(first)
            def _():
                ref[idx] = val

            @pl.when(jnp.logical_not(first))
            def _():
                ref[idx] += val

        accumulate(dg_ref, slice(None), jnp.sum(dln * xhat, axis=0, keepdims=True))
        accumulate(db_ref, slice(None), jnp.sum(dln, axis=0, keepdims=True))
        accumulate(ddwb_ref, slice(None), jnp.sum(dyc, axis=0, keepdims=True))
        @pl.when(first)
        def _():
            acc_ref[...] = jnp.zeros_like(acc_ref)

        for cols in _lane_blocks(CONV_WIDTH):
            dyc_b = dyc_ref[:, cols].reshape(sub, 8, LANES)
            for j in range(CONV_KERNEL):
                window = _conv_window(ext_ref, sh_ref, CONV_HALO - (CONV_KERNEL - 1) + j, tm, cols)
                acc_ref[j, :, cols] += jnp.sum(dyc_b * window.reshape(sub, 8, LANES), axis=0)

        @pl.when(i == nb - 1)
        def _():
            for j in range(CONV_KERNEL):
                ddw_ref[j:j + 1, :] = jnp.sum(acc_ref[j], axis=0, keepdims=True)

    row = pl.BlockSpec((1, CONV_WIDTH), lambda i: (0, 0))
    taps = pl.BlockSpec((CONV_KERNEL, CONV_WIDTH), lambda i: (0, 0))
    tile = pl.BlockSpec((tm, CONV_WIDTH), lambda i: (i, 0))
    vec = jax.ShapeDtypeStruct((1, CONV_WIDTH), F32)
    return pl.pallas_call(
        body, name=name, grid=(nb,), in_specs=_conv_specs(T, tm) + [tile, tile, row, row],
        out_specs=[tile, taps, row, row, row],
        out_shape=[jax.ShapeDtypeStruct((T, CONV_WIDTH), F32), jax.ShapeDtypeStruct((CONV_KERNEL, CONV_WIDTH), F32), vec, vec, vec],
        scratch_shapes=_conv_scratch(tm) + [pltpu.VMEM((CONV_KERNEL, 8, CONV_WIDTH), F32)],
        compiler_params=_params(1))(*([proj] * 8), yc_saved, ds, lng, lnb)


def _conv_bwd_input(proj, dyc, dw, *, name):
    T = proj.shape[0]
    tm = _div(T, 256, 32)
    per = tm // CONV_HALO
    last = T // CONV_HALO - 1
    nb = T // tm
    ca, cb = OFF_CA // HALF, OFF_CB // HALF

    def body(a0, a1, b0, b1, c_ref, n_ref, dw_ref, da_ref, db_ref, ext_ref, sh_ref, dg_ref):
        i = pl.program_id(0)
        ext_ref[:tm, :] = c_ref[...]
        ext_ref[tm:, :] = jnp.where(i == nb - 1, 0.0, n_ref[...])
        _conv_fill_shifted(ext_ref, sh_ref)
        _conv_taps(ext_ref, sh_ref, dw_ref, dg_ref, CONV_KERNEL - 1, -1, tm)
        dg = dg_ref[...]
        a = jnp.concatenate([a0[...], a1[...]], axis=1).astype(F32)
        sb = jax.nn.sigmoid(jnp.concatenate([b0[...], b1[...]], axis=1).astype(F32))
        da_ref[...] = (dg * sb).astype(da_ref.dtype)
        db_ref[...] = (dg * a * sb * (1.0 - sb)).astype(db_ref.dtype)

    cur = lambda col: pl.BlockSpec((tm, HALF), lambda i, c=col: (i, c))
    tile = pl.BlockSpec((tm, CONV_WIDTH), lambda i: (i, 0))
    nxt = pl.BlockSpec((CONV_HALO, CONV_WIDTH), lambda i: (jnp.minimum((i + 1) * per, last), 0))
    taps = pl.BlockSpec((CONV_KERNEL, CONV_WIDTH), lambda i: (0, 0))
    return pl.pallas_call(
        body, name=name, grid=(nb,), in_specs=[cur(ca), cur(ca + 1), cur(cb), cur(cb + 1), tile, nxt, taps],
        out_specs=[tile, tile], out_shape=[jax.ShapeDtypeStruct((T, CONV_WIDTH), BF16)] * 2,
        scratch_shapes=_conv_scratch(tm) + [pltpu.VMEM((tm, CONV_WIDTH), F32)],
        compiler_params=_params(1))(proj, proj, proj, proj, dyc, dyc, dw)


def _ada_mod(c_all, w_ada, b_slab, *, name):
    L, D, N = w_ada.shape
    tn = _div(N, 512)

    def body(c_ref, w_ref, b_ref, o_ref):
        ca = _silu(c_ref[...]).astype(BF16)
        o_ref[...] = jnp.dot(ca, w_ref[...].astype(BF16), preferred_element_type=F32) + b_ref[...]

    return pl.pallas_call(
        body, name=name, grid=(L, N // tn),
        in_specs=[pl.BlockSpec((N_DEV, D), lambda l, j: (0, 0)), pl.BlockSpec((None, D, tn), lambda l, j: (l, 0, j)),
                  pl.BlockSpec((None, 1, tn), lambda l, j: (l, 0, j))],
        out_specs=pl.BlockSpec((None, N_DEV, tn), lambda l, j: (l, 0, j)),
        out_shape=jax.ShapeDtypeStruct((L, N_DEV, N), F32), compiler_params=_params(2))(c_all, w_ada, b_slab)


def _ada_grad(c_all_t, dmod_slab, *, name):
    D = c_all_t.shape[0]
    L, _, N = dmod_slab.shape
    tm = _div(D, 512)
    tn = _div(N, 512)

    def body(c_ref, d_ref, o_ref):
        ca = _silu(c_ref[...]).astype(BF16).astype(F32)
        dm = d_ref[...].astype(BF16).astype(F32)
        acc = None
        for b in range(N_DEV):
            t = ca[:, b:b + 1] * dm[b:b + 1, :]
            acc = t if acc is None else acc + t
        o_ref[...] = acc

    return pl.pallas_call(
        body, name=name, grid=(L, D // tm, N // tn),
        in_specs=[pl.BlockSpec((tm, N_DEV), lambda l, i, j: (i, 0)), pl.BlockSpec((None, N_DEV, tn), lambda l, i, j: (l, 0, j))],
        out_specs=pl.BlockSpec((None, tm, tn), lambda l, i, j: (l, i, j)),
        out_shape=jax.ShapeDtypeStruct((L, D, N), F32), compiler_params=_params(3))(c_all_t, dmod_slab)


def _flat_tile(R, C, n_arrays):
    cap = max(8, (20 * MIB) // (2 * n_arrays * C * 4))
    return _div(R, cap, 8) if R % 8 == 0 else R


def _adamw_math(w, g, m, v):
    m = ADAM_B1 * m + (1.0 - ADAM_B1) * g
    v = ADAM_B2 * v + (1.0 - ADAM_B2) * (g * g)
    m_hat = m / (1.0 - ADAM_B1 ** ADAM_STEP)
    v_hat = v / (1.0 - ADAM_B2 ** ADAM_STEP)
    delta = -ADAM_LR * (m_hat / (jnp.sqrt(v_hat) + ADAM_EPS) + ADAM_WD * w)
    return delta, m, v


def _adamw(w, m, v, gs, *, name):
    R, C = w.shape
    n_g = len(gs)
    tr = _flat_tile(R, C, 7 + n_g)

    def body(*refs):
        w_ref, m_ref, v_ref = refs[:3]
        g_refs = refs[3:3 + n_g]
        go_ref, d_ref, mo_ref, vo_ref = refs[3 + n_g:]
        g = g_refs[0][...]
        for r in g_refs[1:]:
            g = g + r[...]
        d, mn, vn = _adamw_math(w_ref[...], g, m_ref[...], v_ref[...])
        go_ref[...] = g
        d_ref[...] = d
        mo_ref[...] = mn
        vo_ref[...] = vn

    tile = pl.BlockSpec((tr, C), lambda i: (i, 0))
    shp = jax.ShapeDtypeStruct((R, C), F32)
    return pl.pallas_call(body, name=name, grid=(R // tr,), in_specs=[tile] * (3 + n_g), out_specs=[tile] * 4,
                          out_shape=[shp] * 4, compiler_params=_params(1, 2 * (7 + n_g) * tr * C * 4))(w, m, v, *gs)


def _adamw_layer(w, m, v, layer, gs, outs, *, name):
    _, R, C = w.shape
    n_g = len(gs)
    tr = _flat_tile(R, C, 7 + n_g)

    def body(*refs):
        w_ref, m_ref, v_ref = refs[:3]
        g_refs = refs[3:3 + n_g]
        go_ref, d_ref, mo_ref, vo_ref = refs[3 + n_g + 4:]
        g = g_refs[0][...]
        for r in g_refs[1:]:
            g = g + r[...]
        d, mn, vn = _adamw_math(w_ref[...], g, m_ref[...], v_ref[...])
        go_ref[...] = g
        d_ref[...] = d
        mo_ref[...] = mn
        vo_ref[...] = vn

    lay = pl.BlockSpec((None, tr, C), lambda i: (layer, i, 0))
    tile = pl.BlockSpec((tr, C), lambda i: (i, 0))
    return pl.pallas_call(
        body, name=name, grid=(R // tr,), in_specs=[lay] * 3 + [tile] * n_g + [ANY] * 4, out_specs=[lay] * 4,
        out_shape=[jax.ShapeDtypeStruct(o.shape, o.dtype) for o in outs],
        input_output_aliases={3 + n_g + k: k for k in range(4)},
        compiler_params=_params(1, 2 * (7 + n_g) * tr * C * 4))(w, m, v, *gs, *outs)


def _full_shape(kind, slab_shape):
    G, r, c = slab_shape
    return (G, r, c * N_CHIP) if kind == "cols" else (G, r * N_CHIP, c)


def _slab_tile(G, r, c, n_arrays):
    cap = max(16, (20 * MIB) // (2 * n_arrays * G * c * 4))
    return _div(r, cap, 16)


def _slab_block(kind, G, r, c, tr):
    if kind == "cols":
        return pl.BlockSpec((G, tr, c), lambda i, chip: (0, i, chip[0]))
    per = r // tr
    return pl.BlockSpec((G, tr, c), lambda i, chip: (0, chip[0] * per + i, 0))


def _cast_into_full(chip, w, layer, kind, *, name):
    _, G, r, c = w.shape
    tr = _slab_tile(G, r, c, 2)

    def body(chip_ref, w_ref, o_ref):
        o_ref[...] = w_ref[...].astype(BF16)

    grid_spec = pltpu.PrefetchScalarGridSpec(
        num_scalar_prefetch=1, grid=(r // tr,),
        in_specs=[pl.BlockSpec((None, G, tr, c), lambda i, chip: (layer, 0, i, 0))], out_specs=_slab_block(kind, G, r, c, tr))
    return pl.pallas_call(body, name=name, grid_spec=grid_spec,
                          out_shape=jax.ShapeDtypeStruct(_full_shape(kind, (G, r, c)), BF16),
                          compiler_params=_params(1, 4 * G * tr * c * 4))(chip, w)


def _sum_contribs(chip, full, land, kind, *, name):
    _, G, r, c = land.shape
    tr = _slab_tile(G, r, c, 5)

    def body(chip_ref, f_ref, l_ref, o_ref):
        o_ref[...] = ((f_ref[...].astype(F32) + l_ref[0].astype(F32)) + l_ref[1].astype(F32)) + l_ref[2].astype(F32)

    grid_spec = pltpu.PrefetchScalarGridSpec(
        num_scalar_prefetch=1, grid=(r // tr,),
        in_specs=[_slab_block(kind, G, r, c, tr), pl.BlockSpec((3, G, tr, c), lambda i, chip: (0, 0, i, 0))],
        out_specs=pl.BlockSpec((G, tr, c), lambda i, chip: (0, i, 0)))
    return pl.pallas_call(body, name=name, grid_spec=grid_spec, out_shape=jax.ShapeDtypeStruct((G, r, c), F32),
                          compiler_params=_params(1, 2 * 5 * G * tr * c * 4))(chip, full, land)


def _place():
    x, y, c = lax.axis_index("x"), lax.axis_index("y"), lax.axis_index("c")
    chips = [(1 - x, y), (x, 1 - y), (1 - x, 1 - y)]
    return x, y, c, chips


def _small_exchange(v, reduce, *, name):
    m_per, n = v.shape
    assert m_per % 8 == 0 and n % 128 == 0

    def body(x_ref, out_ref, *scratch):
        if reduce:
            all_ref, send_sems, recv_sems, local_sem = scratch
        else:
            all_ref = out_ref
            send_sems, recv_sems, local_sem = scratch
        x, y, c, chips = _place()
        me, sibling = (x, y, c), (x, y, 1 - c)

        def rows(px, py, pc):
            return all_ref.at[pl.ds((4 * px + 2 * py + pc) * m_per, m_per), :]

        def copy(k, block, to, src=None):
            return pltpu.make_async_remote_copy(
                src_ref=rows(*block) if src is None else src, dst_ref=rows(*block), send_sem=send_sems.at[k],
                recv_sem=recv_sems.at[k], device_id=to, device_id_type=MESH)

        mine = pltpu.make_async_copy(x_ref, rows(*me), local_sem)
        mine.start()
        first = [copy(0, me, sibling, src=x_ref)]
        first += [copy(1 + j, me, (*chip, c), src=x_ref) for j, chip in enumerate(chips)]
        for cp in first:
            cp.start()
        passed = [copy(4 + j, (*chip, c), sibling) for j, chip in enumerate(chips)]
        for j, chip in enumerate(chips):
            copy(1 + j, (*chip, c), me).wait_recv()
            passed[j].start()
        copy(0, sibling, me).wait_recv()
        for j, chip in enumerate(chips):
            copy(4 + j, (*chip, 1 - c), me).wait_recv()
        for cp in first + passed:
            cp.wait_send()
        mine.wait()
        if reduce:
            acc = all_ref[0:m_per, :]
            for d in range(1, N_DEV):
                acc = acc + all_ref[d * m_per:(d + 1) * m_per, :]
            out_ref[...] = acc

    scratch = [pltpu.SemaphoreType.DMA((7,)), pltpu.SemaphoreType.DMA((7,)), pltpu.SemaphoreType.DMA]
    if reduce:
        scratch = [pltpu.VMEM((N_DEV * m_per, n), F32)] + scratch
    out_rows = m_per if reduce else N_DEV * m_per
    return pl.pallas_call(
        body, name=name, out_shape=jax.ShapeDtypeStruct((out_rows, n), v.dtype),
        in_specs=[pl.BlockSpec(memory_space=pltpu.VMEM)], out_specs=pl.BlockSpec(memory_space=pltpu.VMEM),
        scratch_shapes=scratch,
        compiler_params=pltpu.CompilerParams(vmem_limit_bytes=int(min(VMEM_CAP_BYTES, 4 * N_DEV * m_per * n * 4 + 16 * MIB))))(v)


def _slab(kind, ref, s):
    if kind == "cols":
        w = ref.shape[2] // N_CHIP
        return ref.at[:, :, pl.ds(s * w, w)]
    w = ref.shape[1] // N_CHIP
    return ref.at[:, pl.ds(s * w, w), :]


HBM = pl.BlockSpec(memory_space=pltpu.HBM)
SEM = pl.BlockSpec(memory_space=pltpu.SEMAPHORE)
EFFECT = pltpu.SideEffectType.DATAFLOW_SIDE_EFFECTING


def _in_hbm(v):
    return pltpu.with_memory_space_constraint(v, pltpu.HBM)


def _hbm_like(arrays):
    return [pltpu.HBM(v.shape, v.dtype) for v in arrays]


def _gather_copy(kinds, full, send_sems, recv_sems, a, j, peer, c, s_src, s_dst):
    return pltpu.make_async_remote_copy(
        src_ref=_slab(kinds[a], full[a], s_src), dst_ref=_slab(kinds[a], full[a], s_dst), send_sem=send_sems.at[a * 3 + j],
        recv_sem=recv_sems.at[a * 3 + j], device_id=(*peer, c), device_id_type=MESH)


def _gather_start(fulls, kinds, after, *, name):
    n = len(fulls)

    def body(*refs):
        k = n + len(after)
        full, send_sems, recv_sems, token = refs[:n], refs[k], refs[k + 1], refs[-1]
        x, y, c, chips = _place()
        s_me = 2 * x + y
        for a in range(n):
            for j, peer in enumerate(chips):
                _gather_copy(kinds, full, send_sems, recv_sems, a, j, peer, c, s_me, s_me).start()
        token[...] = jnp.zeros_like(token)

    sems = pltpu.SemaphoreType.DMA((3 * n,))
    out = pl.pallas_call(
        body, name=name, out_shape=(sems, sems, *_hbm_like(fulls), jax.ShapeDtypeStruct((8, 128), F32)),
        in_specs=[HBM] * n + [ANY] * len(after), out_specs=(SEM, SEM, *[HBM] * n, pl.BlockSpec(memory_space=pltpu.VMEM)),
        input_output_aliases={a: 2 + a for a in range(n)},
        compiler_params=pltpu.CompilerParams(has_side_effects=EFFECT))(*[_in_hbm(f) for f in fulls], *after)
    return out[0], out[1], list(out[2:2 + n]), out[-1]


def _gather_wait(fulls, kinds, send, recv, after, *, name):
    n = len(fulls)

    def body(*refs):
        full, send_sems, recv_sems = refs[:n], refs[n], refs[n + 1]
        x, y, c, chips = _place()
        s_me = 2 * x + y
        for a in range(n):
            for j, peer in enumerate(chips):
                cp = _gather_copy(kinds, full, send_sems, recv_sems, a, j, peer, c, s_me, 2 * peer[0] + peer[1])
                cp.wait_send()
                cp.wait_recv()

    return pl.pallas_call(
        body, name=name, out_shape=_hbm_like(fulls), in_specs=[HBM] * n + [SEM, SEM] + [ANY] * len(after),
        out_specs=[HBM] * n, input_output_aliases={a: a for a in range(n)},
        compiler_params=pltpu.CompilerParams(has_side_effects=EFFECT))(*fulls, send, recv, *after)


def _scatter_copy(kinds, full, land, send_sems, recv_sems, a, j, peer, c):
    return pltpu.make_async_remote_copy(
        src_ref=_slab(kinds[a], full[a], 2 * peer[0] + peer[1]), dst_ref=land[a].at[j], send_sem=send_sems.at[a * 3 + j],
        recv_sem=recv_sems.at[a * 3 + j], device_id=(*peer, c), device_id_type=MESH)


def _scatter_start(fulls, lands, kinds, after, *, name):
    n = len(fulls)

    def body(*refs):
        k = 2 * n + len(after)
        full, land, send_sems, recv_sems, token = refs[:n], refs[n:2 * n], refs[k], refs[k + 1], refs[-1]
        _, _, c, chips = _place()
        for a in range(n):
            for j, peer in enumerate(chips):
                _scatter_copy(kinds, full, land, send_sems, recv_sems, a, j, peer, c).start()
        token[...] = jnp.zeros_like(token)

    sems = pltpu.SemaphoreType.DMA((3 * n,))
    out = pl.pallas_call(
        body, name=name,
        out_shape=(sems, sems, *_hbm_like(fulls), *_hbm_like(lands), jax.ShapeDtypeStruct((8, 128), F32)),
        in_specs=[HBM] * (2 * n) + [ANY] * len(after),
        out_specs=(SEM, SEM, *[HBM] * (2 * n), pl.BlockSpec(memory_space=pltpu.VMEM)),
        input_output_aliases={a: 2 + a for a in range(2 * n)},
        compiler_params=pltpu.CompilerParams(has_side_effects=EFFECT))(*[_in_hbm(f) for f in list(fulls) + list(lands)], *after)
    return out[0], out[1], list(out[2:2 + n]), list(out[2 + n:2 + 2 * n]), out[-1]


def _scatter_wait(fulls, lands, kinds, send, recv, after, *, name):
    n = len(fulls)

    def body(*refs):
        full, land, send_sems, recv_sems = refs[:n], refs[n:2 * n], refs[2 * n], refs[2 * n + 1]
        _, _, c, chips = _place()
        for a in range(n):
            for j, peer in enumerate(chips):
                cp = _scatter_copy(kinds, full, land, send_sems, recv_sems, a, j, peer, c)
                cp.wait_send()
                cp.wait_recv()

    out = pl.pallas_call(
        body, name=name, out_shape=_hbm_like(list(fulls) + list(lands)),
        in_specs=[HBM] * (2 * n) + [SEM, SEM] + [ANY] * len(after), out_specs=[HBM] * (2 * n),
        input_output_aliases={a: a for a in range(2 * n)},
        compiler_params=pltpu.CompilerParams(has_side_effects=EFFECT))(*fulls, *lands, send, recv, *after)
    return list(out[:n]), list(out[n:])


def _swap_copy(src, dst, send_sems, recv_sems, a):
    x, y, c, _ = _place()
    return pltpu.make_async_remote_copy(src_ref=src[a], dst_ref=dst[a], send_sem=send_sems.at[a], recv_sem=recv_sems.at[a],
                                        device_id=(x, y, 1 - c), device_id_type=MESH)


def _swap_start(parts, lands, *, name):
    n = len(parts)

    def body(*refs):
        src, dst, send_sems, recv_sems, token = refs[:n], refs[n:2 * n], refs[2 * n], refs[2 * n + 1], refs[-1]
        for a in range(n):
            _swap_copy(src, dst, send_sems, recv_sems, a).start()
        token[...] = jnp.zeros_like(token)

    sems = pltpu.SemaphoreType.DMA((n,))
    out = pl.pallas_call(
        body, name=name,
        out_shape=(sems, sems, *_hbm_like(parts), *_hbm_like(lands), jax.ShapeDtypeStruct((8, 128), F32)),
        in_specs=[HBM] * (2 * n), out_specs=(SEM, SEM, *[HBM] * (2 * n), pl.BlockSpec(memory_space=pltpu.VMEM)),
        input_output_aliases={a: 2 + a for a in range(2 * n)},
        compiler_params=pltpu.CompilerParams(has_side_effects=EFFECT))(*[_in_hbm(f) for f in list(parts) + list(lands)])
    return out[0], out[1], list(out[2:2 + n]), list(out[2 + n:2 + 2 * n]), out[-1]


def _swap_wait(parts, lands, send, recv, after, *, name):
    n = len(parts)

    def body(*refs):
        src, dst, send_sems, recv_sems = refs[:n], refs[n:2 * n], refs[2 * n], refs[2 * n + 1]
        for a in range(n):
            cp = _swap_copy(src, dst, send_sems, recv_sems, a)
            cp.wait_send()
            cp.wait_recv()

    out = pl.pallas_call(
        body, name=name, out_shape=_hbm_like(list(parts) + list(lands)),
        in_specs=[HBM] * (2 * n) + [SEM, SEM] + [ANY] * len(after), out_specs=[HBM] * (2 * n),
        input_output_aliases={a: a for a in range(2 * n)},
        compiler_params=pltpu.CompilerParams(has_side_effects=EFFECT))(*parts, *lands, send, recv, *after)
    return list(out[:n]), list(out[n:])


def _pad_rows(v, rows):
    return jnp.pad(v, ((0, rows - v.shape[0]), (0, 0)))


def _pack(vectors):
    flat = jnp.concatenate([v.reshape(-1) for v in vectors])
    n = -(-flat.shape[0] // 1024) * 1024
    return jnp.pad(flat, (0, n - flat.shape[0])).reshape(8, n // 8)


def _unpack(block, shapes):
    flat = block.reshape(-1)
    out, pos = [], 0
    for shp in shapes:
        size = 1
        for d in shp:
            size *= d
        out.append(flat[pos:pos + size].reshape(shp))
        pos += size
    return out


def kernel(x, c, norm_g, w_ada, b_ada, w_in, pool_w, pool_scale, attn_sink, conv_dw, conv_dw_b, conv_ln_g, conv_ln_b, conv_pw, w_branch_pool, w_branch_attn, w_branch_conv, w_out, final_g, loss_target, m_norm_g, m_w_ada, m_b_ada, m_w_in, m_pool_w, m_pool_scale, m_attn_sink, m_conv_dw, m_conv_dw_b, m_conv_ln_g, m_conv_ln_b, m_conv_pw, m_w_branch_pool, m_w_branch_attn, m_w_branch_conv, m_w_out, m_final_g, v_norm_g, v_w_ada, v_b_ada, v_w_in, v_pool_w, v_pool_scale, v_attn_sink, v_conv_dw, v_conv_dw_b, v_conv_ln_g, v_conv_ln_b, v_conv_pw, v_w_branch_pool, v_w_branch_attn, v_w_branch_conv, v_w_out, v_final_g):
    _, T, D = x.shape
    L = norm_g.shape[0]
    IN = w_in.shape[2] * N_CHIP
    assert IN == OFF_G + 3 * D and D % HALF == 0 and T % 512 == 0
    xi, yi, ci = lax.axis_index("x"), lax.axis_index("y"), lax.axis_index("c")
    chip = 2 * xi + yi
    dev = 2 * chip + ci
    x0 = x.reshape(T, D)
    target = loss_target.reshape(T, D)

    big = [("cols", w_in, m_w_in, v_w_in), ("cols", w_branch_pool, m_w_branch_pool, v_w_branch_pool),
           ("cols", w_branch_attn, m_w_branch_attn, v_w_branch_attn), ("cols", w_branch_conv, m_w_branch_conv, v_w_branch_conv),
           ("rows", w_out, m_w_out, v_w_out), ("rows", conv_pw, m_conv_pw, v_conv_pw), ("rows", pool_w, m_pool_w, v_pool_w)]
    kinds = [b[0] for b in big]
    n_big = len(big)
    as_groups = lambda t: t if t.ndim == 4 else t.reshape(L, 1, t.shape[1], t.shape[2])
    chip_arr = jnp.reshape(chip, (1,)).astype(jnp.int32)

    c_all = _small_exchange(_pad_rows(c, 8), False, name="gather_c")[0::8]
    taps_rows = -(-(L * CONV_KERNEL) // 8) * 8
    dw_blocks = _small_exchange(_pad_rows(conv_dw.reshape(L * CONV_KERNEL, -1), taps_rows), False, name="gather_taps")
    dw_blocks = dw_blocks.reshape(N_CHIP, 2, taps_rows, -1)[:, 0, :L * CONV_KERNEL]
    conv_dw_full = dw_blocks.reshape(N_CHIP, L, CONV_KERNEL, -1).transpose(1, 2, 0, 3).reshape(L, CONV_KERNEL, CONV_WIDTH)
    n_ada = w_ada.shape[2]
    b_slab = lax.dynamic_slice_in_dim(b_ada, chip * n_ada, n_ada, axis=1).reshape(L, 1, n_ada)
    mod_part = _ada_mod(c_all, w_ada, b_slab, name="ada_mod")
    mod_blocks = _small_exchange(mod_part.reshape(L * N_DEV, n_ada), False, name="gather_mod")
    mod_blocks = mod_blocks.reshape(N_CHIP, 2, L, N_DEV, n_ada)[:, 0]
    mod_all = mod_blocks.transpose(1, 2, 0, 3).reshape(L, N_DEV, 3 * D)
    mod = lax.dynamic_index_in_dim(mod_all, dev, axis=1, keepdims=False)
    shift, scale, gate = mod[:, :D], mod[:, D:2 * D], mod[:, 2 * D:]

    weights, gather_tokens = [], []
    for l in range(L):
        fulls = [_cast_into_full(chip_arr, as_groups(b[1]), l, b[0], name=f"cast{a}_{l}") for a, b in enumerate(big)]
        send, recv, fulls, token = _gather_start(fulls, kinds, [mod_all, conv_dw_full], name=f"gather_start{l}")
        weights.append((fulls, send, recv))
        gather_tokens.append(token[0:1, 0:1])
    started = functools.reduce(lambda p, q: p + q, gather_tokens)

    row = lambda v: v.reshape(1, -1)

    xs, saved = [x0], []
    xl = x0
    full_w = []
    for l in range(L):
        h = _norm_mod(xl, row(norm_g[l]), row(scale[l]) + started if l == 0 else row(scale[l]), row(shift[l]), name=f"norm{l}")
        fulls, send, recv = weights[l]
        win_f, wbp_f, wba_f, wbc_f, wout_f, cpw_f, poolw_f = _gather_wait(fulls, kinds, send, recv, [h], name=f"gather_wait{l}")
        full_w.append((win_f, wbp_f, wba_f, wbc_f, wout_f, cpw_f, poolw_f))
        proj = _mm(h, win_f, "nn", [BF16], name=f"proj{l}", b_layer=0, tm_cap=2048)
        y_pool = _pool_fwd(proj, poolw_f, row(pool_scale[l]), name=f"pool{l}")
        o_attn, y_attn, lse = _attn_fwd(proj, attn_sink[l], name=f"attn{l}")
        s_conv, yc = _conv_fwd(proj, conv_dw_full[l], row(conv_dw_b[l]), row(conv_ln_g[l]), row(conv_ln_b[l]), name=f"conv{l}")
        cpre, y_conv = _mm(s_conv, cpw_f, "nn", [BF16, BF16], name=f"conv_pw{l}", b_layer=0, tn_cap=HALF,
                           extras=[(proj, "tile", OFF_CZ)], epilogue=lambda acc, z: (acc, acc * _silu(z.astype(F32))))
        merged, bp, ba, bc = _merge((y_pool, y_attn, y_conv), (wbp_f, wba_f, wbc_f), proj, D, name=f"merge{l}")
        x_new, o = _mm(merged, wout_f, "nn", [F32, BF16], name=f"out{l}", b_layer=0,
                       extras=[(xl, "tile", 0), (row(gate[l]), "row", 0)],
                       epilogue=lambda acc, xv, g: (xv + g * acc, acc))
        saved.append(dict(h=h, proj=proj, y_pool=y_pool, o_attn=o_attn, y_attn=y_attn, lse=lse, s_conv=s_conv, yc=yc, cpre=cpre,
                          y_conv=y_conv, merged=merged, bp=bp, ba=ba, bc=bc, o=o))
        xl = x_new
        xs.append(xl)

    loss_part, dx, d_final_g = _final_loss(xl, target, row(final_g), name="final_loss")
    loss = lax.psum(loss_part[0, 0], ("x", "y", "c"))

    small, dmods, scattering = [], [], {}
    scattered = jnp.zeros((1, 1), F32)
    for l in reversed(range(L)):
        sv = saved[l]
        proj = sv["proj"]
        win_f, wbp_f, wba_f, wbc_f, wout_f, cpw_f, poolw_f = full_w[l]
        dmo, d_gate = _gate_out_bwd(dx, sv["o"], row(gate[l]) + scattered, name=f"gate_out_bwd{l}")
        dmerged = _mm(dmo, wout_f, "nt", [BF16], name=f"d_merged{l}", b_layer=0)
        g_wout = _mm(sv["merged"], dmo, "tn", [BF16], name=f"g_wout{l}")
        dbp, dba, dbc, dgp, dga, dgc = _merge_bwd(dmerged, (sv["bp"], sv["ba"], sv["bc"]), proj, D, name=f"merge_bwd{l}")
        dy_pool = _mm(dbp, wbp_f, "nt", [BF16], name=f"dy_pool{l}", b_layer=0)
        dy_attn = _mm(dba, wba_f, "nt", [BF16], name=f"dy_attn{l}", b_layer=0)
        dy_conv = _mm(dbc, wbc_f, "nt", [BF16], name=f"dy_conv{l}", b_layer=0)
        g_wbp = _mm(sv["y_pool"], dbp, "tn", [BF16], name=f"g_wbp{l}")
        g_wba = _mm(sv["y_attn"], dba, "tn", [BF16], name=f"g_wba{l}")
        g_wbc = _mm(sv["y_conv"], dbc, "tn", [BF16], name=f"g_wbc{l}")
        dz_pool, dmn, d_pool_scale, g_poolw = _pool_bwd(proj, dy_pool, poolw_f, row(pool_scale[l]), name=f"pool_bwd{l}")
        du_pool = _pool_bwd_window(dmn, name=f"pool_bwd_window{l}")
        dq, dk, dv, dz_attn, d_sink = _attn_bwd(proj, attn_sink[l], sv["o_attn"], sv["lse"], dy_attn, name=f"attn_bwd{l}")
        dcpre, dz_conv = _conv_out_bwd(dy_conv, sv["cpre"], proj, name=f"conv_out_bwd{l}")
        ds_conv = _mm(dcpre, cpw_f, "nt", [BF16], name=f"ds_conv{l}", b_layer=0)
        g_cpw = _mm(sv["s_conv"], dcpre, "tn", [BF16], name=f"g_cpw{l}")
        taps = conv_dw_full[l]
        dyc, d_taps, d_dwb, d_lng, d_lnb = _conv_bwd(proj, sv["yc"], ds_conv, row(conv_ln_g[l]), row(conv_ln_b[l]),
                                                    name=f"conv_bwd{l}")
        da_conv, db_conv = _conv_bwd_input(proj, dyc, taps, name=f"conv_bwd_input{l}")
        dproj = jnp.concatenate([du_pool, dz_pool, dq, dk, dv, dz_attn, da_conv, db_conv, dz_conv, dgp, dga, dgc], axis=1)
        dh = _mm(dproj, win_f, "nt", [F32], name=f"dh{l}", b_layer=0, tm_cap=2048, tk_cap=1536)
        g_win = _mm(sv["h"], dproj, "tn", [BF16], name=f"g_win{l}", tm_cap=2048, tn_cap=1536, tk_cap=1024)
        dx, d_ng, d_scale, d_shift = _norm_mod_bwd(xs[l], dh, dx, row(norm_g[l]), row(scale[l]), name=f"norm_bwd{l}")
        dmods.append(jnp.concatenate([d_shift, d_scale, d_gate], axis=1))
        small.append([d_ng, d_pool_scale, d_sink[:, :N_Q_HEADS], d_taps, d_dwb, d_lng, d_lnb])
        before_start = []
        if l == 0:
            stacked = [jnp.stack([small[L - 1 - k][q] for k in range(L)]) for q in range(len(small[0]))]
            dmod_mine = jnp.concatenate(dmods[::-1], axis=0)
            small_shapes = [s.shape for s in stacked] + [d_final_g.shape, dmod_mine.shape]
            reduced = _small_exchange(_pack(stacked + [d_final_g, dmod_mine]), True, name="reduce_small")
            dmod_all = _small_exchange(_pad_rows(dmod_mine, 8), False, name="gather_dmod").reshape(N_DEV, 8, 3 * D)[:, :L]
            before_start = [reduced, dmod_all]
        grads = [g[None] for g in (g_win, g_wbp, g_wba, g_wbc, g_wout, g_cpw)] + [g_poolw.astype(BF16)]
        lands = [lax.empty((3,) + as_groups(b[1]).shape[1:], BF16) for b in big]
        send, recv, grads, lands, token = _scatter_start(grads, lands, kinds, before_start, name=f"scatter_start{l}")
        scattering[l] = (grads, lands, send, recv)
        scattered = token[0:1, 0:1]
    grad_x = dx.reshape(1, T, D)

    r_ng, r_ps, r_sink, r_taps, r_dwb, r_lng, r_lnb, r_fg, r_bada = _unpack(reduced, small_shapes)
    g_norm_g, g_pool_scale, g_attn_sink = r_ng.reshape(L, D), r_ps.reshape(L, POOL_WIDTH), r_sink.reshape(L, N_Q_HEADS)
    g_conv_dw = lax.dynamic_slice_in_dim(r_taps, chip * (CONV_WIDTH // N_CHIP), CONV_WIDTH // N_CHIP, axis=2)
    g_dwb, g_lng, g_lnb = r_dwb.reshape(L, CONV_WIDTH), r_lng.reshape(L, CONV_WIDTH), r_lnb.reshape(L, CONV_WIDTH)
    g_final_g, g_b_ada = r_fg.reshape(D), r_bada

    dmod_slab = lax.dynamic_slice_in_dim(dmod_all, chip * n_ada, n_ada, axis=2).transpose(1, 0, 2)
    g_w_ada = _ada_grad(c_all.T, dmod_slab + scattered, name="ada_grad")

    flat = lambda t: t.reshape(-1, t.shape[-1])
    ada = [t.reshape(w_ada.shape) for t in _adamw(flat(w_ada), flat(m_w_ada), flat(v_w_ada), [flat(g_w_ada)], name="adamw_ada")]
    small_w = [norm_g, b_ada, pool_scale, attn_sink, conv_dw, conv_dw_b, conv_ln_g, conv_ln_b, final_g]
    small_m = [m_norm_g, m_b_ada, m_pool_scale, m_attn_sink, m_conv_dw, m_conv_dw_b, m_conv_ln_g, m_conv_ln_b, m_final_g]
    small_v = [v_norm_g, v_b_ada, v_pool_scale, v_attn_sink, v_conv_dw, v_conv_dw_b, v_conv_ln_g, v_conv_ln_b, v_final_g]
    small_g = [g_norm_g, g_b_ada, g_pool_scale, g_attn_sink, g_conv_dw, g_dwb, g_lng, g_lnb, g_final_g]
    sm = _adamw(_pack(small_w), _pack(small_m), _pack(small_v), [_pack(small_g)], name="adamw_small")
    shp = [t.shape for t in small_w]
    sm_g, sm_d, sm_m, sm_v = [_unpack(t, shp) for t in sm]

    stacked3 = lambda t: t.reshape(L, -1, t.shape[-1])
    two = lambda t: t.reshape(-1, t.shape[-1])
    outs = [[lax.empty(stacked3(b[1]).shape, F32) for _ in range(4)] for b in big]

    def update(l, swapping, after):
        parts, lands, send, recv = swapping
        parts, others = _swap_wait(parts, lands, send, recv, after, name=f"swap_wait{l}")
        for a, (_, w, m, v) in enumerate(big):
            outs[a] = _adamw_layer(stacked3(w), stacked3(m), stacked3(v), l, [two(parts[a]), two(others[a])], outs[a],
                                   name=f"adamw{a}_{l}")

    after = [ada[0], sm[0]]
    swapping = None
    for l in reversed(range(L)):
        if l == 0 and swapping is not None:
            update(1, swapping, after)
            after, swapping = [outs[a][0] for a in range(n_big)], None
        grads, lands, send, recv = scattering[l]
        grads, lands = _scatter_wait(grads, lands, kinds, send, recv, after, name=f"scatter_wait{l}")
        parts = [_sum_contribs(chip_arr, grads[a], lands[a], kinds[a], name=f"sum_grads{a}_{l}") for a in range(n_big)]
        send, recv, parts, lands, token = _swap_start(parts, [lax.empty(p.shape, F32) for p in parts], name=f"swap_start{l}")
        if swapping is not None:
            update(l + 1, swapping, [token])
            after = [outs[a][0] for a in range(n_big)]
        swapping = (parts, lands, send, recv)
    update(0, swapping, [outs[a][0] for a in range(n_big)] if L > 1 else [ada[0]])
    results = {a: [t.reshape(big[a][1].shape) for t in outs[a]] for a in range(n_big)}

    def leaves(k, pick):
        s = pick
        return [s[0], ada[k], s[1], results[0][k], results[6][k], s[2], s[3], s[4], s[5], s[6], s[7], results[5][k],
                results[1][k], results[2][k], results[3][k], results[4][k], s[8]]

    return (loss, grad_x, *leaves(0, sm_g), *leaves(1, sm_d), *leaves(2, sm_m), *leaves(3, sm_v))
```

```python
import functools

import jax
import jax.numpy as jnp
from jax import lax
from jax.experimental import pallas as pl
from jax.experimental.pallas import tpu as pltpu

F32 = jnp.float32
BF16 = jnp.bfloat16
MESH = pl.DeviceIdType.MESH
ANY = pl.BlockSpec(memory_space=pl.ANY)

CHUNK = 64
HEAD_DIM = 64
N_Q_HEADS = 16
N_KV_HEADS = 4
Q_PER_KV = N_Q_HEADS // N_KV_HEADS
WINDOW_CHUNKS = 2
POOL_WIDTH = 1024
POOL_WINDOWS = (2, 4, 8, 16)
POOL_GROUP = 256
ATTN_WIDTH = 1024
KV_WIDTH = 256
CONV_WIDTH = 1024
CONV_KERNEL = 31
EPS = 1e-6
OFF_U, OFF_Z, OFF_Q, OFF_K, OFF_V, OFF_AZ, OFF_CA, OFF_CB, OFF_CZ, OFF_G = (
    0, 1024, 2048, 3072, 3328, 3584, 4608, 5632, 6656, 7680)
HALF = 512
POOL_HALO = 16
CONV_HALO = 32
ATTN_Q_BLOCK = 256
ATTN_HALO = WINDOW_CHUNKS * CHUNK
NEG_INF = -1e30

ADAM_LR, ADAM_B1, ADAM_B2, ADAM_EPS, ADAM_WD, ADAM_STEP = 0.001, 0.9, 0.999, 1e-08, 0.01, 10

N_DEV = 8
N_CHIP = 4
VMEM_CAP_BYTES = 56 * 2**20
MIB = 2**20


def _div(n, cap, mult=128):
    if n <= cap:
        return n
    best = None
    for t in range(mult, cap + 1, mult):
        if n % t == 0:
            best = t
    assert best is not None, (n, cap, mult)
    return best


def _params(n_grid, vmem_bytes=None):
    kw = dict(dimension_semantics=("arbitrary",) * n_grid)
    if vmem_bytes is not None:
        kw["vmem_limit_bytes"] = int(min(max(vmem_bytes * 5 // 4 + 4 * MIB, 32 * MIB), VMEM_CAP_BYTES))
    return pltpu.CompilerParams(**kw)


def _silu(z):
    return z * jax.nn.sigmoid(z)


def _dsilu(z):
    s = jax.nn.sigmoid(z)
    return s * (1.0 + z * (1.0 - s))


def _nbytes(shape, dtype):
    n = 1
    for d in shape:
        n *= d
    return n * jnp.dtype(dtype).itemsize


MM_VMEM_BUDGET = 50 * MIB


def _mm(a, b, mode, out_dtypes, *, name, b_layer=None, extras=(), epilogue=None, tm_cap=2048, tn_cap=1024, tk_cap=None):
    if mode == "tn":
        K, M = a.shape
        N = b.shape[-1]
    elif mode == "nt":
        M, K = a.shape
        N = b.shape[-2]
    else:
        M, K = a.shape
        N = b.shape[-1]
    tk = _div(K, tk_cap or 2048)
    nk = K // tk
    n_out = len(out_dtypes)
    n_ex = len(extras)
    stacked = b.ndim == 3
    per_elem = sum(jnp.dtype(dt).itemsize for dt in out_dtypes) + sum(e[0].dtype.itemsize for e in extras if e[1] == "tile")

    def need(tm, tn):
        return (2 * (tm * tk * a.dtype.itemsize + tk * tn * b.dtype.itemsize + tm * tn * per_elem)
                + tm * tn * 4 * (2 if nk > 1 else 1))

    tm, tn = min(((_div(M, mc), _div(N, nc)) for mc in (tm_cap, tm_cap // 2, tm_cap // 4) for nc in (tn_cap, tn_cap // 2)),
                 key=lambda t: (need(*t) > MM_VMEM_BUDGET, -t[0], -t[1]))

    def body(*refs):
        a_ref, b_ref = refs[0], refs[1]
        ex_refs = refs[2:2 + n_ex]
        pos = 2 + n_ex
        out_refs = refs[pos:pos + n_out]
        acc_ref = refs[pos + n_out] if nk > 1 else None
        k = pl.program_id(2)
        av = a_ref[...].astype(BF16)
        bv = b_ref[...].astype(BF16)
        if mode == "nn":
            p = jnp.dot(av, bv, preferred_element_type=F32)
        elif mode == "nt":
            p = lax.dot_general(av, bv, (((1,), (1,)), ((), ())), preferred_element_type=F32)
        else:
            p = lax.dot_general(av, bv, (((0,), (0,)), ((), ())), preferred_element_type=F32)

        def finish(acc):
            vals = epilogue(acc, *[r[...] for r in ex_refs]) if epilogue is not None else (acc,)
            for r, v in zip(out_refs, vals):
                r[...] = v.astype(r.dtype)

        if nk == 1:
            finish(p)
        else:
            @pl.when(k == 0)
            def _():
                acc_ref[...] = p

            @pl.when(k > 0)
            def _():
                acc_ref[...] += p

            @pl.when(k == nk - 1)
            def _():
                finish(acc_ref[...])

    if mode == "tn":
        a_spec = pl.BlockSpec((tk, tm), lambda i, j, k: (k, i))
    else:
        a_spec = pl.BlockSpec((tm, tk), lambda i, j, k: (i, k))
    if mode == "nt":
        b_blk, b_idx = (tn, tk), (lambda i, j, k: (j, k))
    else:
        b_blk, b_idx = (tk, tn), (lambda i, j, k: (k, j))
    if stacked:
        b_spec = pl.BlockSpec((None,) + b_blk, lambda i, j, k, f=b_idx: (b_layer,) + f(i, j, k))
    else:
        b_spec = pl.BlockSpec(b_blk, b_idx)
    in_specs = [a_spec, b_spec]
    operands = [a, b]
    vmem = 2 * (tm * tk * a.dtype.itemsize + tk * tn * b.dtype.itemsize) + tm * tn * 4 * 3
    for arr, kind, off in extras:
        if kind == "tile":
            assert off % tn == 0, (name, off, tn)
            in_specs.append(pl.BlockSpec((tm, tn), lambda i, j, k, o=off // tn: (i, o + j)))
        else:
            in_specs.append(pl.BlockSpec((1, tn), lambda i, j, k: (0, j)))
        operands.append(arr)
        vmem += 2 * tm * tn * arr.dtype.itemsize
    out_shape = [jax.ShapeDtypeStruct((M, N), dt) for dt in out_dtypes]
    out_specs = [pl.BlockSpec((tm, tn), lambda i, j, k: (i, j)) for _ in out_dtypes]
    vmem += sum(2 * tm * tn * jnp.dtype(dt).itemsize for dt in out_dtypes)
    outs = pl.pallas_call(
        body, name=name, grid=(M // tm, N // tn, nk), in_specs=in_specs, out_specs=out_specs, out_shape=out_shape,
        scratch_shapes=[pltpu.VMEM((tm, tn), F32)] if nk > 1 else [], compiler_params=_params(3, vmem))(*operands)
    return outs[0] if n_out == 1 else outs


def _merge(ys, wbs, proj, D, *, name):
    T = ys[0].shape[0]
    tm = _div(T, 1024)
    tn = HALF
    kw = ys[0].shape[1]
    g_off = [(OFF_G + b * D) // tn for b in range(3)]

    def body(y0, y1, y2, w0, w1, w2, g0, g1, g2, merged_ref, b0, b1, b2):
        acc = None
        for y, w, g, bo in ((y0, w0, g0, b0), (y1, w1, g1, b1), (y2, w2, g2, b2)):
            p = jnp.dot(y[...], w[...], preferred_element_type=F32)
            bo[...] = p.astype(bo.dtype)
            t = jax.nn.sigmoid(g[...].astype(F32)) * p
            acc = t if acc is None else acc + t
        merged_ref[...] = acc.astype(merged_ref.dtype)

    y_spec = pl.BlockSpec((tm, kw), lambda i, j: (i, 0))
    w_spec = pl.BlockSpec((None, kw, tn), lambda i, j: (0, 0, j))
    g_specs = [pl.BlockSpec((tm, tn), lambda i, j, o=o: (i, o + j)) for o in g_off]
    o_spec = pl.BlockSpec((tm, tn), lambda i, j: (i, j))
    vmem = 2 * (3 * tm * kw * 2 + 3 * kw * tn * 2 + 3 * tm * tn * proj.dtype.itemsize + 4 * tm * tn * 2) + 4 * tm * tn * 4
    return pl.pallas_call(
        body, name=name, grid=(T // tm, D // tn), in_specs=[y_spec] * 3 + [w_spec] * 3 + g_specs,
        out_specs=[o_spec] * 4, out_shape=[jax.ShapeDtypeStruct((T, D), BF16)] * 4,
        compiler_params=_params(2, vmem))(*ys, *wbs, proj, proj, proj)


def _row_tile(T, width, n_arrays):
    cap = max(8, (24 * MIB) // (2 * n_arrays * width * 4))
    return _div(T, min(cap, 1024), 8)


def _norm_mod(x, ng, scale, shift, *, name):
    T, D = x.shape
    tm = _row_tile(T, D, 3)

    def body(x_ref, ng_ref, sc_ref, sh_ref, h_ref):
        xv = x_ref[...]
        r = lax.rsqrt(jnp.mean(xv * xv, axis=-1, keepdims=True) + EPS)
        h = (xv * r) * ng_ref[...] * (1.0 + sc_ref[...]) + sh_ref[...]
        h_ref[...] = h.astype(h_ref.dtype)

    row = pl.BlockSpec((1, D), lambda i: (0, 0))
    tile = pl.BlockSpec((tm, D), lambda i: (i, 0))
    return pl.pallas_call(body, name=name, grid=(T // tm,), in_specs=[tile, row, row, row], out_specs=tile,
                          out_shape=jax.ShapeDtypeStruct((T, D), BF16), compiler_params=_params(1))(x, ng, scale, shift)


def _norm_mod_bwd(x, dh, dx_out, ng, scale, *, name):
    T, D = x.shape
    tm = _row_tile(T, D, 6)

    def body(x_ref, dh_ref, dxo_ref, ng_ref, sc_ref, dx_ref, dng_ref, dsc_ref, dsh_ref):
        i = pl.program_id(0)
        xv = x_ref[...]
        dh_v = dh_ref[...].astype(F32)
        r = lax.rsqrt(jnp.mean(xv * xv, axis=-1, keepdims=True) + EPS)
        xn = xv * r
        one_sc = 1.0 + sc_ref[...]
        dxn = dh_v * (ng_ref[...] * one_sc)
        dx_ref[...] = dxo_ref[...] + r * (dxn - xn * jnp.mean(dxn * xn, axis=-1, keepdims=True))
        t = dh_v * xn
        parts = (jnp.sum(t * one_sc, axis=0, keepdims=True), jnp.sum(t * ng_ref[...], axis=0, keepdims=True),
                 jnp.sum(dh_v, axis=0, keepdims=True))
        for ref, p in zip((dng_ref, dsc_ref, dsh_ref), parts):
            @pl.when(i == 0)
            def _(ref=ref, p=p):
                ref[...] = p

            @pl.when(i > 0)
            def _(ref=ref, p=p):
                ref[...] += p

    row = pl.BlockSpec((1, D), lambda i: (0, 0))
    tile = pl.BlockSpec((tm, D), lambda i: (i, 0))
    vec = jax.ShapeDtypeStruct((1, D), F32)
    return pl.pallas_call(body, name=name, grid=(T // tm,), in_specs=[tile, tile, tile, row, row],
                          out_specs=[tile, row, row, row], out_shape=[jax.ShapeDtypeStruct((T, D), F32), vec, vec, vec],
                          compiler_params=_params(1))(x, dh, dx_out, ng, scale)


def _final_loss(x, target, fg, *, name):
    T, D = x.shape
    tm = _row_tile(T, D, 4)

    def body(x_ref, t_ref, g_ref, loss_ref, dx_ref, dg_ref):
        i = pl.program_id(0)
        xv = x_ref[...]
        r = lax.rsqrt(jnp.mean(xv * xv, axis=-1, keepdims=True) + EPS)
        xn = xv * r
        err = xn * g_ref[...] - t_ref[...]
        part = 0.5 * jnp.sum(jnp.sum(err * err, axis=1, keepdims=True), axis=0, keepdims=True) / D
        dy = err / D
        dxn = dy * g_ref[...]
        dx_ref[...] = r * (dxn - xn * jnp.mean(dxn * xn, axis=-1, keepdims=True))
        dg = jnp.sum(dy * xn, axis=0, keepdims=True)

        @pl.when(i == 0)
        def _():
            loss_ref[...] = part
            dg_ref[...] = dg

        @pl.when(i > 0)
        def _():
            loss_ref[...] += part
            dg_ref[...] += dg

    row = pl.BlockSpec((1, D), lambda i: (0, 0))
    tile = pl.BlockSpec((tm, D), lambda i: (i, 0))
    one = pl.BlockSpec((1, 1), lambda i: (0, 0))
    return pl.pallas_call(
        body, name=name, grid=(T // tm,), in_specs=[tile, tile, row], out_specs=[one, tile, row],
        out_shape=[jax.ShapeDtypeStruct((1, 1), F32), jax.ShapeDtypeStruct((T, D), F32), jax.ShapeDtypeStruct((1, D), F32)],
        compiler_params=_params(1))(x, target, fg)


def _gate_out_bwd(dx_out, o, gate, *, name):
    T, D = dx_out.shape
    tm = _row_tile(T, D, 3)

    def body(dx_ref, o_ref, g_ref, dmo_ref, dg_ref):
        i = pl.program_id(0)
        dxv = dx_ref[...]
        dmo_ref[...] = (dxv * g_ref[...]).astype(dmo_ref.dtype)
        p = jnp.sum(dxv * o_ref[...].astype(F32), axis=0, keepdims=True)

        @pl.when(i == 0)
        def _():
            dg_ref[...] = p

        @pl.when(i > 0)
        def _():
            dg_ref[...] += p

    row = pl.BlockSpec((1, D), lambda i: (0, 0))
    tile = pl.BlockSpec((tm, D), lambda i: (i, 0))
    return pl.pallas_call(body, name=name, grid=(T // tm,), in_specs=[tile, tile, row], out_specs=[tile, row],
                          out_shape=[jax.ShapeDtypeStruct((T, D), BF16), jax.ShapeDtypeStruct((1, D), F32)],
                          compiler_params=_params(1))(dx_out, o, gate)


def _merge_bwd(dmerged, branches, proj, D, *, name):
    T = dmerged.shape[0]
    tm = _div(T, 1024)
    tn = HALF
    g_off = [(OFF_G + b * D) // tn for b in range(3)]

    def body(dm_ref, b0, b1, b2, g0, g1, g2, db0, db1, db2, dg0, dg1, dg2):
        dm = dm_ref[...].astype(F32)
        for b, g, db, dg in ((b0, g0, db0, dg0), (b1, g1, db1, dg1), (b2, g2, db2, dg2)):
            s = jax.nn.sigmoid(g[...].astype(F32))
            db[...] = (dm * s).astype(db.dtype)
            dg[...] = (dm * b[...].astype(F32) * s * (1.0 - s)).astype(dg.dtype)

    tile = pl.BlockSpec((tm, tn), lambda i, j: (i, j))
    g_specs = [pl.BlockSpec((tm, tn), lambda i, j, o=o: (i, o + j)) for o in g_off]
    return pl.pallas_call(body, name=name, grid=(T // tm, D // tn), in_specs=[tile] * 4 + g_specs, out_specs=[tile] * 6,
                          out_shape=[jax.ShapeDtypeStruct((T, D), BF16)] * 6,
                          compiler_params=_params(2))(dmerged, *branches, proj, proj, proj)


def _conv_out_bwd(dy, cpre, proj, *, name):
    T = dy.shape[0]
    tm = _div(T, 1024)
    tn = HALF

    def body(dy_ref, c_ref, z_ref, dc_ref, dz_ref):
        dyv = dy_ref[...].astype(F32)
        z = z_ref[...].astype(F32)
        dc_ref[...] = (dyv * _silu(z)).astype(dc_ref.dtype)
        dz_ref[...] = (dyv * c_ref[...].astype(F32) * _dsilu(z)).astype(dz_ref.dtype)

    tile = pl.BlockSpec((tm, tn), lambda i, j: (i, j))
    z_spec = pl.BlockSpec((tm, tn), lambda i, j: (i, OFF_CZ // tn + j))
    return pl.pallas_call(body, name=name, grid=(T // tm, CONV_WIDTH // tn), in_specs=[tile, tile, z_spec],
                          out_specs=[tile, tile], out_shape=[jax.ShapeDtypeStruct((T, CONV_WIDTH), BF16)] * 2,
                          compiler_params=_params(2))(dy, cpre, proj)


def _pool_mixed(ext, u, g, row0):
    w = POOL_WINDOWS[g]
    s = ext
    shift = 1
    while shift < w:
        s = s + pltpu.roll(s, shift, 0)
        shift *= 2
    tm = u.shape[0]
    t = row0 + lax.broadcasted_iota(jnp.int32, (tm, 1), 0)
    inv = 1.0 / jnp.minimum(t + 1, w).astype(F32)
    return s[POOL_HALO:, :] * inv - u, inv


def _pool_specs(T, tm):
    per = tm // POOL_HALO
    cur = lambda col: pl.BlockSpec((tm, POOL_WIDTH), lambda i, c=col: (i, c))
    prev = pl.BlockSpec((POOL_HALO, POOL_WIDTH), lambda i: (jnp.maximum(i * per - 1, 0), 0))
    return cur, prev


def _pool_fwd(proj, pool_w, scale, *, name):
    T = proj.shape[0]
    tm = _div(T, 512, 16)
    cur, prev = _pool_specs(T, tm)

    def body(u_ref, up_ref, z_ref, w_ref, sc_ref, y_ref):
        i = pl.program_id(0)
        u = u_ref[...].astype(F32)
        halo = jnp.where(i == 0, 0.0, up_ref[...].astype(F32))
        ext = jnp.concatenate([halo, u], axis=0)
        for g in range(len(POOL_WINDOWS)):
            cols = slice(g * POOL_GROUP, (g + 1) * POOL_GROUP)
            mixed, _ = _pool_mixed(ext[:, cols], u[:, cols], g, i * tm)
            p = jnp.dot(mixed.astype(BF16), w_ref[g], preferred_element_type=F32)
            y = p * sc_ref[:, cols] * _silu(z_ref[:, cols].astype(F32))
            y_ref[:, cols] = y.astype(y_ref.dtype)

    w_spec = pl.BlockSpec((len(POOL_WINDOWS), POOL_GROUP, POOL_GROUP), lambda i: (0, 0, 0))
    row = pl.BlockSpec((1, POOL_WIDTH), lambda i: (0, 0))
    return pl.pallas_call(body, name=name, grid=(T // tm,), in_specs=[cur(0), prev, cur(1), w_spec, row],
                          out_specs=pl.BlockSpec((tm, POOL_WIDTH), lambda i: (i, 0)),
                          out_shape=jax.ShapeDtypeStruct((T, POOL_WIDTH), BF16),
                          compiler_params=_params(1))(proj, proj, proj, pool_w, scale)


def _pool_bwd(proj, dy, pool_w, scale, *, name):
    T = proj.shape[0]
    tm = _div(T, 512, 16)
    cur, prev = _pool_specs(T, tm)
    n_g = len(POOL_WINDOWS)

    def body(u_ref, up_ref, z_ref, dy_ref, w_ref, sc_ref, dz_ref, dmn_ref, dsc_ref, dw_ref):
        i = pl.program_id(0)
        u = u_ref[...].astype(F32)
        halo = jnp.where(i == 0, 0.0, up_ref[...].astype(F32))
        ext = jnp.concatenate([halo, u], axis=0)
        for g in range(n_g):
            cols = slice(g * POOL_GROUP, (g + 1) * POOL_GROUP)
            mixed, inv = _pool_mixed(ext[:, cols], u[:, cols], g, i * tm)
            mixed = mixed.astype(BF16)
            w = w_ref[g]
            p = jnp.dot(mixed, w, preferred_element_type=F32)
            z = z_ref[:, cols].astype(F32)
            dyv = dy_ref[:, cols].astype(F32)
            sc = sc_ref[:, cols]
            dypre = dyv * _silu(z)
            dz_ref[:, cols] = (dyv * (p * sc) * _dsilu(z)).astype(dz_ref.dtype)
            dsc = jnp.sum(dypre * p, axis=0, keepdims=True)
            dp = (dypre * sc).astype(BF16)
            dwg = lax.dot_general(mixed, dp, (((0,), (0,)), ((), ())), preferred_element_type=F32)
            dmixed = lax.dot_general(dp, w, (((1,), (1,)), ((), ())), preferred_element_type=F32)
            dmn_ref[:, cols] = dmixed * inv

            @pl.when(i == 0)
            def _(g=g, cols=cols, dsc=dsc, dwg=dwg):
                dsc_ref[:, cols] = dsc
                dw_ref[g] = dwg

            @pl.when(i > 0)
            def _(g=g, cols=cols, dsc=dsc, dwg=dwg):
                dsc_ref[:, cols] += dsc
                dw_ref[g] += dwg

    w_spec = pl.BlockSpec((n_g, POOL_GROUP, POOL_GROUP), lambda i: (0, 0, 0))
    row = pl.BlockSpec((1, POOL_WIDTH), lambda i: (0, 0))
    tile = pl.BlockSpec((tm, POOL_WIDTH), lambda i: (i, 0))
    dw_spec = pl.BlockSpec((n_g, POOL_GROUP, POOL_GROUP), lambda i: (0, 0, 0))
    return pl.pallas_call(
        body, name=name, grid=(T // tm,), in_specs=[cur(0), prev, cur(1), tile, w_spec, row],
        out_specs=[tile, tile, row, dw_spec],
        out_shape=[jax.ShapeDtypeStruct((T, POOL_WIDTH), BF16), jax.ShapeDtypeStruct((T, POOL_WIDTH), F32),
                   jax.ShapeDtypeStruct((1, POOL_WIDTH), F32), jax.ShapeDtypeStruct((n_g, POOL_GROUP, POOL_GROUP), F32)],
        compiler_params=_params(1))(proj, proj, proj, dy, pool_w, scale)


def _pool_bwd_window(dmn, *, name):
    T = dmn.shape[0]
    tm = _div(T, 512, 16)
    per = tm // POOL_HALO
    last = T // POOL_HALO - 1
    nb = T // tm

    def body(c_ref, n_ref, du_ref):
        i = pl.program_id(0)
        cur = c_ref[...]
        nxt = jnp.where(i == nb - 1, 0.0, n_ref[...])
        ext = jnp.concatenate([cur, nxt], axis=0)
        rows = tm + POOL_HALO
        t = i * tm + lax.broadcasted_iota(jnp.int32, (tm, 1), 0)
        for g, w in enumerate(POOL_WINDOWS):
            cols = slice(g * POOL_GROUP, (g + 1) * POOL_GROUP)
            s = ext[:, cols]
            shift = 1
            while shift < w:
                s = s + pltpu.roll(s, rows - shift, 0)
                shift *= 2
            cnt = jnp.minimum(t + 1, w).astype(F32)
            du_ref[:, cols] = (s[:tm, :] - cur[:, cols] * cnt).astype(du_ref.dtype)

    tile = pl.BlockSpec((tm, POOL_WIDTH), lambda i: (i, 0))
    nxt = pl.BlockSpec((POOL_HALO, POOL_WIDTH), lambda i: (jnp.minimum((i + 1) * per, last), 0))
    return pl.pallas_call(body, name=name, grid=(nb,), in_specs=[tile, nxt], out_specs=tile,
                          out_shape=jax.ShapeDtypeStruct((T, POOL_WIDTH), BF16), compiler_params=_params(1))(dmn, dmn)


def _attn_mask(i):
    qb, keys = ATTN_Q_BLOCK, ATTN_Q_BLOCK + ATTN_HALO
    qi = lax.broadcasted_iota(jnp.int32, (qb, keys), 0) // CHUNK
    kj = lax.broadcasted_iota(jnp.int32, (qb, keys), 1) // CHUNK - WINDOW_CHUNKS
    return (kj <= qi) & (kj >= qi - WINDOW_CHUNKS) & (kj + i * (qb // CHUNK) >= 0)


def _attn_specs(T, order):
    qb = ATTN_Q_BLOCK
    per = qb // ATTN_HALO
    cur = lambda width, col: pl.BlockSpec((qb, width), lambda i, c=col: (order(i), c))
    prev = lambda col: pl.BlockSpec((ATTN_HALO, KV_WIDTH), lambda i, c=col: (jnp.maximum(order(i) * per - 1, 0), c))
    return cur, prev


def _attn_fwd(proj, sink, *, name):
    T = proj.shape[0]
    qb = ATTN_Q_BLOCK
    cur, prev = _attn_specs(T, lambda i: i)

    def body(sink_ref, q_ref, kc_ref, kp_ref, vc_ref, vp_ref, z0_ref, z1_ref, o_ref, y_ref, lse_ref):
        i = pl.program_id(0)
        q = q_ref[...].astype(BF16)
        kk = jnp.concatenate([kp_ref[...], kc_ref[...]], axis=0).astype(BF16)
        vv = jnp.concatenate([vp_ref[...], vc_ref[...]], axis=0).astype(BF16)
        mask = _attn_mask(i)
        lane = lax.broadcasted_iota(jnp.int32, (qb, 128), 1)
        lse = jnp.zeros((qb, 128), F32)
        for h in range(N_Q_HEADS):
            hs = slice(h * HEAD_DIM, (h + 1) * HEAD_DIM)
            ks = slice((h // Q_PER_KV) * HEAD_DIM, (h // Q_PER_KV + 1) * HEAD_DIM)
            s = lax.dot_general(q[:, hs], kk[:, ks], (((1,), (1,)), ((), ())), preferred_element_type=F32)
            s = jnp.where(mask, s * (HEAD_DIM ** -0.5), NEG_INF)
            sk = sink_ref[h]
            m = jnp.maximum(jnp.max(s, axis=1, keepdims=True), sk)
            p = jnp.exp(s - m)
            den = jnp.sum(p, axis=1, keepdims=True) + jnp.exp(sk - m)
            oh = jnp.dot(p.astype(BF16), vv[:, ks], preferred_element_type=F32) / den
            zr = z0_ref if h < N_Q_HEADS // 2 else z1_ref
            zs = slice((h % (N_Q_HEADS // 2)) * HEAD_DIM, (h % (N_Q_HEADS // 2) + 1) * HEAD_DIM)
            o_ref[:, hs] = oh.astype(o_ref.dtype)
            y_ref[:, hs] = (oh * _silu(zr[:, zs].astype(F32))).astype(y_ref.dtype)
            lse = jnp.where(lane == h, m + jnp.log(den), lse)
        lse_ref[...] = lse

    kcol, vcol = OFF_K // KV_WIDTH, OFF_V // KV_WIDTH
    tile = pl.BlockSpec((qb, ATTN_WIDTH), lambda i: (i, 0))
    in_specs = [pl.BlockSpec(memory_space=pltpu.SMEM), cur(ATTN_WIDTH, OFF_Q // ATTN_WIDTH), cur(KV_WIDTH, kcol), prev(kcol),
                cur(KV_WIDTH, vcol), prev(vcol), cur(HALF, OFF_AZ // HALF), cur(HALF, OFF_AZ // HALF + 1)]
    return pl.pallas_call(
        body, name=name, grid=(T // qb,), in_specs=in_specs,
        out_specs=[tile, tile, pl.BlockSpec((qb, 128), lambda i: (i, 0))],
        out_shape=[jax.ShapeDtypeStruct((T, ATTN_WIDTH), BF16), jax.ShapeDtypeStruct((T, ATTN_WIDTH), BF16),
                   jax.ShapeDtypeStruct((T, 128), F32)],
        compiler_params=_params(1))(sink, *([proj] * 7))


def _attn_bwd(proj, sink, o, lse, dy, *, name):
    T = proj.shape[0]
    qb = ATTN_Q_BLOCK
    nb = T // qb
    order = lambda i: nb - 1 - i
    cur, prev = _attn_specs(T, order)

    def body(sink_ref, q_ref, kc_ref, kp_ref, vc_ref, vp_ref, z0_ref, z1_ref, o_ref, lse_ref, dy_ref,
             dq_ref, dk_ref, dv_ref, dz_ref, dsink_ref, dk_carry, dv_carry):
        i = pl.program_id(0)
        blk = order(i)
        q = q_ref[...].astype(BF16)
        kk = jnp.concatenate([kp_ref[...], kc_ref[...]], axis=0).astype(BF16)
        vv = jnp.concatenate([vp_ref[...], vc_ref[...]], axis=0).astype(BF16)
        mask = _attn_mask(blk)
        lane = lax.broadcasted_iota(jnp.int32, (1, 128), 1)
        dsink = jnp.zeros((1, 128), F32)
        scale = HEAD_DIM ** -0.5
        for kv in range(N_KV_HEADS):
            ks = slice(kv * HEAD_DIM, (kv + 1) * HEAD_DIM)
            dk_acc = jnp.zeros((qb + ATTN_HALO, HEAD_DIM), F32)
            dv_acc = jnp.zeros((qb + ATTN_HALO, HEAD_DIM), F32)
            for h in range(kv * Q_PER_KV, (kv + 1) * Q_PER_KV):
                hs = slice(h * HEAD_DIM, (h + 1) * HEAD_DIM)
                zr = z0_ref if h < N_Q_HEADS // 2 else z1_ref
                zs = slice((h % (N_Q_HEADS // 2)) * HEAD_DIM, (h % (N_Q_HEADS // 2) + 1) * HEAD_DIM)
                z = zr[:, zs].astype(F32)
                dyh = dy_ref[:, hs].astype(F32)
                oh = o_ref[:, hs].astype(F32)
                do = dyh * _silu(z)
                dz_ref[:, hs] = (dyh * oh * _dsilu(z)).astype(dz_ref.dtype)
                drow = jnp.sum(do * oh, axis=1, keepdims=True)
                lse_h = lse_ref[:, h:h + 1]
                s = lax.dot_general(q[:, hs], kk[:, ks], (((1,), (1,)), ((), ())), preferred_element_type=F32)
                p = jnp.exp(jnp.where(mask, s * scale, NEG_INF) - lse_h)
                do_b = do.astype(BF16)
                dv_acc = dv_acc + lax.dot_general(p.astype(BF16), do_b, (((0,), (0,)), ((), ())),
                                                  preferred_element_type=F32)
                dp = lax.dot_general(do_b, vv[:, ks], (((1,), (1,)), ((), ())), preferred_element_type=F32)
                ds = (p * (dp - drow)).astype(BF16)
                dq_ref[:, hs] = (jnp.dot(ds, kk[:, ks], preferred_element_type=F32) * scale).astype(dq_ref.dtype)
                dk_acc = dk_acc + lax.dot_general(ds, q[:, hs], (((0,), (0,)), ((), ())),
                                                  preferred_element_type=F32) * scale
                p_sink = jnp.exp(sink_ref[h] - lse_h)
                dsink = jnp.where(lane == h, -jnp.sum(p_sink * drow, axis=0, keepdims=True), dsink)
            for acc, carry, out in ((dk_acc, dk_carry, dk_ref), (dv_acc, dv_carry, dv_ref)):
                tail = acc[qb:, :] + jnp.where(i == 0, 0.0, carry[:, ks])
                out[:, ks] = jnp.concatenate([acc[ATTN_HALO:qb, :], tail], axis=0).astype(out.dtype)
                carry[:, ks] = acc[:ATTN_HALO, :]

        @pl.when(i == 0)
        def _():
            dsink_ref[...] = dsink

        @pl.when(i > 0)
        def _():
            dsink_ref[...] += dsink

    kcol, vcol = OFF_K // KV_WIDTH, OFF_V // KV_WIDTH
    tile = pl.BlockSpec((qb, ATTN_WIDTH), lambda i: (order(i), 0))
    kv_tile = pl.BlockSpec((qb, KV_WIDTH), lambda i: (order(i), 0))
    lse_spec = pl.BlockSpec((qb, 128), lambda i: (order(i), 0))
    in_specs = [pl.BlockSpec(memory_space=pltpu.SMEM), cur(ATTN_WIDTH, OFF_Q // ATTN_WIDTH), cur(KV_WIDTH, kcol), prev(kcol),
                cur(KV_WIDTH, vcol), prev(vcol), cur(HALF, OFF_AZ // HALF), cur(HALF, OFF_AZ // HALF + 1),
                tile, lse_spec, tile]
    return pl.pallas_call(
        body, name=name, grid=(nb,), in_specs=in_specs,
        out_specs=[tile, kv_tile, kv_tile, tile, pl.BlockSpec((1, 128), lambda i: (0, 0))],
        out_shape=[jax.ShapeDtypeStruct((T, ATTN_WIDTH), BF16), jax.ShapeDtypeStruct((T, KV_WIDTH), BF16),
                   jax.ShapeDtypeStruct((T, KV_WIDTH), BF16), jax.ShapeDtypeStruct((T, ATTN_WIDTH), BF16),
                   jax.ShapeDtypeStruct((1, 128), F32)],
        scratch_shapes=[pltpu.VMEM((ATTN_HALO, KV_WIDTH), F32), pltpu.VMEM((ATTN_HALO, KV_WIDTH), F32)],
        compiler_params=_params(1))(sink, *([proj] * 7), o, lse, dy)


def _conv_specs(T, tm):
    per = tm // CONV_HALO
    ca, cb = OFF_CA // HALF, OFF_CB // HALF
    cur = lambda col: pl.BlockSpec((tm, HALF), lambda i, c=col: (i, c))
    prev = lambda col: pl.BlockSpec((CONV_HALO, HALF), lambda i, c=col: (jnp.maximum(i * per - 1, 0), c))
    return [cur(ca), cur(ca + 1), cur(cb), cur(cb + 1), prev(ca), prev(ca + 1), prev(cb), prev(cb + 1)]


def _conv_glu_ext(refs, i, ext_ref):
    a0, a1, b0, b1, pa0, pa1, pb0, pb1 = refs
    a = jnp.concatenate([a0[...], a1[...]], axis=1).astype(F32)
    sb = jax.nn.sigmoid(jnp.concatenate([b0[...], b1[...]], axis=1).astype(F32))
    pa = jnp.concatenate([pa0[...], pa1[...]], axis=1).astype(F32)
    pb = jnp.concatenate([pb0[...], pb1[...]], axis=1).astype(F32)
    ext_ref[:CONV_HALO, :] = jnp.where(i == 0, 0.0, pa * jax.nn.sigmoid(pb))
    ext_ref[CONV_HALO:, :] = a * sb
    return a, sb


def _conv_scratch(tm):
    return [pltpu.VMEM((tm + CONV_HALO, CONV_WIDTH), F32), pltpu.VMEM((7, tm + CONV_HALO - 8, CONV_WIDTH), F32)]


def _conv_fill_shifted(ext_ref, sh_ref):
    rows = sh_ref.shape[1]
    for b in range(1, 8):
        sh_ref[b - 1] = ext_ref[b:b + rows, :]


def _conv_window(ext_ref, sh_ref, start, tm, cols):
    a, b = divmod(start, 8)
    if b == 0:
        return ext_ref[8 * a:8 * a + tm, cols]
    return sh_ref[b - 1, 8 * a:8 * a + tm, cols]


LANES = 128


def _lane_blocks(width):
    return [slice(k, k + LANES) for k in range(0, width, LANES)]


def _conv_taps(ext_ref, sh_ref, dw_ref, out_ref, first_start, step, tm, bias_ref=None):
    sub = tm // 8
    for cols in _lane_blocks(CONV_WIDTH):
        y = None
        for j in range(CONV_KERNEL):
            tap = jnp.broadcast_to(dw_ref[j:j + 1, cols], (8, LANES))
            window = _conv_window(ext_ref, sh_ref, first_start + step * j, tm, cols).reshape(sub, 8, LANES)
            t = tap * window
            y = t if y is None else y + t
        y = y.reshape(tm, LANES)
        out_ref[:, cols] = y if bias_ref is None else y + bias_ref[:, cols]


def _conv_fwd(proj, dw, dwb, lng, lnb, *, name):
    T = proj.shape[0]
    tm = _div(T, 256, 32)

    def body(*refs):
        dw_ref, dwb_ref, g_ref, b_ref, s_ref, yc_ref, ext_ref, sh_ref = refs[8:]
        i = pl.program_id(0)
        _conv_glu_ext(refs[:8], i, ext_ref)
        _conv_fill_shifted(ext_ref, sh_ref)
        _conv_taps(ext_ref, sh_ref, dw_ref, yc_ref, CONV_HALO - (CONV_KERNEL - 1), 1, tm, dwb_ref)
        yc = yc_ref[...]
        mu = jnp.mean(yc, axis=-1, keepdims=True)
        d = yc - mu
        rstd = lax.rsqrt(jnp.mean(d * d, axis=-1, keepdims=True) + EPS)
        s_ref[...] = _silu(d * rstd * g_ref[...] + b_ref[...]).astype(s_ref.dtype)

    row = pl.BlockSpec((1, CONV_WIDTH), lambda i: (0, 0))
    taps = pl.BlockSpec((CONV_KERNEL, CONV_WIDTH), lambda i: (0, 0))
    tile = pl.BlockSpec((tm, CONV_WIDTH), lambda i: (i, 0))
    return pl.pallas_call(
        body, name=name, grid=(T // tm,), in_specs=_conv_specs(T, tm) + [taps, row, row, row], out_specs=[tile, tile],
        out_shape=[jax.ShapeDtypeStruct((T, CONV_WIDTH), BF16), jax.ShapeDtypeStruct((T, CONV_WIDTH), F32)],
        scratch_shapes=_conv_scratch(tm), compiler_params=_params(1))(*([proj] * 8), dw, dwb, lng, lnb)


def _conv_bwd(proj, yc_saved, ds, lng, lnb, *, name):
    T = proj.shape[0]
    tm = _div(T, 256, 32)
    nb = T // tm
    sub = tm // 8

    def body(*refs):
        yc_ref, ds_ref, g_ref, b_ref, dyc_ref, ddw_ref, ddwb_ref, dg_ref, db_ref, ext_ref, sh_ref, acc_ref = refs[8:]
        i = pl.program_id(0)
        _conv_glu_ext(refs[:8], i, ext_ref)
        _conv_fill_shifted(ext_ref, sh_ref)
        yc = yc_ref[...]
        mu = jnp.mean(yc, axis=-1, keepdims=True)
        d = yc - mu
        rstd = lax.rsqrt(jnp.mean(d * d, axis=-1, keepdims=True) + EPS)
        xhat = d * rstd
        dln = ds_ref[...].astype(F32) * _dsilu(xhat * g_ref[...] + b_ref[...])
        dxhat = dln * g_ref[...]
        dyc = rstd * (dxhat - jnp.mean(dxhat, axis=-1, keepdims=True)
                      - xhat * jnp.mean(dxhat * xhat, axis=-1, keepdims=True))
        dyc_ref[...] = dyc
        first = i == 0

        def accumulate(ref, idx, val):
            @pl.when(first)
            def _():
                ref[idx] = val

            @pl.when(jnp.logical_not(first))
            def _():
                ref[idx] += val

        accumulate(dg_ref, slice(None), jnp.sum(dln * xhat, axis=0, keepdims=True))
        accumulate(db_ref, slice(None), jnp.sum(dln, axis=0, keepdims=True))
        accumulate(ddwb_ref, slice(None), jnp.sum(dyc, axis=0, keepdims=True))
        @pl.when(first)
        def _():
            acc_ref[...] = jnp.zeros_like(acc_ref)

        for cols in _lane_blocks(CONV_WIDTH):
            dyc_b = dyc_ref[:, cols].reshape(sub, 8, LANES)
            for j in range(CONV_KERNEL):
                window = _conv_window(ext_ref, sh_ref, CONV_HALO - (CONV_KERNEL - 1) + j, tm, cols)
                acc_ref[j, :, cols] += jnp.sum(dyc_b * window.reshape(sub, 8, LANES), axis=0)

        @pl.when(i == nb - 1)
        def _():
            for j in range(CONV_KERNEL):
                ddw_ref[j:j + 1, :] = jnp.sum(acc_ref[j], axis=0, keepdims=True)

    row = pl.BlockSpec((1, CONV_WIDTH), lambda i: (0, 0))
    taps = pl.BlockSpec((CONV_KERNEL, CONV_WIDTH), lambda i: (0, 0))
    tile = pl.BlockSpec((tm, CONV_WIDTH), lambda i: (i, 0))
    vec = jax.ShapeDtypeStruct((1, CONV_WIDTH), F32)
    return pl.pallas_call(
        body, name=name, grid=(nb,), in_specs=_conv_specs(T, tm) + [tile, tile, row, row],
        out_specs=[tile, taps, row, row, row],
        out_shape=[jax.ShapeDtypeStruct((T, CONV_WIDTH), F32), jax.ShapeDtypeStruct((CONV_KERNEL, CONV_WIDTH), F32), vec, vec, vec],
        scratch_shapes=_conv_scratch(tm) + [pltpu.VMEM((CONV_KERNEL, 8, CONV_WIDTH), F32)],
        compiler_params=_params(1))(*([proj] * 8), yc_saved, ds, lng, lnb)


def _conv_bwd_input(proj, dyc, dw, *, name):
    T = proj.shape[0]
    tm = _div(T, 256, 32)
    per = tm // CONV_HALO
    last = T // CONV_HALO - 1
    nb = T // tm
    ca, cb = OFF_CA // HALF, OFF_CB // HALF

    def body(a0, a1, b0, b1, c_ref, n_ref, dw_ref, da_ref, db_ref, ext_ref, sh_ref, dg_ref):
        i = pl.program_id(0)
        ext_ref[:tm, :] = c_ref[...]
        ext_ref[tm:, :] = jnp.where(i == nb - 1, 0.0, n_ref[...])
        _conv_fill_shifted(ext_ref, sh_ref)
        _conv_taps(ext_ref, sh_ref, dw_ref, dg_ref, CONV_KERNEL - 1, -1, tm)
        dg = dg_ref[...]
        a = jnp.concatenate([a0[...], a1[...]], axis=1).astype(F32)
        sb = jax.nn.sigmoid(jnp.concatenate([b0[...], b1[...]], axis=1).astype(F32))
        da_ref[...] = (dg * sb).astype(da_ref.dtype)
        db_ref[...] = (dg * a * sb * (1.0 - sb)).astype(db_ref.dtype)

    cur = lambda col: pl.BlockSpec((tm, HALF), lambda i, c=col: (i, c))
    tile = pl.BlockSpec((tm, CONV_WIDTH), lambda i: (i, 0))
    nxt = pl.BlockSpec((CONV_HALO, CONV_WIDTH), lambda i: (jnp.minimum((i + 1) * per, last), 0))
    taps = pl.BlockSpec((CONV_KERNEL, CONV_WIDTH), lambda i: (0, 0))
    return pl.pallas_call(
        body, name=name, grid=(nb,), in_specs=[cur(ca), cur(ca + 1), cur(cb), cur(cb + 1), tile, nxt, taps],
        out_specs=[tile, tile], out_shape=[jax.ShapeDtypeStruct((T, CONV_WIDTH), BF16)] * 2,
        scratch_shapes=_conv_scratch(tm) + [pltpu.VMEM((tm, CONV_WIDTH), F32)],
        compiler_params=_params(1))(proj, proj, proj, proj, dyc, dyc, dw)


def _ada_mod(c_all, w_ada, b_slab, *, name):
    L, D, N = w_ada.shape
    tn = _div(N, 512)

    def body(c_ref, w_ref, b_ref, o_ref):
        ca = _silu(c_ref[...]).astype(BF16)
        o_ref[...] = jnp.dot(ca, w_ref[...].astype(BF16), preferred_element_type=F32) + b_ref[...]

    return pl.pallas_call(
        body, name=name, grid=(L, N // tn),
        in_specs=[pl.BlockSpec((N_DEV, D), lambda l, j: (0, 0)), pl.BlockSpec((None, D, tn), lambda l, j: (l, 0, j)),
                  pl.BlockSpec((None, 1, tn), lambda l, j: (l, 0, j))],
        out_specs=pl.BlockSpec((None, N_DEV, tn), lambda l, j: (l, 0, j)),
        out_shape=jax.ShapeDtypeStruct((L, N_DEV, N), F32), compiler_params=_params(2))(c_all, w_ada, b_slab)


def _ada_grad(c_all_t, dmod_slab, *, name):
    D = c_all_t.shape[0]
    L, _, N = dmod_slab.shape
    tm = _div(D, 512)
    tn = _div(N, 512)

    def body(c_ref, d_ref, o_ref):
        ca = _silu(c_ref[...]).astype(BF16).astype(F32)
        dm = d_ref[...].astype(BF16).astype(F32)
        acc = None
        for b in range(N_DEV):
            t = ca[:, b:b + 1] * dm[b:b + 1, :]
            acc = t if acc is None else acc + t
        o_ref[...] = acc

    return pl.pallas_call(
        body, name=name, grid=(L, D // tm, N // tn),
        in_specs=[pl.BlockSpec((tm, N_DEV), lambda l, i, j: (i, 0)), pl.BlockSpec((None, N_DEV, tn), lambda l, i, j: (l, 0, j))],
        out_specs=pl.BlockSpec((None, tm, tn), lambda l, i, j: (l, i, j)),
        out_shape=jax.ShapeDtypeStruct((L, D, N), F32), compiler_params=_params(3))(c_all_t, dmod_slab)


def _flat_tile(R, C, n_arrays):
    cap = max(8, (20 * MIB) // (2 * n_arrays * C * 4))
    return _div(R, cap, 8) if R % 8 == 0 else R


def _adamw_math(w, g, m, v):
    m = ADAM_B1 * m + (1.0 - ADAM_B1) * g
    v = ADAM_B2 * v + (1.0 - ADAM_B2) * (g * g)
    m_hat = m / (1.0 - ADAM_B1 ** ADAM_STEP)
    v_hat = v / (1.0 - ADAM_B2 ** ADAM_STEP)
    delta = -ADAM_LR * (m_hat / (jnp.sqrt(v_hat) + ADAM_EPS) + ADAM_WD * w)
    return delta, m, v


def _adamw(w, m, v, gs, *, name):
    R, C = w.shape
    n_g = len(gs)
    tr = _flat_tile(R, C, 7 + n_g)

    def body(*refs):
        w_ref, m_ref, v_ref = refs[:3]
        g_refs = refs[3:3 + n_g]
        go_ref, d_ref, mo_ref, vo_ref = refs[3 + n_g:]
        g = g_refs[0][...]
        for r in g_refs[1:]:
            g = g + r[...]
        d, mn, vn = _adamw_math(w_ref[...], g, m_ref[...], v_ref[...])
        go_ref[...] = g
        d_ref[...] = d
        mo_ref[...] = mn
        vo_ref[...] = vn

    tile = pl.BlockSpec((tr, C), lambda i: (i, 0))
    shp = jax.ShapeDtypeStruct((R, C), F32)
    return pl.pallas_call(body, name=name, grid=(R // tr,), in_specs=[tile] * (3 + n_g), out_specs=[tile] * 4,
                          out_shape=[shp] * 4, compiler_params=_params(1, 2 * (7 + n_g) * tr * C * 4))(w, m, v, *gs)


def _adamw_layer(w, m, v, layer, gs, outs, *, name):
    _, R, C = w.shape
    n_g = len(gs)
    tr = _flat_tile(R, C, 7 + n_g)

    def body(*refs):
        w_ref, m_ref, v_ref = refs[:3]
        g_refs = refs[3:3 + n_g]
        go_ref, d_ref, mo_ref, vo_ref = refs[3 + n_g + 4:]
        g = g_refs[0][...]
        for r in g_refs[1:]:
            g = g + r[...]
        d, mn, vn = _adamw_math(w_ref[...], g, m_ref[...], v_ref[...])
        go_ref[...] = g
        d_ref[...] = d
        mo_ref[...] = mn
        vo_ref[...] = vn

    lay = pl.BlockSpec((None, tr, C), lambda i: (layer, i, 0))
    tile = pl.BlockSpec((tr, C), lambda i: (i, 0))
    return pl.pallas_call(
        body, name=name, grid=(R // tr,), in_specs=[lay] * 3 + [tile] * n_g + [HBM] * 4, out_specs=[lay] * 4,
        out_shape=[jax.ShapeDtypeStruct(o.shape, o.dtype) for o in outs],
        input_output_aliases={3 + n_g + k: k for k in range(4)},
        compiler_params=_params(1, 2 * (7 + n_g) * tr * C * 4))(w, m, v, *gs, *outs)


def _full_shape(kind, slab_shape):
    G, r, c = slab_shape
    return (G, r, c * N_CHIP) if kind == "cols" else (G, r * N_CHIP, c)


def _slab_tile(G, r, c, n_arrays):
    cap = max(16, (20 * MIB) // (2 * n_arrays * G * c * 4))
    return _div(r, cap, 16)


def _slab_block(kind, G, r, c, tr):
    if kind == "cols":
        return pl.BlockSpec((G, tr, c), lambda i, chip: (0, i, chip[0]))
    per = r // tr
    return pl.BlockSpec((G, tr, c), lambda i, chip: (0, chip[0] * per + i, 0))


def _cast_into_full(chip, w, layer, kind, *, name):
    _, G, r, c = w.shape
    tr = _slab_tile(G, r, c, 2)

    def body(chip_ref, w_ref, o_ref):
        o_ref[...] = w_ref[...].astype(BF16)

    grid_spec = pltpu.PrefetchScalarGridSpec(
        num_scalar_prefetch=1, grid=(r // tr,),
        in_specs=[pl.BlockSpec((None, G, tr, c), lambda i, chip: (layer, 0, i, 0))], out_specs=_slab_block(kind, G, r, c, tr))
    return pl.pallas_call(body, name=name, grid_spec=grid_spec,
                          out_shape=jax.ShapeDtypeStruct(_full_shape(kind, (G, r, c)), BF16),
                          compiler_params=_params(1, 4 * G * tr * c * 4))(chip, w)


def _sum_contribs(chip, full, land, kind, *, name):
    _, G, r, c = land.shape
    tr = _slab_tile(G, r, c, 5)

    def body(chip_ref, f_ref, l_ref, o_ref):
        o_ref[...] = ((f_ref[...].astype(F32) + l_ref[0].astype(F32)) + l_ref[1].astype(F32)) + l_ref[2].astype(F32)

    grid_spec = pltpu.PrefetchScalarGridSpec(
        num_scalar_prefetch=1, grid=(r // tr,),
        in_specs=[_slab_block(kind, G, r, c, tr), pl.BlockSpec((3, G, tr, c), lambda i, chip: (0, 0, i, 0))],
        out_specs=pl.BlockSpec((G, tr, c), lambda i, chip: (0, i, 0)))
    return pl.pallas_call(body, name=name, grid_spec=grid_spec, out_shape=jax.ShapeDtypeStruct((G, r, c), F32),
                          compiler_params=_params(1, 2 * 5 * G * tr * c * 4))(chip, full, land)


def _place():
    x, y, c = lax.axis_index("x"), lax.axis_index("y"), lax.axis_index("c")
    chips = [(1 - x, y), (x, 1 - y), (1 - x, 1 - y)]
    return x, y, c, chips


def _small_exchange(v, reduce, *, name):
    m_per, n = v.shape
    assert m_per % 8 == 0 and n % 128 == 0

    def body(x_ref, out_ref, *scratch):
        if reduce:
            all_ref, send_sems, recv_sems, local_sem = scratch
        else:
            all_ref = out_ref
            send_sems, recv_sems, local_sem = scratch
        x, y, c, chips = _place()
        me, sibling = (x, y, c), (x, y, 1 - c)

        def rows(px, py, pc):
            return all_ref.at[pl.ds((4 * px + 2 * py + pc) * m_per, m_per), :]

        def copy(k, block, to, src=None):
            return pltpu.make_async_remote_copy(
                src_ref=rows(*block) if src is None else src, dst_ref=rows(*block), send_sem=send_sems.at[k],
                recv_sem=recv_sems.at[k], device_id=to, device_id_type=MESH)

        mine = pltpu.make_async_copy(x_ref, rows(*me), local_sem)
        mine.start()
        first = [copy(0, me, sibling, src=x_ref)]
        first += [copy(1 + j, me, (*chip, c), src=x_ref) for j, chip in enumerate(chips)]
        for cp in first:
            cp.start()
        passed = [copy(4 + j, (*chip, c), sibling) for j, chip in enumerate(chips)]
        for j, chip in enumerate(chips):
            copy(1 + j, (*chip, c), me).wait_recv()
            passed[j].start()
        copy(0, sibling, me).wait_recv()
        for j, chip in enumerate(chips):
            copy(4 + j, (*chip, 1 - c), me).wait_recv()
        for cp in first + passed:
            cp.wait_send()
        mine.wait()
        if reduce:
            acc = all_ref[0:m_per, :]
            for d in range(1, N_DEV):
                acc = acc + all_ref[d * m_per:(d + 1) * m_per, :]
            out_ref[...] = acc

    scratch = [pltpu.SemaphoreType.DMA((7,)), pltpu.SemaphoreType.DMA((7,)), pltpu.SemaphoreType.DMA]
    if reduce:
        scratch = [pltpu.VMEM((N_DEV * m_per, n), F32)] + scratch
    out_rows = m_per if reduce else N_DEV * m_per
    return pl.pallas_call(
        body, name=name, out_shape=jax.ShapeDtypeStruct((out_rows, n), v.dtype),
        in_specs=[pl.BlockSpec(memory_space=pltpu.VMEM)], out_specs=pl.BlockSpec(memory_space=pltpu.VMEM),
        scratch_shapes=scratch,
        compiler_params=pltpu.CompilerParams(vmem_limit_bytes=int(min(VMEM_CAP_BYTES, 4 * N_DEV * m_per * n * 4 + 16 * MIB))))(v)


def _slab(kind, ref, s):
    if kind == "cols":
        w = ref.shape[2] // N_CHIP
        return ref.at[:, :, pl.ds(s * w, w)]
    w = ref.shape[1] // N_CHIP
    return ref.at[:, pl.ds(s * w, w), :]


HBM = pl.BlockSpec(memory_space=pltpu.HBM)
SEM = pl.BlockSpec(memory_space=pltpu.SEMAPHORE)
EFFECT = pltpu.SideEffectType.DATAFLOW_SIDE_EFFECTING


def _in_hbm(v):
    return pltpu.with_memory_space_constraint(v, pltpu.HBM)


def _hbm_like(arrays):
    return [pltpu.HBM(v.shape, v.dtype) for v in arrays]


def _half_rows(ref, c):
    r = ref.shape[1] // 2
    return ref.at[:, pl.ds(c * r, r), :]


def _gather_copy(kinds, full, send_sems, recv_sems, a, j, peer, c, s_src, s_dst, halves):
    src, dst = _slab(kinds[a], full[a], s_src), _slab(kinds[a], full[a], s_dst)
    if halves:
        src, dst = _half_rows(src, c), _half_rows(dst, c)
    return pltpu.make_async_remote_copy(src_ref=src, dst_ref=dst, send_sem=send_sems.at[a * 3 + j],
                                        recv_sem=recv_sems.at[a * 3 + j], device_id=(*peer, c), device_id_type=MESH)


def _gather_start(fulls, kinds, after, *, name, halves=False):
    n = len(fulls)

    def body(*refs):
        k = n + len(after)
        full, send_sems, recv_sems, token = refs[:n], refs[k], refs[k + 1], refs[-1]
        x, y, c, chips = _place()
        s_me = 2 * x + y
        for a in range(n):
            for j, peer in enumerate(chips):
                _gather_copy(kinds, full, send_sems, recv_sems, a, j, peer, c, s_me, s_me, halves).start()
        token[...] = jnp.zeros_like(token)

    sems = pltpu.SemaphoreType.DMA((3 * n,))
    out = pl.pallas_call(
        body, name=name, out_shape=(sems, sems, *_hbm_like(fulls), jax.ShapeDtypeStruct((8, 128), F32)),
        in_specs=[HBM] * n + [ANY] * len(after), out_specs=(SEM, SEM, *[HBM] * n, pl.BlockSpec(memory_space=pltpu.VMEM)),
        input_output_aliases={a: 2 + a for a in range(n)},
        compiler_params=pltpu.CompilerParams(has_side_effects=EFFECT))(*[_in_hbm(f) for f in fulls], *after)
    return out[0], out[1], list(out[2:2 + n]), out[-1]


def _gather_wait(fulls, kinds, send, recv, after, *, name, halves=False):
    n = len(fulls)

    def body(*refs):
        full, send_sems, recv_sems = refs[:n], refs[n], refs[n + 1]
        x, y, c, chips = _place()
        s_me = 2 * x + y
        for a in range(n):
            for j, peer in enumerate(chips):
                cp = _gather_copy(kinds, full, send_sems, recv_sems, a, j, peer, c, s_me, 2 * peer[0] + peer[1], halves)
                cp.wait_send()
                cp.wait_recv()

    return pl.pallas_call(
        body, name=name, out_shape=_hbm_like(fulls), in_specs=[HBM] * n + [SEM, SEM] + [ANY] * len(after),
        out_specs=[HBM] * n, input_output_aliases={a: a for a in range(n)},
        compiler_params=pltpu.CompilerParams(has_side_effects=EFFECT))(*fulls, send, recv, *after)


def _sibling_fill(full, kind, *, name):
    def body(in_ref, out_ref, send_sems, recv_sems):
        x, y, c, chips = _place()

        def copy(j, peer, half):
            part = _half_rows(_slab(kind, out_ref, 2 * peer[0] + peer[1]), half)
            return pltpu.make_async_remote_copy(src_ref=part, dst_ref=part, send_sem=send_sems.at[j], recv_sem=recv_sems.at[j],
                                                device_id=(x, y, 1 - c), device_id_type=MESH)

        sent = [copy(j, peer, c) for j, peer in enumerate(chips)]
        for cp in sent:
            cp.start()
        for j, peer in enumerate(chips):
            copy(j, peer, 1 - c).wait_recv()
        for cp in sent:
            cp.wait_send()

    return pl.pallas_call(
        body, name=name, out_shape=jax.ShapeDtypeStruct(full.shape, full.dtype), in_specs=[ANY], out_specs=ANY,
        input_output_aliases={0: 0}, scratch_shapes=[pltpu.SemaphoreType.DMA((3,)), pltpu.SemaphoreType.DMA((3,))])(full)


def _scatter_copy(kinds, full, land, send_sems, recv_sems, a, j, peer, c):
    return pltpu.make_async_remote_copy(
        src_ref=_slab(kinds[a], full[a], 2 * peer[0] + peer[1]), dst_ref=land[a].at[j], send_sem=send_sems.at[a * 3 + j],
        recv_sem=recv_sems.at[a * 3 + j], device_id=(*peer, c), device_id_type=MESH)


def _scatter_start(fulls, lands, kinds, after, *, name):
    n = len(fulls)

    def body(*refs):
        k = 2 * n + len(after)
        full, land, send_sems, recv_sems, token = refs[:n], refs[n:2 * n], refs[k], refs[k + 1], refs[-1]
        _, _, c, chips = _place()
        for a in range(n):
            for j, peer in enumerate(chips):
                _scatter_copy(kinds, full, land, send_sems, recv_sems, a, j, peer, c).start()
        token[...] = jnp.zeros_like(token)

    sems = pltpu.SemaphoreType.DMA((3 * n,))
    out = pl.pallas_call(
        body, name=name,
        out_shape=(sems, sems, *_hbm_like(fulls), *_hbm_like(lands), jax.ShapeDtypeStruct((8, 128), F32)),
        in_specs=[HBM] * (2 * n) + [ANY] * len(after),
        out_specs=(SEM, SEM, *[HBM] * (2 * n), pl.BlockSpec(memory_space=pltpu.VMEM)),
        input_output_aliases={a: 2 + a for a in range(2 * n)},
        compiler_params=pltpu.CompilerParams(has_side_effects=EFFECT))(*[_in_hbm(f) for f in list(fulls) + list(lands)], *after)
    return out[0], out[1], list(out[2:2 + n]), list(out[2 + n:2 + 2 * n]), out[-1]


def _scatter_wait(fulls, lands, kinds, send, recv, after, *, name):
    n = len(fulls)

    def body(*refs):
        full, land, send_sems, recv_sems = refs[:n], refs[n:2 * n], refs[2 * n], refs[2 * n + 1]
        _, _, c, chips = _place()
        for a in range(n):
            for j, peer in enumerate(chips):
                cp = _scatter_copy(kinds, full, land, send_sems, recv_sems, a, j, peer, c)
                cp.wait_send()
                cp.wait_recv()

    out = pl.pallas_call(
        body, name=name, out_shape=_hbm_like(list(fulls) + list(lands)),
        in_specs=[HBM] * (2 * n) + [SEM, SEM] + [ANY] * len(after), out_specs=[HBM] * (2 * n),
        input_output_aliases={a: a for a in range(2 * n)},
        compiler_params=pltpu.CompilerParams(has_side_effects=EFFECT))(*fulls, *lands, send, recv, *after)
    return list(out[:n]), list(out[n:])


def _swap_copy(src, dst, send_sems, recv_sems, a):
    x, y, c, _ = _place()
    return pltpu.make_async_remote_copy(src_ref=src[a], dst_ref=dst[a], send_sem=send_sems.at[a], recv_sem=recv_sems.at[a],
                                        device_id=(x, y, 1 - c), device_id_type=MESH)


def _swap_start(parts, lands, *, name):
    n = len(parts)

    def body(*refs):
        src, dst, send_sems, recv_sems, token = refs[:n], refs[n:2 * n], refs[2 * n], refs[2 * n + 1], refs[-1]
        for a in range(n):
            _swap_copy(src, dst, send_sems, recv_sems, a).start()
        token[...] = jnp.zeros_like(token)

    sems = pltpu.SemaphoreType.DMA((n,))
    out = pl.pallas_call(
        body, name=name,
        out_shape=(sems, sems, *_hbm_like(parts), *_hbm_like(lands), jax.ShapeDtypeStruct((8, 128), F32)),
        in_specs=[HBM] * (2 * n), out_specs=(SEM, SEM, *[HBM] * (2 * n), pl.BlockSpec(memory_space=pltpu.VMEM)),
        input_output_aliases={a: 2 + a for a in range(2 * n)},
        compiler_params=pltpu.CompilerParams(has_side_effects=EFFECT))(*[_in_hbm(f) for f in list(parts) + list(lands)])
    return out[0], out[1], list(out[2:2 + n]), list(out[2 + n:2 + 2 * n]), out[-1]


def _swap_wait(parts, lands, send, recv, after, *, name):
    n = len(parts)

    def body(*refs):
        src, dst, send_sems, recv_sems = refs[:n], refs[n:2 * n], refs[2 * n], refs[2 * n + 1]
        for a in range(n):
            cp = _swap_copy(src, dst, send_sems, recv_sems, a)
            cp.wait_send()
            cp.wait_recv()

    out = pl.pallas_call(
        body, name=name, out_shape=_hbm_like(list(parts) + list(lands)),
        in_specs=[HBM] * (2 * n) + [SEM, SEM] + [ANY] * len(after), out_specs=[HBM] * (2 * n),
        input_output_aliases={a: a for a in range(2 * n)},
        compiler_params=pltpu.CompilerParams(has_side_effects=EFFECT))(*parts, *lands, send, recv, *after)
    return list(out[:n]), list(out[n:])


def _pad_rows(v, rows):
    return jnp.pad(v, ((0, rows - v.shape[0]), (0, 0)))


def _pack(vectors):
    flat = jnp.concatenate([v.reshape(-1) for v in vectors])
    n = -(-flat.shape[0] // 1024) * 1024
    return jnp.pad(flat, (0, n - flat.shape[0])).reshape(8, n // 8)


def _unpack(block, shapes):
    flat = block.reshape(-1)
    out, pos = [], 0
    for shp in shapes:
        size = 1
        for d in shp:
            size *= d
        out.append(flat[pos:pos + size].reshape(shp))
        pos += size
    return out


def kernel(x, c, norm_g, w_ada, b_ada, w_in, pool_w, pool_scale, attn_sink, conv_dw, conv_dw_b, conv_ln_g, conv_ln_b, conv_pw, w_branch_pool, w_branch_attn, w_branch_conv, w_out, final_g, loss_target, m_norm_g, m_w_ada, m_b_ada, m_w_in, m_pool_w, m_pool_scale, m_attn_sink, m_conv_dw, m_conv_dw_b, m_conv_ln_g, m_conv_ln_b, m_conv_pw, m_w_branch_pool, m_w_branch_attn, m_w_branch_conv, m_w_out, m_final_g, v_norm_g, v_w_ada, v_b_ada, v_w_in, v_pool_w, v_pool_scale, v_attn_sink, v_conv_dw, v_conv_dw_b, v_conv_ln_g, v_conv_ln_b, v_conv_pw, v_w_branch_pool, v_w_branch_attn, v_w_branch_conv, v_w_out, v_final_g):
    _, T, D = x.shape
    L = norm_g.shape[0]
    IN = w_in.shape[2] * N_CHIP
    assert IN == OFF_G + 3 * D and D % HALF == 0 and T % 512 == 0
    xi, yi, ci = lax.axis_index("x"), lax.axis_index("y"), lax.axis_index("c")
    chip = 2 * xi + yi
    dev = 2 * chip + ci
    x0 = x.reshape(T, D)
    target = loss_target.reshape(T, D)

    big = [("cols", w_in, m_w_in, v_w_in), ("cols", w_branch_pool, m_w_branch_pool, v_w_branch_pool),
           ("cols", w_branch_attn, m_w_branch_attn, v_w_branch_attn), ("cols", w_branch_conv, m_w_branch_conv, v_w_branch_conv),
           ("rows", w_out, m_w_out, v_w_out), ("rows", conv_pw, m_conv_pw, v_conv_pw), ("rows", pool_w, m_pool_w, v_pool_w)]
    kinds = [b[0] for b in big]
    n_big = len(big)
    as_groups = lambda t: t if t.ndim == 4 else t.reshape(L, 1, t.shape[1], t.shape[2])
    chip_arr = jnp.reshape(chip, (1,)).astype(jnp.int32)

    c_all = _small_exchange(_pad_rows(c, 8), False, name="gather_c")[0::8]
    taps_rows = -(-(L * CONV_KERNEL) // 8) * 8
    dw_blocks = _small_exchange(_pad_rows(conv_dw.reshape(L * CONV_KERNEL, -1), taps_rows), False, name="gather_taps")
    dw_blocks = dw_blocks.reshape(N_CHIP, 2, taps_rows, -1)[:, 0, :L * CONV_KERNEL]
    conv_dw_full = dw_blocks.reshape(N_CHIP, L, CONV_KERNEL, -1).transpose(1, 2, 0, 3).reshape(L, CONV_KERNEL, CONV_WIDTH)
    n_ada = w_ada.shape[2]
    b_slab = lax.dynamic_slice_in_dim(b_ada, chip * n_ada, n_ada, axis=1).reshape(L, 1, n_ada)
    mod_part = _ada_mod(c_all, w_ada, b_slab, name="ada_mod")
    mod_blocks = _small_exchange(mod_part.reshape(L * N_DEV, n_ada), False, name="gather_mod")
    mod_blocks = mod_blocks.reshape(N_CHIP, 2, L, N_DEV, n_ada)[:, 0]
    mod_all = mod_blocks.transpose(1, 2, 0, 3).reshape(L, N_DEV, 3 * D)
    mod = lax.dynamic_index_in_dim(mod_all, dev, axis=1, keepdims=False)
    shift, scale, gate = mod[:, :D], mod[:, D:2 * D], mod[:, 2 * D:]

    groups = [[0], list(range(1, n_big))]
    weights, gather_tokens = [], []
    for l in range(L):
        fulls = [_cast_into_full(chip_arr, as_groups(b[1]), l, b[0], name=f"cast{a}_{l}") for a, b in enumerate(big)]
        started_groups = []
        for gi, idx in enumerate(groups):
            send, recv, part, token = _gather_start([fulls[a] for a in idx], [kinds[a] for a in idx], [mod_all, conv_dw_full],
                                                    name=f"gather_start{l}_{gi}", halves=(l == 0 and gi == 0))
            started_groups.append((part, [kinds[a] for a in idx], send, recv))
            gather_tokens.append(token[0:1, 0:1])
        weights.append(started_groups)
    started = functools.reduce(lambda p, q: p + q, gather_tokens)

    row = lambda v: v.reshape(1, -1)

    xs, saved = [x0], []
    xl = x0
    full_w = []
    for l in range(L):
        h = _norm_mod(xl, row(norm_g[l]), row(scale[l]) + started if l == 0 else row(scale[l]), row(shift[l]), name=f"norm{l}")
        (part, part_kinds, send, recv), rest = weights[l]
        win_f, = _gather_wait(part, part_kinds, send, recv, [h], name=f"gather_wait{l}_0", halves=(l == 0))
        if l == 0:
            win_f = _sibling_fill(win_f, part_kinds[0], name="sibling_fill")
        proj = _mm(h, win_f, "nn", [BF16], name=f"proj{l}", b_layer=0)
        part, part_kinds, send, recv = rest
        wbp_f, wba_f, wbc_f, wout_f, cpw_f, poolw_f = _gather_wait(part, part_kinds, send, recv, [proj], name=f"gather_wait{l}_1")
        full_w.append((win_f, wbp_f, wba_f, wbc_f, wout_f, cpw_f, poolw_f))
        y_pool = _pool_fwd(proj, poolw_f, row(pool_scale[l]), name=f"pool{l}")
        o_attn, y_attn, lse = _attn_fwd(proj, attn_sink[l], name=f"attn{l}")
        s_conv, yc = _conv_fwd(proj, conv_dw_full[l], row(conv_dw_b[l]), row(conv_ln_g[l]), row(conv_ln_b[l]), name=f"conv{l}")
        cpre, y_conv = _mm(s_conv, cpw_f, "nn", [BF16, BF16], name=f"conv_pw{l}", b_layer=0, tn_cap=HALF,
                           extras=[(proj, "tile", OFF_CZ)], epilogue=lambda acc, z: (acc, acc * _silu(z.astype(F32))))
        merged, bp, ba, bc = _merge((y_pool, y_attn, y_conv), (wbp_f, wba_f, wbc_f), proj, D, name=f"merge{l}")
        x_new, o = _mm(merged, wout_f, "nn", [F32, BF16], name=f"out{l}", b_layer=0,
                       extras=[(xl, "tile", 0), (row(gate[l]), "row", 0)],
                       epilogue=lambda acc, xv, g: (xv + g * acc, acc))
        saved.append(dict(h=h, proj=proj, y_pool=y_pool, o_attn=o_attn, y_attn=y_attn, lse=lse, s_conv=s_conv, yc=yc, cpre=cpre,
                          y_conv=y_conv, merged=merged, bp=bp, ba=ba, bc=bc, o=o))
        xl = x_new
        xs.append(xl)

    loss_part, dx, d_final_g = _final_loss(xl, target, row(final_g), name="final_loss")
    loss = lax.psum(loss_part[0, 0], ("x", "y", "c"))

    small, dmods, scattering = [], [], {}
    scattered = jnp.zeros((1, 1), F32)
    for l in reversed(range(L)):
        sv = saved[l]
        proj = sv["proj"]
        win_f, wbp_f, wba_f, wbc_f, wout_f, cpw_f, poolw_f = full_w[l]
        dmo, d_gate = _gate_out_bwd(dx, sv["o"], row(gate[l]) + scattered, name=f"gate_out_bwd{l}")
        dmerged = _mm(dmo, wout_f, "nt", [BF16], name=f"d_merged{l}", b_layer=0)
        g_wout = _mm(sv["merged"], dmo, "tn", [BF16], name=f"g_wout{l}")
        dbp, dba, dbc, dgp, dga, dgc = _merge_bwd(dmerged, (sv["bp"], sv["ba"], sv["bc"]), proj, D, name=f"merge_bwd{l}")
        dy_pool = _mm(dbp, wbp_f, "nt", [BF16], name=f"dy_pool{l}", b_layer=0)
        dy_attn = _mm(dba, wba_f, "nt", [BF16], name=f"dy_attn{l}", b_layer=0)
        dy_conv = _mm(dbc, wbc_f, "nt", [BF16], name=f"dy_conv{l}", b_layer=0)
        g_wbp = _mm(sv["y_pool"], dbp, "tn", [BF16], name=f"g_wbp{l}")
        g_wba = _mm(sv["y_attn"], dba, "tn", [BF16], name=f"g_wba{l}")
        g_wbc = _mm(sv["y_conv"], dbc, "tn", [BF16], name=f"g_wbc{l}")
        dz_pool, dmn, d_pool_scale, g_poolw = _pool_bwd(proj, dy_pool, poolw_f, row(pool_scale[l]), name=f"pool_bwd{l}")
        du_pool = _pool_bwd_window(dmn, name=f"pool_bwd_window{l}")
        dq, dk, dv, dz_attn, d_sink = _attn_bwd(proj, attn_sink[l], sv["o_attn"], sv["lse"], dy_attn, name=f"attn_bwd{l}")
        dcpre, dz_conv = _conv_out_bwd(dy_conv, sv["cpre"], proj, name=f"conv_out_bwd{l}")
        ds_conv = _mm(dcpre, cpw_f, "nt", [BF16], name=f"ds_conv{l}", b_layer=0)
        g_cpw = _mm(sv["s_conv"], dcpre, "tn", [BF16], name=f"g_cpw{l}")
        taps = conv_dw_full[l]
        dyc, d_taps, d_dwb, d_lng, d_lnb = _conv_bwd(proj, sv["yc"], ds_conv, row(conv_ln_g[l]), row(conv_ln_b[l]),
                                                    name=f"conv_bwd{l}")
        da_conv, db_conv = _conv_bwd_input(proj, dyc, taps, name=f"conv_bwd_input{l}")
        dproj = jnp.concatenate([du_pool, dz_pool, dq, dk, dv, dz_attn, da_conv, db_conv, dz_conv, dgp, dga, dgc], axis=1)
        dh = _mm(dproj, win_f, "nt", [F32], name=f"dh{l}", b_layer=0, tm_cap=2048, tk_cap=1536)
        g_win = _mm(sv["h"], dproj, "tn", [BF16], name=f"g_win{l}", tn_cap=768)
        dx, d_ng, d_scale, d_shift = _norm_mod_bwd(xs[l], dh, dx, row(norm_g[l]), row(scale[l]), name=f"norm_bwd{l}")
        dmods.append(jnp.concatenate([d_shift, d_scale, d_gate], axis=1))
        small.append([d_ng, d_pool_scale, d_sink[:, :N_Q_HEADS], d_taps, d_dwb, d_lng, d_lnb])
        before_start = []
        if l == 0:
            stacked = [jnp.stack([small[L - 1 - k][q] for k in range(L)]) for q in range(len(small[0]))]
            dmod_mine = jnp.concatenate(dmods[::-1], axis=0)
            small_shapes = [s.shape for s in stacked] + [d_final_g.shape, dmod_mine.shape]
            reduced = _small_exchange(_pack(stacked + [d_final_g, dmod_mine]), True, name="reduce_small")
            dmod_all = _small_exchange(_pad_rows(dmod_mine, 8), False, name="gather_dmod").reshape(N_DEV, 8, 3 * D)[:, :L]
            before_start = [reduced, dmod_all]
        grads = [g[None] for g in (g_win, g_wbp, g_wba, g_wbc, g_wout, g_cpw)] + [g_poolw.astype(BF16)]
        lands = [lax.empty((3,) + as_groups(b[1]).shape[1:], BF16) for b in big]
        send, recv, grads, lands, token = _scatter_start(grads, lands, kinds, before_start, name=f"scatter_start{l}")
        scattering[l] = (grads, lands, send, recv)
        scattered = token[0:1, 0:1]
    grad_x = dx.reshape(1, T, D)

    r_ng, r_ps, r_sink, r_taps, r_dwb, r_lng, r_lnb, r_fg, r_bada = _unpack(reduced, small_shapes)
    g_norm_g, g_pool_scale, g_attn_sink = r_ng.reshape(L, D), r_ps.reshape(L, POOL_WIDTH), r_sink.reshape(L, N_Q_HEADS)
    g_conv_dw = lax.dynamic_slice_in_dim(r_taps, chip * (CONV_WIDTH // N_CHIP), CONV_WIDTH // N_CHIP, axis=2)
    g_dwb, g_lng, g_lnb = r_dwb.reshape(L, CONV_WIDTH), r_lng.reshape(L, CONV_WIDTH), r_lnb.reshape(L, CONV_WIDTH)
    g_final_g, g_b_ada = r_fg.reshape(D), r_bada

    dmod_slab = lax.dynamic_slice_in_dim(dmod_all, chip * n_ada, n_ada, axis=2).transpose(1, 0, 2)
    g_w_ada = _ada_grad(c_all.T, dmod_slab + scattered, name="ada_grad")

    flat = lambda t: t.reshape(-1, t.shape[-1])
    ada = [t.reshape(w_ada.shape) for t in _adamw(flat(w_ada), flat(m_w_ada), flat(v_w_ada), [flat(g_w_ada)], name="adamw_ada")]
    small_w = [norm_g, b_ada, pool_scale, attn_sink, conv_dw, conv_dw_b, conv_ln_g, conv_ln_b, final_g]
    small_m = [m_norm_g, m_b_ada, m_pool_scale, m_attn_sink, m_conv_dw, m_conv_dw_b, m_conv_ln_g, m_conv_ln_b, m_final_g]
    small_v = [v_norm_g, v_b_ada, v_pool_scale, v_attn_sink, v_conv_dw, v_conv_dw_b, v_conv_ln_g, v_conv_ln_b, v_final_g]
    small_g = [g_norm_g, g_b_ada, g_pool_scale, g_attn_sink, g_conv_dw, g_dwb, g_lng, g_lnb, g_final_g]
    sm = _adamw(_pack(small_w), _pack(small_m), _pack(small_v), [_pack(small_g)], name="adamw_small")
    shp = [t.shape for t in small_w]
    sm_g, sm_d, sm_m, sm_v = [_unpack(t, shp) for t in sm]

    stacked3 = lambda t: t.reshape(L, -1, t.shape[-1])
    two = lambda t: t.reshape(-1, t.shape[-1])
    outs = [[lax.empty(stacked3(b[1]).shape, F32) for _ in range(4)] for b in big]

    def update(l, swapping, after):
        parts, lands, send, recv = swapping
        parts, others = _swap_wait(parts, lands, send, recv, after, name=f"swap_wait{l}")
        for a, (_, w, m, v) in enumerate(big):
            outs[a] = _adamw_layer(stacked3(w), stacked3(m), stacked3(v), l, [two(parts[a]), two(others[a])], outs[a],
                                   name=f"adamw{a}_{l}")

    after = [ada[0], sm[0]]
    swapping = None
    for l in reversed(range(L)):
        if l == 0 and swapping is not None:
            update(1, swapping, after)
            after, swapping = [outs[a][0] for a in range(n_big)], None
        grads, lands, send, recv = scattering[l]
        grads, lands = _scatter_wait(grads, lands, kinds, send, recv, after, name=f"scatter_wait{l}")
        parts = [_sum_contribs(chip_arr, grads[a], lands[a], kinds[a], name=f"sum_grads{a}_{l}") for a in range(n_big)]
        send, recv, parts, lands, token = _swap_start(parts, [lax.empty(p.shape, F32) for p in parts], name=f"swap_start{l}")
        if swapping is not None:
            update(l + 1, swapping, [token])
            after = [outs[a][0] for a in range(n_big)]
        swapping = (parts, lands, send, recv)
    update(0, swapping, [outs[a][0] for a in range(n_big)] if L > 1 else [ada[0]])
    results = {a: [t.reshape(big[a][1].shape) for t in outs[a]] for a in range(n_big)}

    def leaves(k, pick):
        s = pick
        return [s[0], ada[k], s[1], results[0][k], results[6][k], s[2], s[3], s[4], s[5], s[6], s[7], results[5][k],
                results[1][k], results[2][k], results[3][k], results[4][k], s[8]]

    return (loss, grad_x, *leaves(0, sm_g), *leaves(1, sm_d), *leaves(2, sm_m), *leaves(3, sm_v))
```

```python
import functools

import jax
import jax.numpy as jnp
from jax import lax
from jax.experimental import pallas as pl
from jax.experimental.pallas import tpu as pltpu

F32 = jnp.float32
BF16 = jnp.bfloat16
MESH = pl.DeviceIdType.MESH
ANY = pl.BlockSpec(memory_space=pl.ANY)

CHUNK = 64
HEAD_DIM = 64
N_Q_HEADS = 16
N_KV_HEADS = 4
Q_PER_KV = N_Q_HEADS // N_KV_HEADS
WINDOW_CHUNKS = 2
POOL_WIDTH = 1024
POOL_WINDOWS = (2, 4, 8, 16)
POOL_GROUP = 256
ATTN_WIDTH = 1024
KV_WIDTH = 256
CONV_WIDTH = 1024
CONV_KERNEL = 31
EPS = 1e-6
OFF_U, OFF_Z, OFF_Q, OFF_K, OFF_V, OFF_AZ, OFF_CA, OFF_CB, OFF_CZ, OFF_G = (
    0, 1024, 2048, 3072, 3328, 3584, 4608, 5632, 6656, 7680)
HALF = 512
POOL_HALO = 16
CONV_HALO = 32
ATTN_Q_BLOCK = 256
ATTN_HALO = WINDOW_CHUNKS * CHUNK
NEG_INF = -1e30

ADAM_LR, ADAM_B1, ADAM_B2, ADAM_EPS, ADAM_WD, ADAM_STEP = 0.001, 0.9, 0.999, 1e-08, 0.01, 10

N_DEV = 8
N_CHIP = 4
VMEM_CAP_BYTES = 56 * 2**20
MIB = 2**20


def _div(n, cap, mult=128):
    if n <= cap:
        return n
    best = None
    for t in range(mult, cap + 1, mult):
        if n % t == 0:
            best = t
    assert best is not None, (n, cap, mult)
    return best


def _params(n_grid, vmem_bytes=None):
    kw = dict(dimension_semantics=("arbitrary",) * n_grid)
    if vmem_bytes is not None:
        kw["vmem_limit_bytes"] = int(min(max(vmem_bytes * 5 // 4 + 4 * MIB, 32 * MIB), VMEM_CAP_BYTES))
    return pltpu.CompilerParams(**kw)


def _silu(z):
    return z * jax.nn.sigmoid(z)


def _dsilu(z):
    s = jax.nn.sigmoid(z)
    return s * (1.0 + z * (1.0 - s))


def _nbytes(shape, dtype):
    n = 1
    for d in shape:
        n *= d
    return n * jnp.dtype(dtype).itemsize


MM_VMEM_BUDGET = 50 * MIB


def _mm(a, b, mode, out_dtypes, *, name, b_layer=None, extras=(), epilogue=None, tm_cap=2048, tn_cap=1024, tk_cap=None):
    if mode == "tn":
        K, M = a.shape
        N = b.shape[-1]
    elif mode == "nt":
        M, K = a.shape
        N = b.shape[-2]
    else:
        M, K = a.shape
        N = b.shape[-1]
    tk = _div(K, tk_cap or 2048)
    nk = K // tk
    n_out = len(out_dtypes)
    n_ex = len(extras)
    stacked = b.ndim == 3
    per_elem = sum(jnp.dtype(dt).itemsize for dt in out_dtypes) + sum(e[0].dtype.itemsize for e in extras if e[1] == "tile")

    def need(tm, tn):
        return (2 * (tm * tk * a.dtype.itemsize + tk * tn * b.dtype.itemsize + tm * tn * per_elem)
                + tm * tn * 4 * (2 if nk > 1 else 1))

    tm, tn = min(((_div(M, mc), _div(N, nc)) for mc in (tm_cap, tm_cap // 2, tm_cap // 4) for nc in (tn_cap, tn_cap // 2)),
                 key=lambda t: (need(*t) > MM_VMEM_BUDGET, -t[0], -t[1]))

    def body(*refs):
        a_ref, b_ref = refs[0], refs[1]
        ex_refs = refs[2:2 + n_ex]
        pos = 2 + n_ex
        out_refs = refs[pos:pos + n_out]
        acc_ref = refs[pos + n_out] if nk > 1 else None
        k = pl.program_id(2)
        dims = {"nn": (((1,), (0,)), ((), ())), "nt": (((1,), (1,)), ((), ())), "tn": (((0,), (0,)), ((), ()))}[mode]

        def product():
            return lax.dot_general(a_ref[...].astype(BF16), b_ref[...].astype(BF16), dims, preferred_element_type=F32)

        def finish(acc):
            vals = epilogue(acc, *[r[...] for r in ex_refs]) if epilogue is not None else (acc,)
            for r, v in zip(out_refs, vals):
                r[...] = v.astype(r.dtype)

        if nk == 1:
            finish(product())
        else:
            @pl.when(k == 0)
            def _():
                acc_ref[...] = jnp.zeros_like(acc_ref)

            acc_ref[...] += product()

            @pl.when(k == nk - 1)
            def _():
                finish(acc_ref[...])

    if mode == "tn":
        a_spec = pl.BlockSpec((tk, tm), lambda i, j, k: (k, i))
    else:
        a_spec = pl.BlockSpec((tm, tk), lambda i, j, k: (i, k))
    if mode == "nt":
        b_blk, b_idx = (tn, tk), (lambda i, j, k: (j, k))
    else:
        b_blk, b_idx = (tk, tn), (lambda i, j, k: (k, j))
    if stacked:
        b_spec = pl.BlockSpec((None,) + b_blk, lambda i, j, k, f=b_idx: (b_layer,) + f(i, j, k))
    else:
        b_spec = pl.BlockSpec(b_blk, b_idx)
    in_specs = [a_spec, b_spec]
    operands = [a, b]
    vmem = 2 * (tm * tk * a.dtype.itemsize + tk * tn * b.dtype.itemsize) + tm * tn * 4 * 3
    for arr, kind, off in extras:
        if kind == "tile":
            assert off % tn == 0, (name, off, tn)
            in_specs.append(pl.BlockSpec((tm, tn), lambda i, j, k, o=off // tn: (i, o + j)))
        else:
            in_specs.append(pl.BlockSpec((1, tn), lambda i, j, k: (0, j)))
        operands.append(arr)
        vmem += 2 * tm * tn * arr.dtype.itemsize
    out_shape = [jax.ShapeDtypeStruct((M, N), dt) for dt in out_dtypes]
    out_specs = [pl.BlockSpec((tm, tn), lambda i, j, k: (i, j)) for _ in out_dtypes]
    vmem += sum(2 * tm * tn * jnp.dtype(dt).itemsize for dt in out_dtypes)
    outs = pl.pallas_call(
        body, name=name, grid=(M // tm, N // tn, nk), in_specs=in_specs, out_specs=out_specs, out_shape=out_shape,
        scratch_shapes=[pltpu.VMEM((tm, tn), F32)] if nk > 1 else [], compiler_params=_params(3, vmem))(*operands)
    return outs[0] if n_out == 1 else outs


def _merge(ys, wbs, proj, D, *, name):
    T = ys[0].shape[0]
    tm = _div(T, 1024)
    tn = HALF
    kw = ys[0].shape[1]
    g_off = [(OFF_G + b * D) // tn for b in range(3)]

    def body(y0, y1, y2, w0, w1, w2, g0, g1, g2, merged_ref, b0, b1, b2):
        acc = None
        for y, w, g, bo in ((y0, w0, g0, b0), (y1, w1, g1, b1), (y2, w2, g2, b2)):
            p = jnp.dot(y[...], w[...], preferred_element_type=F32)
            bo[...] = p.astype(bo.dtype)
            t = jax.nn.sigmoid(g[...].astype(F32)) * p
            acc = t if acc is None else acc + t
        merged_ref[...] = acc.astype(merged_ref.dtype)

    y_spec = pl.BlockSpec((tm, kw), lambda i, j: (i, 0))
    w_spec = pl.BlockSpec((None, kw, tn), lambda i, j: (0, 0, j))
    g_specs = [pl.BlockSpec((tm, tn), lambda i, j, o=o: (i, o + j)) for o in g_off]
    o_spec = pl.BlockSpec((tm, tn), lambda i, j: (i, j))
    vmem = 2 * (3 * tm * kw * 2 + 3 * kw * tn * 2 + 3 * tm * tn * proj.dtype.itemsize + 4 * tm * tn * 2) + 4 * tm * tn * 4
    return pl.pallas_call(
        body, name=name, grid=(T // tm, D // tn), in_specs=[y_spec] * 3 + [w_spec] * 3 + g_specs,
        out_specs=[o_spec] * 4, out_shape=[jax.ShapeDtypeStruct((T, D), BF16)] * 4,
        compiler_params=_params(2, vmem))(*ys, *wbs, proj, proj, proj)


def _row_tile(T, width, n_arrays):
    cap = max(8, (24 * MIB) // (2 * n_arrays * width * 4))
    return _div(T, min(cap, 1024), 8)


def _norm_mod(x, ng, scale, shift, *, name):
    T, D = x.shape
    tm = _row_tile(T, D, 3)

    def body(x_ref, ng_ref, sc_ref, sh_ref, h_ref):
        xv = x_ref[...]
        r = lax.rsqrt(jnp.mean(xv * xv, axis=-1, keepdims=True) + EPS)
        h = (xv * r) * ng_ref[...] * (1.0 + sc_ref[...]) + sh_ref[...]
        h_ref[...] = h.astype(h_ref.dtype)

    row = pl.BlockSpec((1, D), lambda i: (0, 0))
    tile = pl.BlockSpec((tm, D), lambda i: (i, 0))
    return pl.pallas_call(body, name=name, grid=(T // tm,), in_specs=[tile, row, row, row], out_specs=tile,
                          out_shape=jax.ShapeDtypeStruct((T, D), BF16), compiler_params=_params(1))(x, ng, scale, shift)


def _norm_mod_bwd(x, dh, dx_out, ng, scale, *, name):
    T, D = x.shape
    tm = _row_tile(T, D, 6)

    def body(x_ref, dh_ref, dxo_ref, ng_ref, sc_ref, dx_ref, dng_ref, dsc_ref, dsh_ref):
        i = pl.program_id(0)
        xv = x_ref[...]
        dh_v = dh_ref[...].astype(F32)
        r = lax.rsqrt(jnp.mean(xv * xv, axis=-1, keepdims=True) + EPS)
        xn = xv * r
        one_sc = 1.0 + sc_ref[...]
        dxn = dh_v * (ng_ref[...] * one_sc)
        dx_ref[...] = dxo_ref[...] + r * (dxn - xn * jnp.mean(dxn * xn, axis=-1, keepdims=True))
        t = dh_v * xn
        parts = (jnp.sum(t * one_sc, axis=0, keepdims=True), jnp.sum(t * ng_ref[...], axis=0, keepdims=True),
                 jnp.sum(dh_v, axis=0, keepdims=True))
        for ref, p in zip((dng_ref, dsc_ref, dsh_ref), parts):
            @pl.when(i == 0)
            def _(ref=ref, p=p):
                ref[...] = p

            @pl.when(i > 0)
            def _(ref=ref, p=p):
                ref[...] += p

    row = pl.BlockSpec((1, D), lambda i: (0, 0))
    tile = pl.BlockSpec((tm, D), lambda i: (i, 0))
    vec = jax.ShapeDtypeStruct((1, D), F32)
    return pl.pallas_call(body, name=name, grid=(T // tm,), in_specs=[tile, tile, tile, row, row],
                          out_specs=[tile, row, row, row], out_shape=[jax.ShapeDtypeStruct((T, D), F32), vec, vec, vec],
                          compiler_params=_params(1))(x, dh, dx_out, ng, scale)


def _final_loss(x, target, fg, *, name):
    T, D = x.shape
    tm = _row_tile(T, D, 4)

    def body(x_ref, t_ref, g_ref, loss_ref, dx_ref, dg_ref):
        i = pl.program_id(0)
        xv = x_ref[...]
        r = lax.rsqrt(jnp.mean(xv * xv, axis=-1, keepdims=True) + EPS)
        xn = xv * r
        err = xn * g_ref[...] - t_ref[...]
        part = 0.5 * jnp.sum(jnp.sum(err * err, axis=1, keepdims=True), axis=0, keepdims=True) / D
        dy = err / D
        dxn = dy * g_ref[...]
        dx_ref[...] = r * (dxn - xn * jnp.mean(dxn * xn, axis=-1, keepdims=True))
        dg = jnp.sum(dy * xn, axis=0, keepdims=True)

        @pl.when(i == 0)
        def _():
            loss_ref[...] = part
            dg_ref[...] = dg

        @pl.when(i > 0)
        def _():
            loss_ref[...] += part
            dg_ref[...] += dg

    row = pl.BlockSpec((1, D), lambda i: (0, 0))
    tile = pl.BlockSpec((tm, D), lambda i: (i, 0))
    one = pl.BlockSpec((1, 1), lambda i: (0, 0))
    return pl.pallas_call(
        body, name=name, grid=(T // tm,), in_specs=[tile, tile, row], out_specs=[one, tile, row],
        out_shape=[jax.ShapeDtypeStruct((1, 1), F32), jax.ShapeDtypeStruct((T, D), F32), jax.ShapeDtypeStruct((1, D), F32)],
        compiler_params=_params(1))(x, target, fg)


def _gate_out_bwd(dx_out, o, gate, *, name):
    T, D = dx_out.shape
    tm = _row_tile(T, D, 3)

    def body(dx_ref, o_ref, g_ref, dmo_ref, dg_ref):
        i = pl.program_id(0)
        dxv = dx_ref[...]
        dmo_ref[...] = (dxv * g_ref[...]).astype(dmo_ref.dtype)
        p = jnp.sum(dxv * o_ref[...].astype(F32), axis=0, keepdims=True)

        @pl.when(i == 0)
        def _():
            dg_ref[...] = p

        @pl.when(i > 0)
        def _():
            dg_ref[...] += p

    row = pl.BlockSpec((1, D), lambda i: (0, 0))
    tile = pl.BlockSpec((tm, D), lambda i: (i, 0))
    return pl.pallas_call(body, name=name, grid=(T // tm,), in_specs=[tile, tile, row], out_specs=[tile, row],
                          out_shape=[jax.ShapeDtypeStruct((T, D), BF16), jax.ShapeDtypeStruct((1, D), F32)],
                          compiler_params=_params(1))(dx_out, o, gate)


def _merge_bwd(dmerged, branches, proj, D, *, name):
    T = dmerged.shape[0]
    tm = _div(T, 1024)
    tn = HALF
    g_off = [(OFF_G + b * D) // tn for b in range(3)]

    def body(dm_ref, b0, b1, b2, g0, g1, g2, db0, db1, db2, dg0, dg1, dg2):
        dm = dm_ref[...].astype(F32)
        for b, g, db, dg in ((b0, g0, db0, dg0), (b1, g1, db1, dg1), (b2, g2, db2, dg2)):
            s = jax.nn.sigmoid(g[...].astype(F32))
            db[...] = (dm * s).astype(db.dtype)
            dg[...] = (dm * b[...].astype(F32) * s * (1.0 - s)).astype(dg.dtype)

    tile = pl.BlockSpec((tm, tn), lambda i, j: (i, j))
    g_specs = [pl.BlockSpec((tm, tn), lambda i, j, o=o: (i, o + j)) for o in g_off]
    return pl.pallas_call(body, name=name, grid=(T // tm, D // tn), in_specs=[tile] * 4 + g_specs, out_specs=[tile] * 6,
                          out_shape=[jax.ShapeDtypeStruct((T, D), BF16)] * 6,
                          compiler_params=_params(2))(dmerged, *branches, proj, proj, proj)


def _conv_out_bwd(dy, cpre, proj, *, name):
    T = dy.shape[0]
    tm = _div(T, 1024)
    tn = HALF

    def body(dy_ref, c_ref, z_ref, dc_ref, dz_ref):
        dyv = dy_ref[...].astype(F32)
        z = z_ref[...].astype(F32)
        dc_ref[...] = (dyv * _silu(z)).astype(dc_ref.dtype)
        dz_ref[...] = (dyv * c_ref[...].astype(F32) * _dsilu(z)).astype(dz_ref.dtype)

    tile = pl.BlockSpec((tm, tn), lambda i, j: (i, j))
    z_spec = pl.BlockSpec((tm, tn), lambda i, j: (i, OFF_CZ // tn + j))
    return pl.pallas_call(body, name=name, grid=(T // tm, CONV_WIDTH // tn), in_specs=[tile, tile, z_spec],
                          out_specs=[tile, tile], out_shape=[jax.ShapeDtypeStruct((T, CONV_WIDTH), BF16)] * 2,
                          compiler_params=_params(2))(dy, cpre, proj)


def _pool_mixed(ext, u, g, row0):
    w = POOL_WINDOWS[g]
    s = ext
    shift = 1
    while shift < w:
        s = s + pltpu.roll(s, shift, 0)
        shift *= 2
    tm = u.shape[0]
    t = row0 + lax.broadcasted_iota(jnp.int32, (tm, 1), 0)
    inv = 1.0 / jnp.minimum(t + 1, w).astype(F32)
    return s[POOL_HALO:, :] * inv - u, inv


def _pool_specs(T, tm):
    per = tm // POOL_HALO
    cur = lambda col: pl.BlockSpec((tm, POOL_WIDTH), lambda i, c=col: (i, c))
    prev = pl.BlockSpec((POOL_HALO, POOL_WIDTH), lambda i: (jnp.maximum(i * per - 1, 0), 0))
    return cur, prev


def _pool_fwd(proj, pool_w, scale, *, name):
    T = proj.shape[0]
    tm = _div(T, 512, 16)
    cur, prev = _pool_specs(T, tm)

    def body(u_ref, up_ref, z_ref, w_ref, sc_ref, y_ref):
        i = pl.program_id(0)
        u = u_ref[...].astype(F32)
        halo = jnp.where(i == 0, 0.0, up_ref[...].astype(F32))
        ext = jnp.concatenate([halo, u], axis=0)
        for g in range(len(POOL_WINDOWS)):
            cols = slice(g * POOL_GROUP, (g + 1) * POOL_GROUP)
            mixed, _ = _pool_mixed(ext[:, cols], u[:, cols], g, i * tm)
            p = jnp.dot(mixed.astype(BF16), w_ref[g], preferred_element_type=F32)
            y = p * sc_ref[:, cols] * _silu(z_ref[:, cols].astype(F32))
            y_ref[:, cols] = y.astype(y_ref.dtype)

    w_spec = pl.BlockSpec((len(POOL_WINDOWS), POOL_GROUP, POOL_GROUP), lambda i: (0, 0, 0))
    row = pl.BlockSpec((1, POOL_WIDTH), lambda i: (0, 0))
    return pl.pallas_call(body, name=name, grid=(T // tm,), in_specs=[cur(0), prev, cur(1), w_spec, row],
                          out_specs=pl.BlockSpec((tm, POOL_WIDTH), lambda i: (i, 0)),
                          out_shape=jax.ShapeDtypeStruct((T, POOL_WIDTH), BF16),
                          compiler_params=_params(1))(proj, proj, proj, pool_w, scale)


def _pool_bwd(proj, dy, pool_w, scale, *, name):
    T = proj.shape[0]
    tm = _div(T, 512, 16)
    cur, prev = _pool_specs(T, tm)
    n_g = len(POOL_WINDOWS)

    def body(u_ref, up_ref, z_ref, dy_ref, w_ref, sc_ref, dz_ref, dmn_ref, dsc_ref, dw_ref):
        i = pl.program_id(0)
        u = u_ref[...].astype(F32)
        halo = jnp.where(i == 0, 0.0, up_ref[...].astype(F32))
        ext = jnp.concatenate([halo, u], axis=0)
        for g in range(n_g):
            cols = slice(g * POOL_GROUP, (g + 1) * POOL_GROUP)
            mixed, inv = _pool_mixed(ext[:, cols], u[:, cols], g, i * tm)
            mixed = mixed.astype(BF16)
            w = w_ref[g]
            p = jnp.dot(mixed, w, preferred_element_type=F32)
            z = z_ref[:, cols].astype(F32)
            dyv = dy_ref[:, cols].astype(F32)
            sc = sc_ref[:, cols]
            dypre = dyv * _silu(z)
            dz_ref[:, cols] = (dyv * (p * sc) * _dsilu(z)).astype(dz_ref.dtype)
            dsc = jnp.sum(dypre * p, axis=0, keepdims=True)
            dp = (dypre * sc).astype(BF16)
            dwg = lax.dot_general(mixed, dp, (((0,), (0,)), ((), ())), preferred_element_type=F32)
            dmixed = lax.dot_general(dp, w, (((1,), (1,)), ((), ())), preferred_element_type=F32)
            dmn_ref[:, cols] = dmixed * inv

            @pl.when(i == 0)
            def _(g=g, cols=cols, dsc=dsc, dwg=dwg):
                dsc_ref[:, cols] = dsc
                dw_ref[g] = dwg

            @pl.when(i > 0)
            def _(g=g, cols=cols, dsc=dsc, dwg=dwg):
                dsc_ref[:, cols] += dsc
                dw_ref[g] += dwg

    w_spec = pl.BlockSpec((n_g, POOL_GROUP, POOL_GROUP), lambda i: (0, 0, 0))
    row = pl.BlockSpec((1, POOL_WIDTH), lambda i: (0, 0))
    tile = pl.BlockSpec((tm, POOL_WIDTH), lambda i: (i, 0))
    dw_spec = pl.BlockSpec((n_g, POOL_GROUP, POOL_GROUP), lambda i: (0, 0, 0))
    return pl.pallas_call(
        body, name=name, grid=(T // tm,), in_specs=[cur(0), prev, cur(1), tile, w_spec, row],
        out_specs=[tile, tile, row, dw_spec],
        out_shape=[jax.ShapeDtypeStruct((T, POOL_WIDTH), BF16), jax.ShapeDtypeStruct((T, POOL_WIDTH), F32),
                   jax.ShapeDtypeStruct((1, POOL_WIDTH), F32), jax.ShapeDtypeStruct((n_g, POOL_GROUP, POOL_GROUP), F32)],
        compiler_params=_params(1))(proj, proj, proj, dy, pool_w, scale)


def _pool_bwd_window(dmn, *, name):
    T = dmn.shape[0]
    tm = _div(T, 512, 16)
    per = tm // POOL_HALO
    last = T // POOL_HALO - 1
    nb = T // tm

    def body(c_ref, n_ref, du_ref):
        i = pl.program_id(0)
        cur = c_ref[...]
        nxt = jnp.where(i == nb - 1, 0.0, n_ref[...])
        ext = jnp.concatenate([cur, nxt], axis=0)
        rows = tm + POOL_HALO
        t = i * tm + lax.broadcasted_iota(jnp.int32, (tm, 1), 0)
        for g, w in enumerate(POOL_WINDOWS):
            cols = slice(g * POOL_GROUP, (g + 1) * POOL_GROUP)
            s = ext[:, cols]
            shift = 1
            while shift < w:
                s = s + pltpu.roll(s, rows - shift, 0)
                shift *= 2
            cnt = jnp.minimum(t + 1, w).astype(F32)
            du_ref[:, cols] = (s[:tm, :] - cur[:, cols] * cnt).astype(du_ref.dtype)

    tile = pl.BlockSpec((tm, POOL_WIDTH), lambda i: (i, 0))
    nxt = pl.BlockSpec((POOL_HALO, POOL_WIDTH), lambda i: (jnp.minimum((i + 1) * per, last), 0))
    return pl.pallas_call(body, name=name, grid=(nb,), in_specs=[tile, nxt], out_specs=tile,
                          out_shape=jax.ShapeDtypeStruct((T, POOL_WIDTH), BF16), compiler_params=_params(1))(dmn, dmn)


def _attn_mask(i):
    qb, keys = ATTN_Q_BLOCK, ATTN_Q_BLOCK + ATTN_HALO
    qi = lax.broadcasted_iota(jnp.int32, (qb, keys), 0) // CHUNK
    kj = lax.broadcasted_iota(jnp.int32, (qb, keys), 1) // CHUNK - WINDOW_CHUNKS
    return (kj <= qi) & (kj >= qi - WINDOW_CHUNKS) & (kj + i * (qb // CHUNK) >= 0)


def _attn_specs(T, order):
    qb = ATTN_Q_BLOCK
    per = qb // ATTN_HALO
    cur = lambda width, col: pl.BlockSpec((qb, width), lambda i, c=col: (order(i), c))
    prev = lambda col: pl.BlockSpec((ATTN_HALO, KV_WIDTH), lambda i, c=col: (jnp.maximum(order(i) * per - 1, 0), c))
    return cur, prev


def _attn_fwd(proj, sink, *, name):
    T = proj.shape[0]
    qb = ATTN_Q_BLOCK
    cur, prev = _attn_specs(T, lambda i: i)

    def body(sink_ref, q_ref, kc_ref, kp_ref, vc_ref, vp_ref, z0_ref, z1_ref, o_ref, y_ref, lse_ref):
        i = pl.program_id(0)
        q = q_ref[...].astype(BF16)
        kk = jnp.concatenate([kp_ref[...], kc_ref[...]], axis=0).astype(BF16)
        vv = jnp.concatenate([vp_ref[...], vc_ref[...]], axis=0).astype(BF16)
        mask = _attn_mask(i)
        lane = lax.broadcasted_iota(jnp.int32, (qb, 128), 1)
        lse = jnp.zeros((qb, 128), F32)
        for h in range(N_Q_HEADS):
            hs = slice(h * HEAD_DIM, (h + 1) * HEAD_DIM)
            ks = slice((h // Q_PER_KV) * HEAD_DIM, (h // Q_PER_KV + 1) * HEAD_DIM)
            s = lax.dot_general(q[:, hs], kk[:, ks], (((1,), (1,)), ((), ())), preferred_element_type=F32)
            s = jnp.where(mask, s * (HEAD_DIM ** -0.5), NEG_INF)
            sk = sink_ref[h]
            m = jnp.maximum(jnp.max(s, axis=1, keepdims=True), sk)
            p = jnp.exp(s - m)
            den = jnp.sum(p, axis=1, keepdims=True) + jnp.exp(sk - m)
            oh = jnp.dot(p.astype(BF16), vv[:, ks], preferred_element_type=F32) / den
            zr = z0_ref if h < N_Q_HEADS // 2 else z1_ref
            zs = slice((h % (N_Q_HEADS // 2)) * HEAD_DIM, (h % (N_Q_HEADS // 2) + 1) * HEAD_DIM)
            o_ref[:, hs] = oh.astype(o_ref.dtype)
            y_ref[:, hs] = (oh * _silu(zr[:, zs].astype(F32))).astype(y_ref.dtype)
            lse = jnp.where(lane == h, m + jnp.log(den), lse)
        lse_ref[...] = lse

    kcol, vcol = OFF_K // KV_WIDTH, OFF_V // KV_WIDTH
    tile = pl.BlockSpec((qb, ATTN_WIDTH), lambda i: (i, 0))
    in_specs = [pl.BlockSpec(memory_space=pltpu.SMEM), cur(ATTN_WIDTH, OFF_Q // ATTN_WIDTH), cur(KV_WIDTH, kcol), prev(kcol),
                cur(KV_WIDTH, vcol), prev(vcol), cur(HALF, OFF_AZ // HALF), cur(HALF, OFF_AZ // HALF + 1)]
    return pl.pallas_call(
        body, name=name, grid=(T // qb,), in_specs=in_specs,
        out_specs=[tile, tile, pl.BlockSpec((qb, 128), lambda i: (i, 0))],
        out_shape=[jax.ShapeDtypeStruct((T, ATTN_WIDTH), BF16), jax.ShapeDtypeStruct((T, ATTN_WIDTH), BF16),
                   jax.ShapeDtypeStruct((T, 128), F32)],
        compiler_params=_params(1))(sink, *([proj] * 7))


def _attn_bwd(proj, sink, o, lse, dy, *, name):
    T = proj.shape[0]
    qb = ATTN_Q_BLOCK
    nb = T // qb
    order = lambda i: nb - 1 - i
    cur, prev = _attn_specs(T, order)

    def body(sink_ref, q_ref, kc_ref, kp_ref, vc_ref, vp_ref, z0_ref, z1_ref, o_ref, lse_ref, dy_ref,
             dq_ref, dk_ref, dv_ref, dz_ref, dsink_ref, dk_carry, dv_carry):
        i = pl.program_id(0)
        blk = order(i)
        q = q_ref[...].astype(BF16)
        kk = jnp.concatenate([kp_ref[...], kc_ref[...]], axis=0).astype(BF16)
        vv = jnp.concatenate([vp_ref[...], vc_ref[...]], axis=0).astype(BF16)
        mask = _attn_mask(blk)
        lane = lax.broadcasted_iota(jnp.int32, (1, 128), 1)
        dsink = jnp.zeros((1, 128), F32)
        scale = HEAD_DIM ** -0.5
        for kv in range(N_KV_HEADS):
            ks = slice(kv * HEAD_DIM, (kv + 1) * HEAD_DIM)
            dk_acc = jnp.zeros((qb + ATTN_HALO, HEAD_DIM), F32)
            dv_acc = jnp.zeros((qb + ATTN_HALO, HEAD_DIM), F32)
            for h in range(kv * Q_PER_KV, (kv + 1) * Q_PER_KV):
                hs = slice(h * HEAD_DIM, (h + 1) * HEAD_DIM)
                zr = z0_ref if h < N_Q_HEADS // 2 else z1_ref
                zs = slice((h % (N_Q_HEADS // 2)) * HEAD_DIM, (h % (N_Q_HEADS // 2) + 1) * HEAD_DIM)
                z = zr[:, zs].astype(F32)
                dyh = dy_ref[:, hs].astype(F32)
                oh = o_ref[:, hs].astype(F32)
                do = dyh * _silu(z)
                dz_ref[:, hs] = (dyh * oh * _dsilu(z)).astype(dz_ref.dtype)
                drow = jnp.sum(do * oh, axis=1, keepdims=True)
                lse_h = lse_ref[:, h:h + 1]
                s = lax.dot_general(q[:, hs], kk[:, ks], (((1,), (1,)), ((), ())), preferred_element_type=F32)
                p = jnp.exp(jnp.where(mask, s * scale, NEG_INF) - lse_h)
                do_b = do.astype(BF16)
                dv_acc = dv_acc + lax.dot_general(p.astype(BF16), do_b, (((0,), (0,)), ((), ())),
                                                  preferred_element_type=F32)
                dp = lax.dot_general(do_b, vv[:, ks], (((1,), (1,)), ((), ())), preferred_element_type=F32)
                ds = (p * (dp - drow)).astype(BF16)
                dq_ref[:, hs] = (jnp.dot(ds, kk[:, ks], preferred_element_type=F32) * scale).astype(dq_ref.dtype)
                dk_acc = dk_acc + lax.dot_general(ds, q[:, hs], (((0,), (0,)), ((), ())),
                                                  preferred_element_type=F32) * scale
                p_sink = jnp.exp(sink_ref[h] - lse_h)
                dsink = jnp.where(lane == h, -jnp.sum(p_sink * drow, axis=0, keepdims=True), dsink)
            for acc, carry, out in ((dk_acc, dk_carry, dk_ref), (dv_acc, dv_carry, dv_ref)):
                tail = acc[qb:, :] + jnp.where(i == 0, 0.0, carry[:, ks])
                out[:, ks] = jnp.concatenate([acc[ATTN_HALO:qb, :], tail], axis=0).astype(out.dtype)
                carry[:, ks] = acc[:ATTN_HALO, :]

        @pl.when(i == 0)
        def _():
            dsink_ref[...] = dsink

        @pl.when(i > 0)
        def _():
            dsink_ref[...] += dsink

    kcol, vcol = OFF_K // KV_WIDTH, OFF_V // KV_WIDTH
    tile = pl.BlockSpec((qb, ATTN_WIDTH), lambda i: (order(i), 0))
    kv_tile = pl.BlockSpec((qb, KV_WIDTH), lambda i: (order(i), 0))
    lse_spec = pl.BlockSpec((qb, 128), lambda i: (order(i), 0))
    in_specs = [pl.BlockSpec(memory_space=pltpu.SMEM), cur(ATTN_WIDTH, OFF_Q // ATTN_WIDTH), cur(KV_WIDTH, kcol), prev(kcol),
                cur(KV_WIDTH, vcol), prev(vcol), cur(HALF, OFF_AZ // HALF), cur(HALF, OFF_AZ // HALF + 1),
                tile, lse_spec, tile]
    return pl.pallas_call(
        body, name=name, grid=(nb,), in_specs=in_specs,
        out_specs=[tile, kv_tile, kv_tile, tile, pl.BlockSpec((1, 128), lambda i: (0, 0))],
        out_shape=[jax.ShapeDtypeStruct((T, ATTN_WIDTH), BF16), jax.ShapeDtypeStruct((T, KV_WIDTH), BF16),
                   jax.ShapeDtypeStruct((T, KV_WIDTH), BF16), jax.ShapeDtypeStruct((T, ATTN_WIDTH), BF16),
                   jax.ShapeDtypeStruct((1, 128), F32)],
        scratch_shapes=[pltpu.VMEM((ATTN_HALO, KV_WIDTH), F32), pltpu.VMEM((ATTN_HALO, KV_WIDTH), F32)],
        compiler_params=_params(1))(sink, *([proj] * 7), o, lse, dy)


def _conv_specs(T, tm):
    per = tm // CONV_HALO
    ca, cb = OFF_CA // HALF, OFF_CB // HALF
    cur = lambda col: pl.BlockSpec((tm, HALF), lambda i, c=col: (i, c))
    prev = lambda col: pl.BlockSpec((CONV_HALO, HALF), lambda i, c=col: (jnp.maximum(i * per - 1, 0), c))
    return [cur(ca), cur(ca + 1), cur(cb), cur(cb + 1), prev(ca), prev(ca + 1), prev(cb), prev(cb + 1)]


def _conv_glu_ext(refs, i, ext_ref):
    a0, a1, b0, b1, pa0, pa1, pb0, pb1 = refs
    a = jnp.concatenate([a0[...], a1[...]], axis=1).astype(F32)
    sb = jax.nn.sigmoid(jnp.concatenate([b0[...], b1[...]], axis=1).astype(F32))
    pa = jnp.concatenate([pa0[...], pa1[...]], axis=1).astype(F32)
    pb = jnp.concatenate([pb0[...], pb1[...]], axis=1).astype(F32)
    ext_ref[:CONV_HALO, :] = jnp.where(i == 0, 0.0, pa * jax.nn.sigmoid(pb))
    ext_ref[CONV_HALO:, :] = a * sb
    return a, sb


def _conv_scratch(tm):
    return [pltpu.VMEM((tm + CONV_HALO, CONV_WIDTH), F32), pltpu.VMEM((7, tm + CONV_HALO - 8, CONV_WIDTH), F32)]


def _conv_fill_shifted(ext_ref, sh_ref):
    rows = sh_ref.shape[1]
    for b in range(1, 8):
        sh_ref[b - 1] = ext_ref[b:b + rows, :]


def _conv_window(ext_ref, sh_ref, start, tm, cols):
    a, b = divmod(start, 8)
    if b == 0:
        return ext_ref[8 * a:8 * a + tm, cols]
    return sh_ref[b - 1, 8 * a:8 * a + tm, cols]


LANES = 128


def _lane_blocks(width):
    return [slice(k, k + LANES) for k in range(0, width, LANES)]


def _conv_taps(ext_ref, sh_ref, dw_ref, out_ref, first_start, step, tm, bias_ref=None):
    sub = tm // 8
    for cols in _lane_blocks(CONV_WIDTH):
        y = None
        for j in range(CONV_KERNEL):
            tap = jnp.broadcast_to(dw_ref[j:j + 1, cols], (8, LANES))
            window = _conv_window(ext_ref, sh_ref, first_start + step * j, tm, cols).reshape(sub, 8, LANES)
            t = tap * window
            y = t if y is None else y + t
        y = y.reshape(tm, LANES)
        out_ref[:, cols] = y if bias_ref is None else y + bias_ref[:, cols]


def _conv_fwd(proj, dw, dwb, lng, lnb, *, name):
    T = proj.shape[0]
    tm = _div(T, 256, 32)

    def body(*refs):
        dw_ref, dwb_ref, g_ref, b_ref, s_ref, yc_ref, ext_ref, sh_ref = refs[8:]
        i = pl.program_id(0)
        _conv_glu_ext(refs[:8], i, ext_ref)
        _conv_fill_shifted(ext_ref, sh_ref)
        _conv_taps(ext_ref, sh_ref, dw_ref, yc_ref, CONV_HALO - (CONV_KERNEL - 1), 1, tm, dwb_ref)
        yc = yc_ref[...]
        mu = jnp.mean(yc, axis=-1, keepdims=True)
        d = yc - mu
        rstd = lax.rsqrt(jnp.mean(d * d, axis=-1, keepdims=True) + EPS)
        s_ref[...] = _silu(d * rstd * g_ref[...] + b_ref[...]).astype(s_ref.dtype)

    row = pl.BlockSpec((1, CONV_WIDTH), lambda i: (0, 0))
    taps = pl.BlockSpec((CONV_KERNEL, CONV_WIDTH), lambda i: (0, 0))
    tile = pl.BlockSpec((tm, CONV_WIDTH), lambda i: (i, 0))
    return pl.pallas_call(
        body, name=name, grid=(T // tm,), in_specs=_conv_specs(T, tm) + [taps, row, row, row], out_specs=[tile, tile],
        out_shape=[jax.ShapeDtypeStruct((T, CONV_WIDTH), BF16), jax.ShapeDtypeStruct((T, CONV_WIDTH), F32)],
        scratch_shapes=_conv_scratch(tm), compiler_params=_params(1))(*([proj] * 8), dw, dwb, lng, lnb)


def _conv_bwd(proj, yc_saved, ds, lng, lnb, *, name):
    T = proj.shape[0]
    tm = _div(T, 256, 32)
    nb = T // tm
    sub = tm // 8

    def body(*refs):
        yc_ref, ds_ref, g_ref, b_ref, dyc_ref, ddw_ref, ddwb_ref, dg_ref, db_ref, ext_ref, sh_ref, acc_ref = refs[8:]
        i = pl.program_id(0)
        _conv_glu_ext(refs[:8], i, ext_ref)
        _conv_fill_shifted(ext_ref, sh_ref)
        yc = yc_ref[...]
        mu = jnp.mean(yc, axis=-1, keepdims=True)
        d = yc - mu
        rstd = lax.rsqrt(jnp.mean(d * d, axis=-1, keepdims=True) + EPS)
        xhat = d * rstd
        dln = ds_ref[...].astype(F32) * _dsilu(xhat * g_ref[...] + b_ref[...])
        dxhat = dln * g_ref[...]
        dyc = rstd * (dxhat - jnp.mean(dxhat, axis=-1, keepdims=True)
                      - xhat * jnp.mean(dxhat * xhat, axis=-1, keepdims=True))
        dyc_ref[...] = dyc
        first = i == 0

        def accumulate(ref, idx, val):
            @pl.when(first)
            def _():
                ref[idx] = val

            @pl.when(jnp.logical_not(first))
            def _():
                ref[idx] += val

        accumulate(dg_ref, slice(None), jnp.sum(dln * xhat, axis=0, keepdims=True))
        accumulate(db_ref, slice(None), jnp.sum(dln, axis=0, keepdims=True))
        accumulate(ddwb_ref, slice(None), jnp.sum(dyc, axis=0, keepdims=True))
        @pl.when(first)
        def _():
            acc_ref[...] = jnp.zeros_like(acc_ref)

        for cols in _lane_blocks(CONV_WIDTH):
            dyc_b = dyc_ref[:, cols].reshape(sub, 8, LANES)
            for j in range(CONV_KERNEL):
                window = _conv_window(ext_ref, sh_ref, CONV_HALO - (CONV_KERNEL - 1) + j, tm, cols)
                acc_ref[j, :, cols] += jnp.sum(dyc_b * window.reshape(sub, 8, LANES), axis=0)

        @pl.when(i == nb - 1)
        def _():
            for j in range(CONV_KERNEL):
                ddw_ref[j:j + 1, :] = jnp.sum(acc_ref[j], axis=0, keepdims=True)

    row = pl.BlockSpec((1, CONV_WIDTH), lambda i: (0, 0))
    taps = pl.BlockSpec((CONV_KERNEL, CONV_WIDTH), lambda i: (0, 0))
    tile = pl.BlockSpec((tm, CONV_WIDTH), lambda i: (i, 0))
    vec = jax.ShapeDtypeStruct((1, CONV_WIDTH), F32)
    return pl.pallas_call(
        body, name=name, grid=(nb,), in_specs=_conv_specs(T, tm) + [tile, tile, row, row],
        out_specs=[tile, taps, row, row, row],
        out_shape=[jax.ShapeDtypeStruct((T, CONV_WIDTH), F32), jax.ShapeDtypeStruct((CONV_KERNEL, CONV_WIDTH), F32), vec, vec, vec],
        scratch_shapes=_conv_scratch(tm) + [pltpu.VMEM((CONV_KERNEL, 8, CONV_WIDTH), F32)],
        compiler_params=_params(1))(*([proj] * 8), yc_saved, ds, lng, lnb)


def _conv_bwd_input(proj, dyc, dw, *, name):
    T = proj.shape[0]
    tm = _div(T, 256, 32)
    per = tm // CONV_HALO
    last = T // CONV_HALO - 1
    nb = T // tm
    ca, cb = OFF_CA // HALF, OFF_CB // HALF

    def body(a0, a1, b0, b1, c_ref, n_ref, dw_ref, da_ref, db_ref, ext_ref, sh_ref, dg_ref):
        i = pl.program_id(0)
        ext_ref[:tm, :] = c_ref[...]
        ext_ref[tm:, :] = jnp.where(i == nb - 1, 0.0, n_ref[...])
        _conv_fill_shifted(ext_ref, sh_ref)
        _conv_taps(ext_ref, sh_ref, dw_ref, dg_ref, CONV_KERNEL - 1, -1, tm)
        dg = dg_ref[...]
        a = jnp.concatenate([a0[...], a1[...]], axis=1).astype(F32)
        sb = jax.nn.sigmoid(jnp.concatenate([b0[...], b1[...]], axis=1).astype(F32))
        da_ref[...] = (dg * sb).astype(da_ref.dtype)
        db_ref[...] = (dg * a * sb * (1.0 - sb)).astype(db_ref.dtype)

    cur = lambda col: pl.BlockSpec((tm, HALF), lambda i, c=col: (i, c))
    tile = pl.BlockSpec((tm, CONV_WIDTH), lambda i: (i, 0))
    nxt = pl.BlockSpec((CONV_HALO, CONV_WIDTH), lambda i: (jnp.minimum((i + 1) * per, last), 0))
    taps = pl.BlockSpec((CONV_KERNEL, CONV_WIDTH), lambda i: (0, 0))
    return pl.pallas_call(
        body, name=name, grid=(nb,), in_specs=[cur(ca), cur(ca + 1), cur(cb), cur(cb + 1), tile, nxt, taps],
        out_specs=[tile, tile], out_shape=[jax.ShapeDtypeStruct((T, CONV_WIDTH), BF16)] * 2,
        scratch_shapes=_conv_scratch(tm) + [pltpu.VMEM((tm, CONV_WIDTH), F32)],
        compiler_params=_params(1))(proj, proj, proj, proj, dyc, dyc, dw)


def _ada_mod(c_all, w_ada, b_slab, *, name):
    L, D, N = w_ada.shape
    tn = _div(N, 512)

    def body(c_ref, w_ref, b_ref, o_ref):
        ca = _silu(c_ref[...]).astype(BF16)
        o_ref[...] = jnp.dot(ca, w_ref[...].astype(BF16), preferred_element_type=F32) + b_ref[...]

    return pl.pallas_call(
        body, name=name, grid=(L, N // tn),
        in_specs=[pl.BlockSpec((N_DEV, D), lambda l, j: (0, 0)), pl.BlockSpec((None, D, tn), lambda l, j: (l, 0, j)),
                  pl.BlockSpec((None, 1, tn), lambda l, j: (l, 0, j))],
        out_specs=pl.BlockSpec((None, N_DEV, tn), lambda l, j: (l, 0, j)),
        out_shape=jax.ShapeDtypeStruct((L, N_DEV, N), F32), compiler_params=_params(2))(c_all, w_ada, b_slab)


def _ada_grad(c_all_t, dmod_slab, *, name):
    D = c_all_t.shape[0]
    L, _, N = dmod_slab.shape
    tm = _div(D, 512)
    tn = _div(N, 512)

    def body(c_ref, d_ref, o_ref):
        ca = _silu(c_ref[...]).astype(BF16).astype(F32)
        dm = d_ref[...].astype(BF16).astype(F32)
        acc = None
        for b in range(N_DEV):
            t = ca[:, b:b + 1] * dm[b:b + 1, :]
            acc = t if acc is None else acc + t
        o_ref[...] = acc

    return pl.pallas_call(
        body, name=name, grid=(L, D // tm, N // tn),
        in_specs=[pl.BlockSpec((tm, N_DEV), lambda l, i, j: (i, 0)), pl.BlockSpec((None, N_DEV, tn), lambda l, i, j: (l, 0, j))],
        out_specs=pl.BlockSpec((None, tm, tn), lambda l, i, j: (l, i, j)),
        out_shape=jax.ShapeDtypeStruct((L, D, N), F32), compiler_params=_params(3))(c_all_t, dmod_slab)


def _flat_tile(R, C, n_arrays):
    cap = max(8, (20 * MIB) // (2 * n_arrays * C * 4))
    return _div(R, cap, 8) if R % 8 == 0 else R


def _adamw_math(w, g, m, v):
    m = ADAM_B1 * m + (1.0 - ADAM_B1) * g
    v = ADAM_B2 * v + (1.0 - ADAM_B2) * (g * g)
    m_hat = m / (1.0 - ADAM_B1 ** ADAM_STEP)
    v_hat = v / (1.0 - ADAM_B2 ** ADAM_STEP)
    delta = -ADAM_LR * (m_hat / (jnp.sqrt(v_hat) + ADAM_EPS) + ADAM_WD * w)
    return delta, m, v


def _adamw(w, m, v, gs, *, name):
    R, C = w.shape
    n_g = len(gs)
    tr = _flat_tile(R, C, 7 + n_g)

    def body(*refs):
        w_ref, m_ref, v_ref = refs[:3]
        g_refs = refs[3:3 + n_g]
        go_ref, d_ref, mo_ref, vo_ref = refs[3 + n_g:]
        g = g_refs[0][...]
        for r in g_refs[1:]:
            g = g + r[...]
        d, mn, vn = _adamw_math(w_ref[...], g, m_ref[...], v_ref[...])
        go_ref[...] = g
        d_ref[...] = d
        mo_ref[...] = mn
        vo_ref[...] = vn

    tile = pl.BlockSpec((tr, C), lambda i: (i, 0))
    shp = jax.ShapeDtypeStruct((R, C), F32)
    return pl.pallas_call(body, name=name, grid=(R // tr,), in_specs=[tile] * (3 + n_g), out_specs=[tile] * 4,
                          out_shape=[shp] * 4, compiler_params=_params(1, 2 * (7 + n_g) * tr * C * 4))(w, m, v, *gs)


def _adamw_layer(w, m, v, layer, gs, outs, *, name):
    _, R, C = w.shape
    n_g = len(gs)
    tr = _flat_tile(R, C, 7 + n_g)

    def body(*refs):
        w_ref, m_ref, v_ref = refs[:3]
        g_refs = refs[3:3 + n_g]
        go_ref, d_ref, mo_ref, vo_ref = refs[3 + n_g + 4:]
        g = g_refs[0][...]
        for r in g_refs[1:]:
            g = g + r[...]
        d, mn, vn = _adamw_math(w_ref[...], g, m_ref[...], v_ref[...])
        go_ref[...] = g
        d_ref[...] = d
        mo_ref[...] = mn
        vo_ref[...] = vn

    lay = pl.BlockSpec((None, tr, C), lambda i: (layer, i, 0))
    tile = pl.BlockSpec((tr, C), lambda i: (i, 0))
    return pl.pallas_call(
        body, name=name, grid=(R // tr,), in_specs=[lay] * 3 + [tile] * n_g + [HBM] * 4, out_specs=[lay] * 4,
        out_shape=[jax.ShapeDtypeStruct(o.shape, o.dtype) for o in outs],
        input_output_aliases={3 + n_g + k: k for k in range(4)},
        compiler_params=_params(1, 2 * (7 + n_g) * tr * C * 4))(w, m, v, *gs, *outs)


def _full_shape(kind, slab_shape):
    G, r, c = slab_shape
    return (G, r, c * N_CHIP) if kind == "cols" else (G, r * N_CHIP, c)


def _slab_tile(G, r, c, n_arrays):
    cap = max(16, (20 * MIB) // (2 * n_arrays * G * c * 4))
    return _div(r, cap, 16)


def _slab_block(kind, G, r, c, tr):
    if kind == "cols":
        return pl.BlockSpec((G, tr, c), lambda i, chip: (0, i, chip[0]))
    per = r // tr
    return pl.BlockSpec((G, tr, c), lambda i, chip: (0, chip[0] * per + i, 0))


def _cast_into_full(chip, w, layer, kind, *, name):
    _, G, r, c = w.shape
    tr = _slab_tile(G, r, c, 2)

    def body(chip_ref, w_ref, o_ref):
        o_ref[...] = w_ref[...].astype(BF16)

    grid_spec = pltpu.PrefetchScalarGridSpec(
        num_scalar_prefetch=1, grid=(r // tr,),
        in_specs=[pl.BlockSpec((None, G, tr, c), lambda i, chip: (layer, 0, i, 0))], out_specs=_slab_block(kind, G, r, c, tr))
    return pl.pallas_call(body, name=name, grid_spec=grid_spec,
                          out_shape=jax.ShapeDtypeStruct(_full_shape(kind, (G, r, c)), BF16),
                          compiler_params=_params(1, 4 * G * tr * c * 4))(chip, w)


def _sum_contribs(chip, full, land, kind, *, name):
    _, G, r, c = land.shape
    tr = _slab_tile(G, r, c, 5)

    def body(chip_ref, f_ref, l_ref, o_ref):
        o_ref[...] = ((f_ref[...].astype(F32) + l_ref[0].astype(F32)) + l_ref[1].astype(F32)) + l_ref[2].astype(F32)

    grid_spec = pltpu.PrefetchScalarGridSpec(
        num_scalar_prefetch=1, grid=(r // tr,),
        in_specs=[_slab_block(kind, G, r, c, tr), pl.BlockSpec((3, G, tr, c), lambda i, chip: (0, 0, i, 0))],
        out_specs=pl.BlockSpec((G, tr, c), lambda i, chip: (0, i, 0)))
    return pl.pallas_call(body, name=name, grid_spec=grid_spec, out_shape=jax.ShapeDtypeStruct((G, r, c), F32),
                          compiler_params=_params(1, 2 * 5 * G * tr * c * 4))(chip, full, land)


def _place():
    x, y, c = lax.axis_index("x"), lax.axis_index("y"), lax.axis_index("c")
    chips = [(1 - x, y), (x, 1 - y), (1 - x, 1 - y)]
    return x, y, c, chips


def _small_exchange(v, reduce, *, name):
    m_per, n = v.shape
    assert m_per % 8 == 0 and n % 128 == 0

    def body(x_ref, out_ref, *scratch):
        if reduce:
            all_ref, send_sems, recv_sems, local_sem = scratch
        else:
            all_ref = out_ref
            send_sems, recv_sems, local_sem = scratch
        x, y, c, chips = _place()
        me, sibling = (x, y, c), (x, y, 1 - c)

        def rows(px, py, pc):
            return all_ref.at[pl.ds((4 * px + 2 * py + pc) * m_per, m_per), :]

        def copy(k, block, to, src=None):
            return pltpu.make_async_remote_copy(
                src_ref=rows(*block) if src is None else src, dst_ref=rows(*block), send_sem=send_sems.at[k],
                recv_sem=recv_sems.at[k], device_id=to, device_id_type=MESH)

        mine = pltpu.make_async_copy(x_ref, rows(*me), local_sem)
        mine.start()
        first = [copy(0, me, sibling, src=x_ref)]
        first += [copy(1 + j, me, (*chip, c), src=x_ref) for j, chip in enumerate(chips)]
        for cp in first:
            cp.start()
        passed = [copy(4 + j, (*chip, c), sibling) for j, chip in enumerate(chips)]
        for j, chip in enumerate(chips):
            copy(1 + j, (*chip, c), me).wait_recv()
            passed[j].start()
        copy(0, sibling, me).wait_recv()
        for j, chip in enumerate(chips):
            copy(4 + j, (*chip, 1 - c), me).wait_recv()
        for cp in first + passed:
            cp.wait_send()
        mine.wait()
        if reduce:
            acc = all_ref[0:m_per, :]
            for d in range(1, N_DEV):
                acc = acc + all_ref[d * m_per:(d + 1) * m_per, :]
            out_ref[...] = acc

    scratch = [pltpu.SemaphoreType.DMA((7,)), pltpu.SemaphoreType.DMA((7,)), pltpu.SemaphoreType.DMA]
    if reduce:
        scratch = [pltpu.VMEM((N_DEV * m_per, n), F32)] + scratch
    out_rows = m_per if reduce else N_DEV * m_per
    return pl.pallas_call(
        body, name=name, out_shape=jax.ShapeDtypeStruct((out_rows, n), v.dtype),
        in_specs=[pl.BlockSpec(memory_space=pltpu.VMEM)], out_specs=pl.BlockSpec(memory_space=pltpu.VMEM),
        scratch_shapes=scratch,
        compiler_params=pltpu.CompilerParams(vmem_limit_bytes=int(min(VMEM_CAP_BYTES, 4 * N_DEV * m_per * n * 4 + 16 * MIB))))(v)


def _slab(kind, ref, s):
    if kind == "cols":
        w = ref.shape[2] // N_CHIP
        return ref.at[:, :, pl.ds(s * w, w)]
    w = ref.shape[1] // N_CHIP
    return ref.at[:, pl.ds(s * w, w), :]


HBM = pl.BlockSpec(memory_space=pltpu.HBM)
SEM = pl.BlockSpec(memory_space=pltpu.SEMAPHORE)
EFFECT = pltpu.SideEffectType.DATAFLOW_SIDE_EFFECTING


def _in_hbm(v):
    return pltpu.with_memory_space_constraint(v, pltpu.HBM)


def _hbm_like(arrays):
    return [pltpu.HBM(v.shape, v.dtype) for v in arrays]


def _half_rows(ref, c):
    r = ref.shape[1] // 2
    return ref.at[:, pl.ds(c * r, r), :]


def _gather_copy(kinds, full, send_sems, recv_sems, a, j, peer, c, s_src, s_dst, halves):
    src, dst = _slab(kinds[a], full[a], s_src), _slab(kinds[a], full[a], s_dst)
    if halves:
        src, dst = _half_rows(src, c), _half_rows(dst, c)
    return pltpu.make_async_remote_copy(src_ref=src, dst_ref=dst, send_sem=send_sems.at[a * 3 + j],
                                        recv_sem=recv_sems.at[a * 3 + j], device_id=(*peer, c), device_id_type=MESH)


def _gather_start(fulls, kinds, after, *, name, halves=False):
    n = len(fulls)

    def body(*refs):
        k = n + len(after)
        full, send_sems, recv_sems, token = refs[:n], refs[k], refs[k + 1], refs[-1]
        x, y, c, chips = _place()
        s_me = 2 * x + y
        for a in range(n):
            for j, peer in enumerate(chips):
                _gather_copy(kinds, full, send_sems, recv_sems, a, j, peer, c, s_me, s_me, halves).start()
        token[...] = jnp.zeros_like(token)

    sems = pltpu.SemaphoreType.DMA((3 * n,))
    out = pl.pallas_call(
        body, name=name, out_shape=(sems, sems, *_hbm_like(fulls), jax.ShapeDtypeStruct((8, 128), F32)),
        in_specs=[HBM] * n + [ANY] * len(after), out_specs=(SEM, SEM, *[HBM] * n, pl.BlockSpec(memory_space=pltpu.VMEM)),
        input_output_aliases={a: 2 + a for a in range(n)},
        compiler_params=pltpu.CompilerParams(has_side_effects=EFFECT))(*[_in_hbm(f) for f in fulls], *after)
    return out[0], out[1], list(out[2:2 + n]), out[-1]


def _gather_wait(fulls, kinds, send, recv, after, *, name, halves=False):
    n = len(fulls)

    def body(*refs):
        full, send_sems, recv_sems = refs[:n], refs[n], refs[n + 1]
        x, y, c, chips = _place()
        s_me = 2 * x + y
        for a in range(n):
            for j, peer in enumerate(chips):
                cp = _gather_copy(kinds, full, send_sems, recv_sems, a, j, peer, c, s_me, 2 * peer[0] + peer[1], halves)
                cp.wait_send()
                cp.wait_recv()

    return pl.pallas_call(
        body, name=name, out_shape=_hbm_like(fulls), in_specs=[HBM] * n + [SEM, SEM] + [ANY] * len(after),
        out_specs=[HBM] * n, input_output_aliases={a: a for a in range(n)},
        compiler_params=pltpu.CompilerParams(has_side_effects=EFFECT))(*fulls, send, recv, *after)


def _sibling_fill(full, kind, *, name):
    def body(in_ref, out_ref, send_sems, recv_sems):
        x, y, c, chips = _place()

        def copy(j, peer, half):
            part = _half_rows(_slab(kind, out_ref, 2 * peer[0] + peer[1]), half)
            return pltpu.make_async_remote_copy(src_ref=part, dst_ref=part, send_sem=send_sems.at[j], recv_sem=recv_sems.at[j],
                                                device_id=(x, y, 1 - c), device_id_type=MESH)

        sent = [copy(j, peer, c) for j, peer in enumerate(chips)]
        for cp in sent:
            cp.start()
        for j, peer in enumerate(chips):
            copy(j, peer, 1 - c).wait_recv()
        for cp in sent:
            cp.wait_send()

    return pl.pallas_call(
        body, name=name, out_shape=jax.ShapeDtypeStruct(full.shape, full.dtype), in_specs=[ANY], out_specs=ANY,
        input_output_aliases={0: 0}, scratch_shapes=[pltpu.SemaphoreType.DMA((3,)), pltpu.SemaphoreType.DMA((3,))])(full)


def _scatter_copy(kinds, full, land, send_sems, recv_sems, a, j, peer, c):
    return pltpu.make_async_remote_copy(
        src_ref=_slab(kinds[a], full[a], 2 * peer[0] + peer[1]), dst_ref=land[a].at[j], send_sem=send_sems.at[a * 3 + j],
        recv_sem=recv_sems.at[a * 3 + j], device_id=(*peer, c), device_id_type=MESH)


def _scatter_start(fulls, lands, kinds, after, *, name):
    n = len(fulls)

    def body(*refs):
        k = 2 * n + len(after)
        full, land, send_sems, recv_sems, token = refs[:n], refs[n:2 * n], refs[k], refs[k + 1], refs[-1]
        _, _, c, chips = _place()
        for a in range(n):
            for j, peer in enumerate(chips):
                _scatter_copy(kinds, full, land, send_sems, recv_sems, a, j, peer, c).start()
        token[...] = jnp.zeros_like(token)

    sems = pltpu.SemaphoreType.DMA((3 * n,))
    out = pl.pallas_call(
        body, name=name,
        out_shape=(sems, sems, *_hbm_like(fulls), *_hbm_like(lands), jax.ShapeDtypeStruct((8, 128), F32)),
        in_specs=[HBM] * (2 * n) + [ANY] * len(after),
        out_specs=(SEM, SEM, *[HBM] * (2 * n), pl.BlockSpec(memory_space=pltpu.VMEM)),
        input_output_aliases={a: 2 + a for a in range(2 * n)},
        compiler_params=pltpu.CompilerParams(has_side_effects=EFFECT))(*[_in_hbm(f) for f in list(fulls) + list(lands)], *after)
    return out[0], out[1], list(out[2:2 + n]), list(out[2 + n:2 + 2 * n]), out[-1]


def _scatter_wait(fulls, lands, kinds, send, recv, after, *, name):
    n = len(fulls)

    def body(*refs):
        full, land, send_sems, recv_sems = refs[:n], refs[n:2 * n], refs[2 * n], refs[2 * n + 1]
        _, _, c, chips = _place()
        for a in range(n):
            for j, peer in enumerate(chips):
                cp = _scatter_copy(kinds, full, land, send_sems, recv_sems, a, j, peer, c)
                cp.wait_send()
                cp.wait_recv()

    out = pl.pallas_call(
        body, name=name, out_shape=_hbm_like(list(fulls) + list(lands)),
        in_specs=[HBM] * (2 * n) + [SEM, SEM] + [ANY] * len(after), out_specs=[HBM] * (2 * n),
        input_output_aliases={a: a for a in range(2 * n)},
        compiler_params=pltpu.CompilerParams(has_side_effects=EFFECT))(*fulls, *lands, send, recv, *after)
    return list(out[:n]), list(out[n:])


def _swap_copy(src, dst, send_sems, recv_sems, a):
    x, y, c, _ = _place()
    return pltpu.make_async_remote_copy(src_ref=src[a], dst_ref=dst[a], send_sem=send_sems.at[a], recv_sem=recv_sems.at[a],
                                        device_id=(x, y, 1 - c), device_id_type=MESH)


def _swap_start(parts, lands, *, name):
    n = len(parts)

    def body(*refs):
        src, dst, send_sems, recv_sems, token = refs[:n], refs[n:2 * n], refs[2 * n], refs[2 * n + 1], refs[-1]
        for a in range(n):
            _swap_copy(src, dst, send_sems, recv_sems, a).start()
        token[...] = jnp.zeros_like(token)

    sems = pltpu.SemaphoreType.DMA((n,))
    out = pl.pallas_call(
        body, name=name,
        out_shape=(sems, sems, *_hbm_like(parts), *_hbm_like(lands), jax.ShapeDtypeStruct((8, 128), F32)),
        in_specs=[HBM] * (2 * n), out_specs=(SEM, SEM, *[HBM] * (2 * n), pl.BlockSpec(memory_space=pltpu.VMEM)),
        input_output_aliases={a: 2 + a for a in range(2 * n)},
        compiler_params=pltpu.CompilerParams(has_side_effects=EFFECT))(*[_in_hbm(f) for f in list(parts) + list(lands)])
    return out[0], out[1], list(out[2:2 + n]), list(out[2 + n:2 + 2 * n]), out[-1]


def _swap_wait(parts, lands, send, recv, after, *, name):
    n = len(parts)

    def body(*refs):
        src, dst, send_sems, recv_sems = refs[:n], refs[n:2 * n], refs[2 * n], refs[2 * n + 1]
        for a in range(n):
            cp = _swap_copy(src, dst, send_sems, recv_sems, a)
            cp.wait_send()
            cp.wait_recv()

    out = pl.pallas_call(
        body, name=name, out_shape=_hbm_like(list(parts) + list(lands)),
        in_specs=[HBM] * (2 * n) + [SEM, SEM] + [ANY] * len(after), out_specs=[HBM] * (2 * n),
        input_output_aliases={a: a for a in range(2 * n)},
        compiler_params=pltpu.CompilerParams(has_side_effects=EFFECT))(*parts, *lands, send, recv, *after)
    return list(out[:n]), list(out[n:])


def _pad_rows(v, rows):
    return jnp.pad(v, ((0, rows - v.shape[0]), (0, 0)))


def _pack(vectors):
    flat = jnp.concatenate([v.reshape(-1) for v in vectors])
    n = -(-flat.shape[0] // 1024) * 1024
    return jnp.pad(flat, (0, n - flat.shape[0])).reshape(8, n // 8)


def _unpack(block, shapes):
    flat = block.reshape(-1)
    out, pos = [], 0
    for shp in shapes:
        size = 1
        for d in shp:
            size *= d
        out.append(flat[pos:pos + size].reshape(shp))
        pos += size
    return out


def kernel(x, c, norm_g, w_ada, b_ada, w_in, pool_w, pool_scale, attn_sink, conv_dw, conv_dw_b, conv_ln_g, conv_ln_b, conv_pw, w_branch_pool, w_branch_attn, w_branch_conv, w_out, final_g, loss_target, m_norm_g, m_w_ada, m_b_ada, m_w_in, m_pool_w, m_pool_scale, m_attn_sink, m_conv_dw, m_conv_dw_b, m_conv_ln_g, m_conv_ln_b, m_conv_pw, m_w_branch_pool, m_w_branch_attn, m_w_branch_conv, m_w_out, m_final_g, v_norm_g, v_w_ada, v_b_ada, v_w_in, v_pool_w, v_pool_scale, v_attn_sink, v_conv_dw, v_conv_dw_b, v_conv_ln_g, v_conv_ln_b, v_conv_pw, v_w_branch_pool, v_w_branch_attn, v_w_branch_conv, v_w_out, v_final_g):
    _, T, D = x.shape
    L = norm_g.shape[0]
    IN = w_in.shape[2] * N_CHIP
    assert IN == OFF_G + 3 * D and D % HALF == 0 and T % 512 == 0
    xi, yi, ci = lax.axis_index("x"), lax.axis_index("y"), lax.axis_index("c")
    chip = 2 * xi + yi
    dev = 2 * chip + ci
    x0 = x.reshape(T, D)
    target = loss_target.reshape(T, D)

    big = [("cols", w_in, m_w_in, v_w_in), ("cols", w_branch_pool, m_w_branch_pool, v_w_branch_pool),
           ("cols", w_branch_attn, m_w_branch_attn, v_w_branch_attn), ("cols", w_branch_conv, m_w_branch_conv, v_w_branch_conv),
           ("rows", w_out, m_w_out, v_w_out), ("rows", conv_pw, m_conv_pw, v_conv_pw), ("rows", pool_w, m_pool_w, v_pool_w)]
    kinds = [b[0] for b in big]
    n_big = len(big)
    as_groups = lambda t: t if t.ndim == 4 else t.reshape(L, 1, t.shape[1], t.shape[2])
    chip_arr = jnp.reshape(chip, (1,)).astype(jnp.int32)

    c_all = _small_exchange(_pad_rows(c, 8), False, name="gather_c")[0::8]
    taps_rows = -(-(L * CONV_KERNEL) // 8) * 8
    dw_blocks = _small_exchange(_pad_rows(conv_dw.reshape(L * CONV_KERNEL, -1), taps_rows), False, name="gather_taps")
    dw_blocks = dw_blocks.reshape(N_CHIP, 2, taps_rows, -1)[:, 0, :L * CONV_KERNEL]
    conv_dw_full = dw_blocks.reshape(N_CHIP, L, CONV_KERNEL, -1).transpose(1, 2, 0, 3).reshape(L, CONV_KERNEL, CONV_WIDTH)
    n_ada = w_ada.shape[2]
    b_slab = lax.dynamic_slice_in_dim(b_ada, chip * n_ada, n_ada, axis=1).reshape(L, 1, n_ada)
    mod_part = _ada_mod(c_all, w_ada, b_slab, name="ada_mod")
    mod_blocks = _small_exchange(mod_part.reshape(L * N_DEV, n_ada), False, name="gather_mod")
    mod_blocks = mod_blocks.reshape(N_CHIP, 2, L, N_DEV, n_ada)[:, 0]
    mod_all = mod_blocks.transpose(1, 2, 0, 3).reshape(L, N_DEV, 3 * D)
    mod = lax.dynamic_index_in_dim(mod_all, dev, axis=1, keepdims=False)
    shift, scale, gate = mod[:, :D], mod[:, D:2 * D], mod[:, 2 * D:]

    groups = [[0], list(range(1, n_big))]
    weights, gather_tokens = [], []
    for l in range(L):
        fulls = [_cast_into_full(chip_arr, as_groups(b[1]), l, b[0], name=f"cast{a}_{l}") for a, b in enumerate(big)]
        started_groups = []
        for gi, idx in enumerate(groups):
            send, recv, part, token = _gather_start([fulls[a] for a in idx], [kinds[a] for a in idx], [mod_all, conv_dw_full],
                                                    name=f"gather_start{l}_{gi}", halves=(l == 0 and gi == 0))
            started_groups.append((part, [kinds[a] for a in idx], send, recv))
            gather_tokens.append(token[0:1, 0:1])
        weights.append(started_groups)
    started = functools.reduce(lambda p, q: p + q, gather_tokens)

    row = lambda v: v.reshape(1, -1)

    xs, saved = [x0], []
    xl = x0
    full_w = []
    for l in range(L):
        h = _norm_mod(xl, row(norm_g[l]), row(scale[l]) + started if l == 0 else row(scale[l]), row(shift[l]), name=f"norm{l}")
        (part, part_kinds, send, recv), rest = weights[l]
        win_f, = _gather_wait(part, part_kinds, send, recv, [h], name=f"gather_wait{l}_0", halves=(l == 0))
        if l == 0:
            win_f = _sibling_fill(win_f, part_kinds[0], name="sibling_fill")
        proj = _mm(h, win_f, "nn", [BF16], name=f"proj{l}", b_layer=0)
        part, part_kinds, send, recv = rest
        wbp_f, wba_f, wbc_f, wout_f, cpw_f, poolw_f = _gather_wait(part, part_kinds, send, recv, [proj], name=f"gather_wait{l}_1")
        full_w.append((win_f, wbp_f, wba_f, wbc_f, wout_f, cpw_f, poolw_f))
        y_pool = _pool_fwd(proj, poolw_f, row(pool_scale[l]), name=f"pool{l}")
        o_attn, y_attn, lse = _attn_fwd(proj, attn_sink[l], name=f"attn{l}")
        s_conv, yc = _conv_fwd(proj, conv_dw_full[l], row(conv_dw_b[l]), row(conv_ln_g[l]), row(conv_ln_b[l]), name=f"conv{l}")
        cpre, y_conv = _mm(s_conv, cpw_f, "nn", [BF16, BF16], name=f"conv_pw{l}", b_layer=0, tn_cap=HALF,
                           extras=[(proj, "tile", OFF_CZ)], epilogue=lambda acc, z: (acc, acc * _silu(z.astype(F32))))
        merged, bp, ba, bc = _merge((y_pool, y_attn, y_conv), (wbp_f, wba_f, wbc_f), proj, D, name=f"merge{l}")
        x_new, o = _mm(merged, wout_f, "nn", [F32, BF16], name=f"out{l}", b_layer=0,
                       extras=[(xl, "tile", 0), (row(gate[l]), "row", 0)],
                       epilogue=lambda acc, xv, g: (xv + g * acc, acc))
        saved.append(dict(h=h, proj=proj, y_pool=y_pool, o_attn=o_attn, y_attn=y_attn, lse=lse, s_conv=s_conv, yc=yc, cpre=cpre,
                          y_conv=y_conv, merged=merged, bp=bp, ba=ba, bc=bc, o=o))
        xl = x_new
        xs.append(xl)

    loss_part, dx, d_final_g = _final_loss(xl, target, row(final_g), name="final_loss")
    loss = lax.psum(loss_part[0, 0], ("x", "y", "c"))

    small, dmods, scattering = [], [], {}
    scattered = jnp.zeros((1, 1), F32)
    for l in reversed(range(L)):
        sv = saved[l]
        proj = sv["proj"]
        win_f, wbp_f, wba_f, wbc_f, wout_f, cpw_f, poolw_f = full_w[l]
        dmo, d_gate = _gate_out_bwd(dx, sv["o"], row(gate[l]) + scattered, name=f"gate_out_bwd{l}")
        dmerged = _mm(dmo, wout_f, "nt", [BF16], name=f"d_merged{l}", b_layer=0)
        g_wout = _mm(sv["merged"], dmo, "tn", [BF16], name=f"g_wout{l}")
        dbp, dba, dbc, dgp, dga, dgc = _merge_bwd(dmerged, (sv["bp"], sv["ba"], sv["bc"]), proj, D, name=f"merge_bwd{l}")
        dy_pool = _mm(dbp, wbp_f, "nt", [BF16], name=f"dy_pool{l}", b_layer=0)
        dy_attn = _mm(dba, wba_f, "nt", [BF16], name=f"dy_attn{l}", b_layer=0)
        dy_conv = _mm(dbc, wbc_f, "nt", [BF16], name=f"dy_conv{l}", b_layer=0)
        g_wbp = _mm(sv["y_pool"], dbp, "tn", [BF16], name=f"g_wbp{l}")
        g_wba = _mm(sv["y_attn"], dba, "tn", [BF16], name=f"g_wba{l}")
        g_wbc = _mm(sv["y_conv"], dbc, "tn", [BF16], name=f"g_wbc{l}")
        dz_pool, dmn, d_pool_scale, g_poolw = _pool_bwd(proj, dy_pool, poolw_f, row(pool_scale[l]), name=f"pool_bwd{l}")
        du_pool = _pool_bwd_window(dmn, name=f"pool_bwd_window{l}")
        dq, dk, dv, dz_attn, d_sink = _attn_bwd(proj, attn_sink[l], sv["o_attn"], sv["lse"], dy_attn, name=f"attn_bwd{l}")
        dcpre, dz_conv = _conv_out_bwd(dy_conv, sv["cpre"], proj, name=f"conv_out_bwd{l}")
        ds_conv = _mm(dcpre, cpw_f, "nt", [BF16], name=f"ds_conv{l}", b_layer=0)
        g_cpw = _mm(sv["s_conv"], dcpre, "tn", [BF16], name=f"g_cpw{l}")
        taps = conv_dw_full[l]
        dyc, d_taps, d_dwb, d_lng, d_lnb = _conv_bwd(proj, sv["yc"], ds_conv, row(conv_ln_g[l]), row(conv_ln_b[l]),
                                                    name=f"conv_bwd{l}")
        da_conv, db_conv = _conv_bwd_input(proj, dyc, taps, name=f"conv_bwd_input{l}")
        dproj = jnp.concatenate([du_pool, dz_pool, dq, dk, dv, dz_attn, da_conv, db_conv, dz_conv, dgp, dga, dgc], axis=1)
        dh = _mm(dproj, win_f, "nt", [F32], name=f"dh{l}", b_layer=0, tm_cap=2048, tk_cap=1536)
        g_win = _mm(sv["h"], dproj, "tn", [BF16], name=f"g_win{l}", tn_cap=768)
        dx, d_ng, d_scale, d_shift = _norm_mod_bwd(xs[l], dh, dx, row(norm_g[l]), row(scale[l]), name=f"norm_bwd{l}")
        dmods.append(jnp.concatenate([d_shift, d_scale, d_gate], axis=1))
        small.append([d_ng, d_pool_scale, d_sink[:, :N_Q_HEADS], d_taps, d_dwb, d_lng, d_lnb])
        before_start = []
        if l == 0:
            stacked = [jnp.stack([small[L - 1 - k][q] for k in range(L)]) for q in range(len(small[0]))]
            dmod_mine = jnp.concatenate(dmods[::-1], axis=0)
            small_shapes = [s.shape for s in stacked] + [d_final_g.shape, dmod_mine.shape]
            reduced = _small_exchange(_pack(stacked + [d_final_g, dmod_mine]), True, name="reduce_small")
            dmod_all = _small_exchange(_pad_rows(dmod_mine, 8), False, name="gather_dmod").reshape(N_DEV, 8, 3 * D)[:, :L]
            before_start = [reduced, dmod_all]
        grads = [g[None] for g in (g_win, g_wbp, g_wba, g_wbc, g_wout, g_cpw)] + [g_poolw.astype(BF16)]
        lands = [lax.empty((3,) + as_groups(b[1]).shape[1:], BF16) for b in big]
        send, recv, grads, lands, token = _scatter_start(grads, lands, kinds, before_start, name=f"scatter_start{l}")
        scattering[l] = (grads, lands, send, recv)
        scattered = token[0:1, 0:1]
    grad_x = dx.reshape(1, T, D)

    r_ng, r_ps, r_sink, r_taps, r_dwb, r_lng, r_lnb, r_fg, r_bada = _unpack(reduced, small_shapes)
    g_norm_g, g_pool_scale, g_attn_sink = r_ng.reshape(L, D), r_ps.reshape(L, POOL_WIDTH), r_sink.reshape(L, N_Q_HEADS)
    g_conv_dw = lax.dynamic_slice_in_dim(r_taps, chip * (CONV_WIDTH // N_CHIP), CONV_WIDTH // N_CHIP, axis=2)
    g_dwb, g_lng, g_lnb = r_dwb.reshape(L, CONV_WIDTH), r_lng.reshape(L, CONV_WIDTH), r_lnb.reshape(L, CONV_WIDTH)
    g_final_g, g_b_ada = r_fg.reshape(D), r_bada

    dmod_slab = lax.dynamic_slice_in_dim(dmod_all, chip * n_ada, n_ada, axis=2).transpose(1, 0, 2)
    g_w_ada = _ada_grad(c_all.T, dmod_slab + scattered, name="ada_grad")

    flat = lambda t: t.reshape(-1, t.shape[-1])
    ada = [t.reshape(w_ada.shape) for t in _adamw(flat(w_ada), flat(m_w_ada), flat(v_w_ada), [flat(g_w_ada)], name="adamw_ada")]
    small_w = [norm_g, b_ada, pool_scale, attn_sink, conv_dw, conv_dw_b, conv_ln_g, conv_ln_b, final_g]
    small_m = [m_norm_g, m_b_ada, m_pool_scale, m_attn_sink, m_conv_dw, m_conv_dw_b, m_conv_ln_g, m_conv_ln_b, m_final_g]
    small_v = [v_norm_g, v_b_ada, v_pool_scale, v_attn_sink, v_conv_dw, v_conv_dw_b, v_conv_ln_g, v_conv_ln_b, v_final_g]
    small_g = [g_norm_g, g_b_ada, g_pool_scale, g_attn_sink, g_conv_dw, g_dwb, g_lng, g_lnb, g_final_g]
    sm = _adamw(_pack(small_w), _pack(small_m), _pack(small_v), [_pack(small_g)], name="adamw_small")
    shp = [t.shape for t in small_w]
    sm_g, sm_d, sm_m, sm_v = [_unpack(t, shp) for t in sm]

    stacked3 = lambda t: t.reshape(L, -1, t.shape[-1])
    two = lambda t: t.reshape(-1, t.shape[-1])
    outs = [[lax.empty(stacked3(b[1]).shape, F32) for _ in range(4)] for b in big]

    def update(l, swapping, after):
        parts, lands, send, recv = swapping
        parts, others = _swap_wait(parts, lands, send, recv, after, name=f"swap_wait{l}")
        for a, (_, w, m, v) in enumerate(big):
            outs[a] = _adamw_layer(stacked3(w), stacked3(m), stacked3(v), l, [two(parts[a]), two(others[a])], outs[a],
                                   name=f"adamw{a}_{l}")

    after = [ada[0], sm[0]]
    swapping = None
    for l in reversed(range(L)):
        if l == 0 and swapping is not None:
            update(1, swapping, after)
            after, swapping = [outs[a][0] for a in range(n_big)], None
        grads, lands, send, recv = scattering[l]
        grads, lands = _scatter_wait(grads, lands, kinds, send, recv, after, name=f"scatter_wait{l}")
        parts = [_sum_contribs(chip_arr, grads[a], lands[a], kinds[a], name=f"sum_grads{a}_{l}") for a in range(n_big)]
        send, recv, parts, lands, token = _swap_start(parts, [lax.empty(p.shape, F32) for p in parts], name=f"swap_start{l}")
        if swapping is not None:
            update(l + 1, swapping, [token])
            after = [outs[a][0] for a in range(n_big)]
        swapping = (parts, lands, send, recv)
    update(0, swapping, [outs[a][0] for a in range(n_big)] if L > 1 else [ada[0]])
    results = {a: [t.reshape(big[a][1].shape) for t in outs[a]] for a in range(n_big)}

    def leaves(k, pick):
        s = pick
        return [s[0], ada[k], s[1], results[0][k], results[6][k], s[2], s[3], s[4], s[5], s[6], s[7], results[5][k],
                results[1][k], results[2][k], results[3][k], results[4][k], s[8]]

    return (loss, grad_x, *leaves(0, sm_g), *leaves(1, sm_d), *leaves(2, sm_m), *leaves(3, sm_v))
```

```python
import functools

import jax
import jax.numpy as jnp
from jax import lax
from jax.experimental import pallas as pl
from jax.experimental.pallas import tpu as pltpu

F32 = jnp.float32
BF16 = jnp.bfloat16
MESH = pl.DeviceIdType.MESH
ANY = pl.BlockSpec(memory_space=pl.ANY)

CHUNK = 64
HEAD_DIM = 64
N_Q_HEADS = 16
N_KV_HEADS = 4
Q_PER_KV = N_Q_HEADS // N_KV_HEADS
WINDOW_CHUNKS = 2
POOL_WIDTH = 1024
POOL_WINDOWS = (2, 4, 8, 16)
POOL_GROUP = 256
ATTN_WIDTH = 1024
KV_WIDTH = 256
CONV_WIDTH = 1024
CONV_KERNEL = 31
EPS = 1e-6
OFF_U, OFF_Z, OFF_Q, OFF_K, OFF_V, OFF_AZ, OFF_CA, OFF_CB, OFF_CZ, OFF_G = (
    0, 1024, 2048, 3072, 3328, 3584, 4608, 5632, 6656, 7680)
HALF = 512
POOL_HALO = 16
CONV_HALO = 32
ATTN_Q_BLOCK = 256
ATTN_HALO = WINDOW_CHUNKS * CHUNK
NEG_INF = -1e30

ADAM_LR, ADAM_B1, ADAM_B2, ADAM_EPS, ADAM_WD, ADAM_STEP = 0.001, 0.9, 0.999, 1e-08, 0.01, 10

N_DEV = 8
N_CHIP = 4
VMEM_CAP_BYTES = 56 * 2**20
MIB = 2**20


def _div(n, cap, mult=128):
    if n <= cap:
        return n
    best = None
    for t in range(mult, cap + 1, mult):
        if n % t == 0:
            best = t
    assert best is not None, (n, cap, mult)
    return best


def _params(n_grid, vmem_bytes=None):
    kw = dict(dimension_semantics=("arbitrary",) * n_grid)
    if vmem_bytes is not None:
        kw["vmem_limit_bytes"] = int(min(max(vmem_bytes * 5 // 4 + 4 * MIB, 32 * MIB), VMEM_CAP_BYTES))
    return pltpu.CompilerParams(**kw)


def _silu(z):
    return z * jax.nn.sigmoid(z)


def _dsilu(z):
    s = jax.nn.sigmoid(z)
    return s * (1.0 + z * (1.0 - s))


def _nbytes(shape, dtype):
    n = 1
    for d in shape:
        n *= d
    return n * jnp.dtype(dtype).itemsize


MM_VMEM_BUDGET = 50 * MIB


def _mm(a, b, mode, out_dtypes, *, name, b_layer=None, extras=(), epilogue=None, tm_cap=2048, tn_cap=1024, tk_cap=None):
    if mode == "tn":
        K, M = a.shape
        N = b.shape[-1]
    elif mode == "nt":
        M, K = a.shape
        N = b.shape[-2]
    else:
        M, K = a.shape
        N = b.shape[-1]
    tk = _div(K, tk_cap or 2048)
    nk = K // tk
    n_out = len(out_dtypes)
    n_ex = len(extras)
    stacked = b.ndim == 3
    per_elem = sum(jnp.dtype(dt).itemsize for dt in out_dtypes) + sum(e[0].dtype.itemsize for e in extras if e[1] == "tile")

    def need(tm, tn):
        return (2 * (tm * tk * a.dtype.itemsize + tk * tn * b.dtype.itemsize + tm * tn * per_elem)
                + tm * tn * 4 * (2 if nk > 1 else 1))

    tm, tn = min(((_div(M, mc), _div(N, nc)) for mc in (tm_cap, tm_cap // 2, tm_cap // 4) for nc in (tn_cap, tn_cap // 2)),
                 key=lambda t: (need(*t) > MM_VMEM_BUDGET, -t[0], -t[1]))

    def body(*refs):
        a_ref, b_ref = refs[0], refs[1]
        ex_refs = refs[2:2 + n_ex]
        pos = 2 + n_ex
        out_refs = refs[pos:pos + n_out]
        acc_ref = refs[pos + n_out] if nk > 1 else None
        k = pl.program_id(2)
        dims = {"nn": (((1,), (0,)), ((), ())), "nt": (((1,), (1,)), ((), ())), "tn": (((0,), (0,)), ((), ()))}[mode]

        def product():
            return lax.dot_general(a_ref[...].astype(BF16), b_ref[...].astype(BF16), dims, preferred_element_type=F32)

        def finish(acc):
            vals = epilogue(acc, *[r[...] for r in ex_refs]) if epilogue is not None else (acc,)
            for r, v in zip(out_refs, vals):
                r[...] = v.astype(r.dtype)

        if nk == 1:
            finish(product())
        else:
            @pl.when(k == 0)
            def _():
                acc_ref[...] = jnp.zeros_like(acc_ref)

            acc_ref[...] += product()

            @pl.when(k == nk - 1)
            def _():
                finish(acc_ref[...])

    if mode == "tn":
        a_spec = pl.BlockSpec((tk, tm), lambda i, j, k: (k, i))
    else:
        a_spec = pl.BlockSpec((tm, tk), lambda i, j, k: (i, k))
    if mode == "nt":
        b_blk, b_idx = (tn, tk), (lambda i, j, k: (j, k))
    else:
        b_blk, b_idx = (tk, tn), (lambda i, j, k: (k, j))
    if stacked:
        b_spec = pl.BlockSpec((None,) + b_blk, lambda i, j, k, f=b_idx: (b_layer,) + f(i, j, k))
    else:
        b_spec = pl.BlockSpec(b_blk, b_idx)
    in_specs = [a_spec, b_spec]
    operands = [a, b]
    vmem = 2 * (tm * tk * a.dtype.itemsize + tk * tn * b.dtype.itemsize) + tm * tn * 4 * 3
    for arr, kind, off in extras:
        if kind == "tile":
            assert off % tn == 0, (name, off, tn)
            in_specs.append(pl.BlockSpec((tm, tn), lambda i, j, k, o=off // tn: (i, o + j)))
        else:
            in_specs.append(pl.BlockSpec((1, tn), lambda i, j, k: (0, j)))
        operands.append(arr)
        vmem += 2 * tm * tn * arr.dtype.itemsize
    out_shape = [jax.ShapeDtypeStruct((M, N), dt) for dt in out_dtypes]
    out_specs = [pl.BlockSpec((tm, tn), lambda i, j, k: (i, j)) for _ in out_dtypes]
    vmem += sum(2 * tm * tn * jnp.dtype(dt).itemsize for dt in out_dtypes)
    outs = pl.pallas_call(
        body, name=name, grid=(M // tm, N // tn, nk), in_specs=in_specs, out_specs=out_specs, out_shape=out_shape,
        scratch_shapes=[pltpu.VMEM((tm, tn), F32)] if nk > 1 else [], compiler_params=_params(3, vmem))(*operands)
    return outs[0] if n_out == 1 else outs


def _merge(ys, wbs, proj, D, *, name):
    T = ys[0].shape[0]
    tm = _div(T, 1024)
    tn = HALF
    kw = ys[0].shape[1]
    g_off = [(OFF_G + b * D) // tn for b in range(3)]

    def body(y0, y1, y2, w0, w1, w2, g0, g1, g2, merged_ref, b0, b1, b2):
        acc = None
        for y, w, g, bo in ((y0, w0, g0, b0), (y1, w1, g1, b1), (y2, w2, g2, b2)):
            p = jnp.dot(y[...], w[...], preferred_element_type=F32)
            bo[...] = p.astype(bo.dtype)
            t = jax.nn.sigmoid(g[...].astype(F32)) * p
            acc = t if acc is None else acc + t
        merged_ref[...] = acc.astype(merged_ref.dtype)

    y_spec = pl.BlockSpec((tm, kw), lambda i, j: (i, 0))
    w_spec = pl.BlockSpec((None, kw, tn), lambda i, j: (0, 0, j))
    g_specs = [pl.BlockSpec((tm, tn), lambda i, j, o=o: (i, o + j)) for o in g_off]
    o_spec = pl.BlockSpec((tm, tn), lambda i, j: (i, j))
    vmem = 2 * (3 * tm * kw * 2 + 3 * kw * tn * 2 + 3 * tm * tn * proj.dtype.itemsize + 4 * tm * tn * 2) + 4 * tm * tn * 4
    return pl.pallas_call(
        body, name=name, grid=(T // tm, D // tn), in_specs=[y_spec] * 3 + [w_spec] * 3 + g_specs,
        out_specs=[o_spec] * 4, out_shape=[jax.ShapeDtypeStruct((T, D), BF16)] * 4,
        compiler_params=_params(2, vmem))(*ys, *wbs, proj, proj, proj)


def _row_tile(T, width, n_arrays):
    cap = max(8, (24 * MIB) // (2 * n_arrays * width * 4))
    return _div(T, min(cap, 1024), 8)


def _norm_mod(x, ng, scale, shift, *, name):
    T, D = x.shape
    tm = _row_tile(T, D, 3)

    def body(x_ref, ng_ref, sc_ref, sh_ref, h_ref):
        xv = x_ref[...]
        r = lax.rsqrt(jnp.mean(xv * xv, axis=-1, keepdims=True) + EPS)
        h = (xv * r) * ng_ref[...] * (1.0 + sc_ref[...]) + sh_ref[...]
        h_ref[...] = h.astype(h_ref.dtype)

    row = pl.BlockSpec((1, D), lambda i: (0, 0))
    tile = pl.BlockSpec((tm, D), lambda i: (i, 0))
    return pl.pallas_call(body, name=name, grid=(T // tm,), in_specs=[tile, row, row, row], out_specs=tile,
                          out_shape=jax.ShapeDtypeStruct((T, D), BF16), compiler_params=_params(1))(x, ng, scale, shift)


def _gate_out_bwd(dx, o_ref, gate_ref, dmo_ref, dgate_ref, first):
    dmo_ref[...] = (dx * gate_ref[...]).astype(dmo_ref.dtype)
    p = jnp.sum(dx * o_ref[...].astype(F32), axis=0, keepdims=True)

    @pl.when(first)
    def _():
        dgate_ref[...] = p

    @pl.when(jnp.logical_not(first))
    def _():
        dgate_ref[...] += p


def _norm_mod_bwd(x, dh, dx_out, ng, scale, below=None, *, name):
    T, D = x.shape
    tm = _row_tile(T, D, 8)

    def body(*refs):
        if below is None:
            x_ref, dh_ref, dxo_ref, ng_ref, sc_ref, dx_ref, dng_ref, dsc_ref, dsh_ref = refs
        else:
            x_ref, dh_ref, dxo_ref, ng_ref, sc_ref, o_ref, gate_ref, dx_ref, dng_ref, dsc_ref, dsh_ref, dmo_ref, dgate_ref = refs
            below_refs = (o_ref, gate_ref, dmo_ref, dgate_ref)
        i = pl.program_id(0)
        xv = x_ref[...]
        dh_v = dh_ref[...].astype(F32)
        r = lax.rsqrt(jnp.mean(xv * xv, axis=-1, keepdims=True) + EPS)
        xn = xv * r
        one_sc = 1.0 + sc_ref[...]
        dxn = dh_v * (ng_ref[...] * one_sc)
        dx = dxo_ref[...] + r * (dxn - xn * jnp.mean(dxn * xn, axis=-1, keepdims=True))
        dx_ref[...] = dx
        if below is not None:
            _gate_out_bwd(dx, *below_refs, i == 0)
        t = dh_v * xn
        parts = (jnp.sum(t * one_sc, axis=0, keepdims=True), jnp.sum(t * ng_ref[...], axis=0, keepdims=True),
                 jnp.sum(dh_v, axis=0, keepdims=True))
        for ref, p in zip((dng_ref, dsc_ref, dsh_ref), parts):
            @pl.when(i == 0)
            def _(ref=ref, p=p):
                ref[...] = p

            @pl.when(i > 0)
            def _(ref=ref, p=p):
                ref[...] += p

    row = pl.BlockSpec((1, D), lambda i: (0, 0))
    tile = pl.BlockSpec((tm, D), lambda i: (i, 0))
    vec = jax.ShapeDtypeStruct((1, D), F32)
    in_specs, out_specs = [tile, tile, tile, row, row], [tile, row, row, row]
    out_shape = [jax.ShapeDtypeStruct((T, D), F32), vec, vec, vec]
    if below is not None:
        in_specs, out_specs = in_specs + [tile, row], out_specs + [tile, row]
        out_shape = out_shape + [jax.ShapeDtypeStruct((T, D), BF16), vec]
    return pl.pallas_call(body, name=name, grid=(T // tm,), in_specs=in_specs, out_specs=out_specs, out_shape=out_shape,
                          compiler_params=_params(1))(x, dh, dx_out, ng, scale, *(below or ()))


def _final_loss(x, target, fg, below, *, name):
    T, D = x.shape
    tm = _row_tile(T, D, 6)

    def body(x_ref, t_ref, g_ref, o_ref, gate_ref, loss_ref, dx_ref, dg_ref, dmo_ref, dgate_ref):
        i = pl.program_id(0)
        xv = x_ref[...]
        r = lax.rsqrt(jnp.mean(xv * xv, axis=-1, keepdims=True) + EPS)
        xn = xv * r
        err = xn * g_ref[...] - t_ref[...]
        part = 0.5 * jnp.sum(jnp.sum(err * err, axis=1, keepdims=True), axis=0, keepdims=True) / D
        dy = err / D
        dxn = dy * g_ref[...]
        dx = r * (dxn - xn * jnp.mean(dxn * xn, axis=-1, keepdims=True))
        dx_ref[...] = dx
        _gate_out_bwd(dx, o_ref, gate_ref, dmo_ref, dgate_ref, i == 0)
        dg = jnp.sum(dy * xn, axis=0, keepdims=True)

        @pl.when(i == 0)
        def _():
            loss_ref[...] = part
            dg_ref[...] = dg

        @pl.when(i > 0)
        def _():
            loss_ref[...] += part
            dg_ref[...] += dg

    row = pl.BlockSpec((1, D), lambda i: (0, 0))
    tile = pl.BlockSpec((tm, D), lambda i: (i, 0))
    one = pl.BlockSpec((1, 1), lambda i: (0, 0))
    vec = jax.ShapeDtypeStruct((1, D), F32)
    return pl.pallas_call(
        body, name=name, grid=(T // tm,), in_specs=[tile, tile, row, tile, row], out_specs=[one, tile, row, tile, row],
        out_shape=[jax.ShapeDtypeStruct((1, 1), F32), jax.ShapeDtypeStruct((T, D), F32), vec, jax.ShapeDtypeStruct((T, D), BF16), vec],
        compiler_params=_params(1))(x, target, fg, *below)


def _merge_bwd(dmerged, branches, proj, D, *, name):
    T = dmerged.shape[0]
    tm = _div(T, 1024)
    tn = HALF
    g_off = [(OFF_G + b * D) // tn for b in range(3)]

    def body(dm_ref, b0, b1, b2, g0, g1, g2, db0, db1, db2, dg0, dg1, dg2):
        dm = dm_ref[...].astype(F32)
        for b, g, db, dg in ((b0, g0, db0, dg0), (b1, g1, db1, dg1), (b2, g2, db2, dg2)):
            s = jax.nn.sigmoid(g[...].astype(F32))
            db[...] = (dm * s).astype(db.dtype)
            dg[...] = (dm * b[...].astype(F32) * s * (1.0 - s)).astype(dg.dtype)

    tile = pl.BlockSpec((tm, tn), lambda i, j: (i, j))
    g_specs = [pl.BlockSpec((tm, tn), lambda i, j, o=o: (i, o + j)) for o in g_off]
    return pl.pallas_call(body, name=name, grid=(T // tm, D // tn), in_specs=[tile] * 4 + g_specs, out_specs=[tile] * 6,
                          out_shape=[jax.ShapeDtypeStruct((T, D), BF16)] * 6,
                          compiler_params=_params(2))(dmerged, *branches, proj, proj, proj)


def _conv_out_bwd(dy, cpre, proj, *, name):
    T = dy.shape[0]
    tm = _div(T, 1024)
    tn = HALF

    def body(dy_ref, c_ref, z_ref, dc_ref, dz_ref):
        dyv = dy_ref[...].astype(F32)
        z = z_ref[...].astype(F32)
        dc_ref[...] = (dyv * _silu(z)).astype(dc_ref.dtype)
        dz_ref[...] = (dyv * c_ref[...].astype(F32) * _dsilu(z)).astype(dz_ref.dtype)

    tile = pl.BlockSpec((tm, tn), lambda i, j: (i, j))
    z_spec = pl.BlockSpec((tm, tn), lambda i, j: (i, OFF_CZ // tn + j))
    return pl.pallas_call(body, name=name, grid=(T // tm, CONV_WIDTH // tn), in_specs=[tile, tile, z_spec],
                          out_specs=[tile, tile], out_shape=[jax.ShapeDtypeStruct((T, CONV_WIDTH), BF16)] * 2,
                          compiler_params=_params(2))(dy, cpre, proj)


def _pool_mixed(ext, u, g, row0):
    w = POOL_WINDOWS[g]
    s = ext
    shift = 1
    while shift < w:
        s = s + pltpu.roll(s, shift, 0)
        shift *= 2
    tm = u.shape[0]
    t = row0 + lax.broadcasted_iota(jnp.int32, (tm, 1), 0)
    inv = 1.0 / jnp.minimum(t + 1, w).astype(F32)
    return s[POOL_HALO:, :] * inv - u, inv


def _pool_specs(T, tm):
    per = tm // POOL_HALO
    cur = lambda col: pl.BlockSpec((tm, POOL_WIDTH), lambda i, c=col: (i, c))
    prev = pl.BlockSpec((POOL_HALO, POOL_WIDTH), lambda i: (jnp.maximum(i * per - 1, 0), 0))
    return cur, prev


def _pool_fwd(proj, pool_w, scale, *, name):
    T = proj.shape[0]
    tm = _div(T, 512, 16)
    cur, prev = _pool_specs(T, tm)

    def body(u_ref, up_ref, z_ref, w_ref, sc_ref, y_ref):
        i = pl.program_id(0)
        u = u_ref[...].astype(F32)
        halo = jnp.where(i == 0, 0.0, up_ref[...].astype(F32))
        ext = jnp.concatenate([halo, u], axis=0)
        for g in range(len(POOL_WINDOWS)):
            cols = slice(g * POOL_GROUP, (g + 1) * POOL_GROUP)
            mixed, _ = _pool_mixed(ext[:, cols], u[:, cols], g, i * tm)
            p = jnp.dot(mixed.astype(BF16), w_ref[g], preferred_element_type=F32)
            y = p * sc_ref[:, cols] * _silu(z_ref[:, cols].astype(F32))
            y_ref[:, cols] = y.astype(y_ref.dtype)

    w_spec = pl.BlockSpec((len(POOL_WINDOWS), POOL_GROUP, POOL_GROUP), lambda i: (0, 0, 0))
    row = pl.BlockSpec((1, POOL_WIDTH), lambda i: (0, 0))
    return pl.pallas_call(body, name=name, grid=(T // tm,), in_specs=[cur(0), prev, cur(1), w_spec, row],
                          out_specs=pl.BlockSpec((tm, POOL_WIDTH), lambda i: (i, 0)),
                          out_shape=jax.ShapeDtypeStruct((T, POOL_WIDTH), BF16),
                          compiler_params=_params(1))(proj, proj, proj, pool_w, scale)


def _pool_bwd(proj, dy, pool_w, scale, *, name):
    T = proj.shape[0]
    tm = _div(T, 512, 16)
    cur, prev = _pool_specs(T, tm)
    n_g = len(POOL_WINDOWS)

    def body(u_ref, up_ref, z_ref, dy_ref, w_ref, sc_ref, dz_ref, dmn_ref, dsc_ref, dw_ref):
        i = pl.program_id(0)
        u = u_ref[...].astype(F32)
        halo = jnp.where(i == 0, 0.0, up_ref[...].astype(F32))
        ext = jnp.concatenate([halo, u], axis=0)
        for g in range(n_g):
            cols = slice(g * POOL_GROUP, (g + 1) * POOL_GROUP)
            mixed, inv = _pool_mixed(ext[:, cols], u[:, cols], g, i * tm)
            mixed = mixed.astype(BF16)
            w = w_ref[g]
            p = jnp.dot(mixed, w, preferred_element_type=F32)
            z = z_ref[:, cols].astype(F32)
            dyv = dy_ref[:, cols].astype(F32)
            sc = sc_ref[:, cols]
            dypre = dyv * _silu(z)
            dz_ref[:, cols] = (dyv * (p * sc) * _dsilu(z)).astype(dz_ref.dtype)
            dsc = jnp.sum(dypre * p, axis=0, keepdims=True)
            dp = (dypre * sc).astype(BF16)
            dwg = lax.dot_general(mixed, dp, (((0,), (0,)), ((), ())), preferred_element_type=F32)
            dmixed = lax.dot_general(dp, w, (((1,), (1,)), ((), ())), preferred_element_type=F32)
            dmn_ref[:, cols] = dmixed * inv

            @pl.when(i == 0)
            def _(g=g, cols=cols, dsc=dsc, dwg=dwg):
                dsc_ref[:, cols] = dsc
                dw_ref[g] = dwg

            @pl.when(i > 0)
            def _(g=g, cols=cols, dsc=dsc, dwg=dwg):
                dsc_ref[:, cols] += dsc
                dw_ref[g] += dwg

    w_spec = pl.BlockSpec((n_g, POOL_GROUP, POOL_GROUP), lambda i: (0, 0, 0))
    row = pl.BlockSpec((1, POOL_WIDTH), lambda i: (0, 0))
    tile = pl.BlockSpec((tm, POOL_WIDTH), lambda i: (i, 0))
    dw_spec = pl.BlockSpec((n_g, POOL_GROUP, POOL_GROUP), lambda i: (0, 0, 0))
    return pl.pallas_call(
        body, name=name, grid=(T // tm,), in_specs=[cur(0), prev, cur(1), tile, w_spec, row],
        out_specs=[tile, tile, row, dw_spec],
        out_shape=[jax.ShapeDtypeStruct((T, POOL_WIDTH), BF16), jax.ShapeDtypeStruct((T, POOL_WIDTH), F32),
                   jax.ShapeDtypeStruct((1, POOL_WIDTH), F32), jax.ShapeDtypeStruct((n_g, POOL_GROUP, POOL_GROUP), F32)],
        compiler_params=_params(1))(proj, proj, proj, dy, pool_w, scale)


def _pool_bwd_window(dmn, *, name):
    T = dmn.shape[0]
    tm = _div(T, 512, 16)
    per = tm // POOL_HALO
    last = T // POOL_HALO - 1
    nb = T // tm

    def body(c_ref, n_ref, du_ref):
        i = pl.program_id(0)
        cur = c_ref[...]
        nxt = jnp.where(i == nb - 1, 0.0, n_ref[...])
        ext = jnp.concatenate([cur, nxt], axis=0)
        rows = tm + POOL_HALO
        t = i * tm + lax.broadcasted_iota(jnp.int32, (tm, 1), 0)
        for g, w in enumerate(POOL_WINDOWS):
            cols = slice(g * POOL_GROUP, (g + 1) * POOL_GROUP)
            s = ext[:, cols]
            shift = 1
            while shift < w:
                s = s + pltpu.roll(s, rows - shift, 0)
                shift *= 2
            cnt = jnp.minimum(t + 1, w).astype(F32)
            du_ref[:, cols] = (s[:tm, :] - cur[:, cols] * cnt).astype(du_ref.dtype)

    tile = pl.BlockSpec((tm, POOL_WIDTH), lambda i: (i, 0))
    nxt = pl.BlockSpec((POOL_HALO, POOL_WIDTH), lambda i: (jnp.minimum((i + 1) * per, last), 0))
    return pl.pallas_call(body, name=name, grid=(nb,), in_specs=[tile, nxt], out_specs=tile,
                          out_shape=jax.ShapeDtypeStruct((T, POOL_WIDTH), BF16), compiler_params=_params(1))(dmn, dmn)


def _attn_mask(i):
    qb, keys = ATTN_Q_BLOCK, ATTN_Q_BLOCK + ATTN_HALO
    qi = lax.broadcasted_iota(jnp.int32, (qb, keys), 0) // CHUNK
    kj = lax.broadcasted_iota(jnp.int32, (qb, keys), 1) // CHUNK - WINDOW_CHUNKS
    return (kj <= qi) & (kj >= qi - WINDOW_CHUNKS) & (kj + i * (qb // CHUNK) >= 0)


def _attn_specs(T, order):
    qb = ATTN_Q_BLOCK
    per = qb // ATTN_HALO
    cur = lambda width, col: pl.BlockSpec((qb, width), lambda i, c=col: (order(i), c))
    prev = lambda col: pl.BlockSpec((ATTN_HALO, KV_WIDTH), lambda i, c=col: (jnp.maximum(order(i) * per - 1, 0), c))
    return cur, prev


def _attn_fwd(proj, sink, *, name):
    T = proj.shape[0]
    qb = ATTN_Q_BLOCK
    cur, prev = _attn_specs(T, lambda i: i)

    def body(sink_ref, q_ref, kc_ref, kp_ref, vc_ref, vp_ref, z0_ref, z1_ref, o_ref, y_ref, lse_ref):
        i = pl.program_id(0)
        q = q_ref[...].astype(BF16)
        kk = jnp.concatenate([kp_ref[...], kc_ref[...]], axis=0).astype(BF16)
        vv = jnp.concatenate([vp_ref[...], vc_ref[...]], axis=0).astype(BF16)
        mask = _attn_mask(i)
        lane = lax.broadcasted_iota(jnp.int32, (qb, 128), 1)
        lse = jnp.zeros((qb, 128), F32)
        for h in range(N_Q_HEADS):
            hs = slice(h * HEAD_DIM, (h + 1) * HEAD_DIM)
            ks = slice((h // Q_PER_KV) * HEAD_DIM, (h // Q_PER_KV + 1) * HEAD_DIM)
            s = lax.dot_general(q[:, hs], kk[:, ks], (((1,), (1,)), ((), ())), preferred_element_type=F32)
            s = jnp.where(mask, s * (HEAD_DIM ** -0.5), NEG_INF)
            sk = sink_ref[h]
            m = jnp.maximum(jnp.max(s, axis=1, keepdims=True), sk)
            p = jnp.exp(s - m)
            den = jnp.sum(p, axis=1, keepdims=True) + jnp.exp(sk - m)
            oh = jnp.dot(p.astype(BF16), vv[:, ks], preferred_element_type=F32) / den
            zr = z0_ref if h < N_Q_HEADS // 2 else z1_ref
            zs = slice((h % (N_Q_HEADS // 2)) * HEAD_DIM, (h % (N_Q_HEADS // 2) + 1) * HEAD_DIM)
            o_ref[:, hs] = oh.astype(o_ref.dtype)
            y_ref[:, hs] = (oh * _silu(zr[:, zs].astype(F32))).astype(y_ref.dtype)
            lse = jnp.where(lane == h, m + jnp.log(den), lse)
        lse_ref[...] = lse

    kcol, vcol = OFF_K // KV_WIDTH, OFF_V // KV_WIDTH
    tile = pl.BlockSpec((qb, ATTN_WIDTH), lambda i: (i, 0))
    in_specs = [pl.BlockSpec(memory_space=pltpu.SMEM), cur(ATTN_WIDTH, OFF_Q // ATTN_WIDTH), cur(KV_WIDTH, kcol), prev(kcol),
                cur(KV_WIDTH, vcol), prev(vcol), cur(HALF, OFF_AZ // HALF), cur(HALF, OFF_AZ // HALF + 1)]
    return pl.pallas_call(
        body, name=name, grid=(T // qb,), in_specs=in_specs,
        out_specs=[tile, tile, pl.BlockSpec((qb, 128), lambda i: (i, 0))],
        out_shape=[jax.ShapeDtypeStruct((T, ATTN_WIDTH), BF16), jax.ShapeDtypeStruct((T, ATTN_WIDTH), BF16),
                   jax.ShapeDtypeStruct((T, 128), F32)],
        compiler_params=_params(1))(sink, *([proj] * 7))


def _attn_bwd(proj, sink, o, lse, dy, *, name):
    T = proj.shape[0]
    qb = ATTN_Q_BLOCK
    nb = T // qb
    order = lambda i: nb - 1 - i
    cur, prev = _attn_specs(T, order)

    def body(sink_ref, q_ref, kc_ref, kp_ref, vc_ref, vp_ref, z0_ref, z1_ref, o_ref, lse_ref, dy_ref,
             dq_ref, dk_ref, dv_ref, dz_ref, dsink_ref, dk_carry, dv_carry):
        i = pl.program_id(0)
        blk = order(i)
        q = q_ref[...].astype(BF16)
        kk = jnp.concatenate([kp_ref[...], kc_ref[...]], axis=0).astype(BF16)
        vv = jnp.concatenate([vp_ref[...], vc_ref[...]], axis=0).astype(BF16)
        mask = _attn_mask(blk)
        lane = lax.broadcasted_iota(jnp.int32, (1, 128), 1)
        dsink = jnp.zeros((1, 128), F32)
        scale = HEAD_DIM ** -0.5
        for kv in range(N_KV_HEADS):
            ks = slice(kv * HEAD_DIM, (kv + 1) * HEAD_DIM)
            dk_acc = jnp.zeros((qb + ATTN_HALO, HEAD_DIM), F32)
            dv_acc = jnp.zeros((qb + ATTN_HALO, HEAD_DIM), F32)
            for h in range(kv * Q_PER_KV, (kv + 1) * Q_PER_KV):
                hs = slice(h * HEAD_DIM, (h + 1) * HEAD_DIM)
                zr = z0_ref if h < N_Q_HEADS // 2 else z1_ref
                zs = slice((h % (N_Q_HEADS // 2)) * HEAD_DIM, (h % (N_Q_HEADS // 2) + 1) * HEAD_DIM)
                z = zr[:, zs].astype(F32)
                dyh = dy_ref[:, hs].astype(F32)
                oh = o_ref[:, hs].astype(F32)
                do = dyh * _silu(z)
                dz_ref[:, hs] = (dyh * oh * _dsilu(z)).astype(dz_ref.dtype)
                drow = jnp.sum(do * oh, axis=1, keepdims=True)
                lse_h = lse_ref[:, h:h + 1]
                s = lax.dot_general(q[:, hs], kk[:, ks], (((1,), (1,)), ((), ())), preferred_element_type=F32)
                p = jnp.exp(jnp.where(mask, s * scale, NEG_INF) - lse_h)
                do_b = do.astype(BF16)
                dv_acc = dv_acc + lax.dot_general(p.astype(BF16), do_b, (((0,), (0,)), ((), ())),
                                                  preferred_element_type=F32)
                dp = lax.dot_general(do_b, vv[:, ks], (((1,), (1,)), ((), ())), preferred_element_type=F32)
                ds = (p * (dp - drow)).astype(BF16)
                dq_ref[:, hs] = (jnp.dot(ds, kk[:, ks], preferred_element_type=F32) * scale).astype(dq_ref.dtype)
                dk_acc = dk_acc + lax.dot_general(ds, q[:, hs], (((0,), (0,)), ((), ())),
                                                  preferred_element_type=F32) * scale
                p_sink = jnp.exp(sink_ref[h] - lse_h)
                dsink = jnp.where(lane == h, -jnp.sum(p_sink * drow, axis=0, keepdims=True), dsink)
            for acc, carry, out in ((dk_acc, dk_carry, dk_ref), (dv_acc, dv_carry, dv_ref)):
                tail = acc[qb:, :] + jnp.where(i == 0, 0.0, carry[:, ks])
                out[:, ks] = jnp.concatenate([acc[ATTN_HALO:qb, :], tail], axis=0).astype(out.dtype)
                carry[:, ks] = acc[:ATTN_HALO, :]

        @pl.when(i == 0)
        def _():
            dsink_ref[...] = dsink

        @pl.when(i > 0)
        def _():
            dsink_ref[...] += dsink

    kcol, vcol = OFF_K // KV_WIDTH, OFF_V // KV_WIDTH
    tile = pl.BlockSpec((qb, ATTN_WIDTH), lambda i: (order(i), 0))
    kv_tile = pl.BlockSpec((qb, KV_WIDTH), lambda i: (order(i), 0))
    lse_spec = pl.BlockSpec((qb, 128), lambda i: (order(i), 0))
    in_specs = [pl.BlockSpec(memory_space=pltpu.SMEM), cur(ATTN_WIDTH, OFF_Q // ATTN_WIDTH), cur(KV_WIDTH, kcol), prev(kcol),
                cur(KV_WIDTH, vcol), prev(vcol), cur(HALF, OFF_AZ // HALF), cur(HALF, OFF_AZ // HALF + 1),
                tile, lse_spec, tile]
    return pl.pallas_call(
        body, name=name, grid=(nb,), in_specs=in_specs,
        out_specs=[tile, kv_tile, kv_tile, tile, pl.BlockSpec((1, 128), lambda i: (0, 0))],
        out_shape=[jax.ShapeDtypeStruct((T, ATTN_WIDTH), BF16), jax.ShapeDtypeStruct((T, KV_WIDTH), BF16),
                   jax.ShapeDtypeStruct((T, KV_WIDTH), BF16), jax.ShapeDtypeStruct((T, ATTN_WIDTH), BF16),
                   jax.ShapeDtypeStruct((1, 128), F32)],
        scratch_shapes=[pltpu.VMEM((ATTN_HALO, KV_WIDTH), F32), pltpu.VMEM((ATTN_HALO, KV_WIDTH), F32)],
        compiler_params=_params(1))(sink, *([proj] * 7), o, lse, dy)


def _conv_specs(T, tm):
    per = tm // CONV_HALO
    ca, cb = OFF_CA // HALF, OFF_CB // HALF
    cur = lambda col: pl.BlockSpec((tm, HALF), lambda i, c=col: (i, c))
    prev = lambda col: pl.BlockSpec((CONV_HALO, HALF), lambda i, c=col: (jnp.maximum(i * per - 1, 0), c))
    return [cur(ca), cur(ca + 1), cur(cb), cur(cb + 1), prev(ca), prev(ca + 1), prev(cb), prev(cb + 1)]


def _conv_glu_ext(refs, i, ext_ref):
    a0, a1, b0, b1, pa0, pa1, pb0, pb1 = refs
    a = jnp.concatenate([a0[...], a1[...]], axis=1).astype(F32)
    sb = jax.nn.sigmoid(jnp.concatenate([b0[...], b1[...]], axis=1).astype(F32))
    pa = jnp.concatenate([pa0[...], pa1[...]], axis=1).astype(F32)
    pb = jnp.concatenate([pb0[...], pb1[...]], axis=1).astype(F32)
    ext_ref[:CONV_HALO, :] = jnp.where(i == 0, 0.0, pa * jax.nn.sigmoid(pb))
    ext_ref[CONV_HALO:, :] = a * sb
    return a, sb


def _conv_scratch(tm):
    return [pltpu.VMEM((tm + CONV_HALO, CONV_WIDTH), F32), pltpu.VMEM((7, tm + CONV_HALO - 8, CONV_WIDTH), F32)]


def _conv_fill_shifted(ext_ref, sh_ref):
    rows = sh_ref.shape[1]
    for b in range(1, 8):
        sh_ref[b - 1] = ext_ref[b:b + rows, :]


def _conv_window(ext_ref, sh_ref, start, tm, cols):
    a, b = divmod(start, 8)
    if b == 0:
        return ext_ref[8 * a:8 * a + tm, cols]
    return sh_ref[b - 1, 8 * a:8 * a + tm, cols]


LANES = 128


def _lane_blocks(width):
    return [slice(k, k + LANES) for k in range(0, width, LANES)]


def _conv_taps(ext_ref, sh_ref, dw_ref, out_ref, first_start, step, tm, bias_ref=None):
    sub = tm // 8
    for cols in _lane_blocks(CONV_WIDTH):
        y = None
        for j in range(CONV_KERNEL):
            tap = jnp.broadcast_to(dw_ref[j:j + 1, cols], (8, LANES))
            window = _conv_window(ext_ref, sh_ref, first_start + step * j, tm, cols).reshape(sub, 8, LANES)
            t = tap * window
            y = t if y is None else y + t
        y = y.reshape(tm, LANES)
        out_ref[:, cols] = y if bias_ref is None else y + bias_ref[:, cols]


def _conv_fwd(proj, dw, dwb, lng, lnb, *, name):
    T = proj.shape[0]
    tm = _div(T, 256, 32)

    def body(*refs):
        dw_ref, dwb_ref, g_ref, b_ref, s_ref, yc_ref, ext_ref, sh_ref = refs[8:]
        i = pl.program_id(0)
        _conv_glu_ext(refs[:8], i, ext_ref)
        _conv_fill_shifted(ext_ref, sh_ref)
        _conv_taps(ext_ref, sh_ref, dw_ref, yc_ref, CONV_HALO - (CONV_KERNEL - 1), 1, tm, dwb_ref)
        yc = yc_ref[...]
        mu = jnp.mean(yc, axis=-1, keepdims=True)
        d = yc - mu
        rstd = lax.rsqrt(jnp.mean(d * d, axis=-1, keepdims=True) + EPS)
        s_ref[...] = _silu(d * rstd * g_ref[...] + b_ref[...]).astype(s_ref.dtype)

    row = pl.BlockSpec((1, CONV_WIDTH), lambda i: (0, 0))
    taps = pl.BlockSpec((CONV_KERNEL, CONV_WIDTH), lambda i: (0, 0))
    tile = pl.BlockSpec((tm, CONV_WIDTH), lambda i: (i, 0))
    return pl.pallas_call(
        body, name=name, grid=(T // tm,), in_specs=_conv_specs(T, tm) + [taps, row, row, row], out_specs=[tile, tile],
        out_shape=[jax.ShapeDtypeStruct((T, CONV_WIDTH), BF16), jax.ShapeDtypeStruct((T, CONV_WIDTH), F32)],
        scratch_shapes=_conv_scratch(tm), compiler_params=_params(1))(*([proj] * 8), dw, dwb, lng, lnb)


def _conv_bwd(proj, yc_saved, ds, lng, lnb, *, name):
    T = proj.shape[0]
    tm = _div(T, 256, 32)
    nb = T // tm
    sub = tm // 8

    def body(*refs):
        yc_ref, ds_ref, g_ref, b_ref, dyc_ref, ddw_ref, ddwb_ref, dg_ref, db_ref, ext_ref, sh_ref, acc_ref = refs[8:]
        i = pl.program_id(0)
        _conv_glu_ext(refs[:8], i, ext_ref)
        _conv_fill_shifted(ext_ref, sh_ref)
        yc = yc_ref[...]
        mu = jnp.mean(yc, axis=-1, keepdims=True)
        d = yc - mu
        rstd = lax.rsqrt(jnp.mean(d * d, axis=-1, keepdims=True) + EPS)
        xhat = d * rstd
        dln = ds_ref[...].astype(F32) * _dsilu(xhat * g_ref[...] + b_ref[...])
        dxhat = dln * g_ref[...]
        dyc = rstd * (dxhat - jnp.mean(dxhat, axis=-1, keepdims=True)
                      - xhat * jnp.mean(dxhat * xhat, axis=-1, keepdims=True))
        dyc_ref[...] = dyc
        first = i == 0

        def accumulate(ref, idx, val):
            @pl.when(first)
            def _():
                ref[idx] = val

            @pl.when(jnp.logical_not(first))
            def _():
                ref[idx] += val

        accumulate(dg_ref, slice(None), jnp.sum(dln * xhat, axis=0, keepdims=True))
        accumulate(db_ref, slice(None), jnp.sum(dln, axis=0, keepdims=True))
        accumulate(ddwb_ref, slice(None), jnp.sum(dyc, axis=0, keepdims=True))
        @pl.when(first)
        def _():
            acc_ref[...] = jnp.zeros_like(acc_ref)

        for cols in _lane_blocks(CONV_WIDTH):
            dyc_b = dyc_ref[:, cols].reshape(sub, 8, LANES)
            for j in range(CONV_KERNEL):
                window = _conv_window(ext_ref, sh_ref, CONV_HALO - (CONV_KERNEL - 1) + j, tm, cols)
                acc_ref[j, :, cols] += jnp.sum(dyc_b * window.reshape(sub, 8, LANES), axis=0)

        @pl.when(i == nb - 1)
        def _():
            for j in range(CONV_KERNEL):
                ddw_ref[j:j + 1, :] = jnp.sum(acc_ref[j], axis=0, keepdims=True)

    row = pl.BlockSpec((1, CONV_WIDTH), lambda i: (0, 0))
    taps = pl.BlockSpec((CONV_KERNEL, CONV_WIDTH), lambda i: (0, 0))
    tile = pl.BlockSpec((tm, CONV_WIDTH), lambda i: (i, 0))
    vec = jax.ShapeDtypeStruct((1, CONV_WIDTH), F32)
    return pl.pallas_call(
        body, name=name, grid=(nb,), in_specs=_conv_specs(T, tm) + [tile, tile, row, row],
        out_specs=[tile, taps, row, row, row],
        out_shape=[jax.ShapeDtypeStruct((T, CONV_WIDTH), F32), jax.ShapeDtypeStruct((CONV_KERNEL, CONV_WIDTH), F32), vec, vec, vec],
        scratch_shapes=_conv_scratch(tm) + [pltpu.VMEM((CONV_KERNEL, 8, CONV_WIDTH), F32)],
        compiler_params=_params(1))(*([proj] * 8), yc_saved, ds, lng, lnb)


def _conv_bwd_input(proj, dyc, dw, *, name):
    T = proj.shape[0]
    tm = _div(T, 256, 32)
    per = tm // CONV_HALO
    last = T // CONV_HALO - 1
    nb = T // tm
    ca, cb = OFF_CA // HALF, OFF_CB // HALF

    def body(a0, a1, b0, b1, c_ref, n_ref, dw_ref, da_ref, db_ref, ext_ref, sh_ref, dg_ref):
        i = pl.program_id(0)
        ext_ref[:tm, :] = c_ref[...]
        ext_ref[tm:, :] = jnp.where(i == nb - 1, 0.0, n_ref[...])
        _conv_fill_shifted(ext_ref, sh_ref)
        _conv_taps(ext_ref, sh_ref, dw_ref, dg_ref, CONV_KERNEL - 1, -1, tm)
        dg = dg_ref[...]
        a = jnp.concatenate([a0[...], a1[...]], axis=1).astype(F32)
        sb = jax.nn.sigmoid(jnp.concatenate([b0[...], b1[...]], axis=1).astype(F32))
        da_ref[...] = (dg * sb).astype(da_ref.dtype)
        db_ref[...] = (dg * a * sb * (1.0 - sb)).astype(db_ref.dtype)

    cur = lambda col: pl.BlockSpec((tm, HALF), lambda i, c=col: (i, c))
    tile = pl.BlockSpec((tm, CONV_WIDTH), lambda i: (i, 0))
    nxt = pl.BlockSpec((CONV_HALO, CONV_WIDTH), lambda i: (jnp.minimum((i + 1) * per, last), 0))
    taps = pl.BlockSpec((CONV_KERNEL, CONV_WIDTH), lambda i: (0, 0))
    return pl.pallas_call(
        body, name=name, grid=(nb,), in_specs=[cur(ca), cur(ca + 1), cur(cb), cur(cb + 1), tile, nxt, taps],
        out_specs=[tile, tile], out_shape=[jax.ShapeDtypeStruct((T, CONV_WIDTH), BF16)] * 2,
        scratch_shapes=_conv_scratch(tm) + [pltpu.VMEM((tm, CONV_WIDTH), F32)],
        compiler_params=_params(1))(proj, proj, proj, proj, dyc, dyc, dw)


def _ada_mod(c_all, w_ada, b_slab, *, name):
    L, D, N = w_ada.shape
    tn = _div(N, 512)

    def body(c_ref, w_ref, b_ref, o_ref):
        ca = _silu(c_ref[...]).astype(BF16)
        o_ref[...] = jnp.dot(ca, w_ref[...].astype(BF16), preferred_element_type=F32) + b_ref[...]

    return pl.pallas_call(
        body, name=name, grid=(L, N // tn),
        in_specs=[pl.BlockSpec((N_DEV, D), lambda l, j: (0, 0)), pl.BlockSpec((None, D, tn), lambda l, j: (l, 0, j)),
                  pl.BlockSpec((None, 1, tn), lambda l, j: (l, 0, j))],
        out_specs=pl.BlockSpec((None, N_DEV, tn), lambda l, j: (l, 0, j)),
        out_shape=jax.ShapeDtypeStruct((L, N_DEV, N), F32), compiler_params=_params(2))(c_all, w_ada, b_slab)


def _ada_grad(c_all_t, dmod_slab, *, name):
    D = c_all_t.shape[0]
    L, _, N = dmod_slab.shape
    tm = _div(D, 512)
    tn = _div(N, 512)

    def body(c_ref, d_ref, o_ref):
        ca = _silu(c_ref[...]).astype(BF16).astype(F32)
        dm = d_ref[...].astype(BF16).astype(F32)
        acc = None
        for b in range(N_DEV):
            t = ca[:, b:b + 1] * dm[b:b + 1, :]
            acc = t if acc is None else acc + t
        o_ref[...] = acc

    return pl.pallas_call(
        body, name=name, grid=(L, D // tm, N // tn),
        in_specs=[pl.BlockSpec((tm, N_DEV), lambda l, i, j: (i, 0)), pl.BlockSpec((None, N_DEV, tn), lambda l, i, j: (l, 0, j))],
        out_specs=pl.BlockSpec((None, tm, tn), lambda l, i, j: (l, i, j)),
        out_shape=jax.ShapeDtypeStruct((L, D, N), F32), compiler_params=_params(3))(c_all_t, dmod_slab)


def _flat_tile(R, C, n_arrays):
    cap = max(8, (20 * MIB) // (2 * n_arrays * C * 4))
    return _div(R, cap, 8) if R % 8 == 0 else R


def _adamw_math(w, g, m, v):
    m = ADAM_B1 * m + (1.0 - ADAM_B1) * g
    v = ADAM_B2 * v + (1.0 - ADAM_B2) * (g * g)
    m_hat = m / (1.0 - ADAM_B1 ** ADAM_STEP)
    v_hat = v / (1.0 - ADAM_B2 ** ADAM_STEP)
    delta = -ADAM_LR * (m_hat / (jnp.sqrt(v_hat) + ADAM_EPS) + ADAM_WD * w)
    return delta, m, v


def _adamw(w, m, v, gs, *, name):
    R, C = w.shape
    n_g = len(gs)
    tr = _flat_tile(R, C, 7 + n_g)

    def body(*refs):
        w_ref, m_ref, v_ref = refs[:3]
        g_refs = refs[3:3 + n_g]
        go_ref, d_ref, mo_ref, vo_ref = refs[3 + n_g:]
        g = g_refs[0][...]
        for r in g_refs[1:]:
            g = g + r[...]
        d, mn, vn = _adamw_math(w_ref[...], g, m_ref[...], v_ref[...])
        go_ref[...] = g
        d_ref[...] = d
        mo_ref[...] = mn
        vo_ref[...] = vn

    tile = pl.BlockSpec((tr, C), lambda i: (i, 0))
    shp = jax.ShapeDtypeStruct((R, C), F32)
    return pl.pallas_call(body, name=name, grid=(R // tr,), in_specs=[tile] * (3 + n_g), out_specs=[tile] * 4,
                          out_shape=[shp] * 4, compiler_params=_params(1, 2 * (7 + n_g) * tr * C * 4))(w, m, v, *gs)


def _adamw_layer(w, m, v, layer, gs, outs, *, name):
    _, R, C = w.shape
    n_g = len(gs)
    tr = _flat_tile(R, C, 7 + n_g)

    def body(*refs):
        w_ref, m_ref, v_ref = refs[:3]
        g_refs = refs[3:3 + n_g]
        go_ref, d_ref, mo_ref, vo_ref = refs[3 + n_g + 4:]
        g = g_refs[0][...]
        for r in g_refs[1:]:
            g = g + r[...]
        d, mn, vn = _adamw_math(w_ref[...], g, m_ref[...], v_ref[...])
        go_ref[...] = g
        d_ref[...] = d
        mo_ref[...] = mn
        vo_ref[...] = vn

    lay = pl.BlockSpec((None, tr, C), lambda i: (layer, i, 0))
    tile = pl.BlockSpec((tr, C), lambda i: (i, 0))
    return pl.pallas_call(
        body, name=name, grid=(R // tr,), in_specs=[lay] * 3 + [tile] * n_g + [HBM] * 4, out_specs=[lay] * 4,
        out_shape=[jax.ShapeDtypeStruct(o.shape, o.dtype) for o in outs],
        input_output_aliases={3 + n_g + k: k for k in range(4)},
        compiler_params=_params(1, 2 * (7 + n_g) * tr * C * 4))(w, m, v, *gs, *outs)


def _full_shape(kind, slab_shape):
    G, r, c = slab_shape
    return (G, r, c * N_CHIP) if kind == "cols" else (G, r * N_CHIP, c)


def _slab_tile(G, r, c, n_arrays):
    cap = max(16, (20 * MIB) // (2 * n_arrays * G * c * 4))
    return _div(r, cap, 16)


def _slab_block(kind, G, r, c, tr):
    if kind == "cols":
        return pl.BlockSpec((G, tr, c), lambda i, chip: (0, i, chip[0]))
    per = r // tr
    return pl.BlockSpec((G, tr, c), lambda i, chip: (0, chip[0] * per + i, 0))


def _cast_into_full(chip, w, layer, kind, *, name):
    _, G, r, c = w.shape
    tr = _slab_tile(G, r, c, 2)

    def body(chip_ref, w_ref, o_ref):
        o_ref[...] = w_ref[...].astype(BF16)

    grid_spec = pltpu.PrefetchScalarGridSpec(
        num_scalar_prefetch=1, grid=(r // tr,),
        in_specs=[pl.BlockSpec((None, G, tr, c), lambda i, chip: (layer, 0, i, 0))], out_specs=_slab_block(kind, G, r, c, tr))
    return pl.pallas_call(body, name=name, grid_spec=grid_spec,
                          out_shape=jax.ShapeDtypeStruct(_full_shape(kind, (G, r, c)), BF16),
                          compiler_params=_params(1, 4 * G * tr * c * 4))(chip, w)


def _sum_contribs(chip, full, land, kind, *, name):
    _, G, r, c = land.shape
    tr = _slab_tile(G, r, c, 5)

    def body(chip_ref, f_ref, l_ref, o_ref):
        o_ref[...] = ((f_ref[...].astype(F32) + l_ref[0].astype(F32)) + l_ref[1].astype(F32)) + l_ref[2].astype(F32)

    grid_spec = pltpu.PrefetchScalarGridSpec(
        num_scalar_prefetch=1, grid=(r // tr,),
        in_specs=[_slab_block(kind, G, r, c, tr), pl.BlockSpec((3, G, tr, c), lambda i, chip: (0, 0, i, 0))],
        out_specs=pl.BlockSpec((G, tr, c), lambda i, chip: (0, i, 0)))
    return pl.pallas_call(body, name=name, grid_spec=grid_spec, out_shape=jax.ShapeDtypeStruct((G, r, c), F32),
                          compiler_params=_params(1, 2 * 5 * G * tr * c * 4))(chip, full, land)


def _place():
    x, y, c = lax.axis_index("x"), lax.axis_index("y"), lax.axis_index("c")
    chips = [(1 - x, y), (x, 1 - y), (1 - x, 1 - y)]
    return x, y, c, chips


def _small_exchange(v, reduce, *, name):
    m_per, n = v.shape
    assert m_per % 8 == 0 and n % 128 == 0

    def body(x_ref, out_ref, *scratch):
        if reduce:
            all_ref, send_sems, recv_sems, local_sem = scratch
        else:
            all_ref = out_ref
            send_sems, recv_sems, local_sem = scratch
        x, y, c, chips = _place()
        me, sibling = (x, y, c), (x, y, 1 - c)

        def rows(px, py, pc):
            return all_ref.at[pl.ds((4 * px + 2 * py + pc) * m_per, m_per), :]

        def copy(k, block, to, src=None):
            return pltpu.make_async_remote_copy(
                src_ref=rows(*block) if src is None else src, dst_ref=rows(*block), send_sem=send_sems.at[k],
                recv_sem=recv_sems.at[k], device_id=to, device_id_type=MESH)

        mine = pltpu.make_async_copy(x_ref, rows(*me), local_sem)
        mine.start()
        first = [copy(0, me, sibling, src=x_ref)]
        first += [copy(1 + j, me, (*chip, c), src=x_ref) for j, chip in enumerate(chips)]
        for cp in first:
            cp.start()
        passed = [copy(4 + j, (*chip, c), sibling) for j, chip in enumerate(chips)]
        for j, chip in enumerate(chips):
            copy(1 + j, (*chip, c), me).wait_recv()
            passed[j].start()
        copy(0, sibling, me).wait_recv()
        for j, chip in enumerate(chips):
            copy(4 + j, (*chip, 1 - c), me).wait_recv()
        for cp in first + passed:
            cp.wait_send()
        mine.wait()
        if reduce:
            acc = all_ref[0:m_per, :]
            for d in range(1, N_DEV):
                acc = acc + all_ref[d * m_per:(d + 1) * m_per, :]
            out_ref[...] = acc

    scratch = [pltpu.SemaphoreType.DMA((7,)), pltpu.SemaphoreType.DMA((7,)), pltpu.SemaphoreType.DMA]
    if reduce:
        scratch = [pltpu.VMEM((N_DEV * m_per, n), F32)] + scratch
    out_rows = m_per if reduce else N_DEV * m_per
    return pl.pallas_call(
        body, name=name, out_shape=jax.ShapeDtypeStruct((out_rows, n), v.dtype),
        in_specs=[pl.BlockSpec(memory_space=pltpu.VMEM)], out_specs=pl.BlockSpec(memory_space=pltpu.VMEM),
        scratch_shapes=scratch,
        compiler_params=pltpu.CompilerParams(vmem_limit_bytes=int(min(VMEM_CAP_BYTES, 4 * N_DEV * m_per * n * 4 + 16 * MIB))))(v)


def _slab(kind, ref, s):
    if kind == "cols":
        w = ref.shape[2] // N_CHIP
        return ref.at[:, :, pl.ds(s * w, w)]
    w = ref.shape[1] // N_CHIP
    return ref.at[:, pl.ds(s * w, w), :]


HBM = pl.BlockSpec(memory_space=pltpu.HBM)
SEM = pl.BlockSpec(memory_space=pltpu.SEMAPHORE)
EFFECT = pltpu.SideEffectType.DATAFLOW_SIDE_EFFECTING


def _in_hbm(v):
    return pltpu.with_memory_space_constraint(v, pltpu.HBM)


def _hbm_like(arrays):
    return [pltpu.HBM(v.shape, v.dtype) for v in arrays]


def _half_rows(ref, c):
    r = ref.shape[1] // 2
    return ref.at[:, pl.ds(c * r, r), :]


def _gather_copy(kinds, full, send_sems, recv_sems, a, j, peer, c, s_src, s_dst, halves):
    src, dst = _slab(kinds[a], full[a], s_src), _slab(kinds[a], full[a], s_dst)
    if halves:
        src, dst = _half_rows(src, c), _half_rows(dst, c)
    return pltpu.make_async_remote_copy(src_ref=src, dst_ref=dst, send_sem=send_sems.at[a * 3 + j],
                                        recv_sem=recv_sems.at[a * 3 + j], device_id=(*peer, c), device_id_type=MESH)


def _gather_start(fulls, kinds, after, *, name, halves=False):
    n = len(fulls)

    def body(*refs):
        k = n + len(after)
        full, send_sems, recv_sems, token = refs[:n], refs[k], refs[k + 1], refs[-1]
        x, y, c, chips = _place()
        s_me = 2 * x + y
        for a in range(n):
            for j, peer in enumerate(chips):
                _gather_copy(kinds, full, send_sems, recv_sems, a, j, peer, c, s_me, s_me, halves).start()
        token[...] = jnp.zeros_like(token)

    sems = pltpu.SemaphoreType.DMA((3 * n,))
    out = pl.pallas_call(
        body, name=name, out_shape=(sems, sems, *_hbm_like(fulls), jax.ShapeDtypeStruct((8, 128), F32)),
        in_specs=[HBM] * n + [ANY] * len(after), out_specs=(SEM, SEM, *[HBM] * n, pl.BlockSpec(memory_space=pltpu.VMEM)),
        input_output_aliases={a: 2 + a for a in range(n)},
        compiler_params=pltpu.CompilerParams(has_side_effects=EFFECT))(*[_in_hbm(f) for f in fulls], *after)
    return out[0], out[1], list(out[2:2 + n]), out[-1]


def _gather_wait(fulls, kinds, send, recv, after, *, name, halves=False):
    n = len(fulls)

    def body(*refs):
        full, send_sems, recv_sems = refs[:n], refs[n], refs[n + 1]
        x, y, c, chips = _place()
        s_me = 2 * x + y
        for a in range(n):
            for j, peer in enumerate(chips):
                cp = _gather_copy(kinds, full, send_sems, recv_sems, a, j, peer, c, s_me, 2 * peer[0] + peer[1], halves)
                cp.wait_send()
                cp.wait_recv()

    return pl.pallas_call(
        body, name=name, out_shape=_hbm_like(fulls), in_specs=[HBM] * n + [SEM, SEM] + [ANY] * len(after),
        out_specs=[HBM] * n, input_output_aliases={a: a for a in range(n)},
        compiler_params=pltpu.CompilerParams(has_side_effects=EFFECT))(*fulls, send, recv, *after)


def _sibling_fill(full, kind, *, name):
    def body(in_ref, out_ref, send_sems, recv_sems):
        x, y, c, chips = _place()

        def copy(j, peer, half):
            part = _half_rows(_slab(kind, out_ref, 2 * peer[0] + peer[1]), half)
            return pltpu.make_async_remote_copy(src_ref=part, dst_ref=part, send_sem=send_sems.at[j], recv_sem=recv_sems.at[j],
                                                device_id=(x, y, 1 - c), device_id_type=MESH)

        sent = [copy(j, peer, c) for j, peer in enumerate(chips)]
        for cp in sent:
            cp.start()
        for j, peer in enumerate(chips):
            copy(j, peer, 1 - c).wait_recv()
        for cp in sent:
            cp.wait_send()

    return pl.pallas_call(
        body, name=name, out_shape=jax.ShapeDtypeStruct(full.shape, full.dtype), in_specs=[ANY], out_specs=ANY,
        input_output_aliases={0: 0}, scratch_shapes=[pltpu.SemaphoreType.DMA((3,)), pltpu.SemaphoreType.DMA((3,))])(full)


def _scatter_copy(kinds, full, land, send_sems, recv_sems, a, j, peer, c):
    return pltpu.make_async_remote_copy(
        src_ref=_slab(kinds[a], full[a], 2 * peer[0] + peer[1]), dst_ref=land[a].at[j], send_sem=send_sems.at[a * 3 + j],
        recv_sem=recv_sems.at[a * 3 + j], device_id=(*peer, c), device_id_type=MESH)


def _scatter_start(fulls, lands, kinds, after, *, name):
    n = len(fulls)

    def body(*refs):
        k = 2 * n + len(after)
        full, land, send_sems, recv_sems, token = refs[:n], refs[n:2 * n], refs[k], refs[k + 1], refs[-1]
        _, _, c, chips = _place()
        for a in range(n):
            for j, peer in enumerate(chips):
                _scatter_copy(kinds, full, land, send_sems, recv_sems, a, j, peer, c).start()
        token[...] = jnp.zeros_like(token)

    sems = pltpu.SemaphoreType.DMA((3 * n,))
    out = pl.pallas_call(
        body, name=name,
        out_shape=(sems, sems, *_hbm_like(fulls), *_hbm_like(lands), jax.ShapeDtypeStruct((8, 128), F32)),
        in_specs=[HBM] * (2 * n) + [ANY] * len(after),
        out_specs=(SEM, SEM, *[HBM] * (2 * n), pl.BlockSpec(memory_space=pltpu.VMEM)),
        input_output_aliases={a: 2 + a for a in range(2 * n)},
        compiler_params=pltpu.CompilerParams(has_side_effects=EFFECT))(*[_in_hbm(f) for f in list(fulls) + list(lands)], *after)
    return out[0], out[1], list(out[2:2 + n]), list(out[2 + n:2 + 2 * n]), out[-1]


def _scatter_wait(fulls, lands, kinds, send, recv, after, *, name):
    n = len(fulls)

    def body(*refs):
        full, land, send_sems, recv_sems = refs[:n], refs[n:2 * n], refs[2 * n], refs[2 * n + 1]
        _, _, c, chips = _place()
        for a in range(n):
            for j, peer in enumerate(chips):
                cp = _scatter_copy(kinds, full, land, send_sems, recv_sems, a, j, peer, c)
                cp.wait_send()
                cp.wait_recv()

    out = pl.pallas_call(
        body, name=name, out_shape=_hbm_like(list(fulls) + list(lands)),
        in_specs=[HBM] * (2 * n) + [SEM, SEM] + [ANY] * len(after), out_specs=[HBM] * (2 * n),
        input_output_aliases={a: a for a in range(2 * n)},
        compiler_params=pltpu.CompilerParams(has_side_effects=EFFECT))(*fulls, *lands, send, recv, *after)
    return list(out[:n]), list(out[n:])


def _swap_copy(src, dst, send_sems, recv_sems, a):
    x, y, c, _ = _place()
    return pltpu.make_async_remote_copy(src_ref=src[a], dst_ref=dst[a], send_sem=send_sems.at[a], recv_sem=recv_sems.at[a],
                                        device_id=(x, y, 1 - c), device_id_type=MESH)


def _swap_start(parts, lands, *, name):
    n = len(parts)

    def body(*refs):
        src, dst, send_sems, recv_sems, token = refs[:n], refs[n:2 * n], refs[2 * n], refs[2 * n + 1], refs[-1]
        for a in range(n):
            _swap_copy(src, dst, send_sems, recv_sems, a).start()
        token[...] = jnp.zeros_like(token)

    sems = pltpu.SemaphoreType.DMA((n,))
    out = pl.pallas_call(
        body, name=name,
        out_shape=(sems, sems, *_hbm_like(parts), *_hbm_like(lands), jax.ShapeDtypeStruct((8, 128), F32)),
        in_specs=[HBM] * (2 * n), out_specs=(SEM, SEM, *[HBM] * (2 * n), pl.BlockSpec(memory_space=pltpu.VMEM)),
        input_output_aliases={a: 2 + a for a in range(2 * n)},
        compiler_params=pltpu.CompilerParams(has_side_effects=EFFECT))(*[_in_hbm(f) for f in list(parts) + list(lands)])
    return out[0], out[1], list(out[2:2 + n]), list(out[2 + n:2 + 2 * n]), out[-1]


def _swap_wait(parts, lands, send, recv, after, *, name):
    n = len(parts)

    def body(*refs):
        src, dst, send_sems, recv_sems = refs[:n], refs[n:2 * n], refs[2 * n], refs[2 * n + 1]
        for a in range(n):
            cp = _swap_copy(src, dst, send_sems, recv_sems, a)
            cp.wait_send()
            cp.wait_recv()

    out = pl.pallas_call(
        body, name=name, out_shape=_hbm_like(list(parts) + list(lands)),
        in_specs=[HBM] * (2 * n) + [SEM, SEM] + [ANY] * len(after), out_specs=[HBM] * (2 * n),
        input_output_aliases={a: a for a in range(2 * n)},
        compiler_params=pltpu.CompilerParams(has_side_effects=EFFECT))(*parts, *lands, send, recv, *after)
    return list(out[:n]), list(out[n:])


def _pad_rows(v, rows):
    return jnp.pad(v, ((0, rows - v.shape[0]), (0, 0)))


def _pack(vectors):
    flat = jnp.concatenate([v.reshape(-1) for v in vectors])
    n = -(-flat.shape[0] // 1024) * 1024
    return jnp.pad(flat, (0, n - flat.shape[0])).reshape(8, n // 8)


def _unpack(block, shapes):
    flat = block.reshape(-1)
    out, pos = [], 0
    for shp in shapes:
        size = 1
        for d in shp:
            size *= d
        out.append(flat[pos:pos + size].reshape(shp))
        pos += size
    return out


def kernel(x, c, norm_g, w_ada, b_ada, w_in, pool_w, pool_scale, attn_sink, conv_dw, conv_dw_b, conv_ln_g, conv_ln_b, conv_pw, w_branch_pool, w_branch_attn, w_branch_conv, w_out, final_g, loss_target, m_norm_g, m_w_ada, m_b_ada, m_w_in, m_pool_w, m_pool_scale, m_attn_sink, m_conv_dw, m_conv_dw_b, m_conv_ln_g, m_conv_ln_b, m_conv_pw, m_w_branch_pool, m_w_branch_attn, m_w_branch_conv, m_w_out, m_final_g, v_norm_g, v_w_ada, v_b_ada, v_w_in, v_pool_w, v_pool_scale, v_attn_sink, v_conv_dw, v_conv_dw_b, v_conv_ln_g, v_conv_ln_b, v_conv_pw, v_w_branch_pool, v_w_branch_attn, v_w_branch_conv, v_w_out, v_final_g):
    _, T, D = x.shape
    L = norm_g.shape[0]
    IN = w_in.shape[2] * N_CHIP
    assert IN == OFF_G + 3 * D and D % HALF == 0 and T % 512 == 0
    xi, yi, ci = lax.axis_index("x"), lax.axis_index("y"), lax.axis_index("c")
    chip = 2 * xi + yi
    dev = 2 * chip + ci
    x0 = x.reshape(T, D)
    target = loss_target.reshape(T, D)

    big = [("cols", w_in, m_w_in, v_w_in), ("cols", w_branch_pool, m_w_branch_pool, v_w_branch_pool),
           ("cols", w_branch_attn, m_w_branch_attn, v_w_branch_attn), ("cols", w_branch_conv, m_w_branch_conv, v_w_branch_conv),
           ("rows", w_out, m_w_out, v_w_out), ("rows", conv_pw, m_conv_pw, v_conv_pw), ("rows", pool_w, m_pool_w, v_pool_w)]
    kinds = [b[0] for b in big]
    n_big = len(big)
    as_groups = lambda t: t if t.ndim == 4 else t.reshape(L, 1, t.shape[1], t.shape[2])
    chip_arr = jnp.reshape(chip, (1,)).astype(jnp.int32)

    c_all = _small_exchange(_pad_rows(c, 8), False, name="gather_c")[0::8]
    taps_rows = -(-(L * CONV_KERNEL) // 8) * 8
    dw_blocks = _small_exchange(_pad_rows(conv_dw.reshape(L * CONV_KERNEL, -1), taps_rows), False, name="gather_taps")
    dw_blocks = dw_blocks.reshape(N_CHIP, 2, taps_rows, -1)[:, 0, :L * CONV_KERNEL]
    conv_dw_full = dw_blocks.reshape(N_CHIP, L, CONV_KERNEL, -1).transpose(1, 2, 0, 3).reshape(L, CONV_KERNEL, CONV_WIDTH)
    n_ada = w_ada.shape[2]
    b_slab = lax.dynamic_slice_in_dim(b_ada, chip * n_ada, n_ada, axis=1).reshape(L, 1, n_ada)
    mod_part = _ada_mod(c_all, w_ada, b_slab, name="ada_mod")
    mod_blocks = _small_exchange(mod_part.reshape(L * N_DEV, n_ada), False, name="gather_mod")
    mod_blocks = mod_blocks.reshape(N_CHIP, 2, L, N_DEV, n_ada)[:, 0]
    mod_all = mod_blocks.transpose(1, 2, 0, 3).reshape(L, N_DEV, 3 * D)
    mod = lax.dynamic_index_in_dim(mod_all, dev, axis=1, keepdims=False)
    shift, scale, gate = mod[:, :D], mod[:, D:2 * D], mod[:, 2 * D:]

    groups = [[0], list(range(1, n_big))]
    weights, gather_tokens = [], []
    for l in range(L):
        fulls = [_cast_into_full(chip_arr, as_groups(b[1]), l, b[0], name=f"cast{a}_{l}") for a, b in enumerate(big)]
        started_groups = []
        for gi, idx in enumerate(groups):
            send, recv, part, token = _gather_start([fulls[a] for a in idx], [kinds[a] for a in idx], [mod_all, conv_dw_full],
                                                    name=f"gather_start{l}_{gi}", halves=(l == 0 and gi == 0))
            started_groups.append((part, [kinds[a] for a in idx], send, recv))
            gather_tokens.append(token[0:1, 0:1])
        weights.append(started_groups)
    started = functools.reduce(lambda p, q: p + q, gather_tokens)

    row = lambda v: v.reshape(1, -1)

    xs, saved = [x0], []
    xl = x0
    full_w = []
    for l in range(L):
        h = _norm_mod(xl, row(norm_g[l]), row(scale[l]) + started if l == 0 else row(scale[l]), row(shift[l]), name=f"norm{l}")
        (part, part_kinds, send, recv), rest = weights[l]
        win_f, = _gather_wait(part, part_kinds, send, recv, [h], name=f"gather_wait{l}_0", halves=(l == 0))
        if l == 0:
            win_f = _sibling_fill(win_f, part_kinds[0], name="sibling_fill")
        proj = _mm(h, win_f, "nn", [BF16], name=f"proj{l}", b_layer=0)
        part, part_kinds, send, recv = rest
        wbp_f, wba_f, wbc_f, wout_f, cpw_f, poolw_f = _gather_wait(part, part_kinds, send, recv, [proj], name=f"gather_wait{l}_1")
        full_w.append((win_f, wbp_f, wba_f, wbc_f, wout_f, cpw_f, poolw_f))
        y_pool = _pool_fwd(proj, poolw_f, row(pool_scale[l]), name=f"pool{l}")
        o_attn, y_attn, lse = _attn_fwd(proj, attn_sink[l], name=f"attn{l}")
        s_conv, yc = _conv_fwd(proj, conv_dw_full[l], row(conv_dw_b[l]), row(conv_ln_g[l]), row(conv_ln_b[l]), name=f"conv{l}")
        cpre, y_conv = _mm(s_conv, cpw_f, "nn", [BF16, BF16], name=f"conv_pw{l}", b_layer=0, tn_cap=HALF,
                           extras=[(proj, "tile", OFF_CZ)], epilogue=lambda acc, z: (acc, acc * _silu(z.astype(F32))))
        merged, bp, ba, bc = _merge((y_pool, y_attn, y_conv), (wbp_f, wba_f, wbc_f), proj, D, name=f"merge{l}")
        x_new, o = _mm(merged, wout_f, "nn", [F32, BF16], name=f"out{l}", b_layer=0,
                       extras=[(xl, "tile", 0), (row(gate[l]), "row", 0)],
                       epilogue=lambda acc, xv, g: (xv + g * acc, acc))
        saved.append(dict(h=h, proj=proj, y_pool=y_pool, o_attn=o_attn, y_attn=y_attn, lse=lse, s_conv=s_conv, yc=yc, cpre=cpre,
                          y_conv=y_conv, merged=merged, bp=bp, ba=ba, bc=bc, o=o))
        xl = x_new
        xs.append(xl)

    loss_part, dx, d_final_g, dmo, d_gate = _final_loss(xl, target, row(final_g), (saved[L - 1]["o"], row(gate[L - 1])),
                                                        name="final_loss")
    loss = lax.psum(loss_part[0, 0], ("x", "y", "c"))

    small, dmods, scattering = [], [], {}
    scattered = jnp.zeros((1, 1), F32)
    for l in reversed(range(L)):
        sv = saved[l]
        proj = sv["proj"]
        win_f, wbp_f, wba_f, wbc_f, wout_f, cpw_f, poolw_f = full_w[l]
        dmerged = _mm(dmo, wout_f, "nt", [BF16], name=f"d_merged{l}", b_layer=0)
        g_wout = _mm(sv["merged"], dmo, "tn", [BF16], name=f"g_wout{l}")
        dbp, dba, dbc, dgp, dga, dgc = _merge_bwd(dmerged, (sv["bp"], sv["ba"], sv["bc"]), proj, D, name=f"merge_bwd{l}")
        dy_pool = _mm(dbp, wbp_f, "nt", [BF16], name=f"dy_pool{l}", b_layer=0)
        dy_attn = _mm(dba, wba_f, "nt", [BF16], name=f"dy_attn{l}", b_layer=0)
        dy_conv = _mm(dbc, wbc_f, "nt", [BF16], name=f"dy_conv{l}", b_layer=0)
        g_wbp = _mm(sv["y_pool"], dbp, "tn", [BF16], name=f"g_wbp{l}")
        g_wba = _mm(sv["y_attn"], dba, "tn", [BF16], name=f"g_wba{l}")
        g_wbc = _mm(sv["y_conv"], dbc, "tn", [BF16], name=f"g_wbc{l}")
        dz_pool, dmn, d_pool_scale, g_poolw = _pool_bwd(proj, dy_pool, poolw_f, row(pool_scale[l]) + scattered, name=f"pool_bwd{l}")
        du_pool = _pool_bwd_window(dmn, name=f"pool_bwd_window{l}")
        dq, dk, dv, dz_attn, d_sink = _attn_bwd(proj, attn_sink[l], sv["o_attn"], sv["lse"], dy_attn, name=f"attn_bwd{l}")
        dcpre, dz_conv = _conv_out_bwd(dy_conv, sv["cpre"], proj, name=f"conv_out_bwd{l}")
        ds_conv = _mm(dcpre, cpw_f, "nt", [BF16], name=f"ds_conv{l}", b_layer=0)
        g_cpw = _mm(sv["s_conv"], dcpre, "tn", [BF16], name=f"g_cpw{l}")
        taps = conv_dw_full[l]
        dyc, d_taps, d_dwb, d_lng, d_lnb = _conv_bwd(proj, sv["yc"], ds_conv, row(conv_ln_g[l]), row(conv_ln_b[l]),
                                                    name=f"conv_bwd{l}")
        da_conv, db_conv = _conv_bwd_input(proj, dyc, taps, name=f"conv_bwd_input{l}")
        dproj = jnp.concatenate([du_pool, dz_pool, dq, dk, dv, dz_attn, da_conv, db_conv, dz_conv, dgp, dga, dgc], axis=1)
        dh = _mm(dproj, win_f, "nt", [BF16], name=f"dh{l}", b_layer=0, tk_cap=1536)
        g_win = _mm(sv["h"], dproj, "tn", [BF16], name=f"g_win{l}", tn_cap=768)
        d_gate_here = d_gate
        if l > 0:
            dx, d_ng, d_scale, d_shift, dmo, d_gate = _norm_mod_bwd(
                xs[l], dh, dx, row(norm_g[l]), row(scale[l]), (saved[l - 1]["o"], row(gate[l - 1])), name=f"norm_bwd{l}")
        else:
            dx, d_ng, d_scale, d_shift = _norm_mod_bwd(xs[l], dh, dx, row(norm_g[l]), row(scale[l]), name=f"norm_bwd{l}")
        dmods.append(jnp.concatenate([d_shift, d_scale, d_gate_here], axis=1))
        small.append([d_ng, d_pool_scale, d_sink[:, :N_Q_HEADS], d_taps, d_dwb, d_lng, d_lnb])
        before_start = []
        if l == 0:
            stacked = [jnp.stack([small[L - 1 - k][q] for k in range(L)]) for q in range(len(small[0]))]
            dmod_mine = jnp.concatenate(dmods[::-1], axis=0)
            small_shapes = [s.shape for s in stacked] + [d_final_g.shape, dmod_mine.shape]
            reduced = _small_exchange(_pack(stacked + [d_final_g, dmod_mine]), True, name="reduce_small")
            dmod_all = _small_exchange(_pad_rows(dmod_mine, 8), False, name="gather_dmod").reshape(N_DEV, 8, 3 * D)[:, :L]
            before_start = [reduced, dmod_all]
        grads = [g[None] for g in (g_win, g_wbp, g_wba, g_wbc, g_wout, g_cpw)] + [g_poolw.astype(BF16)]
        lands = [lax.empty((3,) + as_groups(b[1]).shape[1:], BF16) for b in big]
        send, recv, grads, lands, token = _scatter_start(grads, lands, kinds, before_start, name=f"scatter_start{l}")
        scattering[l] = (grads, lands, send, recv)
        scattered = token[0:1, 0:1]
    grad_x = dx.reshape(1, T, D)

    r_ng, r_ps, r_sink, r_taps, r_dwb, r_lng, r_lnb, r_fg, r_bada = _unpack(reduced, small_shapes)
    g_norm_g, g_pool_scale, g_attn_sink = r_ng.reshape(L, D), r_ps.reshape(L, POOL_WIDTH), r_sink.reshape(L, N_Q_HEADS)
    g_conv_dw = lax.dynamic_slice_in_dim(r_taps, chip * (CONV_WIDTH // N_CHIP), CONV_WIDTH // N_CHIP, axis=2)
    g_dwb, g_lng, g_lnb = r_dwb.reshape(L, CONV_WIDTH), r_lng.reshape(L, CONV_WIDTH), r_lnb.reshape(L, CONV_WIDTH)
    g_final_g, g_b_ada = r_fg.reshape(D), r_bada

    dmod_slab = lax.dynamic_slice_in_dim(dmod_all, chip * n_ada, n_ada, axis=2).transpose(1, 0, 2)
    g_w_ada = _ada_grad(c_all.T, dmod_slab + scattered, name="ada_grad")

    flat = lambda t: t.reshape(-1, t.shape[-1])
    ada = [t.reshape(w_ada.shape) for t in _adamw(flat(w_ada), flat(m_w_ada), flat(v_w_ada), [flat(g_w_ada)], name="adamw_ada")]
    small_w = [norm_g, b_ada, pool_scale, attn_sink, conv_dw, conv_dw_b, conv_ln_g, conv_ln_b, final_g]
    small_m = [m_norm_g, m_b_ada, m_pool_scale, m_attn_sink, m_conv_dw, m_conv_dw_b, m_conv_ln_g, m_conv_ln_b, m_final_g]
    small_v = [v_norm_g, v_b_ada, v_pool_scale, v_attn_sink, v_conv_dw, v_conv_dw_b, v_conv_ln_g, v_conv_ln_b, v_final_g]
    small_g = [g_norm_g, g_b_ada, g_pool_scale, g_attn_sink, g_conv_dw, g_dwb, g_lng, g_lnb, g_final_g]
    sm = _adamw(_pack(small_w), _pack(small_m), _pack(small_v), [_pack(small_g)], name="adamw_small")
    shp = [t.shape for t in small_w]
    sm_g, sm_d, sm_m, sm_v = [_unpack(t, shp) for t in sm]

    stacked3 = lambda t: t.reshape(L, -1, t.shape[-1])
    two = lambda t: t.reshape(-1, t.shape[-1])
    outs = [[lax.empty(stacked3(b[1]).shape, F32) for _ in range(4)] for b in big]

    def update(l, swapping, after):
        parts, lands, send, recv = swapping
        parts, others = _swap_wait(parts, lands, send, recv, after, name=f"swap_wait{l}")
        for a, (_, w, m, v) in enumerate(big):
            outs[a] = _adamw_layer(stacked3(w), stacked3(m), stacked3(v), l, [two(parts[a]), two(others[a])], outs[a],
                                   name=f"adamw{a}_{l}")

    after = [ada[0], sm[0]]
    swapping = None
    for l in reversed(range(L)):
        if l == 0 and swapping is not None:
            update(1, swapping, after)
            after, swapping = [outs[a][0] for a in range(n_big)], None
        grads, lands, send, recv = scattering[l]
        grads, lands = _scatter_wait(grads, lands, kinds, send, recv, after, name=f"scatter_wait{l}")
        parts = [_sum_contribs(chip_arr, grads[a], lands[a], kinds[a], name=f"sum_grads{a}_{l}") for a in range(n_big)]
        send, recv, parts, lands, token = _swap_start(parts, [lax.empty(p.shape, F32) for p in parts], name=f"swap_start{l}")
        if swapping is not None:
            update(l + 1, swapping, [token])
            after = [outs[a][0] for a in range(n_big)]
        swapping = (parts, lands, send, recv)
    update(0, swapping, [outs[a][0] for a in range(n_big)] if L > 1 else [ada[0]])
    results = {a: [t.reshape(big[a][1].shape) for t in outs[a]] for a in range(n_big)}

    def leaves(k, pick):
        s = pick
        return [s[0], ada[k], s[1], results[0][k], results[6][k], s[2], s[3], s[4], s[5], s[6], s[7], results[5][k],
                results[1][k], results[2][k], results[3][k], results[4][k], s[8]]

    return (loss, grad_x, *leaves(0, sm_g), *leaves(1, sm_d), *leaves(2, sm_m), *leaves(3, sm_v))
```

```python
import functools

import jax
import jax.numpy as jnp
from jax import lax
from jax.experimental import pallas as pl
from jax.experimental.pallas import tpu as pltpu

F32 = jnp.float32
BF16 = jnp.bfloat16
MESH = pl.DeviceIdType.MESH
ANY = pl.BlockSpec(memory_space=pl.ANY)

CHUNK = 64
HEAD_DIM = 64
N_Q_HEADS = 16
N_KV_HEADS = 4
Q_PER_KV = N_Q_HEADS // N_KV_HEADS
WINDOW_CHUNKS = 2
POOL_WIDTH = 1024
POOL_WINDOWS = (2, 4, 8, 16)
POOL_GROUP = 256
ATTN_WIDTH = 1024
KV_WIDTH = 256
CONV_WIDTH = 1024
CONV_KERNEL = 31
EPS = 1e-6
OFF_U, OFF_Z, OFF_Q, OFF_K, OFF_V, OFF_AZ, OFF_CA, OFF_CB, OFF_CZ, OFF_G = (
    0, 1024, 2048, 3072, 3328, 3584, 4608, 5632, 6656, 7680)
HALF = 512
POOL_HALO = 16
CONV_HALO = 32
ATTN_Q_BLOCK = 256
ATTN_HALO = WINDOW_CHUNKS * CHUNK
NEG_INF = -1e30

ADAM_LR, ADAM_B1, ADAM_B2, ADAM_EPS, ADAM_WD, ADAM_STEP = 0.001, 0.9, 0.999, 1e-08, 0.01, 10

N_DEV = 8
N_CHIP = 4
VMEM_CAP_BYTES = 56 * 2**20
MIB = 2**20


def _div(n, cap, mult=128):
    if n <= cap:
        return n
    best = None
    for t in range(mult, cap + 1, mult):
        if n % t == 0:
            best = t
    assert best is not None, (n, cap, mult)
    return best


def _params(n_grid, vmem_bytes=None):
    kw = dict(dimension_semantics=("arbitrary",) * n_grid)
    if vmem_bytes is not None:
        kw["vmem_limit_bytes"] = int(min(max(vmem_bytes * 5 // 4 + 4 * MIB, 32 * MIB), VMEM_CAP_BYTES))
    return pltpu.CompilerParams(**kw)


def _silu(z):
    return z * jax.nn.sigmoid(z)


def _dsilu(z):
    s = jax.nn.sigmoid(z)
    return s * (1.0 + z * (1.0 - s))


def _nbytes(shape, dtype):
    n = 1
    for d in shape:
        n *= d
    return n * jnp.dtype(dtype).itemsize


MM_VMEM_BUDGET = 50 * MIB


def _mm(a, b, mode, out_dtypes, *, name, b_layer=None, extras=(), epilogue=None, tm_cap=2048, tn_cap=1024, tk_cap=None):
    if mode == "tn":
        K, M = a.shape
        N = b.shape[-1]
    elif mode == "nt":
        M, K = a.shape
        N = b.shape[-2]
    else:
        M, K = a.shape
        N = b.shape[-1]
    tk = _div(K, tk_cap or 2048)
    nk = K // tk
    n_out = len(out_dtypes)
    n_ex = len(extras)
    stacked = b.ndim == 3
    per_elem = sum(jnp.dtype(dt).itemsize for dt in out_dtypes) + sum(e[0].dtype.itemsize for e in extras if e[1] == "tile")

    def need(tm, tn):
        return (2 * (tm * tk * a.dtype.itemsize + tk * tn * b.dtype.itemsize + tm * tn * per_elem)
                + tm * tn * 4 * (2 if nk > 1 else 1))

    tm, tn = min(((_div(M, mc), _div(N, nc)) for mc in (tm_cap, tm_cap // 2, tm_cap // 4) for nc in (tn_cap, tn_cap // 2)),
                 key=lambda t: (need(*t) > MM_VMEM_BUDGET, -t[0], -t[1]))

    def body(*refs):
        a_ref, b_ref = refs[0], refs[1]
        ex_refs = refs[2:2 + n_ex]
        pos = 2 + n_ex
        out_refs = refs[pos:pos + n_out]
        acc_ref = refs[pos + n_out] if nk > 1 else None
        k = pl.program_id(2)
        dims = {"nn": (((1,), (0,)), ((), ())), "nt": (((1,), (1,)), ((), ())), "tn": (((0,), (0,)), ((), ()))}[mode]

        def product():
            return lax.dot_general(a_ref[...].astype(BF16), b_ref[...].astype(BF16), dims, preferred_element_type=F32)

        def finish(acc):
            vals = epilogue(acc, *[r[...] for r in ex_refs]) if epilogue is not None else (acc,)
            for r, v in zip(out_refs, vals):
                r[...] = v.astype(r.dtype)

        if nk == 1:
            finish(product())
        else:
            @pl.when(k == 0)
            def _():
                acc_ref[...] = jnp.zeros_like(acc_ref)

            acc_ref[...] += product()

            @pl.when(k == nk - 1)
            def _():
                finish(acc_ref[...])

    if mode == "tn":
        a_spec = pl.BlockSpec((tk, tm), lambda i, j, k: (k, i))
    else:
        a_spec = pl.BlockSpec((tm, tk), lambda i, j, k: (i, k))
    if mode == "nt":
        b_blk, b_idx = (tn, tk), (lambda i, j, k: (j, k))
    else:
        b_blk, b_idx = (tk, tn), (lambda i, j, k: (k, j))
    if stacked:
        b_spec = pl.BlockSpec((None,) + b_blk, lambda i, j, k, f=b_idx: (b_layer,) + f(i, j, k))
    else:
        b_spec = pl.BlockSpec(b_blk, b_idx)
    in_specs = [a_spec, b_spec]
    operands = [a, b]
    vmem = 2 * (tm * tk * a.dtype.itemsize + tk * tn * b.dtype.itemsize) + tm * tn * 4 * 3
    for arr, kind, off in extras:
        if kind == "tile":
            assert off % tn == 0, (name, off, tn)
            in_specs.append(pl.BlockSpec((tm, tn), lambda i, j, k, o=off // tn: (i, o + j)))
        else:
            in_specs.append(pl.BlockSpec((1, tn), lambda i, j, k: (0, j)))
        operands.append(arr)
        vmem += 2 * tm * tn * arr.dtype.itemsize
    out_shape = [jax.ShapeDtypeStruct((M, N), dt) for dt in out_dtypes]
    out_specs = [pl.BlockSpec((tm, tn), lambda i, j, k: (i, j)) for _ in out_dtypes]
    vmem += sum(2 * tm * tn * jnp.dtype(dt).itemsize for dt in out_dtypes)
    outs = pl.pallas_call(
        body, name=name, grid=(M // tm, N // tn, nk), in_specs=in_specs, out_specs=out_specs, out_shape=out_shape,
        scratch_shapes=[pltpu.VMEM((tm, tn), F32)] if nk > 1 else [], compiler_params=_params(3, vmem))(*operands)
    return outs[0] if n_out == 1 else outs


def _merge(ys, wbs, proj, D, *, name):
    T = ys[0].shape[0]
    tm = _div(T, 1024)
    tn = HALF
    kw = ys[0].shape[1]
    g_off = [(OFF_G + b * D) // tn for b in range(3)]

    def body(y0, y1, y2, w0, w1, w2, g0, g1, g2, merged_ref, b0, b1, b2):
        acc = None
        for y, w, g, bo in ((y0, w0, g0, b0), (y1, w1, g1, b1), (y2, w2, g2, b2)):
            p = jnp.dot(y[...], w[...], preferred_element_type=F32)
            bo[...] = p.astype(bo.dtype)
            t = jax.nn.sigmoid(g[...].astype(F32)) * p
            acc = t if acc is None else acc + t
        merged_ref[...] = acc.astype(merged_ref.dtype)

    y_spec = pl.BlockSpec((tm, kw), lambda i, j: (i, 0))
    w_spec = pl.BlockSpec((None, kw, tn), lambda i, j: (0, 0, j))
    g_specs = [pl.BlockSpec((tm, tn), lambda i, j, o=o: (i, o + j)) for o in g_off]
    o_spec = pl.BlockSpec((tm, tn), lambda i, j: (i, j))
    vmem = 2 * (3 * tm * kw * 2 + 3 * kw * tn * 2 + 3 * tm * tn * proj.dtype.itemsize + 4 * tm * tn * 2) + 4 * tm * tn * 4
    return pl.pallas_call(
        body, name=name, grid=(T // tm, D // tn), in_specs=[y_spec] * 3 + [w_spec] * 3 + g_specs,
        out_specs=[o_spec] * 4, out_shape=[jax.ShapeDtypeStruct((T, D), BF16)] * 4,
        compiler_params=_params(2, vmem))(*ys, *wbs, proj, proj, proj)


def _row_tile(T, width, n_arrays):
    cap = max(8, (24 * MIB) // (2 * n_arrays * width * 4))
    return _div(T, min(cap, 1024), 8)


def _norm_mod(x, ng, scale, shift, *, name):
    T, D = x.shape
    tm = _row_tile(T, D, 3)

    def body(x_ref, ng_ref, sc_ref, sh_ref, h_ref):
        xv = x_ref[...]
        r = lax.rsqrt(jnp.mean(xv * xv, axis=-1, keepdims=True) + EPS)
        h = (xv * r) * ng_ref[...] * (1.0 + sc_ref[...]) + sh_ref[...]
        h_ref[...] = h.astype(h_ref.dtype)

    row = pl.BlockSpec((1, D), lambda i: (0, 0))
    tile = pl.BlockSpec((tm, D), lambda i: (i, 0))
    return pl.pallas_call(body, name=name, grid=(T // tm,), in_specs=[tile, row, row, row], out_specs=tile,
                          out_shape=jax.ShapeDtypeStruct((T, D), BF16), compiler_params=_params(1))(x, ng, scale, shift)


def _gate_out_bwd(dx, o_ref, gate_ref, dmo_ref, dgate_ref, first):
    dmo_ref[...] = (dx * gate_ref[...]).astype(dmo_ref.dtype)
    p = jnp.sum(dx * o_ref[...].astype(F32), axis=0, keepdims=True)

    @pl.when(first)
    def _():
        dgate_ref[...] = p

    @pl.when(jnp.logical_not(first))
    def _():
        dgate_ref[...] += p


def _norm_mod_bwd(x, dh, dx_out, ng, scale, below=None, *, name):
    T, D = x.shape
    tm = _row_tile(T, D, 8)

    def body(*refs):
        if below is None:
            x_ref, dh_ref, dxo_ref, ng_ref, sc_ref, dx_ref, dng_ref, dsc_ref, dsh_ref = refs
        else:
            x_ref, dh_ref, dxo_ref, ng_ref, sc_ref, o_ref, gate_ref, dx_ref, dng_ref, dsc_ref, dsh_ref, dmo_ref, dgate_ref = refs
            below_refs = (o_ref, gate_ref, dmo_ref, dgate_ref)
        i = pl.program_id(0)
        xv = x_ref[...]
        dh_v = dh_ref[...].astype(F32)
        r = lax.rsqrt(jnp.mean(xv * xv, axis=-1, keepdims=True) + EPS)
        xn = xv * r
        one_sc = 1.0 + sc_ref[...]
        dxn = dh_v * (ng_ref[...] * one_sc)
        dx = dxo_ref[...] + r * (dxn - xn * jnp.mean(dxn * xn, axis=-1, keepdims=True))
        dx_ref[...] = dx
        if below is not None:
            _gate_out_bwd(dx, *below_refs, i == 0)
        t = dh_v * xn
        parts = (jnp.sum(t * one_sc, axis=0, keepdims=True), jnp.sum(t * ng_ref[...], axis=0, keepdims=True),
                 jnp.sum(dh_v, axis=0, keepdims=True))
        for ref, p in zip((dng_ref, dsc_ref, dsh_ref), parts):
            @pl.when(i == 0)
            def _(ref=ref, p=p):
                ref[...] = p

            @pl.when(i > 0)
            def _(ref=ref, p=p):
                ref[...] += p

    row = pl.BlockSpec((1, D), lambda i: (0, 0))
    tile = pl.BlockSpec((tm, D), lambda i: (i, 0))
    vec = jax.ShapeDtypeStruct((1, D), F32)
    in_specs, out_specs = [tile, tile, tile, row, row], [tile, row, row, row]
    out_shape = [jax.ShapeDtypeStruct((T, D), F32), vec, vec, vec]
    if below is not None:
        in_specs, out_specs = in_specs + [tile, row], out_specs + [tile, row]
        out_shape = out_shape + [jax.ShapeDtypeStruct((T, D), BF16), vec]
    return pl.pallas_call(body, name=name, grid=(T // tm,), in_specs=in_specs, out_specs=out_specs, out_shape=out_shape,
                          compiler_params=_params(1))(x, dh, dx_out, ng, scale, *(below or ()))


def _final_loss(x, target, fg, below, *, name):
    T, D = x.shape
    tm = _row_tile(T, D, 6)

    def body(x_ref, t_ref, g_ref, o_ref, gate_ref, loss_ref, dx_ref, dg_ref, dmo_ref, dgate_ref):
        i = pl.program_id(0)
        xv = x_ref[...]
        r = lax.rsqrt(jnp.mean(xv * xv, axis=-1, keepdims=True) + EPS)
        xn = xv * r
        err = xn * g_ref[...] - t_ref[...]
        part = 0.5 * jnp.sum(jnp.sum(err * err, axis=1, keepdims=True), axis=0, keepdims=True) / D
        dy = err / D
        dxn = dy * g_ref[...]
        dx = r * (dxn - xn * jnp.mean(dxn * xn, axis=-1, keepdims=True))
        dx_ref[...] = dx
        _gate_out_bwd(dx, o_ref, gate_ref, dmo_ref, dgate_ref, i == 0)
        dg = jnp.sum(dy * xn, axis=0, keepdims=True)

        @pl.when(i == 0)
        def _():
            loss_ref[...] = part
            dg_ref[...] = dg

        @pl.when(i > 0)
        def _():
            loss_ref[...] += part
            dg_ref[...] += dg

    row = pl.BlockSpec((1, D), lambda i: (0, 0))
    tile = pl.BlockSpec((tm, D), lambda i: (i, 0))
    one = pl.BlockSpec((1, 1), lambda i: (0, 0))
    vec = jax.ShapeDtypeStruct((1, D), F32)
    return pl.pallas_call(
        body, name=name, grid=(T // tm,), in_specs=[tile, tile, row, tile, row], out_specs=[one, tile, row, tile, row],
        out_shape=[jax.ShapeDtypeStruct((1, 1), F32), jax.ShapeDtypeStruct((T, D), F32), vec, jax.ShapeDtypeStruct((T, D), BF16), vec],
        compiler_params=_params(1))(x, target, fg, *below)


def _merge_bwd(dmerged, branches, proj, D, *, name):
    T = dmerged.shape[0]
    tm = _div(T, 1024)
    tn = HALF
    g_off = [(OFF_G + b * D) // tn for b in range(3)]

    def body(dm_ref, b0, b1, b2, g0, g1, g2, db0, db1, db2, dg0, dg1, dg2):
        dm = dm_ref[...].astype(F32)
        for b, g, db, dg in ((b0, g0, db0, dg0), (b1, g1, db1, dg1), (b2, g2, db2, dg2)):
            s = jax.nn.sigmoid(g[...].astype(F32))
            db[...] = (dm * s).astype(db.dtype)
            dg[...] = (dm * b[...].astype(F32) * s * (1.0 - s)).astype(dg.dtype)

    tile = pl.BlockSpec((tm, tn), lambda i, j: (i, j))
    g_specs = [pl.BlockSpec((tm, tn), lambda i, j, o=o: (i, o + j)) for o in g_off]
    return pl.pallas_call(body, name=name, grid=(T // tm, D // tn), in_specs=[tile] * 4 + g_specs, out_specs=[tile] * 6,
                          out_shape=[jax.ShapeDtypeStruct((T, D), BF16)] * 6,
                          compiler_params=_params(2))(dmerged, *branches, proj, proj, proj)


def _conv_out_bwd(dy, cpre, proj, *, name):
    T = dy.shape[0]
    tm = _div(T, 1024)
    tn = HALF

    def body(dy_ref, c_ref, z_ref, dc_ref, dz_ref):
        dyv = dy_ref[...].astype(F32)
        z = z_ref[...].astype(F32)
        dc_ref[...] = (dyv * _silu(z)).astype(dc_ref.dtype)
        dz_ref[...] = (dyv * c_ref[...].astype(F32) * _dsilu(z)).astype(dz_ref.dtype)

    tile = pl.BlockSpec((tm, tn), lambda i, j: (i, j))
    z_spec = pl.BlockSpec((tm, tn), lambda i, j: (i, OFF_CZ // tn + j))
    return pl.pallas_call(body, name=name, grid=(T // tm, CONV_WIDTH // tn), in_specs=[tile, tile, z_spec],
                          out_specs=[tile, tile], out_shape=[jax.ShapeDtypeStruct((T, CONV_WIDTH), BF16)] * 2,
                          compiler_params=_params(2))(dy, cpre, proj)


def _pool_mixed(ext, u, g, row0):
    w = POOL_WINDOWS[g]
    s = ext
    shift = 1
    while shift < w:
        s = s + pltpu.roll(s, shift, 0)
        shift *= 2
    tm = u.shape[0]
    t = row0 + lax.broadcasted_iota(jnp.int32, (tm, 1), 0)
    inv = 1.0 / jnp.minimum(t + 1, w).astype(F32)
    return s[POOL_HALO:, :] * inv - u, inv


def _pool_specs(T, tm):
    per = tm // POOL_HALO
    cur = lambda col: pl.BlockSpec((tm, POOL_WIDTH), lambda i, c=col: (i, c))
    prev = pl.BlockSpec((POOL_HALO, POOL_WIDTH), lambda i: (jnp.maximum(i * per - 1, 0), 0))
    return cur, prev


def _pool_fwd(proj, pool_w, scale, *, name):
    T = proj.shape[0]
    tm = _div(T, 512, 16)
    cur, prev = _pool_specs(T, tm)

    def body(u_ref, up_ref, z_ref, w_ref, sc_ref, y_ref):
        i = pl.program_id(0)
        u = u_ref[...].astype(F32)
        halo = jnp.where(i == 0, 0.0, up_ref[...].astype(F32))
        ext = jnp.concatenate([halo, u], axis=0)
        for g in range(len(POOL_WINDOWS)):
            cols = slice(g * POOL_GROUP, (g + 1) * POOL_GROUP)
            mixed, _ = _pool_mixed(ext[:, cols], u[:, cols], g, i * tm)
            p = jnp.dot(mixed.astype(BF16), w_ref[g], preferred_element_type=F32)
            y = p * sc_ref[:, cols] * _silu(z_ref[:, cols].astype(F32))
            y_ref[:, cols] = y.astype(y_ref.dtype)

    w_spec = pl.BlockSpec((len(POOL_WINDOWS), POOL_GROUP, POOL_GROUP), lambda i: (0, 0, 0))
    row = pl.BlockSpec((1, POOL_WIDTH), lambda i: (0, 0))
    return pl.pallas_call(body, name=name, grid=(T // tm,), in_specs=[cur(0), prev, cur(1), w_spec, row],
                          out_specs=pl.BlockSpec((tm, POOL_WIDTH), lambda i: (i, 0)),
                          out_shape=jax.ShapeDtypeStruct((T, POOL_WIDTH), BF16),
                          compiler_params=_params(1))(proj, proj, proj, pool_w, scale)


def _pool_bwd(proj, dy, pool_w, scale, *, name):
    T = proj.shape[0]
    tm = _div(T, 512, 16)
    cur, prev = _pool_specs(T, tm)
    n_g = len(POOL_WINDOWS)

    def body(u_ref, up_ref, z_ref, dy_ref, w_ref, sc_ref, dz_ref, dmn_ref, dsc_ref, dw_ref):
        i = pl.program_id(0)
        u = u_ref[...].astype(F32)
        halo = jnp.where(i == 0, 0.0, up_ref[...].astype(F32))
        ext = jnp.concatenate([halo, u], axis=0)
        for g in range(n_g):
            cols = slice(g * POOL_GROUP, (g + 1) * POOL_GROUP)
            mixed, inv = _pool_mixed(ext[:, cols], u[:, cols], g, i * tm)
            mixed = mixed.astype(BF16)
            w = w_ref[g]
            p = jnp.dot(mixed, w, preferred_element_type=F32)
            z = z_ref[:, cols].astype(F32)
            dyv = dy_ref[:, cols].astype(F32)
            sc = sc_ref[:, cols]
            dypre = dyv * _silu(z)
            dz_ref[:, cols] = (dyv * (p * sc) * _dsilu(z)).astype(dz_ref.dtype)
            dsc = jnp.sum(dypre * p, axis=0, keepdims=True)
            dp = (dypre * sc).astype(BF16)
            dwg = lax.dot_general(mixed, dp, (((0,), (0,)), ((), ())), preferred_element_type=F32)
            dmixed = lax.dot_general(dp, w, (((1,), (1,)), ((), ())), preferred_element_type=F32)
            dmn_ref[:, cols] = dmixed * inv

            @pl.when(i == 0)
            def _(g=g, cols=cols, dsc=dsc, dwg=dwg):
                dsc_ref[:, cols] = dsc
                dw_ref[g] = dwg

            @pl.when(i > 0)
            def _(g=g, cols=cols, dsc=dsc, dwg=dwg):
                dsc_ref[:, cols] += dsc
                dw_ref[g] += dwg

    w_spec = pl.BlockSpec((n_g, POOL_GROUP, POOL_GROUP), lambda i: (0, 0, 0))
    row = pl.BlockSpec((1, POOL_WIDTH), lambda i: (0, 0))
    tile = pl.BlockSpec((tm, POOL_WIDTH), lambda i: (i, 0))
    dw_spec = pl.BlockSpec((n_g, POOL_GROUP, POOL_GROUP), lambda i: (0, 0, 0))
    return pl.pallas_call(
        body, name=name, grid=(T // tm,), in_specs=[cur(0), prev, cur(1), tile, w_spec, row],
        out_specs=[tile, tile, row, dw_spec],
        out_shape=[jax.ShapeDtypeStruct((T, POOL_WIDTH), BF16), jax.ShapeDtypeStruct((T, POOL_WIDTH), F32),
                   jax.ShapeDtypeStruct((1, POOL_WIDTH), F32), jax.ShapeDtypeStruct((n_g, POOL_GROUP, POOL_GROUP), F32)],
        compiler_params=_params(1))(proj, proj, proj, dy, pool_w, scale)


def _pool_bwd_window(dmn, *, name):
    T = dmn.shape[0]
    tm = _div(T, 512, 16)
    per = tm // POOL_HALO
    last = T // POOL_HALO - 1
    nb = T // tm

    def body(c_ref, n_ref, du_ref):
        i = pl.program_id(0)
        cur = c_ref[...]
        nxt = jnp.where(i == nb - 1, 0.0, n_ref[...])
        ext = jnp.concatenate([cur, nxt], axis=0)
        rows = tm + POOL_HALO
        t = i * tm + lax.broadcasted_iota(jnp.int32, (tm, 1), 0)
        for g, w in enumerate(POOL_WINDOWS):
            cols = slice(g * POOL_GROUP, (g + 1) * POOL_GROUP)
            s = ext[:, cols]
            shift = 1
            while shift < w:
                s = s + pltpu.roll(s, rows - shift, 0)
                shift *= 2
            cnt = jnp.minimum(t + 1, w).astype(F32)
            du_ref[:, cols] = (s[:tm, :] - cur[:, cols] * cnt).astype(du_ref.dtype)

    tile = pl.BlockSpec((tm, POOL_WIDTH), lambda i: (i, 0))
    nxt = pl.BlockSpec((POOL_HALO, POOL_WIDTH), lambda i: (jnp.minimum((i + 1) * per, last), 0))
    return pl.pallas_call(body, name=name, grid=(nb,), in_specs=[tile, nxt], out_specs=tile,
                          out_shape=jax.ShapeDtypeStruct((T, POOL_WIDTH), BF16), compiler_params=_params(1))(dmn, dmn)


def _attn_mask(i):
    qb, keys = ATTN_Q_BLOCK, ATTN_Q_BLOCK + ATTN_HALO
    qi = lax.broadcasted_iota(jnp.int32, (qb, keys), 0) // CHUNK
    kj = lax.broadcasted_iota(jnp.int32, (qb, keys), 1) // CHUNK - WINDOW_CHUNKS
    return (kj <= qi) & (kj >= qi - WINDOW_CHUNKS) & (kj + i * (qb // CHUNK) >= 0)


def _attn_specs(T, order):
    qb = ATTN_Q_BLOCK
    per = qb // ATTN_HALO
    cur = lambda width, col: pl.BlockSpec((qb, width), lambda i, c=col: (order(i), c))
    prev = lambda col: pl.BlockSpec((ATTN_HALO, KV_WIDTH), lambda i, c=col: (jnp.maximum(order(i) * per - 1, 0), c))
    return cur, prev


def _attn_fwd(proj, sink, *, name):
    T = proj.shape[0]
    qb = ATTN_Q_BLOCK
    cur, prev = _attn_specs(T, lambda i: i)

    def body(sink_ref, q_ref, kc_ref, kp_ref, vc_ref, vp_ref, z0_ref, z1_ref, o_ref, y_ref, lse_ref):
        i = pl.program_id(0)
        q = q_ref[...].astype(BF16)
        kk = jnp.concatenate([kp_ref[...], kc_ref[...]], axis=0).astype(BF16)
        vv = jnp.concatenate([vp_ref[...], vc_ref[...]], axis=0).astype(BF16)
        mask = _attn_mask(i)
        lane = lax.broadcasted_iota(jnp.int32, (qb, 128), 1)
        lse = jnp.zeros((qb, 128), F32)
        for h in range(N_Q_HEADS):
            hs = slice(h * HEAD_DIM, (h + 1) * HEAD_DIM)
            ks = slice((h // Q_PER_KV) * HEAD_DIM, (h // Q_PER_KV + 1) * HEAD_DIM)
            s = lax.dot_general(q[:, hs], kk[:, ks], (((1,), (1,)), ((), ())), preferred_element_type=F32)
            s = jnp.where(mask, s * (HEAD_DIM ** -0.5), NEG_INF)
            sk = sink_ref[h]
            m = jnp.maximum(jnp.max(s, axis=1, keepdims=True), sk)
            p = jnp.exp(s - m)
            den = jnp.sum(p, axis=1, keepdims=True) + jnp.exp(sk - m)
            oh = jnp.dot(p.astype(BF16), vv[:, ks], preferred_element_type=F32) / den
            zr = z0_ref if h < N_Q_HEADS // 2 else z1_ref
            zs = slice((h % (N_Q_HEADS // 2)) * HEAD_DIM, (h % (N_Q_HEADS // 2) + 1) * HEAD_DIM)
            o_ref[:, hs] = oh.astype(o_ref.dtype)
            y_ref[:, hs] = (oh * _silu(zr[:, zs].astype(F32))).astype(y_ref.dtype)
            lse = jnp.where(lane == h, m + jnp.log(den), lse)
        lse_ref[...] = lse

    kcol, vcol = OFF_K // KV_WIDTH, OFF_V // KV_WIDTH
    tile = pl.BlockSpec((qb, ATTN_WIDTH), lambda i: (i, 0))
    in_specs = [pl.BlockSpec(memory_space=pltpu.SMEM), cur(ATTN_WIDTH, OFF_Q // ATTN_WIDTH), cur(KV_WIDTH, kcol), prev(kcol),
                cur(KV_WIDTH, vcol), prev(vcol), cur(HALF, OFF_AZ // HALF), cur(HALF, OFF_AZ // HALF + 1)]
    return pl.pallas_call(
        body, name=name, grid=(T // qb,), in_specs=in_specs,
        out_specs=[tile, tile, pl.BlockSpec((qb, 128), lambda i: (i, 0))],
        out_shape=[jax.ShapeDtypeStruct((T, ATTN_WIDTH), BF16), jax.ShapeDtypeStruct((T, ATTN_WIDTH), BF16),
                   jax.ShapeDtypeStruct((T, 128), F32)],
        compiler_params=_params(1))(sink, *([proj] * 7))


def _attn_bwd(proj, sink, o, lse, dy, *, name):
    T = proj.shape[0]
    qb = ATTN_Q_BLOCK
    nb = T // qb
    order = lambda i: nb - 1 - i
    cur, prev = _attn_specs(T, order)

    def body(sink_ref, q_ref, kc_ref, kp_ref, vc_ref, vp_ref, z0_ref, z1_ref, o_ref, lse_ref, dy_ref,
             dq_ref, dk_ref, dv_ref, dz_ref, dsink_ref, dk_carry, dv_carry):
        i = pl.program_id(0)
        blk = order(i)
        q = q_ref[...].astype(BF16)
        kk = jnp.concatenate([kp_ref[...], kc_ref[...]], axis=0).astype(BF16)
        vv = jnp.concatenate([vp_ref[...], vc_ref[...]], axis=0).astype(BF16)
        mask = _attn_mask(blk)
        lane = lax.broadcasted_iota(jnp.int32, (1, 128), 1)
        dsink = jnp.zeros((1, 128), F32)
        scale = HEAD_DIM ** -0.5
        for kv in range(N_KV_HEADS):
            ks = slice(kv * HEAD_DIM, (kv + 1) * HEAD_DIM)
            dk_acc = jnp.zeros((qb + ATTN_HALO, HEAD_DIM), F32)
            dv_acc = jnp.zeros((qb + ATTN_HALO, HEAD_DIM), F32)
            for h in range(kv * Q_PER_KV, (kv + 1) * Q_PER_KV):
                hs = slice(h * HEAD_DIM, (h + 1) * HEAD_DIM)
                zr = z0_ref if h < N_Q_HEADS // 2 else z1_ref
                zs = slice((h % (N_Q_HEADS // 2)) * HEAD_DIM, (h % (N_Q_HEADS // 2) + 1) * HEAD_DIM)
                z = zr[:, zs].astype(F32)
                dyh = dy_ref[:, hs].astype(F32)
                oh = o_ref[:, hs].astype(F32)
                do = dyh * _silu(z)
                dz_ref[:, hs] = (dyh * oh * _dsilu(z)).astype(dz_ref.dtype)
                drow = jnp.sum(do * oh, axis=1, keepdims=True)
                lse_h = lse_ref[:, h:h + 1]
                s = lax.dot_general(q[:, hs], kk[:, ks], (((1,), (1,)), ((), ())), preferred_element_type=F32)
                p = jnp.exp(jnp.where(mask, s * scale, NEG_INF) - lse_h)
                do_b = do.astype(BF16)
                dv_acc = dv_acc + lax.dot_general(p.astype(BF16), do_b, (((0,), (0,)), ((), ())),
                                                  preferred_element_type=F32)
                dp = lax.dot_general(do_b, vv[:, ks], (((1,), (1,)), ((), ())), preferred_element_type=F32)
                ds = (p * (dp - drow)).astype(BF16)
                dq_ref[:, hs] = (jnp.dot(ds, kk[:, ks], preferred_element_type=F32) * scale).astype(dq_ref.dtype)
                dk_acc = dk_acc + lax.dot_general(ds, q[:, hs], (((0,), (0,)), ((), ())),
                                                  preferred_element_type=F32) * scale
                p_sink = jnp.exp(sink_ref[h] - lse_h)
                dsink = jnp.where(lane == h, -jnp.sum(p_sink * drow, axis=0, keepdims=True), dsink)
            for acc, carry, out in ((dk_acc, dk_carry, dk_ref), (dv_acc, dv_carry, dv_ref)):
                tail = acc[qb:, :] + jnp.where(i == 0, 0.0, carry[:, ks])
                out[:, ks] = jnp.concatenate([acc[ATTN_HALO:qb, :], tail], axis=0).astype(out.dtype)
                carry[:, ks] = acc[:ATTN_HALO, :]

        @pl.when(i == 0)
        def _():
            dsink_ref[...] = dsink

        @pl.when(i > 0)
        def _():
            dsink_ref[...] += dsink

    kcol, vcol = OFF_K // KV_WIDTH, OFF_V // KV_WIDTH
    tile = pl.BlockSpec((qb, ATTN_WIDTH), lambda i: (order(i), 0))
    kv_tile = pl.BlockSpec((qb, KV_WIDTH), lambda i: (order(i), 0))
    lse_spec = pl.BlockSpec((qb, 128), lambda i: (order(i), 0))
    in_specs = [pl.BlockSpec(memory_space=pltpu.SMEM), cur(ATTN_WIDTH, OFF_Q // ATTN_WIDTH), cur(KV_WIDTH, kcol), prev(kcol),
                cur(KV_WIDTH, vcol), prev(vcol), cur(HALF, OFF_AZ // HALF), cur(HALF, OFF_AZ // HALF + 1),
                tile, lse_spec, tile]
    return pl.pallas_call(
        body, name=name, grid=(nb,), in_specs=in_specs,
        out_specs=[tile, kv_tile, kv_tile, tile, pl.BlockSpec((1, 128), lambda i: (0, 0))],
        out_shape=[jax.ShapeDtypeStruct((T, ATTN_WIDTH), BF16), jax.ShapeDtypeStruct((T, KV_WIDTH), BF16),
                   jax.ShapeDtypeStruct((T, KV_WIDTH), BF16), jax.ShapeDtypeStruct((T, ATTN_WIDTH), BF16),
                   jax.ShapeDtypeStruct((1, 128), F32)],
        scratch_shapes=[pltpu.VMEM((ATTN_HALO, KV_WIDTH), F32), pltpu.VMEM((ATTN_HALO, KV_WIDTH), F32)],
        compiler_params=_params(1))(sink, *([proj] * 7), o, lse, dy)


def _conv_specs(T, tm):
    per = tm // CONV_HALO
    ca, cb = OFF_CA // HALF, OFF_CB // HALF
    cur = lambda col: pl.BlockSpec((tm, HALF), lambda i, c=col: (i, c))
    prev = lambda col: pl.BlockSpec((CONV_HALO, HALF), lambda i, c=col: (jnp.maximum(i * per - 1, 0), c))
    return [cur(ca), cur(ca + 1), cur(cb), cur(cb + 1), prev(ca), prev(ca + 1), prev(cb), prev(cb + 1)]


def _conv_glu_ext(refs, i, ext_ref):
    a0, a1, b0, b1, pa0, pa1, pb0, pb1 = refs
    a = jnp.concatenate([a0[...], a1[...]], axis=1).astype(F32)
    sb = jax.nn.sigmoid(jnp.concatenate([b0[...], b1[...]], axis=1).astype(F32))
    pa = jnp.concatenate([pa0[...], pa1[...]], axis=1).astype(F32)
    pb = jnp.concatenate([pb0[...], pb1[...]], axis=1).astype(F32)
    ext_ref[:CONV_HALO, :] = jnp.where(i == 0, 0.0, pa * jax.nn.sigmoid(pb))
    ext_ref[CONV_HALO:, :] = a * sb
    return a, sb


def _conv_scratch(tm):
    return [pltpu.VMEM((tm + CONV_HALO, CONV_WIDTH), F32), pltpu.VMEM((7, tm + CONV_HALO - 8, CONV_WIDTH), F32)]


def _conv_fill_shifted(ext_ref, sh_ref):
    rows = sh_ref.shape[1]
    for b in range(1, 8):
        sh_ref[b - 1] = ext_ref[b:b + rows, :]


def _conv_window(ext_ref, sh_ref, start, tm, cols):
    a, b = divmod(start, 8)
    if b == 0:
        return ext_ref[8 * a:8 * a + tm, cols]
    return sh_ref[b - 1, 8 * a:8 * a + tm, cols]


LANES = 128


def _lane_blocks(width):
    return [slice(k, k + LANES) for k in range(0, width, LANES)]


def _conv_taps(ext_ref, sh_ref, dw_ref, out_ref, first_start, step, tm, bias_ref=None):
    sub = tm // 8
    for cols in _lane_blocks(CONV_WIDTH):
        y = None
        for j in range(CONV_KERNEL):
            tap = jnp.broadcast_to(dw_ref[j:j + 1, cols], (8, LANES))
            window = _conv_window(ext_ref, sh_ref, first_start + step * j, tm, cols).reshape(sub, 8, LANES)
            t = tap * window
            y = t if y is None else y + t
        y = y.reshape(tm, LANES)
        out_ref[:, cols] = y if bias_ref is None else y + bias_ref[:, cols]


def _conv_fwd(proj, dw, dwb, lng, lnb, *, name):
    T = proj.shape[0]
    tm = _div(T, 256, 32)

    def body(*refs):
        dw_ref, dwb_ref, g_ref, b_ref, s_ref, yc_ref, ext_ref, sh_ref = refs[8:]
        i = pl.program_id(0)
        _conv_glu_ext(refs[:8], i, ext_ref)
        _conv_fill_shifted(ext_ref, sh_ref)
        _conv_taps(ext_ref, sh_ref, dw_ref, yc_ref, CONV_HALO - (CONV_KERNEL - 1), 1, tm, dwb_ref)
        yc = yc_ref[...]
        mu = jnp.mean(yc, axis=-1, keepdims=True)
        d = yc - mu
        rstd = lax.rsqrt(jnp.mean(d * d, axis=-1, keepdims=True) + EPS)
        s_ref[...] = _silu(d * rstd * g_ref[...] + b_ref[...]).astype(s_ref.dtype)

    row = pl.BlockSpec((1, CONV_WIDTH), lambda i: (0, 0))
    taps = pl.BlockSpec((CONV_KERNEL, CONV_WIDTH), lambda i: (0, 0))
    tile = pl.BlockSpec((tm, CONV_WIDTH), lambda i: (i, 0))
    return pl.pallas_call(
        body, name=name, grid=(T // tm,), in_specs=_conv_specs(T, tm) + [taps, row, row, row], out_specs=[tile, tile],
        out_shape=[jax.ShapeDtypeStruct((T, CONV_WIDTH), BF16), jax.ShapeDtypeStruct((T, CONV_WIDTH), F32)],
        scratch_shapes=_conv_scratch(tm), compiler_params=_params(1))(*([proj] * 8), dw, dwb, lng, lnb)


def _conv_bwd(proj, yc_saved, ds, lng, lnb, *, name):
    T = proj.shape[0]
    tm = _div(T, 256, 32)
    nb = T // tm
    sub = tm // 8

    def body(*refs):
        yc_ref, ds_ref, g_ref, b_ref, dyc_ref, ddw_ref, ddwb_ref, dg_ref, db_ref, ext_ref, sh_ref, acc_ref = refs[8:]
        i = pl.program_id(0)
        _conv_glu_ext(refs[:8], i, ext_ref)
        _conv_fill_shifted(ext_ref, sh_ref)
        yc = yc_ref[...]
        mu = jnp.mean(yc, axis=-1, keepdims=True)
        d = yc - mu
        rstd = lax.rsqrt(jnp.mean(d * d, axis=-1, keepdims=True) + EPS)
        xhat = d * rstd
        dln = ds_ref[...].astype(F32) * _dsilu(xhat * g_ref[...] + b_ref[...])
        dxhat = dln * g_ref[...]
        dyc = rstd * (dxhat - jnp.mean(dxhat, axis=-1, keepdims=True)
                      - xhat * jnp.mean(dxhat * xhat, axis=-1, keepdims=True))
        dyc_ref[...] = dyc
        first = i == 0

        def accumulate(ref, idx, val):
            @pl.when(first)
            def _():
                ref[idx] = val

            @pl.when(jnp.logical_not(first))
            def _():
                ref[idx] += val

        accumulate(dg_ref, slice(None), jnp.sum(dln * xhat, axis=0, keepdims=True))
        accumulate(db_ref, slice(None), jnp.sum(dln, axis=0, keepdims=True))
        accumulate(ddwb_ref, slice(None), jnp.sum(dyc, axis=0, keepdims=True))
        @pl.when(first)
        def _():
            acc_ref[...] = jnp.zeros_like(acc_ref)

        for cols in _lane_blocks(CONV_WIDTH):
            dyc_b = dyc_ref[:, cols].reshape(sub, 8, LANES)
            for j in range(CONV_KERNEL):
                window = _conv_window(ext_ref, sh_ref, CONV_HALO - (CONV_KERNEL - 1) + j, tm, cols)
                acc_ref[j, :, cols] += jnp.sum(dyc_b * window.reshape(sub, 8, LANES), axis=0)

        @pl.when(i == nb - 1)
        def _():
            for j in range(CONV_KERNEL):
                ddw_ref[j:j + 1, :] = jnp.sum(acc_ref[j], axis=0, keepdims=True)

    row = pl.BlockSpec((1, CONV_WIDTH), lambda i: (0, 0))
    taps = pl.BlockSpec((CONV_KERNEL, CONV_WIDTH), lambda i: (0, 0))
    tile = pl.BlockSpec((tm, CONV_WIDTH), lambda i: (i, 0))
    vec = jax.ShapeDtypeStruct((1, CONV_WIDTH), F32)
    return pl.pallas_call(
        body, name=name, grid=(nb,), in_specs=_conv_specs(T, tm) + [tile, tile, row, row],
        out_specs=[tile, taps, row, row, row],
        out_shape=[jax.ShapeDtypeStruct((T, CONV_WIDTH), F32), jax.ShapeDtypeStruct((CONV_KERNEL, CONV_WIDTH), F32), vec, vec, vec],
        scratch_shapes=_conv_scratch(tm) + [pltpu.VMEM((CONV_KERNEL, 8, CONV_WIDTH), F32)],
        compiler_params=_params(1))(*([proj] * 8), yc_saved, ds, lng, lnb)


def _conv_bwd_input(proj, dyc, dw, *, name):
    T = proj.shape[0]
    tm = _div(T, 256, 32)
    per = tm // CONV_HALO
    last = T // CONV_HALO - 1
    nb = T // tm
    ca, cb = OFF_CA // HALF, OFF_CB // HALF

    def body(a0, a1, b0, b1, c_ref, n_ref, dw_ref, da_ref, db_ref, ext_ref, sh_ref, dg_ref):
        i = pl.program_id(0)
        ext_ref[:tm, :] = c_ref[...]
        ext_ref[tm:, :] = jnp.where(i == nb - 1, 0.0, n_ref[...])
        _conv_fill_shifted(ext_ref, sh_ref)
        _conv_taps(ext_ref, sh_ref, dw_ref, dg_ref, CONV_KERNEL - 1, -1, tm)
        dg = dg_ref[...]
        a = jnp.concatenate([a0[...], a1[...]], axis=1).astype(F32)
        sb = jax.nn.sigmoid(jnp.concatenate([b0[...], b1[...]], axis=1).astype(F32))
        da_ref[...] = (dg * sb).astype(da_ref.dtype)
        db_ref[...] = (dg * a * sb * (1.0 - sb)).astype(db_ref.dtype)

    cur = lambda col: pl.BlockSpec((tm, HALF), lambda i, c=col: (i, c))
    tile = pl.BlockSpec((tm, CONV_WIDTH), lambda i: (i, 0))
    nxt = pl.BlockSpec((CONV_HALO, CONV_WIDTH), lambda i: (jnp.minimum((i + 1) * per, last), 0))
    taps = pl.BlockSpec((CONV_KERNEL, CONV_WIDTH), lambda i: (0, 0))
    return pl.pallas_call(
        body, name=name, grid=(nb,), in_specs=[cur(ca), cur(ca + 1), cur(cb), cur(cb + 1), tile, nxt, taps],
        out_specs=[tile, tile], out_shape=[jax.ShapeDtypeStruct((T, CONV_WIDTH), BF16)] * 2,
        scratch_shapes=_conv_scratch(tm) + [pltpu.VMEM((tm, CONV_WIDTH), F32)],
        compiler_params=_params(1))(proj, proj, proj, proj, dyc, dyc, dw)


def _ada_mod(c_all, w_ada, b_slab, *, name):
    L, D, N = w_ada.shape
    tn = _div(N, 512)

    def body(c_ref, w_ref, b_ref, o_ref):
        ca = _silu(c_ref[...]).astype(BF16)
        o_ref[...] = jnp.dot(ca, w_ref[...].astype(BF16), preferred_element_type=F32) + b_ref[...]

    return pl.pallas_call(
        body, name=name, grid=(L, N // tn),
        in_specs=[pl.BlockSpec((N_DEV, D), lambda l, j: (0, 0)), pl.BlockSpec((None, D, tn), lambda l, j: (l, 0, j)),
                  pl.BlockSpec((None, 1, tn), lambda l, j: (l, 0, j))],
        out_specs=pl.BlockSpec((None, N_DEV, tn), lambda l, j: (l, 0, j)),
        out_shape=jax.ShapeDtypeStruct((L, N_DEV, N), F32), compiler_params=_params(2))(c_all, w_ada, b_slab)


def _ada_grad(c_all_t, dmod_slab, *, name):
    D = c_all_t.shape[0]
    L, _, N = dmod_slab.shape
    tm = _div(D, 512)
    tn = _div(N, 512)

    def body(c_ref, d_ref, o_ref):
        ca = _silu(c_ref[...]).astype(BF16).astype(F32)
        dm = d_ref[...].astype(BF16).astype(F32)
        acc = None
        for b in range(N_DEV):
            t = ca[:, b:b + 1] * dm[b:b + 1, :]
            acc = t if acc is None else acc + t
        o_ref[...] = acc

    return pl.pallas_call(
        body, name=name, grid=(L, D // tm, N // tn),
        in_specs=[pl.BlockSpec((tm, N_DEV), lambda l, i, j: (i, 0)), pl.BlockSpec((None, N_DEV, tn), lambda l, i, j: (l, 0, j))],
        out_specs=pl.BlockSpec((None, tm, tn), lambda l, i, j: (l, i, j)),
        out_shape=jax.ShapeDtypeStruct((L, D, N), F32), compiler_params=_params(3))(c_all_t, dmod_slab)


def _flat_tile(R, C, n_arrays):
    cap = max(8, (20 * MIB) // (2 * n_arrays * C * 4))
    return _div(R, cap, 8) if R % 8 == 0 else R


def _adamw_math(w, g, m, v):
    m = ADAM_B1 * m + (1.0 - ADAM_B1) * g
    v = ADAM_B2 * v + (1.0 - ADAM_B2) * (g * g)
    m_hat = m / (1.0 - ADAM_B1 ** ADAM_STEP)
    v_hat = v / (1.0 - ADAM_B2 ** ADAM_STEP)
    delta = -ADAM_LR * (m_hat / (jnp.sqrt(v_hat) + ADAM_EPS) + ADAM_WD * w)
    return delta, m, v


def _adamw(w, m, v, gs, *, name):
    R, C = w.shape
    n_g = len(gs)
    tr = _flat_tile(R, C, 7 + n_g)

    def body(*refs):
        w_ref, m_ref, v_ref = refs[:3]
        g_refs = refs[3:3 + n_g]
        go_ref, d_ref, mo_ref, vo_ref = refs[3 + n_g:]
        g = g_refs[0][...]
        for r in g_refs[1:]:
            g = g + r[...]
        d, mn, vn = _adamw_math(w_ref[...], g, m_ref[...], v_ref[...])
        go_ref[...] = g
        d_ref[...] = d
        mo_ref[...] = mn
        vo_ref[...] = vn

    tile = pl.BlockSpec((tr, C), lambda i: (i, 0))
    shp = jax.ShapeDtypeStruct((R, C), F32)
    return pl.pallas_call(body, name=name, grid=(R // tr,), in_specs=[tile] * (3 + n_g), out_specs=[tile] * 4,
                          out_shape=[shp] * 4, compiler_params=_params(1, 2 * (7 + n_g) * tr * C * 4))(w, m, v, *gs)


def _adamw_layer(w, m, v, layer, gs, outs, *, name):
    _, R, C = w.shape
    n_g = len(gs)
    tr = _flat_tile(R, C, 7 + n_g)

    def body(*refs):
        w_ref, m_ref, v_ref = refs[:3]
        g_refs = refs[3:3 + n_g]
        go_ref, d_ref, mo_ref, vo_ref = refs[3 + n_g + 4:]
        g = g_refs[0][...]
        for r in g_refs[1:]:
            g = g + r[...]
        d, mn, vn = _adamw_math(w_ref[...], g, m_ref[...], v_ref[...])
        go_ref[...] = g
        d_ref[...] = d
        mo_ref[...] = mn
        vo_ref[...] = vn

    lay = pl.BlockSpec((None, tr, C), lambda i: (layer, i, 0))
    tile = pl.BlockSpec((tr, C), lambda i: (i, 0))
    return pl.pallas_call(
        body, name=name, grid=(R // tr,), in_specs=[lay] * 3 + [tile] * n_g + [HBM] * 4, out_specs=[lay] * 4,
        out_shape=[jax.ShapeDtypeStruct(o.shape, o.dtype) for o in outs],
        input_output_aliases={3 + n_g + k: k for k in range(4)},
        compiler_params=_params(1, 2 * (7 + n_g) * tr * C * 4))(w, m, v, *gs, *outs)


def _full_shape(kind, slab_shape):
    G, r, c = slab_shape
    return (G, r, c * N_CHIP) if kind == "cols" else (G, r * N_CHIP, c)


def _slab_tile(G, r, c, n_arrays):
    cap = max(16, (20 * MIB) // (2 * n_arrays * G * c * 4))
    return _div(r, cap, 16)


def _slab_block(kind, G, r, c, tr):
    if kind == "cols":
        return pl.BlockSpec((G, tr, c), lambda i, chip: (0, i, chip[0]))
    per = r // tr
    return pl.BlockSpec((G, tr, c), lambda i, chip: (0, chip[0] * per + i, 0))


def _cast_into_full(chip, w, layer, kind, *, name):
    _, G, r, c = w.shape
    tr = _slab_tile(G, r, c, 2)

    def body(chip_ref, w_ref, o_ref):
        o_ref[...] = w_ref[...].astype(BF16)

    grid_spec = pltpu.PrefetchScalarGridSpec(
        num_scalar_prefetch=1, grid=(r // tr,),
        in_specs=[pl.BlockSpec((None, G, tr, c), lambda i, chip: (layer, 0, i, 0))], out_specs=_slab_block(kind, G, r, c, tr))
    return pl.pallas_call(body, name=name, grid_spec=grid_spec,
                          out_shape=jax.ShapeDtypeStruct(_full_shape(kind, (G, r, c)), BF16),
                          compiler_params=_params(1, 4 * G * tr * c * 4))(chip, w)


def _sum_contribs(chip, full, land, kind, *, name):
    _, G, r, c = land.shape
    tr = _slab_tile(G, r, c, 5)

    def body(chip_ref, f_ref, l_ref, o_ref):
        o_ref[...] = ((f_ref[...].astype(F32) + l_ref[0].astype(F32)) + l_ref[1].astype(F32)) + l_ref[2].astype(F32)

    grid_spec = pltpu.PrefetchScalarGridSpec(
        num_scalar_prefetch=1, grid=(r // tr,),
        in_specs=[_slab_block(kind, G, r, c, tr), pl.BlockSpec((3, G, tr, c), lambda i, chip: (0, 0, i, 0))],
        out_specs=pl.BlockSpec((G, tr, c), lambda i, chip: (0, i, 0)))
    return pl.pallas_call(body, name=name, grid_spec=grid_spec, out_shape=jax.ShapeDtypeStruct((G, r, c), F32),
                          compiler_params=_params(1, 2 * 5 * G * tr * c * 4))(chip, full, land)


def _place():
    x, y, c = lax.axis_index("x"), lax.axis_index("y"), lax.axis_index("c")
    chips = [(1 - x, y), (x, 1 - y), (1 - x, 1 - y)]
    return x, y, c, chips


def _small_exchange(v, reduce, *, name):
    m_per, n = v.shape
    assert m_per % 8 == 0 and n % 128 == 0

    def body(x_ref, out_ref, *scratch):
        if reduce:
            all_ref, send_sems, recv_sems, local_sem = scratch
        else:
            all_ref = out_ref
            send_sems, recv_sems, local_sem = scratch
        x, y, c, chips = _place()
        me, sibling = (x, y, c), (x, y, 1 - c)

        def rows(px, py, pc):
            return all_ref.at[pl.ds((4 * px + 2 * py + pc) * m_per, m_per), :]

        def copy(k, block, to, src=None):
            return pltpu.make_async_remote_copy(
                src_ref=rows(*block) if src is None else src, dst_ref=rows(*block), send_sem=send_sems.at[k],
                recv_sem=recv_sems.at[k], device_id=to, device_id_type=MESH)

        mine = pltpu.make_async_copy(x_ref, rows(*me), local_sem)
        mine.start()
        first = [copy(0, me, sibling, src=x_ref)]
        first += [copy(1 + j, me, (*chip, c), src=x_ref) for j, chip in enumerate(chips)]
        for cp in first:
            cp.start()
        passed = [copy(4 + j, (*chip, c), sibling) for j, chip in enumerate(chips)]
        for j, chip in enumerate(chips):
            copy(1 + j, (*chip, c), me).wait_recv()
            passed[j].start()
        copy(0, sibling, me).wait_recv()
        for j, chip in enumerate(chips):
            copy(4 + j, (*chip, 1 - c), me).wait_recv()
        for cp in first + passed:
            cp.wait_send()
        mine.wait()
        if reduce:
            acc = all_ref[0:m_per, :]
            for d in range(1, N_DEV):
                acc = acc + all_ref[d * m_per:(d + 1) * m_per, :]
            out_ref[...] = acc

    scratch = [pltpu.SemaphoreType.DMA((7,)), pltpu.SemaphoreType.DMA((7,)), pltpu.SemaphoreType.DMA]
    if reduce:
        scratch = [pltpu.VMEM((N_DEV * m_per, n), F32)] + scratch
    out_rows = m_per if reduce else N_DEV * m_per
    return pl.pallas_call(
        body, name=name, out_shape=jax.ShapeDtypeStruct((out_rows, n), v.dtype),
        in_specs=[pl.BlockSpec(memory_space=pltpu.VMEM)], out_specs=pl.BlockSpec(memory_space=pltpu.VMEM),
        scratch_shapes=scratch,
        compiler_params=pltpu.CompilerParams(vmem_limit_bytes=int(min(VMEM_CAP_BYTES, 4 * N_DEV * m_per * n * 4 + 16 * MIB))))(v)


def _slab(kind, ref, s):
    if kind == "cols":
        w = ref.shape[2] // N_CHIP
        return ref.at[:, :, pl.ds(s * w, w)]
    w = ref.shape[1] // N_CHIP
    return ref.at[:, pl.ds(s * w, w), :]


HBM = pl.BlockSpec(memory_space=pltpu.HBM)
SEM = pl.BlockSpec(memory_space=pltpu.SEMAPHORE)
EFFECT = pltpu.SideEffectType.DATAFLOW_SIDE_EFFECTING


def _in_hbm(v):
    return pltpu.with_memory_space_constraint(v, pltpu.HBM)


def _hbm_like(arrays):
    return [pltpu.HBM(v.shape, v.dtype) for v in arrays]


def _half_rows(ref, c):
    r = ref.shape[1] // 2
    return ref.at[:, pl.ds(c * r, r), :]


def _gather_copy(kinds, full, send_sems, recv_sems, a, j, peer, c, s_src, s_dst, halves):
    src, dst = _slab(kinds[a], full[a], s_src), _slab(kinds[a], full[a], s_dst)
    if halves:
        src, dst = _half_rows(src, c), _half_rows(dst, c)
    return pltpu.make_async_remote_copy(src_ref=src, dst_ref=dst, send_sem=send_sems.at[a * 3 + j],
                                        recv_sem=recv_sems.at[a * 3 + j], device_id=(*peer, c), device_id_type=MESH)


def _gather_start(fulls, kinds, after, *, name, halves=False):
    n = len(fulls)

    def body(*refs):
        k = n + len(after)
        full, send_sems, recv_sems, token = refs[:n], refs[k], refs[k + 1], refs[-1]
        x, y, c, chips = _place()
        s_me = 2 * x + y
        for a in range(n):
            for j, peer in enumerate(chips):
                _gather_copy(kinds, full, send_sems, recv_sems, a, j, peer, c, s_me, s_me, halves).start()
        token[...] = jnp.zeros_like(token)

    sems = pltpu.SemaphoreType.DMA((3 * n,))
    out = pl.pallas_call(
        body, name=name, out_shape=(sems, sems, *_hbm_like(fulls), jax.ShapeDtypeStruct((8, 128), F32)),
        in_specs=[HBM] * n + [ANY] * len(after), out_specs=(SEM, SEM, *[HBM] * n, pl.BlockSpec(memory_space=pltpu.VMEM)),
        input_output_aliases={a: 2 + a for a in range(n)},
        compiler_params=pltpu.CompilerParams(has_side_effects=EFFECT))(*[_in_hbm(f) for f in fulls], *after)
    return out[0], out[1], list(out[2:2 + n]), out[-1]


def _gather_wait(fulls, kinds, send, recv, after, *, name, halves=False):
    n = len(fulls)

    def body(*refs):
        full, send_sems, recv_sems = refs[:n], refs[n], refs[n + 1]
        x, y, c, chips = _place()
        s_me = 2 * x + y
        for a in range(n):
            for j, peer in enumerate(chips):
                cp = _gather_copy(kinds, full, send_sems, recv_sems, a, j, peer, c, s_me, 2 * peer[0] + peer[1], halves)
                cp.wait_send()
                cp.wait_recv()

    return pl.pallas_call(
        body, name=name, out_shape=_hbm_like(fulls), in_specs=[HBM] * n + [SEM, SEM] + [ANY] * len(after),
        out_specs=[HBM] * n, input_output_aliases={a: a for a in range(n)},
        compiler_params=pltpu.CompilerParams(has_side_effects=EFFECT))(*fulls, send, recv, *after)


def _sibling_fill(full, kind, *, name):
    def body(in_ref, out_ref, send_sems, recv_sems):
        x, y, c, chips = _place()

        def copy(j, peer, half):
            part = _half_rows(_slab(kind, out_ref, 2 * peer[0] + peer[1]), half)
            return pltpu.make_async_remote_copy(src_ref=part, dst_ref=part, send_sem=send_sems.at[j], recv_sem=recv_sems.at[j],
                                                device_id=(x, y, 1 - c), device_id_type=MESH)

        sent = [copy(j, peer, c) for j, peer in enumerate(chips)]
        for cp in sent:
            cp.start()
        for j, peer in enumerate(chips):
            copy(j, peer, 1 - c).wait_recv()
        for cp in sent:
            cp.wait_send()

    return pl.pallas_call(
        body, name=name, out_shape=jax.ShapeDtypeStruct(full.shape, full.dtype), in_specs=[ANY], out_specs=ANY,
        input_output_aliases={0: 0}, scratch_shapes=[pltpu.SemaphoreType.DMA((3,)), pltpu.SemaphoreType.DMA((3,))])(full)


def _scatter_copy(kinds, full, land, send_sems, recv_sems, a, j, peer, c):
    return pltpu.make_async_remote_copy(
        src_ref=_slab(kinds[a], full[a], 2 * peer[0] + peer[1]), dst_ref=land[a].at[j], send_sem=send_sems.at[a * 3 + j],
        recv_sem=recv_sems.at[a * 3 + j], device_id=(*peer, c), device_id_type=MESH)


def _scatter_start(fulls, lands, kinds, after, *, name):
    n = len(fulls)

    def body(*refs):
        k = 2 * n + len(after)
        full, land, send_sems, recv_sems, token = refs[:n], refs[n:2 * n], refs[k], refs[k + 1], refs[-1]
        _, _, c, chips = _place()
        for a in range(n):
            for j, peer in enumerate(chips):
                _scatter_copy(kinds, full, land, send_sems, recv_sems, a, j, peer, c).start()
        token[...] = jnp.zeros_like(token)

    sems = pltpu.SemaphoreType.DMA((3 * n,))
    out = pl.pallas_call(
        body, name=name,
        out_shape=(sems, sems, *_hbm_like(fulls), *_hbm_like(lands), jax.ShapeDtypeStruct((8, 128), F32)),
        in_specs=[HBM] * (2 * n) + [ANY] * len(after),
        out_specs=(SEM, SEM, *[HBM] * (2 * n), pl.BlockSpec(memory_space=pltpu.VMEM)),
        input_output_aliases={a: 2 + a for a in range(2 * n)},
        compiler_params=pltpu.CompilerParams(has_side_effects=EFFECT))(*[_in_hbm(f) for f in list(fulls) + list(lands)], *after)
    return out[0], out[1], list(out[2:2 + n]), list(out[2 + n:2 + 2 * n]), out[-1]


def _scatter_wait(fulls, lands, kinds, send, recv, after, *, name):
    n = len(fulls)

    def body(*refs):
        full, land, send_sems, recv_sems = refs[:n], refs[n:2 * n], refs[2 * n], refs[2 * n + 1]
        _, _, c, chips = _place()
        for a in range(n):
            for j, peer in enumerate(chips):
                cp = _scatter_copy(kinds, full, land, send_sems, recv_sems, a, j, peer, c)
                cp.wait_send()
                cp.wait_recv()

    out = pl.pallas_call(
        body, name=name, out_shape=_hbm_like(list(fulls) + list(lands)),
        in_specs=[HBM] * (2 * n) + [SEM, SEM] + [ANY] * len(after), out_specs=[HBM] * (2 * n),
        input_output_aliases={a: a for a in range(2 * n)},
        compiler_params=pltpu.CompilerParams(has_side_effects=EFFECT))(*fulls, *lands, send, recv, *after)
    return list(out[:n]), list(out[n:])


def _swap_copy(src, dst, send_sems, recv_sems, a):
    x, y, c, _ = _place()
    return pltpu.make_async_remote_copy(src_ref=src[a], dst_ref=dst[a], send_sem=send_sems.at[a], recv_sem=recv_sems.at[a],
                                        device_id=(x, y, 1 - c), device_id_type=MESH)


def _swap_start(parts, lands, *, name):
    n = len(parts)

    def body(*refs):
        src, dst, send_sems, recv_sems, token = refs[:n], refs[n:2 * n], refs[2 * n], refs[2 * n + 1], refs[-1]
        for a in range(n):
            _swap_copy(src, dst, send_sems, recv_sems, a).start()
        token[...] = jnp.zeros_like(token)

    sems = pltpu.SemaphoreType.DMA((n,))
    out = pl.pallas_call(
        body, name=name,
        out_shape=(sems, sems, *_hbm_like(parts), *_hbm_like(lands), jax.ShapeDtypeStruct((8, 128), F32)),
        in_specs=[HBM] * (2 * n), out_specs=(SEM, SEM, *[HBM] * (2 * n), pl.BlockSpec(memory_space=pltpu.VMEM)),
        input_output_aliases={a: 2 + a for a in range(2 * n)},
        compiler_params=pltpu.CompilerParams(has_side_effects=EFFECT))(*[_in_hbm(f) for f in list(parts) + list(lands)])
    return out[0], out[1], list(out[2:2 + n]), list(out[2 + n:2 + 2 * n]), out[-1]


def _swap_wait(parts, lands, send, recv, after, *, name):
    n = len(parts)

    def body(*refs):
        src, dst, send_sems, recv_sems = refs[:n], refs[n:2 * n], refs[2 * n], refs[2 * n + 1]
        for a in range(n):
            cp = _swap_copy(src, dst, send_sems, recv_sems, a)
            cp.wait_send()
            cp.wait_recv()

    out = pl.pallas_call(
        body, name=name, out_shape=_hbm_like(list(parts) + list(lands)),
        in_specs=[HBM] * (2 * n) + [SEM, SEM] + [ANY] * len(after), out_specs=[HBM] * (2 * n),
        input_output_aliases={a: a for a in range(2 * n)},
        compiler_params=pltpu.CompilerParams(has_side_effects=EFFECT))(*parts, *lands, send, recv, *after)
    return list(out[:n]), list(out[n:])


def _pad_rows(v, rows):
    return jnp.pad(v, ((0, rows - v.shape[0]), (0, 0)))


def _pack(vectors):
    flat = jnp.concatenate([v.reshape(-1) for v in vectors])
    n = -(-flat.shape[0] // 1024) * 1024
    return jnp.pad(flat, (0, n - flat.shape[0])).reshape(8, n // 8)


def _unpack(block, shapes):
    flat = block.reshape(-1)
    out, pos = [], 0
    for shp in shapes:
        size = 1
        for d in shp:
            size *= d
        out.append(flat[pos:pos + size].reshape(shp))
        pos += size
    return out


def kernel(x, c, norm_g, w_ada, b_ada, w_in, pool_w, pool_scale, attn_sink, conv_dw, conv_dw_b, conv_ln_g, conv_ln_b, conv_pw, w_branch_pool, w_branch_attn, w_branch_conv, w_out, final_g, loss_target, m_norm_g, m_w_ada, m_b_ada, m_w_in, m_pool_w, m_pool_scale, m_attn_sink, m_conv_dw, m_conv_dw_b, m_conv_ln_g, m_conv_ln_b, m_conv_pw, m_w_branch_pool, m_w_branch_attn, m_w_branch_conv, m_w_out, m_final_g, v_norm_g, v_w_ada, v_b_ada, v_w_in, v_pool_w, v_pool_scale, v_attn_sink, v_conv_dw, v_conv_dw_b, v_conv_ln_g, v_conv_ln_b, v_conv_pw, v_w_branch_pool, v_w_branch_attn, v_w_branch_conv, v_w_out, v_final_g):
    _, T, D = x.shape
    L = norm_g.shape[0]
    IN = w_in.shape[2] * N_CHIP
    assert IN == OFF_G + 3 * D and D % HALF == 0 and T % 512 == 0
    xi, yi, ci = lax.axis_index("x"), lax.axis_index("y"), lax.axis_index("c")
    chip = 2 * xi + yi
    dev = 2 * chip + ci
    x0 = x.reshape(T, D)
    target = loss_target.reshape(T, D)

    big = [("cols", w_in, m_w_in, v_w_in), ("cols", w_branch_pool, m_w_branch_pool, v_w_branch_pool),
           ("cols", w_branch_attn, m_w_branch_attn, v_w_branch_attn), ("cols", w_branch_conv, m_w_branch_conv, v_w_branch_conv),
           ("rows", w_out, m_w_out, v_w_out), ("rows", conv_pw, m_conv_pw, v_conv_pw), ("rows", pool_w, m_pool_w, v_pool_w)]
    kinds = [b[0] for b in big]
    n_big = len(big)
    as_groups = lambda t: t if t.ndim == 4 else t.reshape(L, 1, t.shape[1], t.shape[2])
    chip_arr = jnp.reshape(chip, (1,)).astype(jnp.int32)

    c_all = _small_exchange(_pad_rows(c, 8), False, name="gather_c")[0::8]
    taps_rows = -(-(L * CONV_KERNEL) // 8) * 8
    dw_blocks = _small_exchange(_pad_rows(conv_dw.reshape(L * CONV_KERNEL, -1), taps_rows), False, name="gather_taps")
    dw_blocks = dw_blocks.reshape(N_CHIP, 2, taps_rows, -1)[:, 0, :L * CONV_KERNEL]
    conv_dw_full = dw_blocks.reshape(N_CHIP, L, CONV_KERNEL, -1).transpose(1, 2, 0, 3).reshape(L, CONV_KERNEL, CONV_WIDTH)
    n_ada = w_ada.shape[2]
    b_slab = lax.dynamic_slice_in_dim(b_ada, chip * n_ada, n_ada, axis=1).reshape(L, 1, n_ada)
    mod_part = _ada_mod(c_all, w_ada, b_slab, name="ada_mod")
    mod_blocks = _small_exchange(mod_part.reshape(L * N_DEV, n_ada), False, name="gather_mod")
    mod_blocks = mod_blocks.reshape(N_CHIP, 2, L, N_DEV, n_ada)[:, 0]
    mod_all = mod_blocks.transpose(1, 2, 0, 3).reshape(L, N_DEV, 3 * D)
    mod = lax.dynamic_index_in_dim(mod_all, dev, axis=1, keepdims=False)
    shift, scale, gate = mod[:, :D], mod[:, D:2 * D], mod[:, 2 * D:]

    groups = [[0], list(range(1, n_big))]
    weights, gather_tokens = [], []
    for l in range(L):
        fulls = [_cast_into_full(chip_arr, as_groups(b[1]), l, b[0], name=f"cast{a}_{l}") for a, b in enumerate(big)]
        started_groups = []
        for gi, idx in enumerate(groups):
            send, recv, part, token = _gather_start([fulls[a] for a in idx], [kinds[a] for a in idx], [mod_all, conv_dw_full],
                                                    name=f"gather_start{l}_{gi}", halves=(l == 0 and gi == 0))
            started_groups.append((part, [kinds[a] for a in idx], send, recv))
            gather_tokens.append(token[0:1, 0:1])
        weights.append(started_groups)
    started = functools.reduce(lambda p, q: p + q, gather_tokens)

    row = lambda v: v.reshape(1, -1)

    xs, saved = [x0], []
    xl = x0
    full_w = []
    for l in range(L):
        h = _norm_mod(xl, row(norm_g[l]), row(scale[l]) + started if l == 0 else row(scale[l]), row(shift[l]), name=f"norm{l}")
        (part, part_kinds, send, recv), rest = weights[l]
        win_f, = _gather_wait(part, part_kinds, send, recv, [h], name=f"gather_wait{l}_0", halves=(l == 0))
        if l == 0:
            win_f = _sibling_fill(win_f, part_kinds[0], name="sibling_fill")
        proj = _mm(h, win_f, "nn", [BF16], name=f"proj{l}", b_layer=0)
        part, part_kinds, send, recv = rest
        wbp_f, wba_f, wbc_f, wout_f, cpw_f, poolw_f = _gather_wait(part, part_kinds, send, recv, [proj], name=f"gather_wait{l}_1")
        full_w.append((win_f, wbp_f, wba_f, wbc_f, wout_f, cpw_f, poolw_f))
        y_pool = _pool_fwd(proj, poolw_f, row(pool_scale[l]), name=f"pool{l}")
        o_attn, y_attn, lse = _attn_fwd(proj, attn_sink[l], name=f"attn{l}")
        s_conv, yc = _conv_fwd(proj, conv_dw_full[l], row(conv_dw_b[l]), row(conv_ln_g[l]), row(conv_ln_b[l]), name=f"conv{l}")
        cpre, y_conv = _mm(s_conv, cpw_f, "nn", [BF16, BF16], name=f"conv_pw{l}", b_layer=0, tn_cap=HALF,
                           extras=[(proj, "tile", OFF_CZ)], epilogue=lambda acc, z: (acc, acc * _silu(z.astype(F32))))
        merged, bp, ba, bc = _merge((y_pool, y_attn, y_conv), (wbp_f, wba_f, wbc_f), proj, D, name=f"merge{l}")
        x_new, o = _mm(merged, wout_f, "nn", [F32, BF16], name=f"out{l}", b_layer=0,
                       extras=[(xl, "tile", 0), (row(gate[l]), "row", 0)],
                       epilogue=lambda acc, xv, g: (xv + g * acc, acc))
        saved.append(dict(h=h, proj=proj, y_pool=y_pool, o_attn=o_attn, y_attn=y_attn, lse=lse, s_conv=s_conv, yc=yc, cpre=cpre,
                          y_conv=y_conv, merged=merged, bp=bp, ba=ba, bc=bc, o=o))
        xl = x_new
        xs.append(xl)

    loss_part, dx, d_final_g, dmo, d_gate = _final_loss(xl, target, row(final_g), (saved[L - 1]["o"], row(gate[L - 1])),
                                                        name="final_loss")
    loss = lax.psum(loss_part[0, 0], ("x", "y", "c"))

    small, dmods, scattering = [], [], {}
    scattered = jnp.zeros((1, 1), F32)
    for l in reversed(range(L)):
        sv = saved[l]
        proj = sv["proj"]
        win_f, wbp_f, wba_f, wbc_f, wout_f, cpw_f, poolw_f = full_w[l]
        dmerged = _mm(dmo, wout_f, "nt", [BF16], name=f"d_merged{l}", b_layer=0)
        g_wout = _mm(sv["merged"], dmo, "tn", [BF16], name=f"g_wout{l}")
        dbp, dba, dbc, dgp, dga, dgc = _merge_bwd(dmerged, (sv["bp"], sv["ba"], sv["bc"]), proj, D, name=f"merge_bwd{l}")
        dy_pool = _mm(dbp, wbp_f, "nt", [BF16], name=f"dy_pool{l}", b_layer=0)
        dy_attn = _mm(dba, wba_f, "nt", [BF16], name=f"dy_attn{l}", b_layer=0)
        dy_conv = _mm(dbc, wbc_f, "nt", [BF16], name=f"dy_conv{l}", b_layer=0)
        g_wbp = _mm(sv["y_pool"], dbp, "tn", [BF16], name=f"g_wbp{l}")
        g_wba = _mm(sv["y_attn"], dba, "tn", [BF16], name=f"g_wba{l}")
        g_wbc = _mm(sv["y_conv"], dbc, "tn", [BF16], name=f"g_wbc{l}")
        dz_pool, dmn, d_pool_scale, g_poolw = _pool_bwd(proj, dy_pool, poolw_f, row(pool_scale[l]) + scattered, name=f"pool_bwd{l}")
        du_pool = _pool_bwd_window(dmn, name=f"pool_bwd_window{l}")
        dq, dk, dv, dz_attn, d_sink = _attn_bwd(proj, attn_sink[l], sv["o_attn"], sv["lse"], dy_attn, name=f"attn_bwd{l}")
        dcpre, dz_conv = _conv_out_bwd(dy_conv, sv["cpre"], proj, name=f"conv_out_bwd{l}")
        ds_conv = _mm(dcpre, cpw_f, "nt", [BF16], name=f"ds_conv{l}", b_layer=0)
        g_cpw = _mm(sv["s_conv"], dcpre, "tn", [BF16], name=f"g_cpw{l}")
        taps = conv_dw_full[l]
        dyc, d_taps, d_dwb, d_lng, d_lnb = _conv_bwd(proj, sv["yc"], ds_conv, row(conv_ln_g[l]), row(conv_ln_b[l]),
                                                    name=f"conv_bwd{l}")
        da_conv, db_conv = _conv_bwd_input(proj, dyc, taps, name=f"conv_bwd_input{l}")
        dproj = jnp.concatenate([du_pool, dz_pool, dq, dk, dv, dz_attn, da_conv, db_conv, dz_conv, dgp, dga, dgc], axis=1)
        dh = _mm(dproj, win_f, "nt", [BF16], name=f"dh{l}", b_layer=0, tk_cap=1536)
        g_win = _mm(sv["h"], dproj, "tn", [BF16], name=f"g_win{l}", tn_cap=768)
        d_gate_here = d_gate
        if l > 0:
            dx, d_ng, d_scale, d_shift, dmo, d_gate = _norm_mod_bwd(
                xs[l], dh, dx, row(norm_g[l]), row(scale[l]), (saved[l - 1]["o"], row(gate[l - 1])), name=f"norm_bwd{l}")
        else:
            dx, d_ng, d_scale, d_shift = _norm_mod_bwd(xs[l], dh, dx, row(norm_g[l]), row(scale[l]), name=f"norm_bwd{l}")
        dmods.append(jnp.concatenate([d_shift, d_scale, d_gate_here], axis=1))
        small.append([d_ng, d_pool_scale, d_sink[:, :N_Q_HEADS], d_taps, d_dwb, d_lng, d_lnb])
        before_start = []
        if l == 0:
            stacked = [jnp.stack([small[L - 1 - k][q] for k in range(L)]) for q in range(len(small[0]))]
            dmod_mine = jnp.concatenate(dmods[::-1], axis=0)
            small_shapes = [s.shape for s in stacked] + [d_final_g.shape, dmod_mine.shape]
            reduced = _small_exchange(_pack(stacked + [d_final_g, dmod_mine]), True, name="reduce_small")
            dmod_all = _small_exchange(_pad_rows(dmod_mine, 8), False, name="gather_dmod").reshape(N_DEV, 8, 3 * D)[:, :L]
            before_start = [reduced, dmod_all]
        grads = [g[None] for g in (g_win, g_wbp, g_wba, g_wbc, g_wout, g_cpw)] + [g_poolw.astype(BF16)]
        lands = [lax.empty((3,) + as_groups(b[1]).shape[1:], BF16) for b in big]
        send, recv, grads, lands, token = _scatter_start(grads, lands, kinds, before_start, name=f"scatter_start{l}")
        scattering[l] = (grads, lands, send, recv)
        scattered = token[0:1, 0:1]
    grad_x = dx.reshape(1, T, D)

    r_ng, r_ps, r_sink, r_taps, r_dwb, r_lng, r_lnb, r_fg, r_bada = _unpack(reduced, small_shapes)
    g_norm_g, g_pool_scale, g_attn_sink = r_ng.reshape(L, D), r_ps.reshape(L, POOL_WIDTH), r_sink.reshape(L, N_Q_HEADS)
    g_conv_dw = lax.dynamic_slice_in_dim(r_taps, chip * (CONV_WIDTH // N_CHIP), CONV_WIDTH // N_CHIP, axis=2)
    g_dwb, g_lng, g_lnb = r_dwb.reshape(L, CONV_WIDTH), r_lng.reshape(L, CONV_WIDTH), r_lnb.reshape(L, CONV_WIDTH)
    g_final_g, g_b_ada = r_fg.reshape(D), r_bada

    small_w = [norm_g, b_ada, pool_scale, attn_sink, conv_dw, conv_dw_b, conv_ln_g, conv_ln_b, final_g]
    small_m = [m_norm_g, m_b_ada, m_pool_scale, m_attn_sink, m_conv_dw, m_conv_dw_b, m_conv_ln_g, m_conv_ln_b, m_final_g]
    small_v = [v_norm_g, v_b_ada, v_pool_scale, v_attn_sink, v_conv_dw, v_conv_dw_b, v_conv_ln_g, v_conv_ln_b, v_final_g]
    small_g = [g_norm_g, g_b_ada, g_pool_scale, g_attn_sink, g_conv_dw, g_dwb, g_lng, g_lnb, g_final_g]
    flat = lambda t: t.reshape(-1, t.shape[-1])

    def small_updates(token):
        dmod_slab = lax.dynamic_slice_in_dim(dmod_all, chip * n_ada, n_ada, axis=2).transpose(1, 0, 2)
        g_w_ada = _ada_grad(c_all.T, dmod_slab + scattered + token, name="ada_grad")
        ada = _adamw(flat(w_ada), flat(m_w_ada), flat(v_w_ada), [flat(g_w_ada)], name="adamw_ada")
        sm = _adamw(_pack(small_w), _pack(small_m), _pack(small_v), [_pack(small_g) + token], name="adamw_small")
        return [t.reshape(w_ada.shape) for t in ada], sm

    stacked3 = lambda t: t.reshape(L, -1, t.shape[-1])
    two = lambda t: t.reshape(-1, t.shape[-1])
    outs = [[lax.empty(stacked3(b[1]).shape, F32) for _ in range(4)] for b in big]

    def update(l, swapping, after):
        parts, lands, send, recv = swapping
        parts, others = _swap_wait(parts, lands, send, recv, after, name=f"swap_wait{l}")
        for a, (_, w, m, v) in enumerate(big):
            outs[a] = _adamw_layer(stacked3(w), stacked3(m), stacked3(v), l, [two(parts[a]), two(others[a])], outs[a],
                                   name=f"adamw{a}_{l}")

    after = [grad_x]
    swapping = ada = sm = None
    for l in reversed(range(L)):
        if l == 0 and swapping is not None:
            update(1, swapping, after)
            after, swapping = [outs[a][0] for a in range(n_big)], None
        grads, lands, send, recv = scattering[l]
        grads, lands = _scatter_wait(grads, lands, kinds, send, recv, after, name=f"scatter_wait{l}")
        parts = [_sum_contribs(chip_arr, grads[a], lands[a], kinds[a], name=f"sum_grads{a}_{l}") for a in range(n_big)]
        send, recv, parts, lands, token = _swap_start(parts, [lax.empty(p.shape, F32) for p in parts], name=f"swap_start{l}")
        if ada is None:
            ada, sm = small_updates(token[0:1, 0:1])
            after = [ada[0], sm[0]]
        if swapping is not None:
            update(l + 1, swapping, [token, ada[0], sm[0]])
            after = [outs[a][0] for a in range(n_big)]
        swapping = (parts, lands, send, recv)
    update(0, swapping, [outs[a][0] for a in range(n_big)] if L > 1 else [ada[0]])
    results = {a: [t.reshape(big[a][1].shape) for t in outs[a]] for a in range(n_big)}
    sm_g, sm_d, sm_m, sm_v = [_unpack(t, [w.shape for w in small_w]) for t in sm]

    def leaves(k, pick):
        s = pick
        return [s[0], ada[k], s[1], results[0][k], results[6][k], s[2], s[3], s[4], s[5], s[6], s[7], results[5][k],
                results[1][k], results[2][k], results[3][k], results[4][k], s[8]]

    return (loss, grad_x, *leaves(0, sm_g), *leaves(1, sm_d), *leaves(2, sm_m), *leaves(3, sm_v))
```
